```python
import jax, jax.numpy as jnp
from jax import lax
import numpy as np

D_MODEL = 2048
BATCH = 4
SEQ = 2048
DEPTH = 1

MEM_LEN = 256
EPS = 1e-6
D_FF = 5504
MLA_HEADS = 8
MLA_NOPE = 128
MLA_ROPE = 64
MLA_QK = MLA_NOPE + MLA_ROPE
MLA_V = 128
MLA_Q_RANK = 512
MLA_KV_RANK = 256
ROPE_BASE = 10000.0
Q_BLOCK = 128
HG_HEADS = 8
HG_DK = 128
HG_DV = 128
HG_CHUNK = 64
MLA_WIDTH = MLA_HEADS * MLA_V
HG_WIDTH = HG_HEADS * HG_DV
MIX_WIDTH = MLA_WIDTH + HG_WIDTH
HG_KW = HG_HEADS * HG_DK
IN_SIZES = (MLA_Q_RANK, MLA_KV_RANK, MLA_ROPE, HG_KW, HG_KW, HG_WIDTH, HG_WIDTH)
IN_COLS = int(sum(IN_SIZES))
IN_SPLITS = [int(v) for v in np.cumsum(IN_SIZES)[:-1]]
MEM_HEADS = 4
MEM_HD = 128
MEM_WIDTH = MEM_HEADS * MEM_HD

kernel_name = "hymba_mla_hgrn2_macaron_layer"


def rmsnorm(x, g):
    xf = x.astype(jnp.float32)
    y = xf * lax.rsqrt(jnp.mean(xf * xf, axis=-1, keepdims=True) + EPS)
    return (y * g.astype(jnp.float32)).astype(x.dtype)


def swiglu(h, w_gate, w_up, w_down):
    return (jax.nn.silu(h @ w_gate) * (h @ w_up)) @ w_down


def rope(x, positions):
    half = MLA_ROPE // 2
    inv_freq = ROPE_BASE ** (-jnp.arange(half, dtype=jnp.float32) / half)
    ang = positions.astype(jnp.float32)[..., None] * inv_freq
    cos = jnp.cos(ang)[:, :, None, :]
    sin = jnp.sin(ang)[:, :, None, :]
    xf = x.astype(jnp.float32)
    x1, x2 = xf[..., :half], xf[..., half:]
    out = jnp.concatenate([x1 * cos - x2 * sin, x2 * cos + x1 * sin], axis=-1)
    return out.astype(x.dtype)


def mla_group(cq, ckv, kpe, positions, q_a_norm, w_q_up, kv_a_norm, w_kv_up, q_norm, k_norm, out_norm):
    B, S, _ = cq.shape
    q = (rmsnorm(cq, q_a_norm) @ w_q_up).reshape(B, S, MLA_HEADS, MLA_QK)
    kv = (rmsnorm(ckv, kv_a_norm) @ w_kv_up).reshape(B, S, MLA_HEADS, MLA_NOPE + MLA_V)
    k_nope, v = kv[..., :MLA_NOPE], kv[..., MLA_NOPE:]
    k_pe = jnp.broadcast_to(kpe[:, :, None, :], (B, S, MLA_HEADS, MLA_ROPE))
    k = jnp.concatenate([k_nope, k_pe], axis=-1)
    q = rmsnorm(q, q_norm)
    k = rmsnorm(k, k_norm)
    q = jnp.concatenate([q[..., :MLA_NOPE], rope(q[..., MLA_NOPE:], positions)], axis=-1)
    k = jnp.concatenate([k[..., :MLA_NOPE], rope(k[..., MLA_NOPE:], positions)], axis=-1)
    scale = MLA_QK ** -0.5
    nb = S // Q_BLOCK
    qb = q.reshape(B, nb, Q_BLOCK, MLA_HEADS, MLA_QK).transpose(1, 0, 2, 3, 4)
    kpos = jnp.arange(S)

    def attend(args):
        qblk, i = args
        qpos = i * Q_BLOCK + jnp.arange(Q_BLOCK)
        s = jnp.einsum('bqhd,bkhd->bhqk', qblk, k, preferred_element_type=jnp.float32) * scale
        s = jnp.where(kpos[None, :] <= qpos[:, None], s, -jnp.inf)
        p = jax.nn.softmax(s, axis=-1).astype(v.dtype)
        return jnp.einsum('bhqk,bkhd->bqhd', p, v)

    o = lax.map(attend, (qb, jnp.arange(nb)))
    o = o.transpose(1, 0, 2, 3, 4).reshape(B, S, MLA_WIDTH)
    return rmsnorm(o, out_norm)


def hgrn2_group(qx, fx, ix, gx, lb, out_norm):
    B, S, _ = qx.shape
    dtype = qx.dtype
    lbf = lb.astype(jnp.float32)
    f = lbf + (1.0 - lbf) * jax.nn.sigmoid(fx.astype(jnp.float32))
    logf = jnp.log(f)
    kx = 1.0 - f
    nc = S // HG_CHUNK

    def heads(t, d):
        return t.astype(jnp.float32).reshape(B, nc, HG_CHUNK, HG_HEADS, d).transpose(1, 0, 3, 2, 4)

    q, k, lf, v = heads(qx, HG_DK), heads(kx, HG_DK), heads(logf, HG_DK), heads(ix, HG_DV)
    causal = jnp.tril(jnp.ones((HG_CHUNK, HG_CHUNK), dtype=bool))

    def step(state, inp):
        qc, kc, vc, lfc = inp
        b = jnp.cumsum(lfc, axis=2)
        o_inter = jnp.einsum('bhtk,bhkv->bhtv', qc * jnp.exp(b), state)
        diff = b[:, :, :, None, :] - b[:, :, None, :, :]
        decay = jnp.exp(jnp.where(causal[None, None, :, :, None], diff, -jnp.inf))
        a = jnp.einsum('bhtk,bhsk,bhtsk->bhts', qc, kc, decay)
        o = o_inter + jnp.einsum('bhts,bhsv->bhtv', a, vc)
        b_last = b[:, :, -1:, :]
        k_dec = kc * jnp.exp(b_last - b)
        new_state = jnp.exp(b_last[:, :, 0, :])[..., None] * state + jnp.einsum('bhsk,bhsv->bhkv', k_dec, vc)
        return new_state, o

    s0 = jnp.zeros((B, HG_HEADS, HG_DK, HG_DV), jnp.float32)
    _, o = lax.scan(step, s0, (q, k, v, lf))
    o = o.transpose(1, 0, 3, 2, 4).reshape(B, S, HG_HEADS, HG_DV)
    o = rmsnorm(o, out_norm) * jax.nn.silu(gx.astype(jnp.float32).reshape(B, S, HG_HEADS, HG_DV))
    return o.reshape(B, S, HG_WIDTH).astype(dtype)


def memory_xattn(h, m, w_q, w_kv, q_norm, k_norm, w_o):
    B, S, _ = h.shape
    M = m.shape[1]
    q = rmsnorm((h @ w_q).reshape(B, S, MEM_HEADS, MEM_HD), q_norm)
    kv = (m @ w_kv).reshape(B, M, MEM_HEADS, 2 * MEM_HD)
    k = rmsnorm(kv[..., :MEM_HD], k_norm)
    v = kv[..., MEM_HD:]
    s = jnp.einsum('bshd,bmhd->bhsm', q, k, preferred_element_type=jnp.float32) * (MEM_HD ** -0.5)
    p = jax.nn.softmax(s, axis=-1).astype(v.dtype)
    o = jnp.einsum('bhsm,bmhd->bshd', p, v).reshape(B, S, MEM_WIDTH)
    return o @ w_o


def setup_inputs(seed: int = 0) -> dict:
    key = jax.random.key(seed)
    ks = iter(jax.random.split(key, 40))
    f32 = jnp.float32

    def w(shape, fan_in):
        return jax.random.normal(next(ks), shape, f32) * (fan_in ** -0.5)

    def gain(shape):
        return 1.0 + 0.02 * jax.random.normal(next(ks), shape, f32)

    L = DEPTH
    x = jax.random.normal(next(ks), (BATCH, SEQ, D_MODEL), f32)
    mem = jax.random.normal(next(ks), (BATCH, MEM_LEN, D_MODEL), f32)
    offset = jax.random.randint(next(ks), (BATCH, 1), 0, 4096, dtype=jnp.int32)
    positions = offset + jnp.arange(SEQ, dtype=jnp.int32)[None, :]
    return {
        "x": x, "mem": mem, "positions": positions,
        "ffn1_norm": gain((L, D_MODEL)),
        "ffn1_w_gate": w((L, D_MODEL, D_FF), D_MODEL),
        "ffn1_w_up": w((L, D_MODEL, D_FF), D_MODEL),
        "ffn1_w_down": w((L, D_FF, D_MODEL), D_FF),
        "mix_norm": gain((L, D_MODEL)),
        "w_in": w((L, D_MODEL, IN_COLS), D_MODEL),
        "mla_q_a_norm": gain((L, MLA_Q_RANK)),
        "mla_w_q_up": w((L, MLA_Q_RANK, MLA_HEADS * MLA_QK), MLA_Q_RANK),
        "mla_kv_a_norm": gain((L, MLA_KV_RANK)),
        "mla_w_kv_up": w((L, MLA_KV_RANK, MLA_HEADS * (MLA_NOPE + MLA_V)), MLA_KV_RANK),
        "mla_q_norm": gain((L, MLA_QK)),
        "mla_k_norm": gain((L, MLA_QK)),
        "mla_out_norm": gain((L, MLA_WIDTH)),
        "hg_lb_logits": 0.1 * jax.random.normal(next(ks), (L + 1, HG_KW), f32),
        "hg_out_norm": gain((L, HG_DV)),
        "w_out": w((L, MIX_WIDTH, D_MODEL), MIX_WIDTH),
        "xattn_norm": gain((L, D_MODEL)),
        "mem_norm": gain((L, D_MODEL)),
        "xattn_w_q": w((L, D_MODEL, MEM_WIDTH), D_MODEL),
        "xattn_w_kv": w((L, D_MODEL, 2 * MEM_WIDTH), D_MODEL),
        "xattn_q_norm": gain((L, MEM_HD)),
        "xattn_k_norm": gain((L, MEM_HD)),
        "xattn_w_o": w((L, MEM_WIDTH, D_MODEL), MEM_WIDTH),
        "ffn2_norm": gain((L, D_MODEL)),
        "ffn2_w_gate": w((L, D_MODEL, D_FF), D_MODEL),
        "ffn2_w_up": w((L, D_MODEL, D_FF), D_MODEL),
        "ffn2_w_down": w((L, D_FF, D_MODEL), D_FF),
    }


def reference(x, mem, positions,
              ffn1_norm, ffn1_w_gate, ffn1_w_up, ffn1_w_down,
              mix_norm, w_in,
              mla_q_a_norm, mla_w_q_up, mla_kv_a_norm, mla_w_kv_up, mla_q_norm, mla_k_norm, mla_out_norm,
              hg_lb_logits, hg_out_norm,
              w_out,
              xattn_norm, mem_norm, xattn_w_q, xattn_w_kv, xattn_q_norm, xattn_k_norm, xattn_w_o,
              ffn2_norm, ffn2_w_gate, ffn2_w_up, ffn2_w_down):
    lb_all = jnp.cumsum(jax.nn.softmax(hg_lb_logits.astype(jnp.float32), axis=0), axis=0)
    for l in range(DEPTH):
        h = rmsnorm(x, ffn1_norm[l])
        x = x + 0.5 * swiglu(h, ffn1_w_gate[l], ffn1_w_up[l], ffn1_w_down[l])
        h = rmsnorm(x, mix_norm[l])
        proj = h @ w_in[l]
        cq, ckv, kpe, hq, hf, hi, hg = jnp.split(proj, IN_SPLITS, axis=-1)
        a = mla_group(cq, ckv, kpe, positions, mla_q_a_norm[l], mla_w_q_up[l], mla_kv_a_norm[l],
                      mla_w_kv_up[l], mla_q_norm[l], mla_k_norm[l], mla_out_norm[l])
        r = hgrn2_group(hq, hf, hi, hg, lb_all[l], hg_out_norm[l])
        x = x + jnp.concatenate([a, r], axis=-1) @ w_out[l]
        h = rmsnorm(x, xattn_norm[l])
        m = rmsnorm(mem, mem_norm[l])
        x = x + memory_xattn(h, m, xattn_w_q[l], xattn_w_kv[l], xattn_q_norm[l], xattn_k_norm[l], xattn_w_o[l])
        h = rmsnorm(x, ffn2_norm[l])
        x = x + 0.5 * swiglu(h, ffn2_w_gate[l], ffn2_w_up[l], ffn2_w_down[l])
    return x
```

```python
import functools

import numpy as np
import jax
import jax.numpy as jnp
from jax import lax
from jax.experimental import pallas as pl
from jax.experimental.pallas import tpu as pltpu

F32 = jnp.float32
BF16 = jnp.bfloat16

EPS = 1e-6
ROPE_BASE = 10000.0
LANES = 128
SUBLANES = 8

D_MODEL = 2048
D_FF = 5504
D_FF_PAD = 5632
FFN_TM = 512
FFN_TF = 512

MLA_HEADS = 8
MLA_NOPE = 128
MLA_ROPE = 64
MLA_QK = MLA_NOPE + MLA_ROPE
MLA_V = 128
MLA_Q_RANK = 512
MLA_KV_RANK = 256
MLA_WIDTH = MLA_HEADS * MLA_V
QK_PAD = 256
Q_UP_PER_HEAD = 384

HG_HEADS = 8
HG_D = 128
HG_CHUNK = 64
HG_LEVELS = (32, 16, 8, 4, 2, 1)
HG_WIDTH = HG_HEADS * HG_D

MEM_HEADS = 4
MEM_HD = 128
MEM_WIDTH = MEM_HEADS * MEM_HD

COL_CQ = 0
COL_CKV = 512
COL_HQ = 768
COL_HF = COL_HQ + HG_WIDTH
COL_HI = COL_HF + HG_WIDTH
COL_HG = COL_HI + HG_WIDTH
COL_KPE = COL_HG + HG_WIDTH
COL_KPE_SW = COL_KPE + LANES
IN_COLS_PAD = COL_KPE_SW + LANES

VMEM_LIMIT = 56 * 1024 * 1024


def _params(*sem):
    return pltpu.CompilerParams(dimension_semantics=sem, vmem_limit_bytes=VMEM_LIMIT)


def _rms_scale(x):
    return lax.rsqrt(jnp.mean(x * x, axis=-1, keepdims=True) + EPS)


def _sigmoid(x):
    return 1.0 / (1.0 + jnp.exp(-x))


def _dot(a, b):
    return jnp.dot(a, b, preferred_element_type=F32)


def _dot_nt(a, b):
    return lax.dot_general(a, b, (((1,), (1,)), ((), ())), preferred_element_type=F32)


def _dot_tn(a, b):
    return lax.dot_general(a, b, (((0,), (0,)), ((), ())), preferred_element_type=F32)


def _ffn_kernel(x_ref, g_ref, wg_ref, wu_ref, wd_ref, o_ref, h_scr):
    @pl.when(pl.program_id(1) == 0)
    def _():
        x = x_ref[...]
        h_scr[...] = (x * _rms_scale(x) * g_ref[...]).astype(BF16)
        o_ref[...] = x

    h = h_scr[...]
    gate = _dot(h, wg_ref[...])
    up = _dot(h, wu_ref[...])
    a = (0.5 * gate * _sigmoid(gate) * up).astype(BF16)
    o_ref[...] += _dot(a, wd_ref[...])


def _ffn(x, gain, wg, wu, wd):
    t = x.shape[0]
    return pl.pallas_call(
        _ffn_kernel,
        out_shape=jax.ShapeDtypeStruct((t, D_MODEL), F32),
        grid=(t // FFN_TM, D_FF_PAD // FFN_TF),
        in_specs=[
            pl.BlockSpec((FFN_TM, D_MODEL), lambda i, f: (i, 0)),
            pl.BlockSpec((1, D_MODEL), lambda i, f: (0, 0)),
            pl.BlockSpec((D_MODEL, FFN_TF), lambda i, f: (0, f)),
            pl.BlockSpec((D_MODEL, FFN_TF), lambda i, f: (0, f)),
            pl.BlockSpec((FFN_TF, D_MODEL), lambda i, f: (f, 0)),
        ],
        out_specs=pl.BlockSpec((FFN_TM, D_MODEL), lambda i, f: (i, 0)),
        scratch_shapes=[pltpu.VMEM((FFN_TM, D_MODEL), BF16)],
        compiler_params=_params("parallel", "arbitrary"),
        name="ffn",
    )(x, gain, wg, wu, wd)


PROJ_TM = 512
PROJ_TN = 512


def _in_proj_kernel(x_ref, g_ref, w_ref, o_ref, h_scr):
    @pl.when(pl.program_id(1) == 0)
    def _():
        x = x_ref[...]
        h_scr[...] = (x * _rms_scale(x) * g_ref[...]).astype(BF16)

    o_ref[...] = _dot(h_scr[...], w_ref[...])


def _in_proj(x, gain, w):
    t = x.shape[0]
    n = w.shape[1]
    return pl.pallas_call(
        _in_proj_kernel,
        out_shape=jax.ShapeDtypeStruct((t, n), F32),
        grid=(t // PROJ_TM, n // PROJ_TN),
        in_specs=[
            pl.BlockSpec((PROJ_TM, D_MODEL), lambda i, j: (i, 0)),
            pl.BlockSpec((1, D_MODEL), lambda i, j: (0, 0)),
            pl.BlockSpec((D_MODEL, PROJ_TN), lambda i, j: (0, j)),
        ],
        out_specs=pl.BlockSpec((PROJ_TM, PROJ_TN), lambda i, j: (i, j)),
        scratch_shapes=[pltpu.VMEM((PROJ_TM, D_MODEL), BF16)],
        compiler_params=_params("parallel", "arbitrary"),
        name="in_proj",
    )(x, gain, w)


PREP_TM = 512


def _mla_prep_kernel(cq_ref, ckv_ref, kpe_ref, kpesw_ref, pos_ref, invf_ref,
                     qa_ref, kva_ref, wq_ref, wkv_ref,
                     qn_ref, qpa_ref, qpb_ref, kn_ref, kpa_ref, kpb_ref,
                     q_ref, k_ref, v_ref):
    ang = pos_ref[...].astype(F32) * invf_ref[...]
    cos = jnp.cos(ang)
    sin = jnp.sin(ang)

    cq = cq_ref[...]
    cqn = (cq * _rms_scale(cq) * qa_ref[...]).astype(BF16)
    qall = _dot(cqn, wq_ref[...])
    q_cos = qpa_ref[...] * cos
    q_sin = qpb_ref[...] * sin
    for h in range(MLA_HEADS):
        base = h * Q_UP_PER_HEAD
        qn = qall[:, base:base + LANES]
        y = qall[:, base + LANES:base + 2 * LANES]
        ysw = qall[:, base + 2 * LANES:base + 3 * LANES]
        ss = jnp.sum(qn * qn, axis=-1, keepdims=True) + jnp.sum(y * y, axis=-1, keepdims=True)
        r = lax.rsqrt(ss * (1.0 / MLA_QK) + EPS)
        q_ref[:, h * QK_PAD:h * QK_PAD + LANES] = (qn * r * qn_ref[...]).astype(BF16)
        q_ref[:, h * QK_PAD + LANES:(h + 1) * QK_PAD] = ((y * q_cos + ysw * q_sin) * r).astype(BF16)

    ckv = ckv_ref[...]
    ckvn = (ckv * _rms_scale(ckv) * kva_ref[...]).astype(BF16)
    kvall = _dot(ckvn, wkv_ref[...])
    yk = kpe_ref[...]
    kr = yk * (kpa_ref[...] * cos) + kpesw_ref[...] * (kpb_ref[...] * sin)
    ss_pe = jnp.sum(yk * yk, axis=-1, keepdims=True)
    for h in range(MLA_HEADS):
        kn = kvall[:, h * 2 * LANES:h * 2 * LANES + LANES]
        ss = jnp.sum(kn * kn, axis=-1, keepdims=True) + ss_pe
        r = lax.rsqrt(ss * (1.0 / MLA_QK) + EPS)
        k_ref[:, h * QK_PAD:h * QK_PAD + LANES] = (kn * r * kn_ref[...]).astype(BF16)
        k_ref[:, h * QK_PAD + LANES:(h + 1) * QK_PAD] = (kr * r).astype(BF16)
        v_ref[:, h * MLA_V:(h + 1) * MLA_V] = kvall[:, h * 2 * LANES + LANES:(h + 1) * 2 * LANES].astype(BF16)


def _mla_prep(proj, pos, invf, qa, kva, wq, wkv, qn, qpa, qpb, kn, kpa, kpb):
    t = proj.shape[0]
    tm = PREP_TM
    row = lambda i: (0, 0)
    vec = lambda n: pl.BlockSpec((1, n), row)
    return pl.pallas_call(
        _mla_prep_kernel,
        out_shape=(jax.ShapeDtypeStruct((t, MLA_HEADS * QK_PAD), BF16),
                   jax.ShapeDtypeStruct((t, MLA_HEADS * QK_PAD), BF16),
                   jax.ShapeDtypeStruct((t, MLA_WIDTH), BF16)),
        grid=(t // tm,),
        in_specs=[
            pl.BlockSpec((tm, MLA_Q_RANK), lambda i: (i, COL_CQ // MLA_Q_RANK)),
            pl.BlockSpec((tm, MLA_KV_RANK), lambda i: (i, COL_CKV // MLA_KV_RANK)),
            pl.BlockSpec((tm, LANES), lambda i: (i, COL_KPE // LANES)),
            pl.BlockSpec((tm, LANES), lambda i: (i, COL_KPE_SW // LANES)),
            pl.BlockSpec((tm, 1), lambda i: (i, 0)),
            vec(LANES),
            vec(MLA_Q_RANK), vec(MLA_KV_RANK),
            pl.BlockSpec(wq.shape, row), pl.BlockSpec(wkv.shape, row),
            vec(LANES), vec(LANES), vec(LANES), vec(LANES), vec(LANES), vec(LANES),
        ],
        out_specs=(pl.BlockSpec((tm, MLA_HEADS * QK_PAD), lambda i: (i, 0)),
                   pl.BlockSpec((tm, MLA_HEADS * QK_PAD), lambda i: (i, 0)),
                   pl.BlockSpec((tm, MLA_WIDTH), lambda i: (i, 0))),
        compiler_params=_params("parallel"),
        name="mla_prep",
    )(proj, proj, proj, proj, pos, invf, qa, kva, wq, wkv, qn, qpa, qpb, kn, kpa, kpb)


ATT_T = 256


def _mla_attn_kernel(q_ref, k_ref, v_ref, o_ref):
    i = pl.program_id(2)
    q = q_ref[...]
    scale = MLA_QK ** -0.5

    def block(j, carry, masked):
        m, l, acc = carry
        rows = pl.ds(pl.multiple_of(j * ATT_T, ATT_T), ATT_T)
        s = _dot_nt(q, k_ref[rows, :]) * scale
        if masked:
            qpos = lax.broadcasted_iota(jnp.int32, (ATT_T, ATT_T), 0)
            kpos = lax.broadcasted_iota(jnp.int32, (ATT_T, ATT_T), 1)
            s = jnp.where(kpos <= qpos, s, -jnp.inf)
        m_new = jnp.maximum(m, jnp.max(s, axis=-1, keepdims=True))
        p = jnp.exp(s - m_new)
        alpha = jnp.exp(m - m_new)
        l = alpha * l + jnp.sum(p, axis=-1, keepdims=True)
        acc = alpha * acc + _dot(p.astype(BF16), v_ref[rows, :])
        return m_new, l, acc

    init = (jnp.full((ATT_T, 1), -jnp.inf, F32), jnp.zeros((ATT_T, 1), F32),
            jnp.zeros((ATT_T, MLA_V), F32))
    carry = lax.fori_loop(0, i, lambda j, c: block(j, c, False), init)
    _, l, acc = block(i, carry, True)
    o_ref[...] = acc / l


def _mla_attn(q, k, v, batch, seq):
    nq = seq // ATT_T
    return pl.pallas_call(
        _mla_attn_kernel,
        out_shape=jax.ShapeDtypeStruct((batch * seq, MLA_WIDTH), F32),
        grid=(batch, MLA_HEADS, nq),
        in_specs=[
            pl.BlockSpec((ATT_T, QK_PAD), lambda b, h, i: (b * nq + i, h)),
            pl.BlockSpec((seq, QK_PAD), lambda b, h, i: (b, h)),
            pl.BlockSpec((seq, MLA_V), lambda b, h, i: (b, h)),
        ],
        out_specs=pl.BlockSpec((ATT_T, MLA_V), lambda b, h, i: (b * nq + i, h)),
        compiler_params=_params("parallel", "parallel", "arbitrary"),
        name="mla_attn",
    )(q, k, v)


def _hgrn_masks():
    t = np.arange(HG_CHUNK)[:, None]
    s = np.arange(HG_CHUNK)[None, :]
    masks = [((t // c) % 2 == 1) & ((s // c) == (t // c) - 1) for c in HG_LEVELS]
    masks.append(t == s)
    return np.stack(masks).astype(np.float32)


def _hgrn_kernel(q_ref, f_ref, i_ref, g_ref, lbl_ref, gn_ref, msk_ref, tri_ref, o_ref, st_scr, b_scr):
    lbl = lbl_ref[...]
    e = jnp.exp(lbl - jnp.max(lbl, axis=0, keepdims=True))
    lb = e[0:1, :] / jnp.sum(e, axis=0, keepdims=True)
    gn = gn_ref[...]
    tri = tri_ref[...]
    sub = lax.broadcasted_iota(jnp.int32, (SUBLANES, LANES), 0)
    zeros8 = jnp.zeros((SUBLANES, LANES), F32)
    groups = HG_CHUNK // SUBLANES
    st_scr[...] = jnp.zeros_like(st_scr)

    def chunk(c, carry):
        rows = pl.ds(pl.multiple_of(c * HG_CHUNK, HG_CHUNK), HG_CHUNK)
        q = q_ref[rows, :]
        v = i_ref[rows, :]
        f = lb + (1.0 - lb) * _sigmoid(f_ref[rows, :])
        logf = jnp.log(f)
        k = 1.0 - f

        l1 = logf.astype(BF16)
        rem = logf - l1.astype(F32)
        l2 = rem.astype(BF16)
        l3 = (rem - l2.astype(F32)).astype(BF16)
        b = _dot(tri, l1) + _dot(tri, l2) + _dot(tri, l3)
        b_scr[...] = b

        cache = {}

        def rowb(r):
            if r < 0:
                return zeros8
            if r not in cache:
                cache[r] = jnp.broadcast_to(b_scr[r:r + 1, :], (SUBLANES, LANES))
            return cache[r]

        def build(fn):
            return jnp.concatenate([fn(j) for j in range(groups)], axis=0)

        def sel(bounds, vals):
            out = vals[-1]
            for bound, val in zip(reversed(bounds), reversed(vals[:-1])):
                out = jnp.where(sub < bound, val, out)
            return out

        a = jnp.zeros((HG_CHUNK, HG_CHUNK), F32)
        for ci, c in enumerate(HG_LEVELS):
            if c >= SUBLANES:
                m = c // SUBLANES
                bs = build(lambda j: rowb(SUBLANES * (j // m) * m - 1))
                be = build(lambda j: rowb(SUBLANES * ((j // m) * m + m) - 1))
            elif c == 4:
                bs = build(lambda j: sel((4,), (rowb(8 * j - 1), rowb(8 * j + 3))))
                be = build(lambda j: sel((4,), (rowb(8 * j + 3), rowb(8 * j + 7))))
            elif c == 2:
                bs = build(lambda j: sel((2, 4, 6), (rowb(8 * j - 1), rowb(8 * j + 1),
                                                      rowb(8 * j + 3), rowb(8 * j + 5))))
                be = build(lambda j: sel((2, 4, 6), (rowb(8 * j + 1), rowb(8 * j + 3),
                                                      rowb(8 * j + 5), rowb(8 * j + 7))))
            if c == 1:
                qd = q * f
                kd = k
            else:
                qd = q * jnp.exp(b - bs)
                kd = k * jnp.exp(be - b)
            a = a + _dot_nt(qd.astype(BF16), kd.astype(BF16)) * msk_ref[ci]
        a = a + jnp.sum(q * k, axis=-1, keepdims=True) * msk_ref[len(HG_LEVELS)]

        st = st_scr[...]
        b_last = build(lambda j: rowb(HG_CHUNK - 1))
        o = _dot(a.astype(BF16), v.astype(BF16)) + _dot_nt((q * jnp.exp(b)).astype(BF16), st.astype(BF16))
        k_dec = k * jnp.exp(b_last - b)
        st_scr[...] = jnp.exp(rowb(HG_CHUNK - 1)[0:1, :]) * st + _dot_tn(v.astype(BF16), k_dec.astype(BF16))

        g = g_ref[rows, :]
        o_ref[rows, :] = o * _rms_scale(o) * gn * (g * _sigmoid(g))
        return carry

    lax.fori_loop(0, q_ref.shape[0] // HG_CHUNK, chunk, 0)


def _hgrn(proj, lb_logits, out_norm, batch, seq):
    hblk = lambda col: pl.BlockSpec((seq, HG_D), lambda b, h: (b, col // HG_D + h))
    masks = jnp.asarray(_hgrn_masks())
    tri = jnp.asarray(np.tril(np.ones((HG_CHUNK, HG_CHUNK), np.float32)), dtype=BF16)
    nlev = masks.shape[0]
    return pl.pallas_call(
        _hgrn_kernel,
        out_shape=jax.ShapeDtypeStruct((batch * seq, HG_WIDTH), F32),
        grid=(batch, HG_HEADS),
        in_specs=[
            hblk(COL_HQ), hblk(COL_HF), hblk(COL_HI), hblk(COL_HG),
            pl.BlockSpec((lb_logits.shape[0], HG_D), lambda b, h: (0, h)),
            pl.BlockSpec((1, HG_D), lambda b, h: (0, 0)),
            pl.BlockSpec((nlev, HG_CHUNK, HG_CHUNK), lambda b, h: (0, 0, 0)),
            pl.BlockSpec((HG_CHUNK, HG_CHUNK), lambda b, h: (0, 0)),
        ],
        out_specs=pl.BlockSpec((seq, HG_D), lambda b, h: (b, h)),
        scratch_shapes=[pltpu.VMEM((HG_D, HG_D), F32), pltpu.VMEM((HG_CHUNK, HG_D), F32)],
        compiler_params=_params("parallel", "parallel"),
        name="hgrn2",
    )(proj, proj, proj, proj, lb_logits, out_norm, masks, tri)


OUT_TM = 512
OUT_TN = 512


def _out_proj_kernel(a_ref, r_ref, x_ref, g_ref, w_ref, o_ref, h_scr):
    @pl.when(pl.program_id(1) == 0)
    def _():
        a = a_ref[...]
        h_scr[:, :MLA_WIDTH] = (a * _rms_scale(a) * g_ref[...]).astype(BF16)
        h_scr[:, MLA_WIDTH:] = r_ref[...].astype(BF16)

    o_ref[...] = x_ref[...] + _dot(h_scr[...], w_ref[...])


def _out_proj(a, r, x, gain, w):
    t = x.shape[0]
    return pl.pallas_call(
        _out_proj_kernel,
        out_shape=jax.ShapeDtypeStruct((t, D_MODEL), F32),
        grid=(t // OUT_TM, D_MODEL // OUT_TN),
        in_specs=[
            pl.BlockSpec((OUT_TM, MLA_WIDTH), lambda i, j: (i, 0)),
            pl.BlockSpec((OUT_TM, HG_WIDTH), lambda i, j: (i, 0)),
            pl.BlockSpec((OUT_TM, OUT_TN), lambda i, j: (i, j)),
            pl.BlockSpec((1, MLA_WIDTH), lambda i, j: (0, 0)),
            pl.BlockSpec((MLA_WIDTH + HG_WIDTH, OUT_TN), lambda i, j: (0, j)),
        ],
        out_specs=pl.BlockSpec((OUT_TM, OUT_TN), lambda i, j: (i, j)),
        scratch_shapes=[pltpu.VMEM((OUT_TM, MLA_WIDTH + HG_WIDTH), BF16)],
        compiler_params=_params("parallel", "arbitrary"),
        name="out_proj",
    )(a, r, x, gain, w)


def _mem_kv_kernel(m_ref, g_ref, w_ref, kn_ref, k_ref, v_ref):
    m = m_ref[...]
    mn = (m * _rms_scale(m) * g_ref[...]).astype(BF16)
    kv = _dot(mn, w_ref[...])
    for h in range(MEM_HEADS):
        k = kv[:, 2 * h * MEM_HD:(2 * h + 1) * MEM_HD]
        k_ref[:, h * MEM_HD:(h + 1) * MEM_HD] = (k * _rms_scale(k) * kn_ref[...]).astype(BF16)
        v_ref[:, h * MEM_HD:(h + 1) * MEM_HD] = kv[:, (2 * h + 1) * MEM_HD:(2 * h + 2) * MEM_HD].astype(BF16)


def _mem_kv(mem, gain, w, k_norm, batch, mem_len):
    return pl.pallas_call(
        _mem_kv_kernel,
        out_shape=(jax.ShapeDtypeStruct((batch * mem_len, MEM_WIDTH), BF16),
                   jax.ShapeDtypeStruct((batch * mem_len, MEM_WIDTH), BF16)),
        grid=(batch,),
        in_specs=[
            pl.BlockSpec((mem_len, D_MODEL), lambda b: (b, 0)),
            pl.BlockSpec((1, D_MODEL), lambda b: (0, 0)),
            pl.BlockSpec(w.shape, lambda b: (0, 0)),
            pl.BlockSpec((1, MEM_HD), lambda b: (0, 0)),
        ],
        out_specs=(pl.BlockSpec((mem_len, MEM_WIDTH), lambda b: (b, 0)),
                   pl.BlockSpec((mem_len, MEM_WIDTH), lambda b: (b, 0))),
        compiler_params=_params("parallel"),
        name="mem_kv",
    )(mem, gain, w, k_norm)


XA_TM = 512


def _xattn_kernel(x_ref, g_ref, wq_ref, qn_ref, k_ref, v_ref, wo_ref, o_ref, att_scr):
    x = x_ref[...]
    h = (x * _rms_scale(x) * g_ref[...]).astype(BF16)
    q = _dot(h, wq_ref[...])
    scale = MEM_HD ** -0.5
    for hd in range(MEM_HEADS):
        cols = slice(hd * MEM_HD, (hd + 1) * MEM_HD)
        qh = q[:, cols]
        qh = (qh * _rms_scale(qh) * qn_ref[...]).astype(BF16)
        s = _dot_nt(qh, k_ref[:, cols]) * scale
        p = jnp.exp(s - jnp.max(s, axis=-1, keepdims=True))
        p = p / jnp.sum(p, axis=-1, keepdims=True)
        att_scr[:, cols] = _dot(p.astype(BF16), v_ref[:, cols]).astype(BF16)
    o_ref[...] = x + _dot(att_scr[...], wo_ref[...])


def _xattn(x, gain, wq, q_norm, kx, vx, wo, seq, mem_len):
    t = x.shape[0]
    per_batch = seq // XA_TM
    return pl.pallas_call(
        _xattn_kernel,
        out_shape=jax.ShapeDtypeStruct((t, D_MODEL), F32),
        grid=(t // XA_TM,),
        in_specs=[
            pl.BlockSpec((XA_TM, D_MODEL), lambda i: (i, 0)),
            pl.BlockSpec((1, D_MODEL), lambda i: (0, 0)),
            pl.BlockSpec(wq.shape, lambda i: (0, 0)),
            pl.BlockSpec((1, MEM_HD), lambda i: (0, 0)),
            pl.BlockSpec((mem_len, MEM_WIDTH), lambda i: (i // per_batch, 0)),
            pl.BlockSpec((mem_len, MEM_WIDTH), lambda i: (i // per_batch, 0)),
            pl.BlockSpec(wo.shape, lambda i: (0, 0)),
        ],
        out_specs=pl.BlockSpec((XA_TM, D_MODEL), lambda i: (i, 0)),
        scratch_shapes=[pltpu.VMEM((XA_TM, MEM_WIDTH), BF16)],
        compiler_params=_params("parallel"),
        name="xattn",
    )(x, gain, wq, q_norm, kx, vx, wo)


def _prep_ffn(wg, wu, wd):
    pad = D_FF_PAD - D_FF
    return (jnp.pad(wg.astype(BF16), ((0, 0), (0, pad))),
            jnp.pad(wu.astype(BF16), ((0, 0), (0, pad))),
            jnp.pad(wd.astype(BF16), ((0, pad), (0, 0))))


def _pe_pair(x1, x2):
    z = jnp.zeros(x1.shape[:-1] + (LANES - MLA_ROPE,), x1.dtype)
    return jnp.concatenate([x1, x2, z], axis=-1), jnp.concatenate([x2, x1, z], axis=-1)


def _prep_w_in(w_in):
    w = w_in.astype(BF16)
    half = MLA_ROPE // 2
    kpe0 = MLA_Q_RANK + MLA_KV_RANK
    hg0 = kpe0 + MLA_ROPE
    pe, pe_sw = _pe_pair(w[:, kpe0:kpe0 + half], w[:, kpe0 + half:hg0])
    return jnp.concatenate([w[:, :kpe0], w[:, hg0:], pe, pe_sw], axis=1)


def _prep_w_q_up(w):
    w = w.astype(BF16).reshape(MLA_Q_RANK, MLA_HEADS, MLA_QK)
    half = MLA_ROPE // 2
    pe, pe_sw = _pe_pair(w[..., MLA_NOPE:MLA_NOPE + half], w[..., MLA_NOPE + half:])
    return jnp.concatenate([w[..., :MLA_NOPE], pe, pe_sw], axis=-1).reshape(MLA_Q_RANK, MLA_HEADS * Q_UP_PER_HEAD)


def _pe_gains(norm):
    half = MLA_ROPE // 2
    g1 = norm[MLA_NOPE:MLA_NOPE + half]
    g2 = norm[MLA_NOPE + half:]
    ga, gb = _pe_pair(g1, g2)
    sign = jnp.concatenate([-jnp.ones((half,), F32), jnp.ones((LANES - half,), F32)])
    return ga[None, :], (gb * sign)[None, :]


def kernel(x, mem, positions, ffn1_norm, ffn1_w_gate, ffn1_w_up, ffn1_w_down, mix_norm, w_in, mla_q_a_norm, mla_w_q_up, mla_kv_a_norm, mla_w_kv_up, mla_q_norm, mla_k_norm, mla_out_norm, hg_lb_logits, hg_out_norm, w_out, xattn_norm, mem_norm, xattn_w_q, xattn_w_kv, xattn_q_norm, xattn_k_norm, xattn_w_o, ffn2_norm, ffn2_w_gate, ffn2_w_up, ffn2_w_down):
    batch, seq, _ = x.shape
    mem_len = mem.shape[1]
    depth = ffn1_norm.shape[0]
    assert depth == 1 and seq % ATT_T == 0 and seq % XA_TM == 0
    t = batch * seq
    xt = x.reshape(t, D_MODEL)
    pos = positions.reshape(t, 1)
    half = MLA_ROPE // 2
    inv_freq = ROPE_BASE ** (-np.arange(half, dtype=np.float32) / half)
    invf = jnp.asarray(np.concatenate([inv_freq, inv_freq, np.zeros(LANES - MLA_ROPE, np.float32)])[None, :])
    l = 0

    xt = _ffn(xt, ffn1_norm[l][None, :], *_prep_ffn(ffn1_w_gate[l], ffn1_w_up[l], ffn1_w_down[l]))

    proj = _in_proj(xt, mix_norm[l][None, :], _prep_w_in(w_in[l]))
    qpa, qpb = _pe_gains(mla_q_norm[l])
    kpa, kpb = _pe_gains(mla_k_norm[l])
    q, k, v = _mla_prep(proj, pos, invf, mla_q_a_norm[l][None, :], mla_kv_a_norm[l][None, :],
                        _prep_w_q_up(mla_w_q_up[l]), mla_w_kv_up[l].astype(BF16),
                        mla_q_norm[l][None, :MLA_NOPE], qpa, qpb,
                        mla_k_norm[l][None, :MLA_NOPE], kpa, kpb)
    a = _mla_attn(q, k, v, batch, seq)
    r = _hgrn(proj, hg_lb_logits, hg_out_norm[l][None, :], batch, seq)
    xt = _out_proj(a, r, xt, mla_out_norm[l][None, :], w_out[l].astype(BF16))

    kx, vx = _mem_kv(mem.reshape(batch * mem_len, D_MODEL), mem_norm[l][None, :],
                     xattn_w_kv[l].astype(BF16), xattn_k_norm[l][None, :], batch, mem_len)
    xt = _xattn(xt, xattn_norm[l][None, :], xattn_w_q[l].astype(BF16), xattn_q_norm[l][None, :],
                kx, vx, xattn_w_o[l].astype(BF16), seq, mem_len)

    xt = _ffn(xt, ffn2_norm[l][None, :], *_prep_ffn(ffn2_w_gate[l], ffn2_w_up[l], ffn2_w_down[l]))
    return xt.reshape(batch, seq, D_MODEL)
```

```python
import functools

import numpy as np
import jax
import jax.numpy as jnp
from jax import lax
from jax.experimental import pallas as pl
from jax.experimental.pallas import tpu as pltpu

F32 = jnp.float32
BF16 = jnp.bfloat16

EPS = 1e-6
ROPE_BASE = 10000.0
LANES = 128
SUBLANES = 8

D_MODEL = 2048
D_FF = 5504
D_FF_PAD = 5632
FFN_TM = 512
FFN_TF = 512

MLA_HEADS = 8
MLA_NOPE = 128
MLA_ROPE = 64
MLA_QK = MLA_NOPE + MLA_ROPE
MLA_V = 128
MLA_Q_RANK = 512
MLA_KV_RANK = 256
MLA_WIDTH = MLA_HEADS * MLA_V
QK_PAD = 256
Q_UP_PER_HEAD = 384

HG_HEADS = 8
HG_D = 128
HG_CHUNK = 64
HG_LEVELS = (32, 16, 8, 4, 2, 1)
HG_UNROLL = 4
HG_GROUP = HG_CHUNK * HG_UNROLL
HG_WIDTH = HG_HEADS * HG_D

MEM_HEADS = 4
MEM_HD = 128
MEM_WIDTH = MEM_HEADS * MEM_HD

COL_CQ = 0
COL_CKV = 512
COL_HQ = 768
COL_HF = COL_HQ + HG_WIDTH
COL_HI = COL_HF + HG_WIDTH
COL_HG = COL_HI + HG_WIDTH
COL_KPE = COL_HG + HG_WIDTH
COL_KPE_SW = COL_KPE + LANES
IN_COLS_PAD = COL_KPE_SW + LANES

VMEM_LIMIT = 56 * 1024 * 1024


def _params(*sem):
    return pltpu.CompilerParams(dimension_semantics=sem, vmem_limit_bytes=VMEM_LIMIT)


def _rms_scale(x):
    return lax.rsqrt(jnp.mean(x * x, axis=-1, keepdims=True) + EPS)


def _sigmoid(x):
    return 1.0 / (1.0 + jnp.exp(-x))


def _dot(a, b):
    return jnp.dot(a, b, preferred_element_type=F32)


def _dot_nt(a, b):
    return lax.dot_general(a, b, (((1,), (1,)), ((), ())), preferred_element_type=F32)


def _dot_tn(a, b):
    return lax.dot_general(a, b, (((0,), (0,)), ((), ())), preferred_element_type=F32)


def _ffn_kernel(x_ref, g_ref, wg_ref, wu_ref, wd_ref, o_ref, h_scr):
    @pl.when(pl.program_id(1) == 0)
    def _():
        x = x_ref[...]
        h_scr[...] = (x * _rms_scale(x) * g_ref[...]).astype(BF16)
        o_ref[...] = x

    h = h_scr[...]
    gate = _dot(h, wg_ref[...])
    up = _dot(h, wu_ref[...])
    a = (0.5 * gate * _sigmoid(gate) * up).astype(BF16)
    o_ref[...] += _dot(a, wd_ref[...])


def _ffn(x, gain, wg, wu, wd):
    t = x.shape[0]
    return pl.pallas_call(
        _ffn_kernel,
        out_shape=jax.ShapeDtypeStruct((t, D_MODEL), F32),
        grid=(t // FFN_TM, D_FF_PAD // FFN_TF),
        in_specs=[
            pl.BlockSpec((FFN_TM, D_MODEL), lambda i, f: (i, 0)),
            pl.BlockSpec((1, D_MODEL), lambda i, f: (0, 0)),
            pl.BlockSpec((D_MODEL, FFN_TF), lambda i, f: (0, f)),
            pl.BlockSpec((D_MODEL, FFN_TF), lambda i, f: (0, f)),
            pl.BlockSpec((FFN_TF, D_MODEL), lambda i, f: (f, 0)),
        ],
        out_specs=pl.BlockSpec((FFN_TM, D_MODEL), lambda i, f: (i, 0)),
        scratch_shapes=[pltpu.VMEM((FFN_TM, D_MODEL), BF16)],
        compiler_params=_params("parallel", "arbitrary"),
        name="ffn",
    )(x, gain, wg, wu, wd)


PROJ_TM = 512
PROJ_TN = 512


def _in_proj_kernel(x_ref, g_ref, w_ref, o_ref, h_scr):
    @pl.when(pl.program_id(1) == 0)
    def _():
        x = x_ref[...]
        h_scr[...] = (x * _rms_scale(x) * g_ref[...]).astype(BF16)

    o_ref[...] = _dot(h_scr[...], w_ref[...])


def _in_proj(x, gain, w):
    t = x.shape[0]
    n = w.shape[1]
    return pl.pallas_call(
        _in_proj_kernel,
        out_shape=jax.ShapeDtypeStruct((t, n), F32),
        grid=(t // PROJ_TM, n // PROJ_TN),
        in_specs=[
            pl.BlockSpec((PROJ_TM, D_MODEL), lambda i, j: (i, 0)),
            pl.BlockSpec((1, D_MODEL), lambda i, j: (0, 0)),
            pl.BlockSpec((D_MODEL, PROJ_TN), lambda i, j: (0, j)),
        ],
        out_specs=pl.BlockSpec((PROJ_TM, PROJ_TN), lambda i, j: (i, j)),
        scratch_shapes=[pltpu.VMEM((PROJ_TM, D_MODEL), BF16)],
        compiler_params=_params("parallel", "arbitrary"),
        name="in_proj",
    )(x, gain, w)


PREP_TM = 512


def _mla_prep_kernel(cq_ref, ckv_ref, kpe_ref, kpesw_ref, pos_ref, invf_ref,
                     qa_ref, kva_ref, wq_ref, wkv_ref,
                     qn_ref, qpa_ref, qpb_ref, kn_ref, kpa_ref, kpb_ref,
                     q_ref, k_ref, v_ref):
    ang = pos_ref[...].astype(F32) * invf_ref[...]
    cos = jnp.cos(ang)
    sin = jnp.sin(ang)

    cq = cq_ref[...]
    cqn = (cq * _rms_scale(cq) * qa_ref[...]).astype(BF16)
    qall = _dot(cqn, wq_ref[...])
    q_cos = qpa_ref[...] * cos
    q_sin = qpb_ref[...] * sin
    for h in range(MLA_HEADS):
        base = h * Q_UP_PER_HEAD
        qn = qall[:, base:base + LANES]
        y = qall[:, base + LANES:base + 2 * LANES]
        ysw = qall[:, base + 2 * LANES:base + 3 * LANES]
        ss = jnp.sum(qn * qn, axis=-1, keepdims=True) + jnp.sum(y * y, axis=-1, keepdims=True)
        r = lax.rsqrt(ss * (1.0 / MLA_QK) + EPS)
        q_ref[:, h * QK_PAD:h * QK_PAD + LANES] = (qn * r * qn_ref[...]).astype(BF16)
        q_ref[:, h * QK_PAD + LANES:(h + 1) * QK_PAD] = ((y * q_cos + ysw * q_sin) * r).astype(BF16)

    ckv = ckv_ref[...]
    ckvn = (ckv * _rms_scale(ckv) * kva_ref[...]).astype(BF16)
    kvall = _dot(ckvn, wkv_ref[...])
    yk = kpe_ref[...]
    kr = yk * (kpa_ref[...] * cos) + kpesw_ref[...] * (kpb_ref[...] * sin)
    ss_pe = jnp.sum(yk * yk, axis=-1, keepdims=True)
    for h in range(MLA_HEADS):
        kn = kvall[:, h * 2 * LANES:h * 2 * LANES + LANES]
        ss = jnp.sum(kn * kn, axis=-1, keepdims=True) + ss_pe
        r = lax.rsqrt(ss * (1.0 / MLA_QK) + EPS)
        k_ref[:, h * QK_PAD:h * QK_PAD + LANES] = (kn * r * kn_ref[...]).astype(BF16)
        k_ref[:, h * QK_PAD + LANES:(h + 1) * QK_PAD] = (kr * r).astype(BF16)
        v_ref[:, h * MLA_V:(h + 1) * MLA_V] = kvall[:, h * 2 * LANES + LANES:(h + 1) * 2 * LANES].astype(BF16)


def _mla_prep(proj, pos, invf, qa, kva, wq, wkv, qn, qpa, qpb, kn, kpa, kpb):
    t = proj.shape[0]
    tm = PREP_TM
    row = lambda i: (0, 0)
    vec = lambda n: pl.BlockSpec((1, n), row)
    return pl.pallas_call(
        _mla_prep_kernel,
        out_shape=(jax.ShapeDtypeStruct((t, MLA_HEADS * QK_PAD), BF16),
                   jax.ShapeDtypeStruct((t, MLA_HEADS * QK_PAD), BF16),
                   jax.ShapeDtypeStruct((t, MLA_WIDTH), BF16)),
        grid=(t // tm,),
        in_specs=[
            pl.BlockSpec((tm, MLA_Q_RANK), lambda i: (i, COL_CQ // MLA_Q_RANK)),
            pl.BlockSpec((tm, MLA_KV_RANK), lambda i: (i, COL_CKV // MLA_KV_RANK)),
            pl.BlockSpec((tm, LANES), lambda i: (i, COL_KPE // LANES)),
            pl.BlockSpec((tm, LANES), lambda i: (i, COL_KPE_SW // LANES)),
            pl.BlockSpec((tm, 1), lambda i: (i, 0)),
            vec(LANES),
            vec(MLA_Q_RANK), vec(MLA_KV_RANK),
            pl.BlockSpec(wq.shape, row), pl.BlockSpec(wkv.shape, row),
            vec(LANES), vec(LANES), vec(LANES), vec(LANES), vec(LANES), vec(LANES),
        ],
        out_specs=(pl.BlockSpec((tm, MLA_HEADS * QK_PAD), lambda i: (i, 0)),
                   pl.BlockSpec((tm, MLA_HEADS * QK_PAD), lambda i: (i, 0)),
                   pl.BlockSpec((tm, MLA_WIDTH), lambda i: (i, 0))),
        compiler_params=_params("parallel"),
        name="mla_prep",
    )(proj, proj, proj, proj, pos, invf, qa, kva, wq, wkv, qn, qpa, qpb, kn, kpa, kpb)


ATT_T = 256


def _mla_attn_kernel(q_ref, k_ref, v_ref, o_ref):
    scale = MLA_QK ** -0.5
    qpos = lax.broadcasted_iota(jnp.int32, (ATT_T, ATT_T), 0)
    kpos = lax.broadcasted_iota(jnp.int32, (ATT_T, ATT_T), 1)
    for i in range(q_ref.shape[0] // ATT_T):
        lo, hi = i * ATT_T, (i + 1) * ATT_T
        q = q_ref[lo:hi, :]
        s_diag = jnp.where(kpos <= qpos, _dot_nt(q, k_ref[lo:hi, :]) * scale, -jnp.inf)
        m = jnp.max(s_diag, axis=-1, keepdims=True)
        if i > 0:
            s_off = _dot_nt(q, k_ref[0:lo, :]) * scale
            m = jnp.maximum(m, jnp.max(s_off, axis=-1, keepdims=True))
        p = jnp.exp(s_diag - m)
        l = jnp.sum(p, axis=-1, keepdims=True)
        acc = _dot(p.astype(BF16), v_ref[lo:hi, :])
        if i > 0:
            p = jnp.exp(s_off - m)
            l = l + jnp.sum(p, axis=-1, keepdims=True)
            acc = acc + _dot(p.astype(BF16), v_ref[0:lo, :])
        o_ref[lo:hi, :] = acc / l


def _mla_attn(q, k, v, batch, seq):
    return pl.pallas_call(
        _mla_attn_kernel,
        out_shape=jax.ShapeDtypeStruct((batch * seq, MLA_WIDTH), F32),
        grid=(batch, MLA_HEADS),
        in_specs=[
            pl.BlockSpec((seq, QK_PAD), lambda b, h: (b, h)),
            pl.BlockSpec((seq, QK_PAD), lambda b, h: (b, h)),
            pl.BlockSpec((seq, MLA_V), lambda b, h: (b, h)),
        ],
        out_specs=pl.BlockSpec((seq, MLA_V), lambda b, h: (b, h)),
        compiler_params=_params("parallel", "parallel"),
        name="mla_attn",
    )(q, k, v)


def _hgrn_masks():
    t = np.arange(HG_CHUNK)[:, None]
    s = np.arange(HG_CHUNK)[None, :]
    masks = [((t // c) % 2 == 1) & ((s // c) == (t // c) - 1) for c in HG_LEVELS]
    masks.append(t == s)
    return np.stack(masks).astype(np.float32)


def _hgrn_kernel(q_ref, f_ref, i_ref, g_ref, lbl_ref, gn_ref, msk_ref, tri_ref, o_ref, b_scr):
    lbl = lbl_ref[...]
    e = jnp.exp(lbl - jnp.max(lbl, axis=0, keepdims=True))
    lb = e[0:1, :] / jnp.sum(e, axis=0, keepdims=True)
    gn = gn_ref[...]
    tri = tri_ref[...]
    sub = lax.broadcasted_iota(jnp.int32, (SUBLANES, LANES), 0)
    zeros8 = jnp.zeros((SUBLANES, LANES), F32)
    per_chunk = HG_CHUNK // SUBLANES
    chunk_rows = [slice(u * HG_CHUNK, (u + 1) * HG_CHUNK) for u in range(HG_UNROLL)]

    def sel(bounds, vals):
        out = vals[-1]
        for bound, val in zip(reversed(bounds), reversed(vals[:-1])):
            out = jnp.where(sub < bound, val, out)
        return out

    def group(gi, st):
        rows = pl.ds(pl.multiple_of(gi * HG_GROUP, HG_GROUP), HG_GROUP)
        q = q_ref[rows, :]
        v = i_ref[rows, :].astype(BF16)
        f = lb + (1.0 - lb) * _sigmoid(f_ref[rows, :])
        logf = jnp.log(f)
        k = 1.0 - f

        l1 = logf.astype(BF16)
        l2 = (logf - l1.astype(F32)).astype(BF16)
        b = _dot(tri, l1) + _dot(tri, l2)
        b_scr[...] = b

        cache = {}

        def rowb(u, r):
            if r < 0:
                return zeros8
            if (u, r) not in cache:
                cache[u, r] = jnp.broadcast_to(b_scr[u * HG_CHUNK + r:u * HG_CHUNK + r + 1, :], (SUBLANES, LANES))
            return cache[u, r]

        def build(fn):
            return jnp.concatenate([fn(u, j) for u in range(HG_UNROLL) for j in range(per_chunk)], axis=0)

        a = [jnp.sum(q[r] * k[r], axis=-1, keepdims=True) * msk_ref[len(HG_LEVELS)] for r in chunk_rows]
        for ci, c in enumerate(HG_LEVELS):
            if c >= SUBLANES:
                m = c // SUBLANES
                bs = build(lambda u, j: rowb(u, SUBLANES * (j // m) * m - 1))
                be = build(lambda u, j: rowb(u, SUBLANES * ((j // m) * m + m) - 1))
            elif c == 4:
                bs = build(lambda u, j: sel((4,), (rowb(u, 8 * j - 1), rowb(u, 8 * j + 3))))
                be = build(lambda u, j: sel((4,), (rowb(u, 8 * j + 3), rowb(u, 8 * j + 7))))
            elif c == 2:
                bs = build(lambda u, j: sel((2, 4, 6), (rowb(u, 8 * j - 1), rowb(u, 8 * j + 1),
                                                         rowb(u, 8 * j + 3), rowb(u, 8 * j + 5))))
                be = build(lambda u, j: sel((2, 4, 6), (rowb(u, 8 * j + 1), rowb(u, 8 * j + 3),
                                                         rowb(u, 8 * j + 5), rowb(u, 8 * j + 7))))
            if c == 1:
                qd = (q * f).astype(BF16)
                kd = k.astype(BF16)
            else:
                qd = (q * jnp.exp(b - bs)).astype(BF16)
                kd = (k * jnp.exp(be - b)).astype(BF16)
            a = [a[u] + _dot_nt(qd[r], kd[r]) * msk_ref[ci] for u, r in enumerate(chunk_rows)]

        q_exp = (q * jnp.exp(b)).astype(BF16)
        b_last = build(lambda u, j: rowb(u, HG_CHUNK - 1))
        k_dec = (k * jnp.exp(b_last - b)).astype(BF16)
        o_intra = [_dot(a[u].astype(BF16), v[r]) for u, r in enumerate(chunk_rows)]
        st_add = [_dot_tn(v[r], k_dec[r]) for r in chunk_rows]

        o = []
        for u, r in enumerate(chunk_rows):
            o.append(o_intra[u] + _dot_nt(q_exp[r], st.astype(BF16)))
            st = jnp.exp(rowb(u, HG_CHUNK - 1)[0:1, :]) * st + st_add[u]
        o = jnp.concatenate(o, axis=0)

        g = g_ref[rows, :]
        o_ref[rows, :] = o * _rms_scale(o) * gn * (g * _sigmoid(g))
        return st

    lax.fori_loop(0, q_ref.shape[0] // HG_GROUP, group, jnp.zeros((HG_D, HG_D), F32))


def _hgrn(proj, lb_logits, out_norm, batch, seq):
    hblk = lambda col: pl.BlockSpec((seq, HG_D), lambda b, h: (b, col // HG_D + h))
    masks = jnp.asarray(_hgrn_masks())
    tri = np.kron(np.eye(HG_UNROLL), np.tril(np.ones((HG_CHUNK, HG_CHUNK))))
    tri = jnp.asarray(tri.astype(np.float32), dtype=BF16)
    nlev = masks.shape[0]
    return pl.pallas_call(
        _hgrn_kernel,
        out_shape=jax.ShapeDtypeStruct((batch * seq, HG_WIDTH), F32),
        grid=(batch, HG_HEADS),
        in_specs=[
            hblk(COL_HQ), hblk(COL_HF), hblk(COL_HI), hblk(COL_HG),
            pl.BlockSpec((lb_logits.shape[0], HG_D), lambda b, h: (0, h)),
            pl.BlockSpec((1, HG_D), lambda b, h: (0, 0)),
            pl.BlockSpec((nlev, HG_CHUNK, HG_CHUNK), lambda b, h: (0, 0, 0)),
            pl.BlockSpec((HG_GROUP, HG_GROUP), lambda b, h: (0, 0)),
        ],
        out_specs=pl.BlockSpec((seq, HG_D), lambda b, h: (b, h)),
        scratch_shapes=[pltpu.VMEM((HG_GROUP, HG_D), F32)],
        compiler_params=_params("parallel", "parallel"),
        name="hgrn2",
    )(proj, proj, proj, proj, lb_logits, out_norm, masks, tri)


OUT_TM = 512
OUT_TN = 512


def _out_proj_kernel(a_ref, r_ref, x_ref, g_ref, w_ref, o_ref, h_scr):
    @pl.when(pl.program_id(1) == 0)
    def _():
        a = a_ref[...]
        h_scr[:, :MLA_WIDTH] = (a * _rms_scale(a) * g_ref[...]).astype(BF16)
        h_scr[:, MLA_WIDTH:] = r_ref[...].astype(BF16)

    o_ref[...] = x_ref[...] + _dot(h_scr[...], w_ref[...])


def _out_proj(a, r, x, gain, w):
    t = x.shape[0]
    return pl.pallas_call(
        _out_proj_kernel,
        out_shape=jax.ShapeDtypeStruct((t, D_MODEL), F32),
        grid=(t // OUT_TM, D_MODEL // OUT_TN),
        in_specs=[
            pl.BlockSpec((OUT_TM, MLA_WIDTH), lambda i, j: (i, 0)),
            pl.BlockSpec((OUT_TM, HG_WIDTH), lambda i, j: (i, 0)),
            pl.BlockSpec((OUT_TM, OUT_TN), lambda i, j: (i, j)),
            pl.BlockSpec((1, MLA_WIDTH), lambda i, j: (0, 0)),
            pl.BlockSpec((MLA_WIDTH + HG_WIDTH, OUT_TN), lambda i, j: (0, j)),
        ],
        out_specs=pl.BlockSpec((OUT_TM, OUT_TN), lambda i, j: (i, j)),
        scratch_shapes=[pltpu.VMEM((OUT_TM, MLA_WIDTH + HG_WIDTH), BF16)],
        compiler_params=_params("parallel", "arbitrary"),
        name="out_proj",
    )(a, r, x, gain, w)


def _mem_kv_kernel(m_ref, g_ref, w_ref, kn_ref, k_ref, v_ref):
    m = m_ref[...]
    mn = (m * _rms_scale(m) * g_ref[...]).astype(BF16)
    kv = _dot(mn, w_ref[...])
    for h in range(MEM_HEADS):
        k = kv[:, 2 * h * MEM_HD:(2 * h + 1) * MEM_HD]
        k_ref[:, h * MEM_HD:(h + 1) * MEM_HD] = (k * _rms_scale(k) * kn_ref[...]).astype(BF16)
        v_ref[:, h * MEM_HD:(h + 1) * MEM_HD] = kv[:, (2 * h + 1) * MEM_HD:(2 * h + 2) * MEM_HD].astype(BF16)


def _mem_kv(mem, gain, w, k_norm, batch, mem_len):
    return pl.pallas_call(
        _mem_kv_kernel,
        out_shape=(jax.ShapeDtypeStruct((batch * mem_len, MEM_WIDTH), BF16),
                   jax.ShapeDtypeStruct((batch * mem_len, MEM_WIDTH), BF16)),
        grid=(batch,),
        in_specs=[
            pl.BlockSpec((mem_len, D_MODEL), lambda b: (b, 0)),
            pl.BlockSpec((1, D_MODEL), lambda b: (0, 0)),
            pl.BlockSpec(w.shape, lambda b: (0, 0)),
            pl.BlockSpec((1, MEM_HD), lambda b: (0, 0)),
        ],
        out_specs=(pl.BlockSpec((mem_len, MEM_WIDTH), lambda b: (b, 0)),
                   pl.BlockSpec((mem_len, MEM_WIDTH), lambda b: (b, 0))),
        compiler_params=_params("parallel"),
        name="mem_kv",
    )(mem, gain, w, k_norm)


XA_TM = 512


def _xattn_kernel(x_ref, g_ref, wq_ref, qn_ref, k_ref, v_ref, wo_ref, o_ref, att_scr):
    x = x_ref[...]
    h = (x * _rms_scale(x) * g_ref[...]).astype(BF16)
    q = _dot(h, wq_ref[...])
    scale = MEM_HD ** -0.5
    for hd in range(MEM_HEADS):
        cols = slice(hd * MEM_HD, (hd + 1) * MEM_HD)
        qh = q[:, cols]
        qh = (qh * _rms_scale(qh) * qn_ref[...]).astype(BF16)
        s = _dot_nt(qh, k_ref[:, cols]) * scale
        p = jnp.exp(s - jnp.max(s, axis=-1, keepdims=True))
        p = p / jnp.sum(p, axis=-1, keepdims=True)
        att_scr[:, cols] = _dot(p.astype(BF16), v_ref[:, cols]).astype(BF16)
    o_ref[...] = x + _dot(att_scr[...], wo_ref[...])


def _xattn(x, gain, wq, q_norm, kx, vx, wo, seq, mem_len):
    t = x.shape[0]
    per_batch = seq // XA_TM
    return pl.pallas_call(
        _xattn_kernel,
        out_shape=jax.ShapeDtypeStruct((t, D_MODEL), F32),
        grid=(t // XA_TM,),
        in_specs=[
            pl.BlockSpec((XA_TM, D_MODEL), lambda i: (i, 0)),
            pl.BlockSpec((1, D_MODEL), lambda i: (0, 0)),
            pl.BlockSpec(wq.shape, lambda i: (0, 0)),
            pl.BlockSpec((1, MEM_HD), lambda i: (0, 0)),
            pl.BlockSpec((mem_len, MEM_WIDTH), lambda i: (i // per_batch, 0)),
            pl.BlockSpec((mem_len, MEM_WIDTH), lambda i: (i // per_batch, 0)),
            pl.BlockSpec(wo.shape, lambda i: (0, 0)),
        ],
        out_specs=pl.BlockSpec((XA_TM, D_MODEL), lambda i: (i, 0)),
        scratch_shapes=[pltpu.VMEM((XA_TM, MEM_WIDTH), BF16)],
        compiler_params=_params("parallel"),
        name="xattn",
    )(x, gain, wq, q_norm, kx, vx, wo)


def _prep_ffn(wg, wu, wd):
    pad = D_FF_PAD - D_FF
    return (jnp.pad(wg.astype(BF16), ((0, 0), (0, pad))),
            jnp.pad(wu.astype(BF16), ((0, 0), (0, pad))),
            jnp.pad(wd.astype(BF16), ((0, pad), (0, 0))))


def _pe_pair(x1, x2):
    z = jnp.zeros(x1.shape[:-1] + (LANES - MLA_ROPE,), x1.dtype)
    return jnp.concatenate([x1, x2, z], axis=-1), jnp.concatenate([x2, x1, z], axis=-1)


def _prep_w_in(w_in):
    w = w_in.astype(BF16)
    half = MLA_ROPE // 2
    kpe0 = MLA_Q_RANK + MLA_KV_RANK
    hg0 = kpe0 + MLA_ROPE
    pe, pe_sw = _pe_pair(w[:, kpe0:kpe0 + half], w[:, kpe0 + half:hg0])
    return jnp.concatenate([w[:, :kpe0], w[:, hg0:], pe, pe_sw], axis=1)


def _prep_w_q_up(w):
    w = w.astype(BF16).reshape(MLA_Q_RANK, MLA_HEADS, MLA_QK)
    half = MLA_ROPE // 2
    pe, pe_sw = _pe_pair(w[..., MLA_NOPE:MLA_NOPE + half], w[..., MLA_NOPE + half:])
    return jnp.concatenate([w[..., :MLA_NOPE], pe, pe_sw], axis=-1).reshape(MLA_Q_RANK, MLA_HEADS * Q_UP_PER_HEAD)


def _pe_gains(norm):
    half = MLA_ROPE // 2
    g1 = norm[MLA_NOPE:MLA_NOPE + half]
    g2 = norm[MLA_NOPE + half:]
    ga, gb = _pe_pair(g1, g2)
    sign = jnp.concatenate([-jnp.ones((half,), F32), jnp.ones((LANES - half,), F32)])
    return ga[None, :], (gb * sign)[None, :]


def kernel(x, mem, positions, ffn1_norm, ffn1_w_gate, ffn1_w_up, ffn1_w_down, mix_norm, w_in, mla_q_a_norm, mla_w_q_up, mla_kv_a_norm, mla_w_kv_up, mla_q_norm, mla_k_norm, mla_out_norm, hg_lb_logits, hg_out_norm, w_out, xattn_norm, mem_norm, xattn_w_q, xattn_w_kv, xattn_q_norm, xattn_k_norm, xattn_w_o, ffn2_norm, ffn2_w_gate, ffn2_w_up, ffn2_w_down):
    batch, seq, _ = x.shape
    mem_len = mem.shape[1]
    depth = ffn1_norm.shape[0]
    assert depth == 1 and seq % ATT_T == 0 and seq % XA_TM == 0
    t = batch * seq
    xt = x.reshape(t, D_MODEL)
    pos = positions.reshape(t, 1)
    half = MLA_ROPE // 2
    inv_freq = ROPE_BASE ** (-np.arange(half, dtype=np.float32) / half)
    invf = jnp.asarray(np.concatenate([inv_freq, inv_freq, np.zeros(LANES - MLA_ROPE, np.float32)])[None, :])
    l = 0

    xt = _ffn(xt, ffn1_norm[l][None, :], *_prep_ffn(ffn1_w_gate[l], ffn1_w_up[l], ffn1_w_down[l]))

    proj = _in_proj(xt, mix_norm[l][None, :], _prep_w_in(w_in[l]))
    qpa, qpb = _pe_gains(mla_q_norm[l])
    kpa, kpb = _pe_gains(mla_k_norm[l])
    q, k, v = _mla_prep(proj, pos, invf, mla_q_a_norm[l][None, :], mla_kv_a_norm[l][None, :],
                        _prep_w_q_up(mla_w_q_up[l]), mla_w_kv_up[l].astype(BF16),
                        mla_q_norm[l][None, :MLA_NOPE], qpa, qpb,
                        mla_k_norm[l][None, :MLA_NOPE], kpa, kpb)
    a = _mla_attn(q, k, v, batch, seq)
    r = _hgrn(proj, hg_lb_logits, hg_out_norm[l][None, :], batch, seq)
    xt = _out_proj(a, r, xt, mla_out_norm[l][None, :], w_out[l].astype(BF16))

    kx, vx = _mem_kv(mem.reshape(batch * mem_len, D_MODEL), mem_norm[l][None, :],
                     xattn_w_kv[l].astype(BF16), xattn_k_norm[l][None, :], batch, mem_len)
    xt = _xattn(xt, xattn_norm[l][None, :], xattn_w_q[l].astype(BF16), xattn_q_norm[l][None, :],
                kx, vx, xattn_w_o[l].astype(BF16), seq, mem_len)

    xt = _ffn(xt, ffn2_norm[l][None, :], *_prep_ffn(ffn2_w_gate[l], ffn2_w_up[l], ffn2_w_down[l]))
    return xt.reshape(batch, seq, D_MODEL)
```

```python
import functools

import numpy as np
import jax
import jax.numpy as jnp
from jax import lax
from jax.experimental import pallas as pl
from jax.experimental.pallas import tpu as pltpu

F32 = jnp.float32
BF16 = jnp.bfloat16

EPS = 1e-6
ROPE_BASE = 10000.0
LANES = 128
SUBLANES = 8

D_MODEL = 2048
D_FF = 5504
FFN_TM = 1024
FFN_TF = 256

MLA_HEADS = 8
MLA_NOPE = 128
MLA_ROPE = 64
MLA_QK = MLA_NOPE + MLA_ROPE
MLA_V = 128
MLA_Q_RANK = 512
MLA_KV_RANK = 256
MLA_WIDTH = MLA_HEADS * MLA_V
QK_PAD = 256
Q_UP_PER_HEAD = 384

HG_HEADS = 8
HG_D = 128
HG_CHUNK = 64
HG_LEVELS = (32, 16, 8, 4, 2, 1)
HG_UNROLL = 4
HG_GROUP = HG_CHUNK * HG_UNROLL
HG_WIDTH = HG_HEADS * HG_D

MEM_HEADS = 4
MEM_HD = 128
MEM_WIDTH = MEM_HEADS * MEM_HD

COL_CQ = 0
COL_CKV = 512
COL_HQ = 768
COL_HF = COL_HQ + HG_WIDTH
COL_HI = COL_HF + HG_WIDTH
COL_HG = COL_HI + HG_WIDTH
COL_KPE = COL_HG + HG_WIDTH
COL_KPE_SW = COL_KPE + LANES
IN_COLS_PAD = COL_KPE_SW + LANES

VMEM_LIMIT = 56 * 1024 * 1024


def _params(*sem):
    return pltpu.CompilerParams(dimension_semantics=sem, vmem_limit_bytes=VMEM_LIMIT)


def _rms_scale(x):
    return lax.rsqrt(jnp.mean(x * x, axis=-1, keepdims=True) + EPS)


def _sigmoid(x):
    return 1.0 / (1.0 + jnp.exp(-x))


def _dot(a, b):
    return jnp.dot(a, b, preferred_element_type=F32)


def _dot_nt(a, b):
    return lax.dot_general(a, b, (((1,), (1,)), ((), ())), preferred_element_type=F32)


def _dot_tn(a, b):
    return lax.dot_general(a, b, (((0,), (0,)), ((), ())), preferred_element_type=F32)


def _ffn_kernel(x_ref, g_ref, wg_ref, wu_ref, wd_ref, o_ref, h_scr):
    @pl.when(pl.program_id(1) == 0)
    def _():
        x = x_ref[...]
        h_scr[...] = (x * _rms_scale(x) * g_ref[...]).astype(BF16)
        o_ref[...] = x

    h = h_scr[...]
    gate = _dot(h, wg_ref[...].astype(BF16))
    up = _dot(h, wu_ref[...].astype(BF16))
    a = 0.5 * gate * _sigmoid(gate) * up
    f0 = pl.program_id(1) * FFN_TF
    a_ok = f0 + lax.broadcasted_iota(jnp.int32, (1, FFN_TF), 1) < D_FF
    w_ok = f0 + lax.broadcasted_iota(jnp.int32, (FFN_TF, 1), 0) < D_FF
    a = jnp.where(a_ok, a, 0.0).astype(BF16)
    wd = jnp.where(w_ok, wd_ref[...], 0.0).astype(BF16)
    o_ref[...] += _dot(a, wd)


def _ffn(x, gain, wg, wu, wd):
    t = x.shape[0]
    return pl.pallas_call(
        _ffn_kernel,
        out_shape=jax.ShapeDtypeStruct((t, D_MODEL), F32),
        grid=(t // FFN_TM, pl.cdiv(D_FF, FFN_TF)),
        in_specs=[
            pl.BlockSpec((FFN_TM, D_MODEL), lambda i, f: (i, 0)),
            pl.BlockSpec((1, D_MODEL), lambda i, f: (0, 0)),
            pl.BlockSpec((D_MODEL, FFN_TF), lambda i, f: (0, f)),
            pl.BlockSpec((D_MODEL, FFN_TF), lambda i, f: (0, f)),
            pl.BlockSpec((FFN_TF, D_MODEL), lambda i, f: (f, 0)),
        ],
        out_specs=pl.BlockSpec((FFN_TM, D_MODEL), lambda i, f: (i, 0)),
        scratch_shapes=[pltpu.VMEM((FFN_TM, D_MODEL), BF16)],
        compiler_params=_params("parallel", "arbitrary"),
        name="ffn",
    )(x, gain, wg, wu, wd)


PROJ_TM = 1024
PROJ_TN = 512


def _in_proj_kernel(x_ref, g_ref, w_ref, o_ref, h_scr):
    @pl.when(pl.program_id(1) == 0)
    def _():
        x = x_ref[...]
        h_scr[...] = (x * _rms_scale(x) * g_ref[...]).astype(BF16)

    o_ref[...] = _dot(h_scr[...], w_ref[...])


def _in_proj(x, gain, w):
    t = x.shape[0]
    n = w.shape[1]
    return pl.pallas_call(
        _in_proj_kernel,
        out_shape=jax.ShapeDtypeStruct((t, n), F32),
        grid=(t // PROJ_TM, n // PROJ_TN),
        in_specs=[
            pl.BlockSpec((PROJ_TM, D_MODEL), lambda i, j: (i, 0)),
            pl.BlockSpec((1, D_MODEL), lambda i, j: (0, 0)),
            pl.BlockSpec((D_MODEL, PROJ_TN), lambda i, j: (0, j)),
        ],
        out_specs=pl.BlockSpec((PROJ_TM, PROJ_TN), lambda i, j: (i, j)),
        scratch_shapes=[pltpu.VMEM((PROJ_TM, D_MODEL), BF16)],
        compiler_params=_params("parallel", "arbitrary"),
        name="in_proj",
    )(x, gain, w)


PREP_TM = 512


def _mla_prep_kernel(cq_ref, ckv_ref, kpe_ref, kpesw_ref, pos_ref, invf_ref,
                     qa_ref, kva_ref, wq_ref, wkv_ref,
                     qn_ref, qpa_ref, qpb_ref, kn_ref, kpa_ref, kpb_ref,
                     q_ref, k_ref, v_ref):
    ang = pos_ref[...].astype(F32) * invf_ref[...]
    cos = jnp.cos(ang)
    sin = jnp.sin(ang)

    cq = cq_ref[...]
    cqn = (cq * _rms_scale(cq) * qa_ref[...]).astype(BF16)
    qall = _dot(cqn, wq_ref[...])
    q_cos = qpa_ref[...] * cos
    q_sin = qpb_ref[...] * sin
    for h in range(MLA_HEADS):
        base = h * Q_UP_PER_HEAD
        qn = qall[:, base:base + LANES]
        y = qall[:, base + LANES:base + 2 * LANES]
        ysw = qall[:, base + 2 * LANES:base + 3 * LANES]
        ss = jnp.sum(qn * qn, axis=-1, keepdims=True) + jnp.sum(y * y, axis=-1, keepdims=True)
        r = lax.rsqrt(ss * (1.0 / MLA_QK) + EPS)
        q_ref[:, h * QK_PAD:h * QK_PAD + LANES] = (qn * r * qn_ref[...]).astype(BF16)
        q_ref[:, h * QK_PAD + LANES:(h + 1) * QK_PAD] = ((y * q_cos + ysw * q_sin) * r).astype(BF16)

    ckv = ckv_ref[...]
    ckvn = (ckv * _rms_scale(ckv) * kva_ref[...]).astype(BF16)
    kvall = _dot(ckvn, wkv_ref[...])
    yk = kpe_ref[...]
    kr = yk * (kpa_ref[...] * cos) + kpesw_ref[...] * (kpb_ref[...] * sin)
    ss_pe = jnp.sum(yk * yk, axis=-1, keepdims=True)
    for h in range(MLA_HEADS):
        kn = kvall[:, h * 2 * LANES:h * 2 * LANES + LANES]
        ss = jnp.sum(kn * kn, axis=-1, keepdims=True) + ss_pe
        r = lax.rsqrt(ss * (1.0 / MLA_QK) + EPS)
        k_ref[:, h * QK_PAD:h * QK_PAD + LANES] = (kn * r * kn_ref[...]).astype(BF16)
        k_ref[:, h * QK_PAD + LANES:(h + 1) * QK_PAD] = (kr * r).astype(BF16)
        v_ref[:, h * MLA_V:(h + 1) * MLA_V] = kvall[:, h * 2 * LANES + LANES:(h + 1) * 2 * LANES].astype(BF16)


def _mla_prep(proj, pos, invf, qa, kva, wq, wkv, qn, qpa, qpb, kn, kpa, kpb):
    t = proj.shape[0]
    tm = PREP_TM
    row = lambda i: (0, 0)
    vec = lambda n: pl.BlockSpec((1, n), row)
    return pl.pallas_call(
        _mla_prep_kernel,
        out_shape=(jax.ShapeDtypeStruct((t, MLA_HEADS * QK_PAD), BF16),
                   jax.ShapeDtypeStruct((t, MLA_HEADS * QK_PAD), BF16),
                   jax.ShapeDtypeStruct((t, MLA_WIDTH), BF16)),
        grid=(t // tm,),
        in_specs=[
            pl.BlockSpec((tm, MLA_Q_RANK), lambda i: (i, COL_CQ // MLA_Q_RANK)),
            pl.BlockSpec((tm, MLA_KV_RANK), lambda i: (i, COL_CKV // MLA_KV_RANK)),
            pl.BlockSpec((tm, LANES), lambda i: (i, COL_KPE // LANES)),
            pl.BlockSpec((tm, LANES), lambda i: (i, COL_KPE_SW // LANES)),
            pl.BlockSpec((tm, 1), lambda i: (i, 0)),
            vec(LANES),
            vec(MLA_Q_RANK), vec(MLA_KV_RANK),
            pl.BlockSpec(wq.shape, row), pl.BlockSpec(wkv.shape, row),
            vec(LANES), vec(LANES), vec(LANES), vec(LANES), vec(LANES), vec(LANES),
        ],
        out_specs=(pl.BlockSpec((tm, MLA_HEADS * QK_PAD), lambda i: (i, 0)),
                   pl.BlockSpec((tm, MLA_HEADS * QK_PAD), lambda i: (i, 0)),
                   pl.BlockSpec((tm, MLA_WIDTH), lambda i: (i, 0))),
        compiler_params=_params("parallel"),
        name="mla_prep",
    )(proj, proj, proj, proj, pos, invf, qa, kva, wq, wkv, qn, qpa, qpb, kn, kpa, kpb)


ATT_T = 256


def _mla_attn_kernel(q_ref, k_ref, v_ref, o_ref):
    scale = MLA_QK ** -0.5
    qpos = lax.broadcasted_iota(jnp.int32, (ATT_T, ATT_T), 0)
    kpos = lax.broadcasted_iota(jnp.int32, (ATT_T, ATT_T), 1)
    for i in range(q_ref.shape[0] // ATT_T):
        lo, hi = i * ATT_T, (i + 1) * ATT_T
        q = q_ref[lo:hi, :]
        s_diag = jnp.where(kpos <= qpos, _dot_nt(q, k_ref[lo:hi, :]) * scale, -jnp.inf)
        m = jnp.max(s_diag, axis=-1, keepdims=True)
        if i > 0:
            s_off = _dot_nt(q, k_ref[0:lo, :]) * scale
            m = jnp.maximum(m, jnp.max(s_off, axis=-1, keepdims=True))
        p = jnp.exp(s_diag - m)
        l = jnp.sum(p, axis=-1, keepdims=True)
        acc = _dot(p.astype(BF16), v_ref[lo:hi, :])
        if i > 0:
            p = jnp.exp(s_off - m)
            l = l + jnp.sum(p, axis=-1, keepdims=True)
            acc = acc + _dot(p.astype(BF16), v_ref[0:lo, :])
        o_ref[lo:hi, :] = acc / l


def _mla_attn(q, k, v, batch, seq):
    return pl.pallas_call(
        _mla_attn_kernel,
        out_shape=jax.ShapeDtypeStruct((batch * seq, MLA_WIDTH), F32),
        grid=(batch, MLA_HEADS),
        in_specs=[
            pl.BlockSpec((seq, QK_PAD), lambda b, h: (b, h)),
            pl.BlockSpec((seq, QK_PAD), lambda b, h: (b, h)),
            pl.BlockSpec((seq, MLA_V), lambda b, h: (b, h)),
        ],
        out_specs=pl.BlockSpec((seq, MLA_V), lambda b, h: (b, h)),
        compiler_params=_params("parallel", "parallel"),
        name="mla_attn",
    )(q, k, v)


def _hgrn_masks():
    t = np.arange(HG_CHUNK)[:, None]
    s = np.arange(HG_CHUNK)[None, :]
    masks = [((t // c) % 2 == 1) & ((s // c) == (t // c) - 1) for c in HG_LEVELS]
    masks.append(t == s)
    return np.stack(masks).astype(np.float32)


def _hgrn_kernel(q_ref, f_ref, i_ref, g_ref, lbl_ref, gn_ref, msk_ref, tri_ref, o_ref, b_scr):
    lbl = lbl_ref[...]
    e = jnp.exp(lbl - jnp.max(lbl, axis=0, keepdims=True))
    lb = e[0:1, :] / jnp.sum(e, axis=0, keepdims=True)
    gn = gn_ref[...]
    tri = tri_ref[...]
    sub = lax.broadcasted_iota(jnp.int32, (SUBLANES, LANES), 0)
    zeros8 = jnp.zeros((SUBLANES, LANES), F32)
    per_chunk = HG_CHUNK // SUBLANES
    chunk_rows = [slice(u * HG_CHUNK, (u + 1) * HG_CHUNK) for u in range(HG_UNROLL)]

    def sel(bounds, vals):
        out = vals[-1]
        for bound, val in zip(reversed(bounds), reversed(vals[:-1])):
            out = jnp.where(sub < bound, val, out)
        return out

    def group(gi, st):
        rows = pl.ds(pl.multiple_of(gi * HG_GROUP, HG_GROUP), HG_GROUP)
        q = q_ref[rows, :]
        v = i_ref[rows, :].astype(BF16)
        f = lb + (1.0 - lb) * _sigmoid(f_ref[rows, :])
        logf = jnp.log(f)
        k = 1.0 - f

        l1 = logf.astype(BF16)
        l2 = (logf - l1.astype(F32)).astype(BF16)
        b = _dot(tri, l1) + _dot(tri, l2)
        b_scr[...] = b

        cache = {}

        def rowb(u, r):
            if r < 0:
                return zeros8
            if (u, r) not in cache:
                cache[u, r] = jnp.broadcast_to(b_scr[u * HG_CHUNK + r:u * HG_CHUNK + r + 1, :], (SUBLANES, LANES))
            return cache[u, r]

        def build(fn):
            return jnp.concatenate([fn(u, j) for u in range(HG_UNROLL) for j in range(per_chunk)], axis=0)

        a = [jnp.sum(q[r] * k[r], axis=-1, keepdims=True) * msk_ref[len(HG_LEVELS)] for r in chunk_rows]
        for ci, c in enumerate(HG_LEVELS):
            if c >= SUBLANES:
                m = c // SUBLANES
                bs = build(lambda u, j: rowb(u, SUBLANES * (j // m) * m - 1))
                be = build(lambda u, j: rowb(u, SUBLANES * ((j // m) * m + m) - 1))
            elif c == 4:
                bs = build(lambda u, j: sel((4,), (rowb(u, 8 * j - 1), rowb(u, 8 * j + 3))))
                be = build(lambda u, j: sel((4,), (rowb(u, 8 * j + 3), rowb(u, 8 * j + 7))))
            elif c == 2:
                bs = build(lambda u, j: sel((2, 4, 6), (rowb(u, 8 * j - 1), rowb(u, 8 * j + 1),
                                                         rowb(u, 8 * j + 3), rowb(u, 8 * j + 5))))
                be = build(lambda u, j: sel((2, 4, 6), (rowb(u, 8 * j + 1), rowb(u, 8 * j + 3),
                                                         rowb(u, 8 * j + 5), rowb(u, 8 * j + 7))))
            if c == 1:
                qd = (q * f).astype(BF16)
                kd = k.astype(BF16)
            else:
                qd = (q * jnp.exp(b - bs)).astype(BF16)
                kd = (k * jnp.exp(be - b)).astype(BF16)
            a = [a[u] + _dot_nt(qd[r], kd[r]) * msk_ref[ci] for u, r in enumerate(chunk_rows)]

        q_exp = (q * jnp.exp(b)).astype(BF16)
        b_last = build(lambda u, j: rowb(u, HG_CHUNK - 1))
        k_dec = (k * jnp.exp(b_last - b)).astype(BF16)
        o_intra = [_dot(a[u].astype(BF16), v[r]) for u, r in enumerate(chunk_rows)]
        st_add = [_dot_tn(v[r], k_dec[r]) for r in chunk_rows]

        o = []
        for u, r in enumerate(chunk_rows):
            o.append(o_intra[u] + _dot_nt(q_exp[r], st.astype(BF16)))
            st = jnp.exp(rowb(u, HG_CHUNK - 1)[0:1, :]) * st + st_add[u]
        o = jnp.concatenate(o, axis=0)

        g = g_ref[rows, :]
        o_ref[rows, :] = o * _rms_scale(o) * gn * (g * _sigmoid(g))
        return st

    lax.fori_loop(0, q_ref.shape[0] // HG_GROUP, group, jnp.zeros((HG_D, HG_D), F32))


def _hgrn(proj, lb_logits, out_norm, batch, seq):
    hblk = lambda col: pl.BlockSpec((seq, HG_D), lambda b, h: (b, col // HG_D + h))
    masks = jnp.asarray(_hgrn_masks())
    tri = np.kron(np.eye(HG_UNROLL), np.tril(np.ones((HG_CHUNK, HG_CHUNK))))
    tri = jnp.asarray(tri.astype(np.float32), dtype=BF16)
    nlev = masks.shape[0]
    return pl.pallas_call(
        _hgrn_kernel,
        out_shape=jax.ShapeDtypeStruct((batch * seq, HG_WIDTH), F32),
        grid=(batch, HG_HEADS),
        in_specs=[
            hblk(COL_HQ), hblk(COL_HF), hblk(COL_HI), hblk(COL_HG),
            pl.BlockSpec((lb_logits.shape[0], HG_D), lambda b, h: (0, h)),
            pl.BlockSpec((1, HG_D), lambda b, h: (0, 0)),
            pl.BlockSpec((nlev, HG_CHUNK, HG_CHUNK), lambda b, h: (0, 0, 0)),
            pl.BlockSpec((HG_GROUP, HG_GROUP), lambda b, h: (0, 0)),
        ],
        out_specs=pl.BlockSpec((seq, HG_D), lambda b, h: (b, h)),
        scratch_shapes=[pltpu.VMEM((HG_GROUP, HG_D), F32)],
        compiler_params=_params("parallel", "parallel"),
        name="hgrn2",
    )(proj, proj, proj, proj, lb_logits, out_norm, masks, tri)


OUT_TM = 512


def _out_proj_kernel(a_ref, r_ref, x_ref, g_ref, w_ref, o_ref):
    a = a_ref[...]
    an = (a * _rms_scale(a) * g_ref[...]).astype(BF16)
    o_ref[...] = (x_ref[...] + _dot(an, w_ref[:MLA_WIDTH, :])
                  + _dot(r_ref[...].astype(BF16), w_ref[MLA_WIDTH:, :]))


def _out_proj(a, r, x, gain, w):
    t = x.shape[0]
    return pl.pallas_call(
        _out_proj_kernel,
        out_shape=jax.ShapeDtypeStruct((t, D_MODEL), F32),
        grid=(t // OUT_TM,),
        in_specs=[
            pl.BlockSpec((OUT_TM, MLA_WIDTH), lambda i: (i, 0)),
            pl.BlockSpec((OUT_TM, HG_WIDTH), lambda i: (i, 0)),
            pl.BlockSpec((OUT_TM, D_MODEL), lambda i: (i, 0)),
            pl.BlockSpec((1, MLA_WIDTH), lambda i: (0, 0)),
            pl.BlockSpec(w.shape, lambda i: (0, 0)),
        ],
        out_specs=pl.BlockSpec((OUT_TM, D_MODEL), lambda i: (i, 0)),
        compiler_params=_params("parallel"),
        name="out_proj",
    )(a, r, x, gain, w)


def _mem_kv_kernel(m_ref, g_ref, w_ref, kn_ref, k_ref, v_ref):
    m = m_ref[...]
    mn = (m * _rms_scale(m) * g_ref[...]).astype(BF16)
    kv = _dot(mn, w_ref[...])
    for h in range(MEM_HEADS):
        k = kv[:, 2 * h * MEM_HD:(2 * h + 1) * MEM_HD]
        k_ref[:, h * MEM_HD:(h + 1) * MEM_HD] = (k * _rms_scale(k) * kn_ref[...]).astype(BF16)
        v_ref[:, h * MEM_HD:(h + 1) * MEM_HD] = kv[:, (2 * h + 1) * MEM_HD:(2 * h + 2) * MEM_HD].astype(BF16)


def _mem_kv(mem, gain, w, k_norm, batch, mem_len):
    return pl.pallas_call(
        _mem_kv_kernel,
        out_shape=(jax.ShapeDtypeStruct((batch * mem_len, MEM_WIDTH), BF16),
                   jax.ShapeDtypeStruct((batch * mem_len, MEM_WIDTH), BF16)),
        grid=(batch,),
        in_specs=[
            pl.BlockSpec((mem_len, D_MODEL), lambda b: (b, 0)),
            pl.BlockSpec((1, D_MODEL), lambda b: (0, 0)),
            pl.BlockSpec(w.shape, lambda b: (0, 0)),
            pl.BlockSpec((1, MEM_HD), lambda b: (0, 0)),
        ],
        out_specs=(pl.BlockSpec((mem_len, MEM_WIDTH), lambda b: (b, 0)),
                   pl.BlockSpec((mem_len, MEM_WIDTH), lambda b: (b, 0))),
        compiler_params=_params("parallel"),
        name="mem_kv",
    )(mem, gain, w, k_norm)


XA_TM = 512


def _xattn_kernel(x_ref, g_ref, wq_ref, qn_ref, k_ref, v_ref, wo_ref, o_ref, att_scr):
    x = x_ref[...]
    h = (x * _rms_scale(x) * g_ref[...]).astype(BF16)
    q = _dot(h, wq_ref[...])
    scale = MEM_HD ** -0.5
    for hd in range(MEM_HEADS):
        cols = slice(hd * MEM_HD, (hd + 1) * MEM_HD)
        qh = q[:, cols]
        qh = (qh * _rms_scale(qh) * qn_ref[...]).astype(BF16)
        s = _dot_nt(qh, k_ref[:, cols]) * scale
        p = jnp.exp(s - jnp.max(s, axis=-1, keepdims=True))
        p = p / jnp.sum(p, axis=-1, keepdims=True)
        att_scr[:, cols] = _dot(p.astype(BF16), v_ref[:, cols]).astype(BF16)
    o_ref[...] = x + _dot(att_scr[...], wo_ref[...])


def _xattn(x, gain, wq, q_norm, kx, vx, wo, seq, mem_len):
    t = x.shape[0]
    per_batch = seq // XA_TM
    return pl.pallas_call(
        _xattn_kernel,
        out_shape=jax.ShapeDtypeStruct((t, D_MODEL), F32),
        grid=(t // XA_TM,),
        in_specs=[
            pl.BlockSpec((XA_TM, D_MODEL), lambda i: (i, 0)),
            pl.BlockSpec((1, D_MODEL), lambda i: (0, 0)),
            pl.BlockSpec(wq.shape, lambda i: (0, 0)),
            pl.BlockSpec((1, MEM_HD), lambda i: (0, 0)),
            pl.BlockSpec((mem_len, MEM_WIDTH), lambda i: (i // per_batch, 0)),
            pl.BlockSpec((mem_len, MEM_WIDTH), lambda i: (i // per_batch, 0)),
            pl.BlockSpec(wo.shape, lambda i: (0, 0)),
        ],
        out_specs=pl.BlockSpec((XA_TM, D_MODEL), lambda i: (i, 0)),
        scratch_shapes=[pltpu.VMEM((XA_TM, MEM_WIDTH), BF16)],
        compiler_params=_params("parallel"),
        name="xattn",
    )(x, gain, wq, q_norm, kx, vx, wo)


def _pe_pair(x1, x2):
    z = jnp.zeros(x1.shape[:-1] + (LANES - MLA_ROPE,), x1.dtype)
    return jnp.concatenate([x1, x2, z], axis=-1), jnp.concatenate([x2, x1, z], axis=-1)


def _prep_w_in(w_in):
    w = w_in.astype(BF16)
    half = MLA_ROPE // 2
    kpe0 = MLA_Q_RANK + MLA_KV_RANK
    hg0 = kpe0 + MLA_ROPE
    pe, pe_sw = _pe_pair(w[:, kpe0:kpe0 + half], w[:, kpe0 + half:hg0])
    return jnp.concatenate([w[:, :kpe0], w[:, hg0:], pe, pe_sw], axis=1)


def _prep_w_q_up(w):
    w = w.astype(BF16).reshape(MLA_Q_RANK, MLA_HEADS, MLA_QK)
    half = MLA_ROPE // 2
    pe, pe_sw = _pe_pair(w[..., MLA_NOPE:MLA_NOPE + half], w[..., MLA_NOPE + half:])
    return jnp.concatenate([w[..., :MLA_NOPE], pe, pe_sw], axis=-1).reshape(MLA_Q_RANK, MLA_HEADS * Q_UP_PER_HEAD)


def _pe_gains(norm):
    half = MLA_ROPE // 2
    g1 = norm[MLA_NOPE:MLA_NOPE + half]
    g2 = norm[MLA_NOPE + half:]
    ga, gb = _pe_pair(g1, g2)
    sign = jnp.concatenate([-jnp.ones((half,), F32), jnp.ones((LANES - half,), F32)])
    return ga[None, :], (gb * sign)[None, :]


def kernel(x, mem, positions, ffn1_norm, ffn1_w_gate, ffn1_w_up, ffn1_w_down, mix_norm, w_in, mla_q_a_norm, mla_w_q_up, mla_kv_a_norm, mla_w_kv_up, mla_q_norm, mla_k_norm, mla_out_norm, hg_lb_logits, hg_out_norm, w_out, xattn_norm, mem_norm, xattn_w_q, xattn_w_kv, xattn_q_norm, xattn_k_norm, xattn_w_o, ffn2_norm, ffn2_w_gate, ffn2_w_up, ffn2_w_down):
    batch, seq, _ = x.shape
    mem_len = mem.shape[1]
    depth = ffn1_norm.shape[0]
    assert depth == 1 and seq % ATT_T == 0 and seq % XA_TM == 0
    t = batch * seq
    xt = x.reshape(t, D_MODEL)
    pos = positions.reshape(t, 1)
    half = MLA_ROPE // 2
    inv_freq = ROPE_BASE ** (-np.arange(half, dtype=np.float32) / half)
    invf = jnp.asarray(np.concatenate([inv_freq, inv_freq, np.zeros(LANES - MLA_ROPE, np.float32)])[None, :])
    l = 0

    xt = _ffn(xt, ffn1_norm[l][None, :], ffn1_w_gate[l], ffn1_w_up[l], ffn1_w_down[l])

    proj = _in_proj(xt, mix_norm[l][None, :], _prep_w_in(w_in[l]))
    qpa, qpb = _pe_gains(mla_q_norm[l])
    kpa, kpb = _pe_gains(mla_k_norm[l])
    q, k, v = _mla_prep(proj, pos, invf, mla_q_a_norm[l][None, :], mla_kv_a_norm[l][None, :],
                        _prep_w_q_up(mla_w_q_up[l]), mla_w_kv_up[l].astype(BF16),
                        mla_q_norm[l][None, :MLA_NOPE], qpa, qpb,
                        mla_k_norm[l][None, :MLA_NOPE], kpa, kpb)
    a = _mla_attn(q, k, v, batch, seq)
    r = _hgrn(proj, hg_lb_logits, hg_out_norm[l][None, :], batch, seq)
    xt = _out_proj(a, r, xt, mla_out_norm[l][None, :], w_out[l].astype(BF16))

    kx, vx = _mem_kv(mem.reshape(batch * mem_len, D_MODEL), mem_norm[l][None, :],
                     xattn_w_kv[l].astype(BF16), xattn_k_norm[l][None, :], batch, mem_len)
    xt = _xattn(xt, xattn_norm[l][None, :], xattn_w_q[l].astype(BF16), xattn_q_norm[l][None, :],
                kx, vx, xattn_w_o[l].astype(BF16), seq, mem_len)

    xt = _ffn(xt, ffn2_norm[l][None, :], ffn2_w_gate[l], ffn2_w_up[l], ffn2_w_down[l])
    return xt.reshape(batch, seq, D_MODEL)
```

```python
import functools

import numpy as np
import jax
import jax.numpy as jnp
from jax import lax
from jax.experimental import pallas as pl
from jax.experimental.pallas import tpu as pltpu

F32 = jnp.float32
BF16 = jnp.bfloat16

EPS = 1e-6
ROPE_BASE = 10000.0
LANES = 128
SUBLANES = 8

D_MODEL = 2048
D_FF = 5504
FFN_TM = 1024
FFN_TF = 256

MLA_HEADS = 8
MLA_NOPE = 128
MLA_ROPE = 64
MLA_QK = MLA_NOPE + MLA_ROPE
MLA_V = 128
MLA_Q_RANK = 512
MLA_KV_RANK = 256
MLA_WIDTH = MLA_HEADS * MLA_V
QK_PAD = 256
Q_UP_PER_HEAD = 384
ATT_Q_SCALE = float(MLA_QK ** -0.5 * np.log2(np.e))

HG_HEADS = 8
HG_D = 128
HG_CHUNK = 64
HG_LEVELS = (32, 16, 8, 4, 2, 1)
HG_UNROLL = 8
HG_GROUP = HG_CHUNK * HG_UNROLL
HG_SLAB = 256
HG_WIDTH = HG_HEADS * HG_D

MEM_HEADS = 4
MEM_HD = 128
MEM_WIDTH = MEM_HEADS * MEM_HD

COL_CQ = 0
COL_CKV = 512
COL_HQ = 768
COL_HF = COL_HQ + HG_WIDTH
COL_HI = COL_HF + HG_WIDTH
COL_HG = COL_HI + HG_WIDTH
COL_KPE = COL_HG + HG_WIDTH
COL_KPE_SW = COL_KPE + LANES
IN_COLS_PAD = COL_KPE_SW + LANES

VMEM_LIMIT = 56 * 1024 * 1024


def _params(*sem):
    return pltpu.CompilerParams(dimension_semantics=sem, vmem_limit_bytes=VMEM_LIMIT)


def _rms_scale(x):
    return lax.rsqrt(jnp.mean(x * x, axis=-1, keepdims=True) + EPS)


def _sigmoid(x):
    return 1.0 / (1.0 + jnp.exp(-x))


def _dot(a, b):
    return jnp.dot(a, b, preferred_element_type=F32)


def _dot_nt(a, b):
    return lax.dot_general(a, b, (((1,), (1,)), ((), ())), preferred_element_type=F32)


def _dot_tn(a, b):
    return lax.dot_general(a, b, (((0,), (0,)), ((), ())), preferred_element_type=F32)


def _ffn_kernel(x_ref, g_ref, wg_ref, wu_ref, wd_ref, o_ref, h_scr):
    @pl.when(pl.program_id(1) == 0)
    def _():
        x = x_ref[...]
        h_scr[...] = (x * _rms_scale(x) * g_ref[...]).astype(BF16)
        o_ref[...] = x

    h = h_scr[...]
    gate = _dot(h, wg_ref[...].astype(BF16))
    up = _dot(h, wu_ref[...].astype(BF16))
    a = 0.5 * gate * _sigmoid(gate) * up
    f0 = pl.program_id(1) * FFN_TF
    a_ok = f0 + lax.broadcasted_iota(jnp.int32, (1, FFN_TF), 1) < D_FF
    w_ok = f0 + lax.broadcasted_iota(jnp.int32, (FFN_TF, 1), 0) < D_FF
    a = jnp.where(a_ok, a, 0.0).astype(BF16)
    wd = jnp.where(w_ok, wd_ref[...], 0.0).astype(BF16)
    o_ref[...] += _dot(a, wd)


def _ffn(x, gain, wg, wu, wd):
    t = x.shape[0]
    return pl.pallas_call(
        _ffn_kernel,
        out_shape=jax.ShapeDtypeStruct((t, D_MODEL), F32),
        grid=(t // FFN_TM, pl.cdiv(D_FF, FFN_TF)),
        in_specs=[
            pl.BlockSpec((FFN_TM, D_MODEL), lambda i, f: (i, 0)),
            pl.BlockSpec((1, D_MODEL), lambda i, f: (0, 0)),
            pl.BlockSpec((D_MODEL, FFN_TF), lambda i, f: (0, f)),
            pl.BlockSpec((D_MODEL, FFN_TF), lambda i, f: (0, f)),
            pl.BlockSpec((FFN_TF, D_MODEL), lambda i, f: (f, 0)),
        ],
        out_specs=pl.BlockSpec((FFN_TM, D_MODEL), lambda i, f: (i, 0)),
        scratch_shapes=[pltpu.VMEM((FFN_TM, D_MODEL), BF16)],
        compiler_params=_params("parallel", "arbitrary"),
        name="ffn",
    )(x, gain, wg, wu, wd)


PROJ_TM = 1024
PROJ_TN = 512


def _in_proj_kernel(x_ref, g_ref, w_ref, o_ref, h_scr):
    @pl.when(pl.program_id(1) == 0)
    def _():
        x = x_ref[...]
        h_scr[...] = (x * _rms_scale(x) * g_ref[...]).astype(BF16)

    o_ref[...] = _dot(h_scr[...], w_ref[...])


def _in_proj(x, gain, w):
    t = x.shape[0]
    n = w.shape[1]
    return pl.pallas_call(
        _in_proj_kernel,
        out_shape=jax.ShapeDtypeStruct((t, n), F32),
        grid=(t // PROJ_TM, n // PROJ_TN),
        in_specs=[
            pl.BlockSpec((PROJ_TM, D_MODEL), lambda i, j: (i, 0)),
            pl.BlockSpec((1, D_MODEL), lambda i, j: (0, 0)),
            pl.BlockSpec((D_MODEL, PROJ_TN), lambda i, j: (0, j)),
        ],
        out_specs=pl.BlockSpec((PROJ_TM, PROJ_TN), lambda i, j: (i, j)),
        scratch_shapes=[pltpu.VMEM((PROJ_TM, D_MODEL), BF16)],
        compiler_params=_params("parallel", "arbitrary"),
        name="in_proj",
    )(x, gain, w)


PREP_TM = 512


def _mla_prep_kernel(cq_ref, ckv_ref, kpe_ref, kpesw_ref, pos_ref, invf_ref,
                     qa_ref, kva_ref, wq_ref, wkv_ref,
                     qn_ref, qpa_ref, qpb_ref, kn_ref, kpa_ref, kpb_ref,
                     q_ref, k_ref, v_ref):
    ang = pos_ref[...].astype(F32) * invf_ref[...]
    cos = jnp.cos(ang)
    sin = jnp.sin(ang)

    cq = cq_ref[...]
    cqn = (cq * _rms_scale(cq) * qa_ref[...]).astype(BF16)
    qall = _dot(cqn, wq_ref[...])
    q_cos = qpa_ref[...] * cos
    q_sin = qpb_ref[...] * sin
    for h in range(MLA_HEADS):
        base = h * Q_UP_PER_HEAD
        qn = qall[:, base:base + LANES]
        y = qall[:, base + LANES:base + 2 * LANES]
        ysw = qall[:, base + 2 * LANES:base + 3 * LANES]
        ss = jnp.sum(qn * qn, axis=-1, keepdims=True) + jnp.sum(y * y, axis=-1, keepdims=True)
        r = lax.rsqrt(ss * (1.0 / MLA_QK) + EPS) * ATT_Q_SCALE
        q_ref[:, h * QK_PAD:h * QK_PAD + LANES] = (qn * r * qn_ref[...]).astype(BF16)
        q_ref[:, h * QK_PAD + LANES:(h + 1) * QK_PAD] = ((y * q_cos + ysw * q_sin) * r).astype(BF16)

    ckv = ckv_ref[...]
    ckvn = (ckv * _rms_scale(ckv) * kva_ref[...]).astype(BF16)
    kvall = _dot(ckvn, wkv_ref[...].astype(BF16))
    yk = kpe_ref[...]
    kr = yk * (kpa_ref[...] * cos) + kpesw_ref[...] * (kpb_ref[...] * sin)
    ss_pe = jnp.sum(yk * yk, axis=-1, keepdims=True)
    for h in range(MLA_HEADS):
        kn = kvall[:, h * 2 * LANES:h * 2 * LANES + LANES]
        ss = jnp.sum(kn * kn, axis=-1, keepdims=True) + ss_pe
        r = lax.rsqrt(ss * (1.0 / MLA_QK) + EPS)
        k_ref[:, h * QK_PAD:h * QK_PAD + LANES] = (kn * r * kn_ref[...]).astype(BF16)
        k_ref[:, h * QK_PAD + LANES:(h + 1) * QK_PAD] = (kr * r).astype(BF16)
        v_ref[:, h * MLA_V:(h + 1) * MLA_V] = kvall[:, h * 2 * LANES + LANES:(h + 1) * 2 * LANES].astype(BF16)


def _mla_prep(proj, pos, invf, qa, kva, wq, wkv, qn, qpa, qpb, kn, kpa, kpb):
    t = proj.shape[0]
    tm = PREP_TM
    row = lambda i: (0, 0)
    vec = lambda n: pl.BlockSpec((1, n), row)
    return pl.pallas_call(
        _mla_prep_kernel,
        out_shape=(jax.ShapeDtypeStruct((t, MLA_HEADS * QK_PAD), BF16),
                   jax.ShapeDtypeStruct((t, MLA_HEADS * QK_PAD), BF16),
                   jax.ShapeDtypeStruct((t, MLA_WIDTH), BF16)),
        grid=(t // tm,),
        in_specs=[
            pl.BlockSpec((tm, MLA_Q_RANK), lambda i: (i, COL_CQ // MLA_Q_RANK)),
            pl.BlockSpec((tm, MLA_KV_RANK), lambda i: (i, COL_CKV // MLA_KV_RANK)),
            pl.BlockSpec((tm, LANES), lambda i: (i, COL_KPE // LANES)),
            pl.BlockSpec((tm, LANES), lambda i: (i, COL_KPE_SW // LANES)),
            pl.BlockSpec((tm, 1), lambda i: (i, 0)),
            vec(LANES),
            vec(MLA_Q_RANK), vec(MLA_KV_RANK),
            pl.BlockSpec(wq.shape, row), pl.BlockSpec(wkv.shape, row),
            vec(LANES), vec(LANES), vec(LANES), vec(LANES), vec(LANES), vec(LANES),
        ],
        out_specs=(pl.BlockSpec((tm, MLA_HEADS * QK_PAD), lambda i: (i, 0)),
                   pl.BlockSpec((tm, MLA_HEADS * QK_PAD), lambda i: (i, 0)),
                   pl.BlockSpec((tm, MLA_WIDTH), lambda i: (i, 0))),
        compiler_params=_params("parallel"),
        name="mla_prep",
    )(proj, proj, proj, proj, pos, invf, qa, kva, wq, wkv, qn, qpa, qpb, kn, kpa, kpb)


ATT_T = 256


def _mla_attn_kernel(q_ref, k_ref, v_ref, o_ref):
    qpos = lax.broadcasted_iota(jnp.int32, (ATT_T, ATT_T), 0)
    kpos = lax.broadcasted_iota(jnp.int32, (ATT_T, ATT_T), 1)
    for i in range(q_ref.shape[0] // ATT_T):
        lo, hi = i * ATT_T, (i + 1) * ATT_T
        q = q_ref[lo:hi, :]
        s_diag = jnp.where(kpos <= qpos, _dot_nt(q, k_ref[lo:hi, :]), -jnp.inf)
        m = jnp.max(s_diag, axis=-1, keepdims=True)
        if i > 0:
            s_off = _dot_nt(q, k_ref[0:lo, :])
            m = jnp.maximum(m, jnp.max(s_off, axis=-1, keepdims=True))
        p = jnp.exp2(s_diag - m)
        l = jnp.sum(p, axis=-1, keepdims=True)
        acc = _dot(p.astype(BF16), v_ref[lo:hi, :])
        if i > 0:
            p = jnp.exp2(s_off - m)
            l = l + jnp.sum(p, axis=-1, keepdims=True)
            acc = acc + _dot(p.astype(BF16), v_ref[0:lo, :])
        o_ref[lo:hi, :] = acc / l


def _mla_attn(q, k, v, batch, seq):
    return pl.pallas_call(
        _mla_attn_kernel,
        out_shape=jax.ShapeDtypeStruct((batch * seq, MLA_WIDTH), F32),
        grid=(batch, MLA_HEADS),
        in_specs=[
            pl.BlockSpec((seq, QK_PAD), lambda b, h: (b, h)),
            pl.BlockSpec((seq, QK_PAD), lambda b, h: (b, h)),
            pl.BlockSpec((seq, MLA_V), lambda b, h: (b, h)),
        ],
        out_specs=pl.BlockSpec((seq, MLA_V), lambda b, h: (b, h)),
        compiler_params=_params("parallel", "parallel"),
        name="mla_attn",
    )(q, k, v)


def _hgrn_masks():
    t = np.arange(HG_CHUNK)[:, None]
    s = np.arange(HG_CHUNK)[None, :]
    masks = [((t // c) % 2 == 1) & ((s // c) == (t // c) - 1) for c in HG_LEVELS]
    masks.append(t == s)
    return np.stack(masks).astype(np.float32)


def _hgrn_kernel(q_ref, f_ref, i_ref, g_ref, lbl_ref, gn_ref, msk_ref, tri_ref, o_ref, b_scr):
    lbl = lbl_ref[...]
    e = jnp.exp(lbl - jnp.max(lbl, axis=0, keepdims=True))
    lb = e[0:1, :] / jnp.sum(e, axis=0, keepdims=True)
    gn = gn_ref[...]
    tri = tri_ref[...]
    sub = lax.broadcasted_iota(jnp.int32, (SUBLANES, LANES), 0)
    zeros8 = jnp.zeros((SUBLANES, LANES), F32)
    per_chunk = HG_CHUNK // SUBLANES
    chunk_rows = [slice(u * HG_CHUNK, (u + 1) * HG_CHUNK) for u in range(HG_UNROLL)]

    def sel(bounds, vals):
        out = vals[-1]
        for bound, val in zip(reversed(bounds), reversed(vals[:-1])):
            out = jnp.where(sub < bound, val, out)
        return out

    def group(gi, st):
        rows = pl.ds(pl.multiple_of(gi * HG_GROUP, HG_GROUP), HG_GROUP)
        q = q_ref[rows, :]
        v = i_ref[rows, :].astype(BF16)
        f = lb + (1.0 - lb) * _sigmoid(f_ref[rows, :])
        lg = jnp.log2(f)
        k = 1.0 - f

        l1 = lg.astype(BF16)
        l2 = (lg - l1.astype(F32)).astype(BF16)
        slabs = [slice(i, i + HG_SLAB) for i in range(0, HG_GROUP, HG_SLAB)]
        b = jnp.concatenate([_dot(tri, l1[r]) + _dot(tri, l2[r]) for r in slabs], axis=0)
        b_scr[...] = b

        cache = {}

        def rowb(u, r):
            if r < 0:
                return zeros8
            if (u, r) not in cache:
                cache[u, r] = jnp.broadcast_to(b_scr[u * HG_CHUNK + r:u * HG_CHUNK + r + 1, :], (SUBLANES, LANES))
            return cache[u, r]

        def build(fn):
            return jnp.concatenate([fn(u, j) for u in range(HG_UNROLL) for j in range(per_chunk)], axis=0)

        a = [jnp.sum(q[r] * k[r], axis=-1, keepdims=True) * msk_ref[len(HG_LEVELS)] for r in chunk_rows]
        for ci, c in enumerate(HG_LEVELS):
            if c >= SUBLANES:
                m = c // SUBLANES
                bs = build(lambda u, j: rowb(u, SUBLANES * (j // m) * m - 1))
                be = build(lambda u, j: rowb(u, SUBLANES * ((j // m) * m + m) - 1))
            elif c == 4:
                bs = build(lambda u, j: sel((4,), (rowb(u, 8 * j - 1), rowb(u, 8 * j + 3))))
                be = build(lambda u, j: sel((4,), (rowb(u, 8 * j + 3), rowb(u, 8 * j + 7))))
            elif c == 2:
                bs = build(lambda u, j: sel((2, 4, 6), (rowb(u, 8 * j - 1), rowb(u, 8 * j + 1),
                                                         rowb(u, 8 * j + 3), rowb(u, 8 * j + 5))))
                be = build(lambda u, j: sel((2, 4, 6), (rowb(u, 8 * j + 1), rowb(u, 8 * j + 3),
                                                         rowb(u, 8 * j + 5), rowb(u, 8 * j + 7))))
            if c == 1:
                qd = (q * f).astype(BF16)
                kd = k.astype(BF16)
            else:
                qd = (q * jnp.exp2(b - bs)).astype(BF16)
                kd = (k * jnp.exp2(be - b)).astype(BF16)
            a = [a[u] + _dot_nt(qd[r], kd[r]) * msk_ref[ci] for u, r in enumerate(chunk_rows)]

        q_exp = (q * jnp.exp2(b)).astype(BF16)
        b_last = build(lambda u, j: rowb(u, HG_CHUNK - 1))
        k_dec = (k * jnp.exp2(b_last - b)).astype(BF16)
        o_intra = [_dot(a[u].astype(BF16), v[r]) for u, r in enumerate(chunk_rows)]
        st_add = [_dot_tn(v[r], k_dec[r]) for r in chunk_rows]

        o = []
        for u, r in enumerate(chunk_rows):
            o.append(o_intra[u] + _dot_nt(q_exp[r], st.astype(BF16)))
            st = jnp.exp2(rowb(u, HG_CHUNK - 1)[0:1, :]) * st + st_add[u]
        o = jnp.concatenate(o, axis=0)

        g = g_ref[rows, :]
        o_ref[rows, :] = o * _rms_scale(o) * gn * (g * _sigmoid(g))
        return st

    lax.fori_loop(0, q_ref.shape[0] // HG_GROUP, group, jnp.zeros((HG_D, HG_D), F32))


def _hgrn(proj, lb_logits, out_norm, batch, seq):
    hblk = lambda col: pl.BlockSpec((seq, HG_D), lambda b, h: (b, col // HG_D + h))
    masks = jnp.asarray(_hgrn_masks())
    tri = np.kron(np.eye(HG_SLAB // HG_CHUNK), np.tril(np.ones((HG_CHUNK, HG_CHUNK))))
    tri = jnp.asarray(tri.astype(np.float32), dtype=BF16)
    nlev = masks.shape[0]
    return pl.pallas_call(
        _hgrn_kernel,
        out_shape=jax.ShapeDtypeStruct((batch * seq, HG_WIDTH), F32),
        grid=(batch, HG_HEADS),
        in_specs=[
            hblk(COL_HQ), hblk(COL_HF), hblk(COL_HI), hblk(COL_HG),
            pl.BlockSpec((lb_logits.shape[0], HG_D), lambda b, h: (0, h)),
            pl.BlockSpec((1, HG_D), lambda b, h: (0, 0)),
            pl.BlockSpec((nlev, HG_CHUNK, HG_CHUNK), lambda b, h: (0, 0, 0)),
            pl.BlockSpec((HG_SLAB, HG_SLAB), lambda b, h: (0, 0)),
        ],
        out_specs=pl.BlockSpec((seq, HG_D), lambda b, h: (b, h)),
        scratch_shapes=[pltpu.VMEM((HG_GROUP, HG_D), F32)],
        compiler_params=_params("parallel", "parallel"),
        name="hgrn2",
    )(proj, proj, proj, proj, lb_logits, out_norm, masks, tri)


OUT_TM = 512


def _out_proj_kernel(a_ref, r_ref, x_ref, g_ref, w_ref, o_ref):
    a = a_ref[...]
    an = (a * _rms_scale(a) * g_ref[...]).astype(BF16)
    o_ref[...] = (x_ref[...] + _dot(an, w_ref[:MLA_WIDTH, :].astype(BF16))
                  + _dot(r_ref[...].astype(BF16), w_ref[MLA_WIDTH:, :].astype(BF16)))


def _out_proj(a, r, x, gain, w):
    t = x.shape[0]
    return pl.pallas_call(
        _out_proj_kernel,
        out_shape=jax.ShapeDtypeStruct((t, D_MODEL), F32),
        grid=(t // OUT_TM,),
        in_specs=[
            pl.BlockSpec((OUT_TM, MLA_WIDTH), lambda i: (i, 0)),
            pl.BlockSpec((OUT_TM, HG_WIDTH), lambda i: (i, 0)),
            pl.BlockSpec((OUT_TM, D_MODEL), lambda i: (i, 0)),
            pl.BlockSpec((1, MLA_WIDTH), lambda i: (0, 0)),
            pl.BlockSpec(w.shape, lambda i: (0, 0), pipeline_mode=pl.Buffered(1)),
        ],
        out_specs=pl.BlockSpec((OUT_TM, D_MODEL), lambda i: (i, 0)),
        compiler_params=_params("parallel"),
        name="out_proj",
    )(a, r, x, gain, w)


def _mem_kv_kernel(m_ref, g_ref, w_ref, kn_ref, k_ref, v_ref):
    m = m_ref[...]
    mn = (m * _rms_scale(m) * g_ref[...]).astype(BF16)
    kv = _dot(mn, w_ref[...].astype(BF16))
    for h in range(MEM_HEADS):
        k = kv[:, 2 * h * MEM_HD:(2 * h + 1) * MEM_HD]
        k_ref[:, h * MEM_HD:(h + 1) * MEM_HD] = (k * _rms_scale(k) * kn_ref[...]).astype(BF16)
        v_ref[:, h * MEM_HD:(h + 1) * MEM_HD] = kv[:, (2 * h + 1) * MEM_HD:(2 * h + 2) * MEM_HD].astype(BF16)


def _mem_kv(mem, gain, w, k_norm, batch, mem_len):
    return pl.pallas_call(
        _mem_kv_kernel,
        out_shape=(jax.ShapeDtypeStruct((batch * mem_len, MEM_WIDTH), BF16),
                   jax.ShapeDtypeStruct((batch * mem_len, MEM_WIDTH), BF16)),
        grid=(batch,),
        in_specs=[
            pl.BlockSpec((mem_len, D_MODEL), lambda b: (b, 0)),
            pl.BlockSpec((1, D_MODEL), lambda b: (0, 0)),
            pl.BlockSpec(w.shape, lambda b: (0, 0)),
            pl.BlockSpec((1, MEM_HD), lambda b: (0, 0)),
        ],
        out_specs=(pl.BlockSpec((mem_len, MEM_WIDTH), lambda b: (b, 0)),
                   pl.BlockSpec((mem_len, MEM_WIDTH), lambda b: (b, 0))),
        compiler_params=_params("parallel"),
        name="mem_kv",
    )(mem, gain, w, k_norm)


XA_TM = 512


def _xattn_kernel(x_ref, g_ref, wq_ref, qn_ref, k_ref, v_ref, wo_ref, o_ref, att_scr):
    x = x_ref[...]
    h = (x * _rms_scale(x) * g_ref[...]).astype(BF16)
    q = _dot(h, wq_ref[...].astype(BF16))
    scale = MEM_HD ** -0.5
    for hd in range(MEM_HEADS):
        cols = slice(hd * MEM_HD, (hd + 1) * MEM_HD)
        qh = q[:, cols]
        qh = (qh * _rms_scale(qh) * qn_ref[...]).astype(BF16)
        s = _dot_nt(qh, k_ref[:, cols]) * scale
        p = jnp.exp(s - jnp.max(s, axis=-1, keepdims=True))
        p = p / jnp.sum(p, axis=-1, keepdims=True)
        att_scr[:, cols] = _dot(p.astype(BF16), v_ref[:, cols]).astype(BF16)
    o_ref[...] = x + _dot(att_scr[...], wo_ref[...].astype(BF16))


def _xattn(x, gain, wq, q_norm, kx, vx, wo, seq, mem_len):
    t = x.shape[0]
    per_batch = seq // XA_TM
    return pl.pallas_call(
        _xattn_kernel,
        out_shape=jax.ShapeDtypeStruct((t, D_MODEL), F32),
        grid=(t // XA_TM,),
        in_specs=[
            pl.BlockSpec((XA_TM, D_MODEL), lambda i: (i, 0)),
            pl.BlockSpec((1, D_MODEL), lambda i: (0, 0)),
            pl.BlockSpec(wq.shape, lambda i: (0, 0)),
            pl.BlockSpec((1, MEM_HD), lambda i: (0, 0)),
            pl.BlockSpec((mem_len, MEM_WIDTH), lambda i: (i // per_batch, 0)),
            pl.BlockSpec((mem_len, MEM_WIDTH), lambda i: (i // per_batch, 0)),
            pl.BlockSpec(wo.shape, lambda i: (0, 0)),
        ],
        out_specs=pl.BlockSpec((XA_TM, D_MODEL), lambda i: (i, 0)),
        scratch_shapes=[pltpu.VMEM((XA_TM, MEM_WIDTH), BF16)],
        compiler_params=_params("parallel"),
        name="xattn",
    )(x, gain, wq, q_norm, kx, vx, wo)


def _pe_pair(x1, x2):
    z = jnp.zeros(x1.shape[:-1] + (LANES - MLA_ROPE,), x1.dtype)
    return jnp.concatenate([x1, x2, z], axis=-1), jnp.concatenate([x2, x1, z], axis=-1)


W_IN_TR = 256


def _w_in_prep_kernel(w_ref, o_ref):
    half = MLA_ROPE // 2
    kpe0 = MLA_Q_RANK + MLA_KV_RANK
    hg0 = kpe0 + MLA_ROPE
    lane = lax.broadcasted_iota(jnp.int32, (1, LANES), 1)
    o_ref[:, :kpe0] = w_ref[:, :kpe0].astype(BF16)
    o_ref[:, kpe0:COL_KPE] = w_ref[:, hg0:].astype(BF16)
    t = w_ref[:, kpe0:kpe0 + LANES]
    o_ref[:, COL_KPE:COL_KPE_SW] = jnp.where(lane < MLA_ROPE, t, 0.0).astype(BF16)
    sw = jnp.where(lane < half, pltpu.roll(t, LANES - half, 1), pltpu.roll(t, half, 1))
    o_ref[:, COL_KPE_SW:] = jnp.where(lane < MLA_ROPE, sw, 0.0).astype(BF16)


def _prep_w_in(w_in):
    k, n = w_in.shape
    return pl.pallas_call(
        _w_in_prep_kernel,
        out_shape=jax.ShapeDtypeStruct((k, IN_COLS_PAD), BF16),
        grid=(k // W_IN_TR,),
        in_specs=[pl.BlockSpec((W_IN_TR, n), lambda i: (i, 0))],
        out_specs=pl.BlockSpec((W_IN_TR, IN_COLS_PAD), lambda i: (i, 0)),
        compiler_params=_params("parallel"),
        name="w_in_prep",
    )(w_in)


def _prep_w_q_up(w):
    w = w.astype(BF16).reshape(MLA_Q_RANK, MLA_HEADS, MLA_QK)
    half = MLA_ROPE // 2
    pe, pe_sw = _pe_pair(w[..., MLA_NOPE:MLA_NOPE + half], w[..., MLA_NOPE + half:])
    return jnp.concatenate([w[..., :MLA_NOPE], pe, pe_sw], axis=-1).reshape(MLA_Q_RANK, MLA_HEADS * Q_UP_PER_HEAD)


def _pe_gains(norm):
    half = MLA_ROPE // 2
    g1 = norm[MLA_NOPE:MLA_NOPE + half]
    g2 = norm[MLA_NOPE + half:]
    ga, gb = _pe_pair(g1, g2)
    sign = jnp.concatenate([-jnp.ones((half,), F32), jnp.ones((LANES - half,), F32)])
    return ga[None, :], (gb * sign)[None, :]


def kernel(x, mem, positions, ffn1_norm, ffn1_w_gate, ffn1_w_up, ffn1_w_down, mix_norm, w_in, mla_q_a_norm, mla_w_q_up, mla_kv_a_norm, mla_w_kv_up, mla_q_norm, mla_k_norm, mla_out_norm, hg_lb_logits, hg_out_norm, w_out, xattn_norm, mem_norm, xattn_w_q, xattn_w_kv, xattn_q_norm, xattn_k_norm, xattn_w_o, ffn2_norm, ffn2_w_gate, ffn2_w_up, ffn2_w_down):
    batch, seq, _ = x.shape
    mem_len = mem.shape[1]
    depth = ffn1_norm.shape[0]
    assert depth == 1 and seq % ATT_T == 0 and seq % XA_TM == 0
    t = batch * seq
    xt = x.reshape(t, D_MODEL)
    pos = positions.reshape(t, 1)
    half = MLA_ROPE // 2
    inv_freq = ROPE_BASE ** (-np.arange(half, dtype=np.float32) / half)
    invf = jnp.asarray(np.concatenate([inv_freq, inv_freq, np.zeros(LANES - MLA_ROPE, np.float32)])[None, :])
    l = 0

    xt = _ffn(xt, ffn1_norm[l][None, :], ffn1_w_gate[l], ffn1_w_up[l], ffn1_w_down[l])

    proj = _in_proj(xt, mix_norm[l][None, :], _prep_w_in(w_in[l]))
    qpa, qpb = _pe_gains(mla_q_norm[l])
    kpa, kpb = _pe_gains(mla_k_norm[l])
    q, k, v = _mla_prep(proj, pos, invf, mla_q_a_norm[l][None, :], mla_kv_a_norm[l][None, :],
                        _prep_w_q_up(mla_w_q_up[l]), mla_w_kv_up[l],
                        mla_q_norm[l][None, :MLA_NOPE], qpa, qpb,
                        mla_k_norm[l][None, :MLA_NOPE], kpa, kpb)
    a = _mla_attn(q, k, v, batch, seq)
    r = _hgrn(proj, hg_lb_logits, hg_out_norm[l][None, :], batch, seq)
    xt = _out_proj(a, r, xt, mla_out_norm[l][None, :], w_out[l])

    kx, vx = _mem_kv(mem.reshape(batch * mem_len, D_MODEL), mem_norm[l][None, :],
                     xattn_w_kv[l], xattn_k_norm[l][None, :], batch, mem_len)
    xt = _xattn(xt, xattn_norm[l][None, :], xattn_w_q[l], xattn_q_norm[l][None, :],
                kx, vx, xattn_w_o[l], seq, mem_len)

    xt = _ffn(xt, ffn2_norm[l][None, :], ffn2_w_gate[l], ffn2_w_up[l], ffn2_w_down[l])
    return xt.reshape(batch, seq, D_MODEL)
```

```python
import functools

import numpy as np
import jax
import jax.numpy as jnp
from jax import lax
from jax.experimental import pallas as pl
from jax.experimental.pallas import tpu as pltpu

F32 = jnp.float32
BF16 = jnp.bfloat16

EPS = 1e-6
ROPE_BASE = 10000.0
LANES = 128
SUBLANES = 8

D_MODEL = 2048
D_FF = 5504
FFN_TM = 1024
FFN_TF = 256

MLA_HEADS = 8
MLA_NOPE = 128
MLA_ROPE = 64
MLA_QK = MLA_NOPE + MLA_ROPE
MLA_V = 128
MLA_Q_RANK = 512
MLA_KV_RANK = 256
MLA_WIDTH = MLA_HEADS * MLA_V
QK_PAD = 256
Q_UP_PER_HEAD = 384
ATT_Q_SCALE = float(MLA_QK ** -0.5 * np.log2(np.e))

HG_HEADS = 8
HG_D = 128
HG_CHUNK = 64
HG_LEVELS = (32, 16, 8, 4, 2, 1)
HG_UNROLL = 8
HG_GROUP = HG_CHUNK * HG_UNROLL
HG_SLAB = 256
HG_WIDTH = HG_HEADS * HG_D

MEM_HEADS = 4
MEM_HD = 128
MEM_WIDTH = MEM_HEADS * MEM_HD

COL_CQ = 0
COL_CKV = 512
COL_HQ = 768
COL_HF = COL_HQ + HG_WIDTH
COL_HI = COL_HF + HG_WIDTH
COL_HG = COL_HI + HG_WIDTH
COL_KPE = COL_HG + HG_WIDTH
COL_KPE_SW = COL_KPE + LANES
IN_COLS_PAD = COL_KPE_SW + LANES

VMEM_LIMIT = 56 * 1024 * 1024


def _params(*sem):
    return pltpu.CompilerParams(dimension_semantics=sem, vmem_limit_bytes=VMEM_LIMIT)


def _rms_scale(x):
    return lax.rsqrt(jnp.mean(x * x, axis=-1, keepdims=True) + EPS)


def _sigmoid(x):
    return 1.0 / (1.0 + jnp.exp(-x))


def _dot(a, b):
    return jnp.dot(a, b, preferred_element_type=F32)


def _dot_nt(a, b):
    return lax.dot_general(a, b, (((1,), (1,)), ((), ())), preferred_element_type=F32)


def _dot_tn(a, b):
    return lax.dot_general(a, b, (((0,), (0,)), ((), ())), preferred_element_type=F32)


def _ffn_kernel(x_ref, g_ref, wg_ref, wu_ref, wd_ref, o_ref, h_scr):
    @pl.when(pl.program_id(1) == 0)
    def _():
        x = x_ref[...]
        h_scr[...] = (x * _rms_scale(x) * g_ref[...]).astype(BF16)
        o_ref[...] = x

    h = h_scr[...]
    gate = _dot(h, wg_ref[...].astype(BF16))
    up = _dot(h, wu_ref[...].astype(BF16))
    a = 0.5 * gate * _sigmoid(gate) * up
    f0 = pl.program_id(1) * FFN_TF
    a_ok = f0 + lax.broadcasted_iota(jnp.int32, (1, FFN_TF), 1) < D_FF
    w_ok = f0 + lax.broadcasted_iota(jnp.int32, (FFN_TF, 1), 0) < D_FF
    a = jnp.where(a_ok, a, 0.0).astype(BF16)
    wd = jnp.where(w_ok, wd_ref[...], 0.0).astype(BF16)
    o_ref[...] += _dot(a, wd)


def _ffn(x, gain, wg, wu, wd):
    t = x.shape[0]
    return pl.pallas_call(
        _ffn_kernel,
        out_shape=jax.ShapeDtypeStruct((t, D_MODEL), F32),
        grid=(t // FFN_TM, pl.cdiv(D_FF, FFN_TF)),
        in_specs=[
            pl.BlockSpec((FFN_TM, D_MODEL), lambda i, f: (i, 0)),
            pl.BlockSpec((1, D_MODEL), lambda i, f: (0, 0)),
            pl.BlockSpec((D_MODEL, FFN_TF), lambda i, f: (0, f)),
            pl.BlockSpec((D_MODEL, FFN_TF), lambda i, f: (0, f)),
            pl.BlockSpec((FFN_TF, D_MODEL), lambda i, f: (f, 0)),
        ],
        out_specs=pl.BlockSpec((FFN_TM, D_MODEL), lambda i, f: (i, 0)),
        scratch_shapes=[pltpu.VMEM((FFN_TM, D_MODEL), BF16)],
        compiler_params=_params("parallel", "arbitrary"),
        name="ffn",
    )(x, gain, wg, wu, wd)


PROJ_TM = 1024
PROJ_TN = 512


def _in_proj_kernel(x_ref, g_ref, w_ref, o_ref, h_scr):
    @pl.when(pl.program_id(1) == 0)
    def _():
        x = x_ref[...]
        h_scr[...] = (x * _rms_scale(x) * g_ref[...]).astype(BF16)

    o_ref[...] = _dot(h_scr[...], w_ref[...])


def _in_proj(x, gain, w):
    t = x.shape[0]
    n = w.shape[1]
    return pl.pallas_call(
        _in_proj_kernel,
        out_shape=jax.ShapeDtypeStruct((t, n), F32),
        grid=(t // PROJ_TM, n // PROJ_TN),
        in_specs=[
            pl.BlockSpec((PROJ_TM, D_MODEL), lambda i, j: (i, 0)),
            pl.BlockSpec((1, D_MODEL), lambda i, j: (0, 0)),
            pl.BlockSpec((D_MODEL, PROJ_TN), lambda i, j: (0, j)),
        ],
        out_specs=pl.BlockSpec((PROJ_TM, PROJ_TN), lambda i, j: (i, j)),
        scratch_shapes=[pltpu.VMEM((PROJ_TM, D_MODEL), BF16)],
        compiler_params=_params("parallel", "arbitrary"),
        name="in_proj",
    )(x, gain, w)


PREP_TM = 512


def _mla_prep_kernel(cq_ref, ckv_ref, kpe_ref, kpesw_ref, pos_ref, invf_ref,
                     qa_ref, kva_ref, wq_ref, wkv_ref,
                     qn_ref, qpa_ref, qpb_ref, kn_ref, kpa_ref, kpb_ref,
                     q_ref, k_ref, vt_ref):
    ang = pos_ref[...].astype(F32) * invf_ref[...]
    cos = jnp.cos(ang)
    sin = jnp.sin(ang)

    cq = cq_ref[...]
    cqn = (cq * _rms_scale(cq) * qa_ref[...]).astype(BF16)
    qall = _dot(cqn, wq_ref[...])
    q_cos = qpa_ref[...] * cos
    q_sin = qpb_ref[...] * sin
    for h in range(MLA_HEADS):
        base = h * Q_UP_PER_HEAD
        qn = qall[:, base:base + LANES]
        y = qall[:, base + LANES:base + 2 * LANES]
        ysw = qall[:, base + 2 * LANES:base + 3 * LANES]
        ss = jnp.sum(qn * qn, axis=-1, keepdims=True) + jnp.sum(y * y, axis=-1, keepdims=True)
        r = lax.rsqrt(ss * (1.0 / MLA_QK) + EPS) * ATT_Q_SCALE
        q_ref[:, h * QK_PAD:h * QK_PAD + LANES] = (qn * r * qn_ref[...]).astype(BF16)
        q_ref[:, h * QK_PAD + LANES:(h + 1) * QK_PAD] = ((y * q_cos + ysw * q_sin) * r).astype(BF16)

    ckv = ckv_ref[...]
    ckvn = (ckv * _rms_scale(ckv) * kva_ref[...]).astype(BF16)
    kvall = _dot(ckvn, wkv_ref[...].astype(BF16))
    yk = kpe_ref[...]
    kr = yk * (kpa_ref[...] * cos) + kpesw_ref[...] * (kpb_ref[...] * sin)
    ss_pe = jnp.sum(yk * yk, axis=-1, keepdims=True)
    for h in range(MLA_HEADS):
        kn = kvall[:, h * 2 * LANES:h * 2 * LANES + LANES]
        ss = jnp.sum(kn * kn, axis=-1, keepdims=True) + ss_pe
        r = lax.rsqrt(ss * (1.0 / MLA_QK) + EPS)
        k_ref[:, h * QK_PAD:h * QK_PAD + LANES] = (kn * r * kn_ref[...]).astype(BF16)
        k_ref[:, h * QK_PAD + LANES:(h + 1) * QK_PAD] = (kr * r).astype(BF16)
        vt_ref[h * MLA_V:(h + 1) * MLA_V, :] = kvall[:, h * 2 * LANES + LANES:(h + 1) * 2 * LANES].T.astype(BF16)


def _mla_prep(proj, pos, invf, qa, kva, wq, wkv, qn, qpa, qpb, kn, kpa, kpb):
    t = proj.shape[0]
    tm = PREP_TM
    row = lambda i: (0, 0)
    vec = lambda n: pl.BlockSpec((1, n), row)
    return pl.pallas_call(
        _mla_prep_kernel,
        out_shape=(jax.ShapeDtypeStruct((t, MLA_HEADS * QK_PAD), BF16),
                   jax.ShapeDtypeStruct((t, MLA_HEADS * QK_PAD), BF16),
                   jax.ShapeDtypeStruct((MLA_WIDTH, t), BF16)),
        grid=(t // tm,),
        in_specs=[
            pl.BlockSpec((tm, MLA_Q_RANK), lambda i: (i, COL_CQ // MLA_Q_RANK)),
            pl.BlockSpec((tm, MLA_KV_RANK), lambda i: (i, COL_CKV // MLA_KV_RANK)),
            pl.BlockSpec((tm, LANES), lambda i: (i, COL_KPE // LANES)),
            pl.BlockSpec((tm, LANES), lambda i: (i, COL_KPE_SW // LANES)),
            pl.BlockSpec((tm, 1), lambda i: (i, 0)),
            vec(LANES),
            vec(MLA_Q_RANK), vec(MLA_KV_RANK),
            pl.BlockSpec(wq.shape, row), pl.BlockSpec(wkv.shape, row),
            vec(LANES), vec(LANES), vec(LANES), vec(LANES), vec(LANES), vec(LANES),
        ],
        out_specs=(pl.BlockSpec((tm, MLA_HEADS * QK_PAD), lambda i: (i, 0)),
                   pl.BlockSpec((tm, MLA_HEADS * QK_PAD), lambda i: (i, 0)),
                   pl.BlockSpec((MLA_WIDTH, tm), lambda i: (0, i))),
        compiler_params=_params("parallel"),
        name="mla_prep",
    )(proj, proj, proj, proj, pos, invf, qa, kva, wq, wkv, qn, qpa, qpb, kn, kpa, kpb)


ATT_T = 256
ATT_AHEAD = 3


def _mla_attn_kernel(q_ref, k_ref, vt_ref, o_ref):
    kpos = lax.broadcasted_iota(jnp.int32, (ATT_T, ATT_T), 0)
    qpos = lax.broadcasted_iota(jnp.int32, (ATT_T, ATT_T), 1)
    nq = q_ref.shape[0] // ATT_T

    def scores(i):
        lo, hi = i * ATT_T, (i + 1) * ATT_T
        q = q_ref[lo:hi, :]
        s_diag = jnp.where(kpos <= qpos, _dot_nt(k_ref[lo:hi, :], q), -jnp.inf)
        s_off = _dot_nt(k_ref[0:lo, :], q) if i > 0 else None
        return s_diag, s_off

    def finish(i, s_diag, s_off):
        lo, hi = i * ATT_T, (i + 1) * ATT_T
        m = jnp.max(s_diag, axis=0, keepdims=True)
        if i > 0:
            m = jnp.maximum(m, jnp.max(s_off, axis=0, keepdims=True))
        p = jnp.exp2(s_diag - m)
        l = jnp.sum(p, axis=0, keepdims=True)
        acc = _dot(vt_ref[:, lo:hi], p.astype(BF16))
        if i > 0:
            p = jnp.exp2(s_off - m)
            l = l + jnp.sum(p, axis=0, keepdims=True)
            acc = acc + _dot(vt_ref[:, 0:lo], p.astype(BF16))
        o_ref[lo:hi, :] = (acc / l).T

    order = list(reversed(range(nq)))
    pending = [scores(i) for i in order[:ATT_AHEAD]]
    for n, i in enumerate(order):
        if n + ATT_AHEAD < nq:
            pending.append(scores(order[n + ATT_AHEAD]))
        finish(i, *pending.pop(0))


def _mla_attn(q, k, vt, batch, seq):
    return pl.pallas_call(
        _mla_attn_kernel,
        out_shape=jax.ShapeDtypeStruct((batch * seq, MLA_WIDTH), F32),
        grid=(batch, MLA_HEADS),
        in_specs=[
            pl.BlockSpec((seq, QK_PAD), lambda b, h: (b, h)),
            pl.BlockSpec((seq, QK_PAD), lambda b, h: (b, h)),
            pl.BlockSpec((MLA_V, seq), lambda b, h: (h, b)),
        ],
        out_specs=pl.BlockSpec((seq, MLA_V), lambda b, h: (b, h)),
        compiler_params=_params("parallel", "parallel"),
        name="mla_attn",
    )(q, k, vt)


def _hgrn_masks():
    t = np.arange(HG_CHUNK)[:, None]
    s = np.arange(HG_CHUNK)[None, :]
    masks = [((t // c) % 2 == 1) & ((s // c) == (t // c) - 1) for c in HG_LEVELS]
    masks.append(t == s)
    return np.stack(masks).astype(np.float32)


def _hgrn_kernel(q_ref, f_ref, i_ref, g_ref, lbl_ref, gn_ref, msk_ref, tri_ref, o_ref, b_scr):
    lbl = lbl_ref[...]
    e = jnp.exp(lbl - jnp.max(lbl, axis=0, keepdims=True))
    lb = e[0:1, :] / jnp.sum(e, axis=0, keepdims=True)
    gn = gn_ref[...]
    tri = tri_ref[...]
    sub = lax.broadcasted_iota(jnp.int32, (SUBLANES, LANES), 0)
    zeros8 = jnp.zeros((SUBLANES, LANES), F32)
    per_chunk = HG_CHUNK // SUBLANES
    chunk_rows = [slice(u * HG_CHUNK, (u + 1) * HG_CHUNK) for u in range(HG_UNROLL)]

    def sel(bounds, vals):
        out = vals[-1]
        for bound, val in zip(reversed(bounds), reversed(vals[:-1])):
            out = jnp.where(sub < bound, val, out)
        return out

    def group(gi, st):
        rows = pl.ds(pl.multiple_of(gi * HG_GROUP, HG_GROUP), HG_GROUP)
        q = q_ref[rows, :]
        v = i_ref[rows, :].astype(BF16)
        f = lb + (1.0 - lb) * _sigmoid(f_ref[rows, :])
        lg = jnp.log2(f)
        k = 1.0 - f

        l1 = lg.astype(BF16)
        l2 = (lg - l1.astype(F32)).astype(BF16)
        slabs = [slice(i, i + HG_SLAB) for i in range(0, HG_GROUP, HG_SLAB)]
        b = jnp.concatenate([_dot(tri, l1[r]) + _dot(tri, l2[r]) for r in slabs], axis=0)
        b_scr[...] = b

        cache = {}

        def rowb(u, r):
            if r < 0:
                return zeros8
            if (u, r) not in cache:
                cache[u, r] = jnp.broadcast_to(b_scr[u * HG_CHUNK + r:u * HG_CHUNK + r + 1, :], (SUBLANES, LANES))
            return cache[u, r]

        def build(fn):
            return jnp.concatenate([fn(u, j) for u in range(HG_UNROLL) for j in range(per_chunk)], axis=0)

        a = [jnp.sum(q[r] * k[r], axis=-1, keepdims=True) * msk_ref[len(HG_LEVELS)] for r in chunk_rows]
        for ci, c in enumerate(HG_LEVELS):
            if c >= SUBLANES:
                m = c // SUBLANES
                bs = build(lambda u, j: rowb(u, SUBLANES * (j // m) * m - 1))
                be = build(lambda u, j: rowb(u, SUBLANES * ((j // m) * m + m) - 1))
            elif c == 4:
                bs = build(lambda u, j: sel((4,), (rowb(u, 8 * j - 1), rowb(u, 8 * j + 3))))
                be = build(lambda u, j: sel((4,), (rowb(u, 8 * j + 3), rowb(u, 8 * j + 7))))
            elif c == 2:
                bs = build(lambda u, j: sel((2, 4, 6), (rowb(u, 8 * j - 1), rowb(u, 8 * j + 1),
                                                         rowb(u, 8 * j + 3), rowb(u, 8 * j + 5))))
                be = build(lambda u, j: sel((2, 4, 6), (rowb(u, 8 * j + 1), rowb(u, 8 * j + 3),
                                                         rowb(u, 8 * j + 5), rowb(u, 8 * j + 7))))
            if c == 1:
                qd = (q * f).astype(BF16)
                kd = k.astype(BF16)
            else:
                qd = (q * jnp.exp2(b - bs)).astype(BF16)
                kd = (k * jnp.exp2(be - b)).astype(BF16)
            a = [a[u] + _dot_nt(qd[r], kd[r]) * msk_ref[ci] for u, r in enumerate(chunk_rows)]

        q_exp = (q * jnp.exp2(b)).astype(BF16)
        b_last = build(lambda u, j: rowb(u, HG_CHUNK - 1))
        k_dec = (k * jnp.exp2(b_last - b)).astype(BF16)
        o_intra = [_dot(a[u].astype(BF16), v[r]) for u, r in enumerate(chunk_rows)]
        st_add = [_dot_tn(v[r], k_dec[r]) for r in chunk_rows]

        o = []
        for u, r in enumerate(chunk_rows):
            o.append(o_intra[u] + _dot_nt(q_exp[r], st.astype(BF16)))
            st = jnp.exp2(rowb(u, HG_CHUNK - 1)[0:1, :]) * st + st_add[u]
        o = jnp.concatenate(o, axis=0)

        g = g_ref[rows, :]
        o_ref[rows, :] = o * _rms_scale(o) * gn * (g * _sigmoid(g))
        return st

    lax.fori_loop(0, q_ref.shape[0] // HG_GROUP, group, jnp.zeros((HG_D, HG_D), F32))


def _hgrn(proj, lb_logits, out_norm, batch, seq):
    hblk = lambda col: pl.BlockSpec((seq, HG_D), lambda b, h: (b, col // HG_D + h))
    masks = jnp.asarray(_hgrn_masks())
    tri = np.kron(np.eye(HG_SLAB // HG_CHUNK), np.tril(np.ones((HG_CHUNK, HG_CHUNK))))
    tri = jnp.asarray(tri.astype(np.float32), dtype=BF16)
    nlev = masks.shape[0]
    return pl.pallas_call(
        _hgrn_kernel,
        out_shape=jax.ShapeDtypeStruct((batch * seq, HG_WIDTH), F32),
        grid=(batch, HG_HEADS),
        in_specs=[
            hblk(COL_HQ), hblk(COL_HF), hblk(COL_HI), hblk(COL_HG),
            pl.BlockSpec((lb_logits.shape[0], HG_D), lambda b, h: (0, h)),
            pl.BlockSpec((1, HG_D), lambda b, h: (0, 0)),
            pl.BlockSpec((nlev, HG_CHUNK, HG_CHUNK), lambda b, h: (0, 0, 0)),
            pl.BlockSpec((HG_SLAB, HG_SLAB), lambda b, h: (0, 0)),
        ],
        out_specs=pl.BlockSpec((seq, HG_D), lambda b, h: (b, h)),
        scratch_shapes=[pltpu.VMEM((HG_GROUP, HG_D), F32)],
        compiler_params=_params("parallel", "parallel"),
        name="hgrn2",
    )(proj, proj, proj, proj, lb_logits, out_norm, masks, tri)


OUT_TM = 512


def _out_proj_kernel(a_ref, r_ref, x_ref, g_ref, w_ref, o_ref):
    a = a_ref[...]
    an = (a * _rms_scale(a) * g_ref[...]).astype(BF16)
    o_ref[...] = (x_ref[...] + _dot(an, w_ref[:MLA_WIDTH, :].astype(BF16))
                  + _dot(r_ref[...].astype(BF16), w_ref[MLA_WIDTH:, :].astype(BF16)))


def _out_proj(a, r, x, gain, w):
    t = x.shape[0]
    return pl.pallas_call(
        _out_proj_kernel,
        out_shape=jax.ShapeDtypeStruct((t, D_MODEL), F32),
        grid=(t // OUT_TM,),
        in_specs=[
            pl.BlockSpec((OUT_TM, MLA_WIDTH), lambda i: (i, 0)),
            pl.BlockSpec((OUT_TM, HG_WIDTH), lambda i: (i, 0)),
            pl.BlockSpec((OUT_TM, D_MODEL), lambda i: (i, 0)),
            pl.BlockSpec((1, MLA_WIDTH), lambda i: (0, 0)),
            pl.BlockSpec(w.shape, lambda i: (0, 0), pipeline_mode=pl.Buffered(1)),
        ],
        out_specs=pl.BlockSpec((OUT_TM, D_MODEL), lambda i: (i, 0)),
        compiler_params=_params("parallel"),
        name="out_proj",
    )(a, r, x, gain, w)


def _mem_kv_kernel(m_ref, g_ref, w_ref, kn_ref, k_ref, v_ref):
    m = m_ref[...]
    mn = (m * _rms_scale(m) * g_ref[...]).astype(BF16)
    kv = _dot(mn, w_ref[...].astype(BF16))
    for h in range(MEM_HEADS):
        k = kv[:, 2 * h * MEM_HD:(2 * h + 1) * MEM_HD]
        k_ref[:, h * MEM_HD:(h + 1) * MEM_HD] = (k * _rms_scale(k) * kn_ref[...]).astype(BF16)
        v_ref[:, h * MEM_HD:(h + 1) * MEM_HD] = kv[:, (2 * h + 1) * MEM_HD:(2 * h + 2) * MEM_HD].astype(BF16)


def _mem_kv(mem, gain, w, k_norm, batch, mem_len):
    return pl.pallas_call(
        _mem_kv_kernel,
        out_shape=(jax.ShapeDtypeStruct((batch * mem_len, MEM_WIDTH), BF16),
                   jax.ShapeDtypeStruct((batch * mem_len, MEM_WIDTH), BF16)),
        grid=(batch,),
        in_specs=[
            pl.BlockSpec((mem_len, D_MODEL), lambda b: (b, 0)),
            pl.BlockSpec((1, D_MODEL), lambda b: (0, 0)),
            pl.BlockSpec(w.shape, lambda b: (0, 0)),
            pl.BlockSpec((1, MEM_HD), lambda b: (0, 0)),
        ],
        out_specs=(pl.BlockSpec((mem_len, MEM_WIDTH), lambda b: (b, 0)),
                   pl.BlockSpec((mem_len, MEM_WIDTH), lambda b: (b, 0))),
        compiler_params=_params("parallel"),
        name="mem_kv",
    )(mem, gain, w, k_norm)


XA_TM = 512


def _xattn_kernel(x_ref, g_ref, wq_ref, qn_ref, k_ref, v_ref, wo_ref, o_ref, att_scr):
    x = x_ref[...]
    h = (x * _rms_scale(x) * g_ref[...]).astype(BF16)
    q = _dot(h, wq_ref[...].astype(BF16))
    scale = MEM_HD ** -0.5
    for hd in range(MEM_HEADS):
        cols = slice(hd * MEM_HD, (hd + 1) * MEM_HD)
        qh = q[:, cols]
        qh = (qh * _rms_scale(qh) * qn_ref[...]).astype(BF16)
        s = _dot_nt(qh, k_ref[:, cols]) * scale
        p = jnp.exp(s - jnp.max(s, axis=-1, keepdims=True))
        p = p / jnp.sum(p, axis=-1, keepdims=True)
        att_scr[:, cols] = _dot(p.astype(BF16), v_ref[:, cols]).astype(BF16)
    o_ref[...] = x + _dot(att_scr[...], wo_ref[...].astype(BF16))


def _xattn(x, gain, wq, q_norm, kx, vx, wo, seq, mem_len):
    t = x.shape[0]
    per_batch = seq // XA_TM
    return pl.pallas_call(
        _xattn_kernel,
        out_shape=jax.ShapeDtypeStruct((t, D_MODEL), F32),
        grid=(t // XA_TM,),
        in_specs=[
            pl.BlockSpec((XA_TM, D_MODEL), lambda i: (i, 0)),
            pl.BlockSpec((1, D_MODEL), lambda i: (0, 0)),
            pl.BlockSpec(wq.shape, lambda i: (0, 0)),
            pl.BlockSpec((1, MEM_HD), lambda i: (0, 0)),
            pl.BlockSpec((mem_len, MEM_WIDTH), lambda i: (i // per_batch, 0)),
            pl.BlockSpec((mem_len, MEM_WIDTH), lambda i: (i // per_batch, 0)),
            pl.BlockSpec(wo.shape, lambda i: (0, 0)),
        ],
        out_specs=pl.BlockSpec((XA_TM, D_MODEL), lambda i: (i, 0)),
        scratch_shapes=[pltpu.VMEM((XA_TM, MEM_WIDTH), BF16)],
        compiler_params=_params("parallel"),
        name="xattn",
    )(x, gain, wq, q_norm, kx, vx, wo)


def _pe_pair(x1, x2):
    z = jnp.zeros(x1.shape[:-1] + (LANES - MLA_ROPE,), x1.dtype)
    return jnp.concatenate([x1, x2, z], axis=-1), jnp.concatenate([x2, x1, z], axis=-1)


W_IN_TR = 256


def _w_in_prep_kernel(w_ref, o_ref):
    half = MLA_ROPE // 2
    kpe0 = MLA_Q_RANK + MLA_KV_RANK
    hg0 = kpe0 + MLA_ROPE
    lane = lax.broadcasted_iota(jnp.int32, (1, LANES), 1)
    o_ref[:, :kpe0] = w_ref[:, :kpe0].astype(BF16)
    o_ref[:, kpe0:COL_KPE] = w_ref[:, hg0:].astype(BF16)
    t = w_ref[:, kpe0:kpe0 + LANES]
    o_ref[:, COL_KPE:COL_KPE_SW] = jnp.where(lane < MLA_ROPE, t, 0.0).astype(BF16)
    sw = jnp.where(lane < half, pltpu.roll(t, LANES - half, 1), pltpu.roll(t, half, 1))
    o_ref[:, COL_KPE_SW:] = jnp.where(lane < MLA_ROPE, sw, 0.0).astype(BF16)


def _prep_w_in(w_in, layer):
    _, k, n = w_in.shape
    return pl.pallas_call(
        _w_in_prep_kernel,
        out_shape=jax.ShapeDtypeStruct((k, IN_COLS_PAD), BF16),
        grid=(k // W_IN_TR,),
        in_specs=[pl.BlockSpec((None, W_IN_TR, n), lambda i: (layer, i, 0))],
        out_specs=pl.BlockSpec((W_IN_TR, IN_COLS_PAD), lambda i: (i, 0)),
        compiler_params=_params("parallel"),
        name="w_in_prep",
    )(w_in)


def _prep_w_q_up(w):
    w = w.astype(BF16).reshape(MLA_Q_RANK, MLA_HEADS, MLA_QK)
    half = MLA_ROPE // 2
    pe, pe_sw = _pe_pair(w[..., MLA_NOPE:MLA_NOPE + half], w[..., MLA_NOPE + half:])
    return jnp.concatenate([w[..., :MLA_NOPE], pe, pe_sw], axis=-1).reshape(MLA_Q_RANK, MLA_HEADS * Q_UP_PER_HEAD)


def _pe_gains(norm):
    half = MLA_ROPE // 2
    g1 = norm[MLA_NOPE:MLA_NOPE + half]
    g2 = norm[MLA_NOPE + half:]
    ga, gb = _pe_pair(g1, g2)
    sign = jnp.concatenate([-jnp.ones((half,), F32), jnp.ones((LANES - half,), F32)])
    return ga[None, :], (gb * sign)[None, :]


def kernel(x, mem, positions, ffn1_norm, ffn1_w_gate, ffn1_w_up, ffn1_w_down, mix_norm, w_in, mla_q_a_norm, mla_w_q_up, mla_kv_a_norm, mla_w_kv_up, mla_q_norm, mla_k_norm, mla_out_norm, hg_lb_logits, hg_out_norm, w_out, xattn_norm, mem_norm, xattn_w_q, xattn_w_kv, xattn_q_norm, xattn_k_norm, xattn_w_o, ffn2_norm, ffn2_w_gate, ffn2_w_up, ffn2_w_down):
    batch, seq, _ = x.shape
    mem_len = mem.shape[1]
    depth = ffn1_norm.shape[0]
    assert depth == 1 and seq % ATT_T == 0 and seq % XA_TM == 0
    t = batch * seq
    xt = x.reshape(t, D_MODEL)
    pos = positions.reshape(t, 1)
    half = MLA_ROPE // 2
    inv_freq = ROPE_BASE ** (-np.arange(half, dtype=np.float32) / half)
    invf = jnp.asarray(np.concatenate([inv_freq, inv_freq, np.zeros(LANES - MLA_ROPE, np.float32)])[None, :])
    l = 0

    xt = _ffn(xt, ffn1_norm[l][None, :], ffn1_w_gate[l], ffn1_w_up[l], ffn1_w_down[l])

    proj = _in_proj(xt, mix_norm[l][None, :], _prep_w_in(w_in, l))
    qpa, qpb = _pe_gains(mla_q_norm[l])
    kpa, kpb = _pe_gains(mla_k_norm[l])
    q, k, v = _mla_prep(proj, pos, invf, mla_q_a_norm[l][None, :], mla_kv_a_norm[l][None, :],
                        _prep_w_q_up(mla_w_q_up[l]), mla_w_kv_up[l],
                        mla_q_norm[l][None, :MLA_NOPE], qpa, qpb,
                        mla_k_norm[l][None, :MLA_NOPE], kpa, kpb)
    a = _mla_attn(q, k, v, batch, seq)
    r = _hgrn(proj, hg_lb_logits, hg_out_norm[l][None, :], batch, seq)
    xt = _out_proj(a, r, xt, mla_out_norm[l][None, :], w_out[l])

    kx, vx = _mem_kv(mem.reshape(batch * mem_len, D_MODEL), mem_norm[l][None, :],
                     xattn_w_kv[l], xattn_k_norm[l][None, :], batch, mem_len)
    xt = _xattn(xt, xattn_norm[l][None, :], xattn_w_q[l], xattn_q_norm[l][None, :],
                kx, vx, xattn_w_o[l], seq, mem_len)

    xt = _ffn(xt, ffn2_norm[l][None, :], ffn2_w_gate[l], ffn2_w_up[l], ffn2_w_down[l])
    return xt.reshape(batch, seq, D_MODEL)
```

```python
import functools

import numpy as np
import jax
import jax.numpy as jnp
from jax import lax
from jax.experimental import pallas as pl
from jax.experimental.pallas import tpu as pltpu

F32 = jnp.float32
BF16 = jnp.bfloat16

EPS = 1e-6
ROPE_BASE = 10000.0
LANES = 128
SUBLANES = 8

D_MODEL = 2048
D_FF = 5504
FFN_TM = 1024
FFN_TF = 256

MLA_HEADS = 8
MLA_NOPE = 128
MLA_ROPE = 64
MLA_QK = MLA_NOPE + MLA_ROPE
MLA_V = 128
MLA_Q_RANK = 512
MLA_KV_RANK = 256
MLA_WIDTH = MLA_HEADS * MLA_V
QK_PAD = 256
Q_UP_PER_HEAD = 384
ATT_Q_SCALE = float(MLA_QK ** -0.5 * np.log2(np.e))

HG_HEADS = 8
HG_D = 128
HG_CHUNK = 64
HG_LEVELS = (32, 16, 8, 4, 2, 1)
HG_UNROLL = 8
HG_GROUP = HG_CHUNK * HG_UNROLL
HG_SLAB = 256
HG_WIDTH = HG_HEADS * HG_D

MEM_HEADS = 4
MEM_HD = 128
MEM_WIDTH = MEM_HEADS * MEM_HD

COL_CQ = 0
COL_CKV = 512
COL_HQ = 768
COL_HF = COL_HQ + HG_WIDTH
COL_HI = COL_HF + HG_WIDTH
COL_HG = COL_HI + HG_WIDTH
COL_KPE = COL_HG + HG_WIDTH
COL_KPE_SW = COL_KPE + LANES
IN_COLS_PAD = COL_KPE_SW + LANES

VMEM_LIMIT = 56 * 1024 * 1024


def _params(*sem):
    return pltpu.CompilerParams(dimension_semantics=sem, vmem_limit_bytes=VMEM_LIMIT)


def _rms_scale(x):
    return lax.rsqrt(jnp.mean(x * x, axis=-1, keepdims=True) + EPS)


def _sigmoid(x):
    return 1.0 / (1.0 + jnp.exp(-x))


def _dot(a, b):
    return jnp.dot(a, b, preferred_element_type=F32)


def _dot_nt(a, b):
    return lax.dot_general(a, b, (((1,), (1,)), ((), ())), preferred_element_type=F32)


def _dot_tn(a, b):
    return lax.dot_general(a, b, (((0,), (0,)), ((), ())), preferred_element_type=F32)


def _ffn_kernel(x_ref, g_ref, wg_ref, wu_ref, wd_ref, o_ref, h_scr):
    @pl.when(pl.program_id(1) == 0)
    def _():
        x = x_ref[...]
        h_scr[...] = (x * _rms_scale(x) * g_ref[...]).astype(BF16)
        o_ref[...] = x

    h = h_scr[...]
    gate = _dot(h, wg_ref[...].astype(BF16))
    up = _dot(h, wu_ref[...].astype(BF16))
    a = 0.5 * gate * _sigmoid(gate) * up
    f0 = pl.program_id(1) * FFN_TF
    a_ok = f0 + lax.broadcasted_iota(jnp.int32, (1, FFN_TF), 1) < D_FF
    w_ok = f0 + lax.broadcasted_iota(jnp.int32, (FFN_TF, 1), 0) < D_FF
    a = jnp.where(a_ok, a, 0.0).astype(BF16)
    wd = jnp.where(w_ok, wd_ref[...], 0.0).astype(BF16)
    o_ref[...] += _dot(a, wd)


def _ffn(x, gain, wg, wu, wd):
    t = x.shape[0]
    return pl.pallas_call(
        _ffn_kernel,
        out_shape=jax.ShapeDtypeStruct((t, D_MODEL), F32),
        grid=(t // FFN_TM, pl.cdiv(D_FF, FFN_TF)),
        in_specs=[
            pl.BlockSpec((FFN_TM, D_MODEL), lambda i, f: (i, 0)),
            pl.BlockSpec((1, D_MODEL), lambda i, f: (0, 0)),
            pl.BlockSpec((D_MODEL, FFN_TF), lambda i, f: (0, f)),
            pl.BlockSpec((D_MODEL, FFN_TF), lambda i, f: (0, f)),
            pl.BlockSpec((FFN_TF, D_MODEL), lambda i, f: (f, 0)),
        ],
        out_specs=pl.BlockSpec((FFN_TM, D_MODEL), lambda i, f: (i, 0)),
        scratch_shapes=[pltpu.VMEM((FFN_TM, D_MODEL), BF16)],
        compiler_params=_params("parallel", "arbitrary"),
        name="ffn",
    )(x, gain, wg, wu, wd)


PROJ_TM = 1024
PROJ_TN = 512


def _in_proj_kernel(x_ref, g_ref, w_ref, o_ref, h_scr):
    @pl.when(pl.program_id(1) == 0)
    def _():
        x = x_ref[...]
        h_scr[...] = (x * _rms_scale(x) * g_ref[...]).astype(BF16)

    o_ref[...] = _dot_nt(h_scr[...], w_ref[...])


def _in_proj(x, gain, w_t):
    t = x.shape[0]
    n = w_t.shape[0]
    return pl.pallas_call(
        _in_proj_kernel,
        out_shape=jax.ShapeDtypeStruct((t, n), F32),
        grid=(t // PROJ_TM, n // PROJ_TN),
        in_specs=[
            pl.BlockSpec((PROJ_TM, D_MODEL), lambda i, j: (i, 0)),
            pl.BlockSpec((1, D_MODEL), lambda i, j: (0, 0)),
            pl.BlockSpec((PROJ_TN, D_MODEL), lambda i, j: (j, 0)),
        ],
        out_specs=pl.BlockSpec((PROJ_TM, PROJ_TN), lambda i, j: (i, j)),
        scratch_shapes=[pltpu.VMEM((PROJ_TM, D_MODEL), BF16)],
        compiler_params=_params("parallel", "arbitrary"),
        name="in_proj",
    )(x, gain, w_t)


PREP_TM = 512


def _mla_prep_kernel(cq_ref, ckv_ref, kpe_ref, kpesw_ref, pos_ref, invf_ref,
                     qa_ref, kva_ref, wq_ref, wkv_ref,
                     qn_ref, qpa_ref, qpb_ref, kn_ref, kpa_ref, kpb_ref,
                     q_ref, k_ref, vt_ref):
    ang = pos_ref[...].astype(F32) * invf_ref[...]
    cos = jnp.cos(ang)
    sin = jnp.sin(ang)

    cq = cq_ref[...]
    cqn = (cq * _rms_scale(cq) * qa_ref[...]).astype(BF16)
    qall = _dot(cqn, wq_ref[...])
    q_cos = qpa_ref[...] * cos
    q_sin = qpb_ref[...] * sin
    for h in range(MLA_HEADS):
        base = h * Q_UP_PER_HEAD
        qn = qall[:, base:base + LANES]
        y = qall[:, base + LANES:base + 2 * LANES]
        ysw = qall[:, base + 2 * LANES:base + 3 * LANES]
        ss = jnp.sum(qn * qn, axis=-1, keepdims=True) + jnp.sum(y * y, axis=-1, keepdims=True)
        r = lax.rsqrt(ss * (1.0 / MLA_QK) + EPS) * ATT_Q_SCALE
        q_ref[:, h * QK_PAD:h * QK_PAD + LANES] = (qn * r * qn_ref[...]).astype(BF16)
        q_ref[:, h * QK_PAD + LANES:(h + 1) * QK_PAD] = ((y * q_cos + ysw * q_sin) * r).astype(BF16)

    ckv = ckv_ref[...]
    ckvn = (ckv * _rms_scale(ckv) * kva_ref[...]).astype(BF16)
    kvall = _dot(ckvn, wkv_ref[...].astype(BF16))
    yk = kpe_ref[...]
    kr = yk * (kpa_ref[...] * cos) + kpesw_ref[...] * (kpb_ref[...] * sin)
    ss_pe = jnp.sum(yk * yk, axis=-1, keepdims=True)
    for h in range(MLA_HEADS):
        kn = kvall[:, h * 2 * LANES:h * 2 * LANES + LANES]
        ss = jnp.sum(kn * kn, axis=-1, keepdims=True) + ss_pe
        r = lax.rsqrt(ss * (1.0 / MLA_QK) + EPS)
        k_ref[:, h * QK_PAD:h * QK_PAD + LANES] = (kn * r * kn_ref[...]).astype(BF16)
        k_ref[:, h * QK_PAD + LANES:(h + 1) * QK_PAD] = (kr * r).astype(BF16)
        vt_ref[h * MLA_V:(h + 1) * MLA_V, :] = kvall[:, h * 2 * LANES + LANES:(h + 1) * 2 * LANES].T.astype(BF16)


def _mla_prep(proj, pos, invf, qa, kva, wq, wkv, qn, qpa, qpb, kn, kpa, kpb):
    t = proj.shape[0]
    tm = PREP_TM
    row = lambda i: (0, 0)
    vec = lambda n: pl.BlockSpec((1, n), row)
    return pl.pallas_call(
        _mla_prep_kernel,
        out_shape=(jax.ShapeDtypeStruct((t, MLA_HEADS * QK_PAD), BF16),
                   jax.ShapeDtypeStruct((t, MLA_HEADS * QK_PAD), BF16),
                   jax.ShapeDtypeStruct((MLA_WIDTH, t), BF16)),
        grid=(t // tm,),
        in_specs=[
            pl.BlockSpec((tm, MLA_Q_RANK), lambda i: (i, COL_CQ // MLA_Q_RANK)),
            pl.BlockSpec((tm, MLA_KV_RANK), lambda i: (i, COL_CKV // MLA_KV_RANK)),
            pl.BlockSpec((tm, LANES), lambda i: (i, COL_KPE // LANES)),
            pl.BlockSpec((tm, LANES), lambda i: (i, COL_KPE_SW // LANES)),
            pl.BlockSpec((tm, 1), lambda i: (i, 0)),
            vec(LANES),
            vec(MLA_Q_RANK), vec(MLA_KV_RANK),
            pl.BlockSpec(wq.shape, row), pl.BlockSpec(wkv.shape, row),
            vec(LANES), vec(LANES), vec(LANES), vec(LANES), vec(LANES), vec(LANES),
        ],
        out_specs=(pl.BlockSpec((tm, MLA_HEADS * QK_PAD), lambda i: (i, 0)),
                   pl.BlockSpec((tm, MLA_HEADS * QK_PAD), lambda i: (i, 0)),
                   pl.BlockSpec((MLA_WIDTH, tm), lambda i: (0, i))),
        compiler_params=_params("parallel"),
        name="mla_prep",
    )(proj, proj, proj, proj, pos, invf, qa, kva, wq, wkv, qn, qpa, qpb, kn, kpa, kpb)


ATT_T = 256
ATT_AHEAD = 3


def _mla_attn_kernel(q_ref, k_ref, vt_ref, o_ref):
    kpos = lax.broadcasted_iota(jnp.int32, (ATT_T, ATT_T), 0)
    qpos = lax.broadcasted_iota(jnp.int32, (ATT_T, ATT_T), 1)
    nq = q_ref.shape[0] // ATT_T

    def scores(i):
        lo, hi = i * ATT_T, (i + 1) * ATT_T
        q = q_ref[lo:hi, :]
        s_diag = jnp.where(kpos <= qpos, _dot_nt(k_ref[lo:hi, :], q), -jnp.inf)
        s_off = _dot_nt(k_ref[0:lo, :], q) if i > 0 else None
        return s_diag, s_off

    def finish(i, s_diag, s_off):
        lo, hi = i * ATT_T, (i + 1) * ATT_T
        m = jnp.max(s_diag, axis=0, keepdims=True)
        if i > 0:
            m = jnp.maximum(m, jnp.max(s_off, axis=0, keepdims=True))
        p = jnp.exp2(s_diag - m)
        l = jnp.sum(p, axis=0, keepdims=True)
        acc = _dot(vt_ref[:, lo:hi], p.astype(BF16))
        if i > 0:
            p = jnp.exp2(s_off - m)
            l = l + jnp.sum(p, axis=0, keepdims=True)
            acc = acc + _dot(vt_ref[:, 0:lo], p.astype(BF16))
        o_ref[lo:hi, :] = (acc / l).T

    order = list(reversed(range(nq)))
    pending = [scores(i) for i in order[:ATT_AHEAD]]
    for n, i in enumerate(order):
        if n + ATT_AHEAD < nq:
            pending.append(scores(order[n + ATT_AHEAD]))
        finish(i, *pending.pop(0))


def _mla_attn(q, k, vt, batch, seq):
    return pl.pallas_call(
        _mla_attn_kernel,
        out_shape=jax.ShapeDtypeStruct((batch * seq, MLA_WIDTH), F32),
        grid=(batch, MLA_HEADS),
        in_specs=[
            pl.BlockSpec((seq, QK_PAD), lambda b, h: (b, h)),
            pl.BlockSpec((seq, QK_PAD), lambda b, h: (b, h)),
            pl.BlockSpec((MLA_V, seq), lambda b, h: (h, b)),
        ],
        out_specs=pl.BlockSpec((seq, MLA_V), lambda b, h: (b, h)),
        compiler_params=_params("parallel", "parallel"),
        name="mla_attn",
    )(q, k, vt)


def _hgrn_masks():
    t = np.arange(HG_CHUNK)[:, None]
    s = np.arange(HG_CHUNK)[None, :]
    masks = [((t // c) % 2 == 1) & ((s // c) == (t // c) - 1) for c in HG_LEVELS]
    masks.append(t == s)
    return np.stack(masks).astype(np.float32)


def _hgrn_kernel(q_ref, f_ref, i_ref, g_ref, lbl_ref, gn_ref, msk_ref, tri_ref, o_ref, b_scr):
    lbl = lbl_ref[...]
    e = jnp.exp(lbl - jnp.max(lbl, axis=0, keepdims=True))
    lb = e[0:1, :] / jnp.sum(e, axis=0, keepdims=True)
    gn = gn_ref[...]
    tri = tri_ref[...]
    sub = lax.broadcasted_iota(jnp.int32, (SUBLANES, LANES), 0)
    zeros8 = jnp.zeros((SUBLANES, LANES), F32)
    per_chunk = HG_CHUNK // SUBLANES
    chunk_rows = [slice(u * HG_CHUNK, (u + 1) * HG_CHUNK) for u in range(HG_UNROLL)]

    def sel(bounds, vals):
        out = vals[-1]
        for bound, val in zip(reversed(bounds), reversed(vals[:-1])):
            out = jnp.where(sub < bound, val, out)
        return out

    def group(gi, st):
        rows = pl.ds(pl.multiple_of(gi * HG_GROUP, HG_GROUP), HG_GROUP)
        q = q_ref[rows, :]
        v = i_ref[rows, :].astype(BF16)
        f = lb + (1.0 - lb) * _sigmoid(f_ref[rows, :])
        lg = jnp.log2(f)
        k = 1.0 - f

        l1 = lg.astype(BF16)
        l2 = (lg - l1.astype(F32)).astype(BF16)
        slabs = [slice(i, i + HG_SLAB) for i in range(0, HG_GROUP, HG_SLAB)]
        b = jnp.concatenate([_dot(tri, l1[r]) + _dot(tri, l2[r]) for r in slabs], axis=0)
        b_scr[...] = b

        cache = {}

        def rowb(u, r):
            if r < 0:
                return zeros8
            if (u, r) not in cache:
                cache[u, r] = jnp.broadcast_to(b_scr[u * HG_CHUNK + r:u * HG_CHUNK + r + 1, :], (SUBLANES, LANES))
            return cache[u, r]

        def build(fn):
            return jnp.concatenate([fn(u, j) for u in range(HG_UNROLL) for j in range(per_chunk)], axis=0)

        a = [jnp.sum(q[r] * k[r], axis=-1, keepdims=True) * msk_ref[len(HG_LEVELS)] for r in chunk_rows]
        for ci, c in enumerate(HG_LEVELS):
            if c >= SUBLANES:
                m = c // SUBLANES
                bs = build(lambda u, j: rowb(u, SUBLANES * (j // m) * m - 1))
                be = build(lambda u, j: rowb(u, SUBLANES * ((j // m) * m + m) - 1))
            elif c == 4:
                bs = build(lambda u, j: sel((4,), (rowb(u, 8 * j - 1), rowb(u, 8 * j + 3))))
                be = build(lambda u, j: sel((4,), (rowb(u, 8 * j + 3), rowb(u, 8 * j + 7))))
            elif c == 2:
                bs = build(lambda u, j: sel((2, 4, 6), (rowb(u, 8 * j - 1), rowb(u, 8 * j + 1),
                                                         rowb(u, 8 * j + 3), rowb(u, 8 * j + 5))))
                be = build(lambda u, j: sel((2, 4, 6), (rowb(u, 8 * j + 1), rowb(u, 8 * j + 3),
                                                         rowb(u, 8 * j + 5), rowb(u, 8 * j + 7))))
            if c == 1:
                qd = (q * f).astype(BF16)
                kd = k.astype(BF16)
            else:
                qd = (q * jnp.exp2(b - bs)).astype(BF16)
                kd = (k * jnp.exp2(be - b)).astype(BF16)
            a = [a[u] + _dot_nt(qd[r], kd[r]) * msk_ref[ci] for u, r in enumerate(chunk_rows)]

        q_exp = (q * jnp.exp2(b)).astype(BF16)
        b_last = build(lambda u, j: rowb(u, HG_CHUNK - 1))
        k_dec = (k * jnp.exp2(b_last - b)).astype(BF16)
        o_intra = [_dot(a[u].astype(BF16), v[r]) for u, r in enumerate(chunk_rows)]
        st_add = [_dot_tn(v[r], k_dec[r]) for r in chunk_rows]

        o = []
        for u, r in enumerate(chunk_rows):
            o.append(o_intra[u] + _dot_nt(q_exp[r], st.astype(BF16)))
            st = jnp.exp2(rowb(u, HG_CHUNK - 1)[0:1, :]) * st + st_add[u]
        o = jnp.concatenate(o, axis=0)

        g = g_ref[rows, :]
        o_ref[rows, :] = o * _rms_scale(o) * gn * (g * _sigmoid(g))
        return st

    lax.fori_loop(0, q_ref.shape[0] // HG_GROUP, group, jnp.zeros((HG_D, HG_D), F32))


def _hgrn(proj, lb_logits, out_norm, batch, seq):
    hblk = lambda col: pl.BlockSpec((seq, HG_D), lambda b, h: (b, col // HG_D + h))
    masks = jnp.asarray(_hgrn_masks())
    tri = np.kron(np.eye(HG_SLAB // HG_CHUNK), np.tril(np.ones((HG_CHUNK, HG_CHUNK))))
    tri = jnp.asarray(tri.astype(np.float32), dtype=BF16)
    nlev = masks.shape[0]
    return pl.pallas_call(
        _hgrn_kernel,
        out_shape=jax.ShapeDtypeStruct((batch * seq, HG_WIDTH), F32),
        grid=(batch, HG_HEADS),
        in_specs=[
            hblk(COL_HQ), hblk(COL_HF), hblk(COL_HI), hblk(COL_HG),
            pl.BlockSpec((lb_logits.shape[0], HG_D), lambda b, h: (0, h)),
            pl.BlockSpec((1, HG_D), lambda b, h: (0, 0)),
            pl.BlockSpec((nlev, HG_CHUNK, HG_CHUNK), lambda b, h: (0, 0, 0)),
            pl.BlockSpec((HG_SLAB, HG_SLAB), lambda b, h: (0, 0)),
        ],
        out_specs=pl.BlockSpec((seq, HG_D), lambda b, h: (b, h)),
        scratch_shapes=[pltpu.VMEM((HG_GROUP, HG_D), F32)],
        compiler_params=_params("parallel", "parallel"),
        name="hgrn2",
    )(proj, proj, proj, proj, lb_logits, out_norm, masks, tri)


OUT_TM = 512


def _out_proj_kernel(a_ref, r_ref, x_ref, g_ref, w_ref, o_ref):
    a = a_ref[...]
    an = (a * _rms_scale(a) * g_ref[...]).astype(BF16)
    o_ref[...] = (x_ref[...] + _dot(an, w_ref[:MLA_WIDTH, :].astype(BF16))
                  + _dot(r_ref[...].astype(BF16), w_ref[MLA_WIDTH:, :].astype(BF16)))


def _out_proj(a, r, x, gain, w):
    t = x.shape[0]
    return pl.pallas_call(
        _out_proj_kernel,
        out_shape=jax.ShapeDtypeStruct((t, D_MODEL), F32),
        grid=(t // OUT_TM,),
        in_specs=[
            pl.BlockSpec((OUT_TM, MLA_WIDTH), lambda i: (i, 0)),
            pl.BlockSpec((OUT_TM, HG_WIDTH), lambda i: (i, 0)),
            pl.BlockSpec((OUT_TM, D_MODEL), lambda i: (i, 0)),
            pl.BlockSpec((1, MLA_WIDTH), lambda i: (0, 0)),
            pl.BlockSpec(w.shape, lambda i: (0, 0), pipeline_mode=pl.Buffered(1)),
        ],
        out_specs=pl.BlockSpec((OUT_TM, D_MODEL), lambda i: (i, 0)),
        compiler_params=_params("parallel"),
        name="out_proj",
    )(a, r, x, gain, w)


def _mem_kv_kernel(m_ref, g_ref, w_ref, kn_ref, k_ref, v_ref):
    m = m_ref[...]
    mn = (m * _rms_scale(m) * g_ref[...]).astype(BF16)
    kv = _dot(mn, w_ref[...].astype(BF16))
    for h in range(MEM_HEADS):
        k = kv[:, 2 * h * MEM_HD:(2 * h + 1) * MEM_HD]
        k_ref[:, h * MEM_HD:(h + 1) * MEM_HD] = (k * _rms_scale(k) * kn_ref[...]).astype(BF16)
        v_ref[:, h * MEM_HD:(h + 1) * MEM_HD] = kv[:, (2 * h + 1) * MEM_HD:(2 * h + 2) * MEM_HD].astype(BF16)


def _mem_kv(mem, gain, w, k_norm, batch, mem_len):
    return pl.pallas_call(
        _mem_kv_kernel,
        out_shape=(jax.ShapeDtypeStruct((batch * mem_len, MEM_WIDTH), BF16),
                   jax.ShapeDtypeStruct((batch * mem_len, MEM_WIDTH), BF16)),
        grid=(batch,),
        in_specs=[
            pl.BlockSpec((mem_len, D_MODEL), lambda b: (b, 0)),
            pl.BlockSpec((1, D_MODEL), lambda b: (0, 0)),
            pl.BlockSpec(w.shape, lambda b: (0, 0)),
            pl.BlockSpec((1, MEM_HD), lambda b: (0, 0)),
        ],
        out_specs=(pl.BlockSpec((mem_len, MEM_WIDTH), lambda b: (b, 0)),
                   pl.BlockSpec((mem_len, MEM_WIDTH), lambda b: (b, 0))),
        compiler_params=_params("parallel"),
        name="mem_kv",
    )(mem, gain, w, k_norm)


XA_TM = 512


def _xattn_kernel(x_ref, g_ref, wq_ref, qn_ref, k_ref, v_ref, wo_ref, o_ref, att_scr):
    x = x_ref[...]
    h = (x * _rms_scale(x) * g_ref[...]).astype(BF16)
    q = _dot(h, wq_ref[...].astype(BF16))
    scale = MEM_HD ** -0.5
    for hd in range(MEM_HEADS):
        cols = slice(hd * MEM_HD, (hd + 1) * MEM_HD)
        qh = q[:, cols]
        qh = (qh * _rms_scale(qh) * qn_ref[...]).astype(BF16)
        s = _dot_nt(qh, k_ref[:, cols]) * scale
        p = jnp.exp(s - jnp.max(s, axis=-1, keepdims=True))
        p = p / jnp.sum(p, axis=-1, keepdims=True)
        att_scr[:, cols] = _dot(p.astype(BF16), v_ref[:, cols]).astype(BF16)
    o_ref[...] = x + _dot(att_scr[...], wo_ref[...].astype(BF16))


def _xattn(x, gain, wq, q_norm, kx, vx, wo, seq, mem_len):
    t = x.shape[0]
    per_batch = seq // XA_TM
    return pl.pallas_call(
        _xattn_kernel,
        out_shape=jax.ShapeDtypeStruct((t, D_MODEL), F32),
        grid=(t // XA_TM,),
        in_specs=[
            pl.BlockSpec((XA_TM, D_MODEL), lambda i: (i, 0)),
            pl.BlockSpec((1, D_MODEL), lambda i: (0, 0)),
            pl.BlockSpec(wq.shape, lambda i: (0, 0)),
            pl.BlockSpec((1, MEM_HD), lambda i: (0, 0)),
            pl.BlockSpec((mem_len, MEM_WIDTH), lambda i: (i // per_batch, 0)),
            pl.BlockSpec((mem_len, MEM_WIDTH), lambda i: (i // per_batch, 0)),
            pl.BlockSpec(wo.shape, lambda i: (0, 0)),
        ],
        out_specs=pl.BlockSpec((XA_TM, D_MODEL), lambda i: (i, 0)),
        scratch_shapes=[pltpu.VMEM((XA_TM, MEM_WIDTH), BF16)],
        compiler_params=_params("parallel"),
        name="xattn",
    )(x, gain, wq, q_norm, kx, vx, wo)


def _pe_pair(x1, x2):
    z = jnp.zeros(x1.shape[:-1] + (LANES - MLA_ROPE,), x1.dtype)
    return jnp.concatenate([x1, x2, z], axis=-1), jnp.concatenate([x2, x1, z], axis=-1)


W_IN_TC = 512


def _w_in_prep_kernel(w_ref, o_ref):
    half = MLA_ROPE // 2
    kpe0 = MLA_Q_RANK + MLA_KV_RANK
    hg0 = kpe0 + MLA_ROPE
    zeros = jnp.zeros((LANES - MLA_ROPE, o_ref.shape[1]), BF16)
    o_ref[:kpe0, :] = w_ref[:kpe0, :].astype(BF16)
    o_ref[kpe0:COL_KPE, :] = w_ref[hg0:, :].astype(BF16)
    x1 = w_ref[kpe0:kpe0 + half, :].astype(BF16)
    x2 = w_ref[kpe0 + half:hg0, :].astype(BF16)
    o_ref[COL_KPE:COL_KPE + half, :] = x1
    o_ref[COL_KPE + half:COL_KPE + MLA_ROPE, :] = x2
    o_ref[COL_KPE + MLA_ROPE:COL_KPE_SW, :] = zeros
    o_ref[COL_KPE_SW:COL_KPE_SW + half, :] = x2
    o_ref[COL_KPE_SW + half:COL_KPE_SW + MLA_ROPE, :] = x1
    o_ref[COL_KPE_SW + MLA_ROPE:, :] = zeros


def _prep_w_in(w_in, layer):
    w_t = jnp.swapaxes(w_in, 1, 2)
    _, n, k = w_t.shape
    return pl.pallas_call(
        _w_in_prep_kernel,
        out_shape=jax.ShapeDtypeStruct((IN_COLS_PAD, k), BF16),
        grid=(k // W_IN_TC,),
        in_specs=[pl.BlockSpec((None, n, W_IN_TC), lambda i: (layer, 0, i))],
        out_specs=pl.BlockSpec((IN_COLS_PAD, W_IN_TC), lambda i: (0, i)),
        compiler_params=_params("parallel"),
        name="w_in_prep",
    )(w_t)


def _prep_w_q_up(w):
    w = w.astype(BF16).reshape(MLA_Q_RANK, MLA_HEADS, MLA_QK)
    half = MLA_ROPE // 2
    pe, pe_sw = _pe_pair(w[..., MLA_NOPE:MLA_NOPE + half], w[..., MLA_NOPE + half:])
    return jnp.concatenate([w[..., :MLA_NOPE], pe, pe_sw], axis=-1).reshape(MLA_Q_RANK, MLA_HEADS * Q_UP_PER_HEAD)


def _pe_gains(norm):
    half = MLA_ROPE // 2
    g1 = norm[MLA_NOPE:MLA_NOPE + half]
    g2 = norm[MLA_NOPE + half:]
    ga, gb = _pe_pair(g1, g2)
    sign = jnp.concatenate([-jnp.ones((half,), F32), jnp.ones((LANES - half,), F32)])
    return ga[None, :], (gb * sign)[None, :]


def kernel(x, mem, positions, ffn1_norm, ffn1_w_gate, ffn1_w_up, ffn1_w_down, mix_norm, w_in, mla_q_a_norm, mla_w_q_up, mla_kv_a_norm, mla_w_kv_up, mla_q_norm, mla_k_norm, mla_out_norm, hg_lb_logits, hg_out_norm, w_out, xattn_norm, mem_norm, xattn_w_q, xattn_w_kv, xattn_q_norm, xattn_k_norm, xattn_w_o, ffn2_norm, ffn2_w_gate, ffn2_w_up, ffn2_w_down):
    batch, seq, _ = x.shape
    mem_len = mem.shape[1]
    depth = ffn1_norm.shape[0]
    assert depth == 1 and seq % ATT_T == 0 and seq % XA_TM == 0
    t = batch * seq
    xt = x.reshape(t, D_MODEL)
    pos = positions.reshape(t, 1)
    half = MLA_ROPE // 2
    inv_freq = ROPE_BASE ** (-np.arange(half, dtype=np.float32) / half)
    invf = jnp.asarray(np.concatenate([inv_freq, inv_freq, np.zeros(LANES - MLA_ROPE, np.float32)])[None, :])
    l = 0

    xt = _ffn(xt, ffn1_norm[l][None, :], ffn1_w_gate[l], ffn1_w_up[l], ffn1_w_down[l])

    proj = _in_proj(xt, mix_norm[l][None, :], _prep_w_in(w_in, l))
    qpa, qpb = _pe_gains(mla_q_norm[l])
    kpa, kpb = _pe_gains(mla_k_norm[l])
    q, k, v = _mla_prep(proj, pos, invf, mla_q_a_norm[l][None, :], mla_kv_a_norm[l][None, :],
                        _prep_w_q_up(mla_w_q_up[l]), mla_w_kv_up[l],
                        mla_q_norm[l][None, :MLA_NOPE], qpa, qpb,
                        mla_k_norm[l][None, :MLA_NOPE], kpa, kpb)
    a = _mla_attn(q, k, v, batch, seq)
    r = _hgrn(proj, hg_lb_logits, hg_out_norm[l][None, :], batch, seq)
    xt = _out_proj(a, r, xt, mla_out_norm[l][None, :], w_out[l])

    kx, vx = _mem_kv(mem.reshape(batch * mem_len, D_MODEL), mem_norm[l][None, :],
                     xattn_w_kv[l], xattn_k_norm[l][None, :], batch, mem_len)
    xt = _xattn(xt, xattn_norm[l][None, :], xattn_w_q[l], xattn_q_norm[l][None, :],
                kx, vx, xattn_w_o[l], seq, mem_len)

    xt = _ffn(xt, ffn2_norm[l][None, :], ffn2_w_gate[l], ffn2_w_up[l], ffn2_w_down[l])
    return xt.reshape(batch, seq, D_MODEL)
```

```python
import functools

import numpy as np
import jax
import jax.numpy as jnp
from jax import lax
from jax.experimental import pallas as pl
from jax.experimental.pallas import tpu as pltpu

F32 = jnp.float32
BF16 = jnp.bfloat16

EPS = 1e-6
ROPE_BASE = 10000.0
LANES = 128
SUBLANES = 8

D_MODEL = 2048
D_FF = 5504
FFN_TM = 1024
FFN_TF = 256

MLA_HEADS = 8
MLA_NOPE = 128
MLA_ROPE = 64
MLA_QK = MLA_NOPE + MLA_ROPE
MLA_V = 128
MLA_Q_RANK = 512
MLA_KV_RANK = 256
MLA_WIDTH = MLA_HEADS * MLA_V
QK_PAD = 256
Q_UP_PER_HEAD = 384
ATT_Q_SCALE = float(MLA_QK ** -0.5 * np.log2(np.e))

HG_HEADS = 8
HG_D = 128
HG_CHUNK = 64
HG_LEVELS = (32, 16, 8, 4, 2, 1)
HG_UNROLL = 8
HG_GROUP = HG_CHUNK * HG_UNROLL
HG_SLAB = 256
HG_WIDTH = HG_HEADS * HG_D

MEM_HEADS = 4
MEM_HD = 128
MEM_WIDTH = MEM_HEADS * MEM_HD
XATT_Q_SCALE = float(MEM_HD ** -0.5 * np.log2(np.e))

COL_CQ = 0
COL_CKV = 512
COL_HQ = 768
COL_HF = COL_HQ + HG_WIDTH
COL_HI = COL_HF + HG_WIDTH
COL_HG = COL_HI + HG_WIDTH
COL_KPE = COL_HG + HG_WIDTH
COL_KPE_SW = COL_KPE + LANES
IN_COLS_PAD = COL_KPE_SW + LANES

VMEM_LIMIT = 56 * 1024 * 1024


def _params(*sem):
    return pltpu.CompilerParams(dimension_semantics=sem, vmem_limit_bytes=VMEM_LIMIT)


def _rms_scale(x):
    return lax.rsqrt(jnp.mean(x * x, axis=-1, keepdims=True) + EPS)


def _sigmoid(x):
    return 1.0 / (1.0 + jnp.exp(-x))


def _dot(a, b):
    return jnp.dot(a, b, preferred_element_type=F32)


def _dot_nt(a, b):
    return lax.dot_general(a, b, (((1,), (1,)), ((), ())), preferred_element_type=F32)


def _dot_tn(a, b):
    return lax.dot_general(a, b, (((0,), (0,)), ((), ())), preferred_element_type=F32)


def _ffn_kernel(x_ref, g_ref, wg_ref, wu_ref, wd_ref, o_ref, h_scr):
    @pl.when(pl.program_id(1) == 0)
    def _():
        x = x_ref[...]
        h_scr[...] = (x * _rms_scale(x) * g_ref[...]).astype(BF16)
        o_ref[...] = x

    h = h_scr[...]
    gate = _dot(h, wg_ref[...].astype(BF16))
    up = _dot(h, wu_ref[...].astype(BF16))
    a = 0.5 * gate * _sigmoid(gate) * up
    f0 = pl.program_id(1) * FFN_TF
    a_ok = f0 + lax.broadcasted_iota(jnp.int32, (1, FFN_TF), 1) < D_FF
    w_ok = f0 + lax.broadcasted_iota(jnp.int32, (FFN_TF, 1), 0) < D_FF
    a = jnp.where(a_ok, a, 0.0).astype(BF16)
    wd = jnp.where(w_ok, wd_ref[...], 0.0).astype(BF16)
    o_ref[...] += _dot(a, wd)


def _ffn(x, gain, wg, wu, wd):
    t = x.shape[0]
    return pl.pallas_call(
        _ffn_kernel,
        out_shape=jax.ShapeDtypeStruct((t, D_MODEL), F32),
        grid=(t // FFN_TM, pl.cdiv(D_FF, FFN_TF)),
        in_specs=[
            pl.BlockSpec((FFN_TM, D_MODEL), lambda i, f: (i, 0)),
            pl.BlockSpec((1, D_MODEL), lambda i, f: (0, 0)),
            pl.BlockSpec((D_MODEL, FFN_TF), lambda i, f: (0, f)),
            pl.BlockSpec((D_MODEL, FFN_TF), lambda i, f: (0, f)),
            pl.BlockSpec((FFN_TF, D_MODEL), lambda i, f: (f, 0)),
        ],
        out_specs=pl.BlockSpec((FFN_TM, D_MODEL), lambda i, f: (i, 0)),
        scratch_shapes=[pltpu.VMEM((FFN_TM, D_MODEL), BF16)],
        compiler_params=_params("parallel", "arbitrary"),
        name="ffn",
    )(x, gain, wg, wu, wd)


PROJ_TM = 1024
PROJ_TN = 512


def _in_proj_kernel(x_ref, g_ref, w_ref, o_ref, h_scr):
    @pl.when(pl.program_id(1) == 0)
    def _():
        x = x_ref[...]
        h_scr[...] = (x * _rms_scale(x) * g_ref[...]).astype(BF16)

    o_ref[...] = _dot_nt(h_scr[...], w_ref[...])


def _in_proj(x, gain, w_t):
    t = x.shape[0]
    n = w_t.shape[0]
    return pl.pallas_call(
        _in_proj_kernel,
        out_shape=jax.ShapeDtypeStruct((t, n), F32),
        grid=(t // PROJ_TM, n // PROJ_TN),
        in_specs=[
            pl.BlockSpec((PROJ_TM, D_MODEL), lambda i, j: (i, 0)),
            pl.BlockSpec((1, D_MODEL), lambda i, j: (0, 0)),
            pl.BlockSpec((PROJ_TN, D_MODEL), lambda i, j: (j, 0)),
        ],
        out_specs=pl.BlockSpec((PROJ_TM, PROJ_TN), lambda i, j: (i, j)),
        scratch_shapes=[pltpu.VMEM((PROJ_TM, D_MODEL), BF16)],
        compiler_params=_params("parallel", "arbitrary"),
        name="in_proj",
    )(x, gain, w_t)


PREP_TM = 512


def _mla_prep_kernel(cq_ref, ckv_ref, kpe_ref, kpesw_ref, pos_ref, invf_ref,
                     qa_ref, kva_ref, wq_ref, wkv_ref,
                     qn_ref, qpa_ref, qpb_ref, kn_ref, kpa_ref, kpb_ref,
                     q_ref, k_ref, vt_ref):
    ang = pos_ref[...].astype(F32) * invf_ref[...]
    cos4 = jnp.cos(ang)
    sin4 = jnp.sin(ang)
    lane = lax.broadcasted_iota(jnp.int32, (1, LANES), 1)
    half = MLA_ROPE // 2

    def expand(t4):
        parts = []
        for part in range(LANES // half):
            t = pltpu.roll(t4, (LANES - half * part) % LANES, 1) if part else t4
            parts.append(jnp.where(lane < half, t, pltpu.roll(t, half, 1)))
        return jnp.concatenate(parts, axis=0)

    cos = expand(cos4)
    sin = expand(sin4)

    cq = cq_ref[...]
    cqn = (cq * _rms_scale(cq) * qa_ref[...]).astype(BF16)
    qall = _dot(cqn, wq_ref[...])
    q_cos = qpa_ref[...] * cos
    q_sin = qpb_ref[...] * sin
    for h in range(MLA_HEADS):
        base = h * Q_UP_PER_HEAD
        qn = qall[:, base:base + LANES]
        y = qall[:, base + LANES:base + 2 * LANES]
        ysw = qall[:, base + 2 * LANES:base + 3 * LANES]
        ss = jnp.sum(qn * qn, axis=-1, keepdims=True) + jnp.sum(y * y, axis=-1, keepdims=True)
        r = lax.rsqrt(ss * (1.0 / MLA_QK) + EPS) * ATT_Q_SCALE
        q_ref[:, h * QK_PAD:h * QK_PAD + LANES] = (qn * r * qn_ref[...]).astype(BF16)
        q_ref[:, h * QK_PAD + LANES:(h + 1) * QK_PAD] = ((y * q_cos + ysw * q_sin) * r).astype(BF16)

    ckv = ckv_ref[...]
    ckvn = (ckv * _rms_scale(ckv) * kva_ref[...]).astype(BF16)
    kvall = _dot(ckvn, wkv_ref[...].astype(BF16))
    yk = kpe_ref[...]
    kr = yk * (kpa_ref[...] * cos) + kpesw_ref[...] * (kpb_ref[...] * sin)
    ss_pe = jnp.sum(yk * yk, axis=-1, keepdims=True)
    for h in range(MLA_HEADS):
        kn = kvall[:, h * 2 * LANES:h * 2 * LANES + LANES]
        ss = jnp.sum(kn * kn, axis=-1, keepdims=True) + ss_pe
        r = lax.rsqrt(ss * (1.0 / MLA_QK) + EPS)
        k_ref[:, h * QK_PAD:h * QK_PAD + LANES] = (kn * r * kn_ref[...]).astype(BF16)
        k_ref[:, h * QK_PAD + LANES:(h + 1) * QK_PAD] = (kr * r).astype(BF16)
        vt_ref[h * MLA_V:(h + 1) * MLA_V, :] = kvall[:, h * 2 * LANES + LANES:(h + 1) * 2 * LANES].T.astype(BF16)


def _mla_prep(proj, pos, invf, qa, kva, wq, wkv, qn, qpa, qpb, kn, kpa, kpb):
    t = proj.shape[0]
    tm = PREP_TM
    row = lambda i: (0, 0)
    vec = lambda n: pl.BlockSpec((1, n), row)
    return pl.pallas_call(
        _mla_prep_kernel,
        out_shape=(jax.ShapeDtypeStruct((t, MLA_HEADS * QK_PAD), BF16),
                   jax.ShapeDtypeStruct((t, MLA_HEADS * QK_PAD), BF16),
                   jax.ShapeDtypeStruct((MLA_WIDTH, t), BF16)),
        grid=(t // tm,),
        in_specs=[
            pl.BlockSpec((tm, MLA_Q_RANK), lambda i: (i, COL_CQ // MLA_Q_RANK)),
            pl.BlockSpec((tm, MLA_KV_RANK), lambda i: (i, COL_CKV // MLA_KV_RANK)),
            pl.BlockSpec((tm, LANES), lambda i: (i, COL_KPE // LANES)),
            pl.BlockSpec((tm, LANES), lambda i: (i, COL_KPE_SW // LANES)),
            pl.BlockSpec((tm // (LANES // (MLA_ROPE // 2)), LANES), lambda i: (i, 0)),
            vec(LANES),
            vec(MLA_Q_RANK), vec(MLA_KV_RANK),
            pl.BlockSpec(wq.shape, row), pl.BlockSpec(wkv.shape, row),
            vec(LANES), vec(LANES), vec(LANES), vec(LANES), vec(LANES), vec(LANES),
        ],
        out_specs=(pl.BlockSpec((tm, MLA_HEADS * QK_PAD), lambda i: (i, 0)),
                   pl.BlockSpec((tm, MLA_HEADS * QK_PAD), lambda i: (i, 0)),
                   pl.BlockSpec((MLA_WIDTH, tm), lambda i: (0, i))),
        compiler_params=_params("parallel"),
        name="mla_prep",
    )(proj, proj, proj, proj, pos, invf, qa, kva, wq, wkv, qn, qpa, qpb, kn, kpa, kpb)


ATT_T = 256
ATT_AHEAD = 3


def _mla_attn_kernel(q_ref, k_ref, vt_ref, o_ref):
    kpos = lax.broadcasted_iota(jnp.int32, (ATT_T, ATT_T), 0)
    qpos = lax.broadcasted_iota(jnp.int32, (ATT_T, ATT_T), 1)
    nq = q_ref.shape[0] // ATT_T

    def scores(i):
        lo, hi = i * ATT_T, (i + 1) * ATT_T
        q = q_ref[lo:hi, :]
        s_diag = jnp.where(kpos <= qpos, _dot_nt(k_ref[lo:hi, :], q), -jnp.inf)
        s_off = _dot_nt(k_ref[0:lo, :], q) if i > 0 else None
        return s_diag, s_off

    def finish(i, s_diag, s_off):
        lo, hi = i * ATT_T, (i + 1) * ATT_T
        m = jnp.max(s_diag, axis=0, keepdims=True)
        if i > 0:
            m = jnp.maximum(m, jnp.max(s_off, axis=0, keepdims=True))
        p = jnp.exp2(s_diag - m)
        l = jnp.sum(p, axis=0, keepdims=True)
        acc = _dot(vt_ref[:, lo:hi], p.astype(BF16))
        if i > 0:
            p = jnp.exp2(s_off - m)
            l = l + jnp.sum(p, axis=0, keepdims=True)
            acc = acc + _dot(vt_ref[:, 0:lo], p.astype(BF16))
        o_ref[lo:hi, :] = (acc / l).T

    order = list(reversed(range(nq)))
    pending = [scores(i) for i in order[:ATT_AHEAD]]
    for n, i in enumerate(order):
        if n + ATT_AHEAD < nq:
            pending.append(scores(order[n + ATT_AHEAD]))
        finish(i, *pending.pop(0))


def _mla_attn(q, k, vt, batch, seq):
    return pl.pallas_call(
        _mla_attn_kernel,
        out_shape=jax.ShapeDtypeStruct((batch * seq, MLA_WIDTH), F32),
        grid=(batch, MLA_HEADS),
        in_specs=[
            pl.BlockSpec((seq, QK_PAD), lambda b, h: (b, h)),
            pl.BlockSpec((seq, QK_PAD), lambda b, h: (b, h)),
            pl.BlockSpec((MLA_V, seq), lambda b, h: (h, b)),
        ],
        out_specs=pl.BlockSpec((seq, MLA_V), lambda b, h: (b, h)),
        compiler_params=_params("parallel", "parallel"),
        name="mla_attn",
    )(q, k, vt)


def _hgrn_masks():
    t = np.arange(HG_CHUNK)[:, None]
    s = np.arange(HG_CHUNK)[None, :]
    masks = [((t // c) % 2 == 1) & ((s // c) == (t // c) - 1) for c in HG_LEVELS]
    masks.append(t == s)
    return np.stack(masks).astype(np.float32)


def _hgrn_kernel(q_ref, f_ref, i_ref, g_ref, lbl_ref, gn_ref, msk_ref, tri_ref, o_ref, b_all):
    lbl = lbl_ref[...]
    e = jnp.exp(lbl - jnp.max(lbl, axis=0, keepdims=True))
    lb = e[0:1, :] / jnp.sum(e, axis=0, keepdims=True)
    gn = gn_ref[...]
    tri = tri_ref[...]
    sub = lax.broadcasted_iota(jnp.int32, (SUBLANES, LANES), 0)
    zeros8 = jnp.zeros((SUBLANES, LANES), F32)
    per_chunk = HG_CHUNK // SUBLANES
    chunk_rows = [slice(u * HG_CHUNK, (u + 1) * HG_CHUNK) for u in range(HG_UNROLL)]

    def sel(bounds, vals):
        out = vals[-1]
        for bound, val in zip(reversed(bounds), reversed(vals[:-1])):
            out = jnp.where(sub < bound, val, out)
        return out

    def head(gi):
        rows = slice(gi * HG_GROUP, (gi + 1) * HG_GROUP)
        q = q_ref[rows, :]
        v = i_ref[rows, :].astype(BF16)
        f = lb + (1.0 - lb) * _sigmoid(f_ref[rows, :])
        lg = jnp.log2(f)
        k = 1.0 - f

        l1 = lg.astype(BF16)
        l2 = (lg - l1.astype(F32)).astype(BF16)
        slabs = [slice(i, i + HG_SLAB) for i in range(0, HG_GROUP, HG_SLAB)]
        b = jnp.concatenate([_dot(tri, l1[r]) + _dot(tri, l2[r]) for r in slabs], axis=0)
        b_all[gi] = b
        return q, k, f, v, b

    def main(gi, hd, st):
        rows = slice(gi * HG_GROUP, (gi + 1) * HG_GROUP)
        q, k, f, v, b = hd
        b_scr = b_all.at[gi]
        cache = {}

        def rowb(u, r):
            if r < 0:
                return zeros8
            if (u, r) not in cache:
                cache[u, r] = jnp.broadcast_to(b_scr[u * HG_CHUNK + r:u * HG_CHUNK + r + 1, :], (SUBLANES, LANES))
            return cache[u, r]

        def build(fn):
            return jnp.concatenate([fn(u, j) for u in range(HG_UNROLL) for j in range(per_chunk)], axis=0)

        eye = msk_ref[len(HG_LEVELS)] != 0.0
        a = [jnp.where(eye, jnp.sum(q[r] * k[r], axis=-1, keepdims=True), 0.0) for r in chunk_rows]
        for ci, c in enumerate(HG_LEVELS):
            if c >= SUBLANES:
                m = c // SUBLANES
                bs = build(lambda u, j: rowb(u, SUBLANES * (j // m) * m - 1))
                be = build(lambda u, j: rowb(u, SUBLANES * ((j // m) * m + m) - 1))
            elif c == 4:
                bs = build(lambda u, j: sel((4,), (rowb(u, 8 * j - 1), rowb(u, 8 * j + 3))))
                be = build(lambda u, j: sel((4,), (rowb(u, 8 * j + 3), rowb(u, 8 * j + 7))))
            elif c == 2:
                bs = build(lambda u, j: sel((2, 4, 6), (rowb(u, 8 * j - 1), rowb(u, 8 * j + 1),
                                                         rowb(u, 8 * j + 3), rowb(u, 8 * j + 5))))
                be = build(lambda u, j: sel((2, 4, 6), (rowb(u, 8 * j + 1), rowb(u, 8 * j + 3),
                                                         rowb(u, 8 * j + 5), rowb(u, 8 * j + 7))))
            if c == 1:
                qd = (q * f).astype(BF16)
                kd = k.astype(BF16)
            else:
                qd = (q * jnp.exp2(b - bs)).astype(BF16)
                kd = (k * jnp.exp2(be - b)).astype(BF16)
            level = msk_ref[ci] != 0.0
            a = [jnp.where(level, _dot_nt(qd[r], kd[r]), a[u]) for u, r in enumerate(chunk_rows)]

        q_exp = (q * jnp.exp2(b)).astype(BF16)
        b_last = build(lambda u, j: rowb(u, HG_CHUNK - 1))
        k_dec = (k * jnp.exp2(b_last - b)).astype(BF16)
        o_intra = [_dot(a[u].astype(BF16), v[r]) for u, r in enumerate(chunk_rows)]
        st_add = [_dot_tn(v[r], k_dec[r]) for r in chunk_rows]

        o = []
        for u, r in enumerate(chunk_rows):
            o.append(o_intra[u] + _dot_nt(q_exp[r], st.astype(BF16)))
            st = jnp.exp2(rowb(u, HG_CHUNK - 1)[0:1, :]) * st + st_add[u]
        o = jnp.concatenate(o, axis=0)

        g = g_ref[rows, :]
        o_ref[rows, :] = o * _rms_scale(o) * gn * (g * _sigmoid(g))
        return st

    n_groups = q_ref.shape[0] // HG_GROUP
    st = jnp.zeros((HG_D, HG_D), F32)
    hd = head(0)
    for gi in range(n_groups):
        nxt = head(gi + 1) if gi + 1 < n_groups else None
        st = main(gi, hd, st)
        hd = nxt


def _hgrn(proj, lb_logits, out_norm, batch, seq):
    hblk = lambda col: pl.BlockSpec((seq, HG_D), lambda b, h: (b, col // HG_D + h))
    masks = jnp.asarray(_hgrn_masks())
    tri = np.kron(np.eye(HG_SLAB // HG_CHUNK), np.tril(np.ones((HG_CHUNK, HG_CHUNK))))
    tri = jnp.asarray(tri.astype(np.float32), dtype=BF16)
    nlev = masks.shape[0]
    return pl.pallas_call(
        _hgrn_kernel,
        out_shape=jax.ShapeDtypeStruct((batch * seq, HG_WIDTH), F32),
        grid=(batch, HG_HEADS),
        in_specs=[
            hblk(COL_HQ), hblk(COL_HF), hblk(COL_HI), hblk(COL_HG),
            pl.BlockSpec((lb_logits.shape[0], HG_D), lambda b, h: (0, h)),
            pl.BlockSpec((1, HG_D), lambda b, h: (0, 0)),
            pl.BlockSpec((nlev, HG_CHUNK, HG_CHUNK), lambda b, h: (0, 0, 0)),
            pl.BlockSpec((HG_SLAB, HG_SLAB), lambda b, h: (0, 0)),
        ],
        out_specs=pl.BlockSpec((seq, HG_D), lambda b, h: (b, h)),
        scratch_shapes=[pltpu.VMEM((seq // HG_GROUP, HG_GROUP, HG_D), F32)],
        compiler_params=_params("parallel", "parallel"),
        name="hgrn2",
    )(proj, proj, proj, proj, lb_logits, out_norm, masks, tri)


OUT_TM = 512


def _out_proj_kernel(a_ref, r_ref, x_ref, g_ref, w_ref, o_ref):
    a = a_ref[...]
    an = (a * _rms_scale(a) * g_ref[...]).astype(BF16)
    o_ref[...] = (x_ref[...] + _dot(an, w_ref[:MLA_WIDTH, :].astype(BF16))
                  + _dot(r_ref[...].astype(BF16), w_ref[MLA_WIDTH:, :].astype(BF16)))


def _out_proj(a, r, x, gain, w):
    t = x.shape[0]
    return pl.pallas_call(
        _out_proj_kernel,
        out_shape=jax.ShapeDtypeStruct((t, D_MODEL), F32),
        grid=(t // OUT_TM,),
        in_specs=[
            pl.BlockSpec((OUT_TM, MLA_WIDTH), lambda i: (i, 0)),
            pl.BlockSpec((OUT_TM, HG_WIDTH), lambda i: (i, 0)),
            pl.BlockSpec((OUT_TM, D_MODEL), lambda i: (i, 0)),
            pl.BlockSpec((1, MLA_WIDTH), lambda i: (0, 0)),
            pl.BlockSpec(w.shape, lambda i: (0, 0), pipeline_mode=pl.Buffered(1)),
        ],
        out_specs=pl.BlockSpec((OUT_TM, D_MODEL), lambda i: (i, 0)),
        compiler_params=_params("parallel"),
        name="out_proj",
    )(a, r, x, gain, w)


def _mem_kv_kernel(m_ref, g_ref, w_ref, kn_ref, k_ref, v_ref):
    m = m_ref[...]
    mn = (m * _rms_scale(m) * g_ref[...]).astype(BF16)
    kv = _dot(mn, w_ref[...].astype(BF16))
    for h in range(MEM_HEADS):
        k = kv[:, 2 * h * MEM_HD:(2 * h + 1) * MEM_HD]
        k_ref[:, h * MEM_HD:(h + 1) * MEM_HD] = (k * _rms_scale(k) * kn_ref[...]).astype(BF16)
        v_ref[:, h * MEM_HD:(h + 1) * MEM_HD] = kv[:, (2 * h + 1) * MEM_HD:(2 * h + 2) * MEM_HD].astype(BF16)


def _mem_kv(mem, gain, w, k_norm, batch, mem_len):
    return pl.pallas_call(
        _mem_kv_kernel,
        out_shape=(jax.ShapeDtypeStruct((batch * mem_len, MEM_WIDTH), BF16),
                   jax.ShapeDtypeStruct((batch * mem_len, MEM_WIDTH), BF16)),
        grid=(batch,),
        in_specs=[
            pl.BlockSpec((mem_len, D_MODEL), lambda b: (b, 0)),
            pl.BlockSpec((1, D_MODEL), lambda b: (0, 0)),
            pl.BlockSpec(w.shape, lambda b: (0, 0)),
            pl.BlockSpec((1, MEM_HD), lambda b: (0, 0)),
        ],
        out_specs=(pl.BlockSpec((mem_len, MEM_WIDTH), lambda b: (b, 0)),
                   pl.BlockSpec((mem_len, MEM_WIDTH), lambda b: (b, 0))),
        compiler_params=_params("parallel"),
        name="mem_kv",
    )(mem, gain, w, k_norm)


XA_TM = 512


def _xattn_kernel(x_ref, g_ref, wq_ref, qn_ref, k_ref, v_ref, wo_ref, o_ref, att_scr):
    x = x_ref[...]
    h = (x * _rms_scale(x) * g_ref[...]).astype(BF16)
    q = _dot(h, wq_ref[...].astype(BF16))
    heads = [slice(hd * MEM_HD, (hd + 1) * MEM_HD) for hd in range(MEM_HEADS)]
    scores = []
    for cols in heads:
        qh = q[:, cols]
        qh = (qh * (_rms_scale(qh) * XATT_Q_SCALE) * qn_ref[...]).astype(BF16)
        scores.append(_dot_nt(qh, k_ref[:, cols]))
    for cols, s in zip(heads, scores):
        p = jnp.exp2(s - jnp.max(s, axis=-1, keepdims=True))
        l = jnp.sum(p, axis=-1, keepdims=True)
        att_scr[:, cols] = (_dot(p.astype(BF16), v_ref[:, cols]) / l).astype(BF16)
    o_ref[...] = x + _dot(att_scr[...], wo_ref[...].astype(BF16))


def _xattn(x, gain, wq, q_norm, kx, vx, wo, seq, mem_len):
    t = x.shape[0]
    per_batch = seq // XA_TM
    return pl.pallas_call(
        _xattn_kernel,
        out_shape=jax.ShapeDtypeStruct((t, D_MODEL), F32),
        grid=(t // XA_TM,),
        in_specs=[
            pl.BlockSpec((XA_TM, D_MODEL), lambda i: (i, 0)),
            pl.BlockSpec((1, D_MODEL), lambda i: (0, 0)),
            pl.BlockSpec(wq.shape, lambda i: (0, 0)),
            pl.BlockSpec((1, MEM_HD), lambda i: (0, 0)),
            pl.BlockSpec((mem_len, MEM_WIDTH), lambda i: (i // per_batch, 0)),
            pl.BlockSpec((mem_len, MEM_WIDTH), lambda i: (i // per_batch, 0)),
            pl.BlockSpec(wo.shape, lambda i: (0, 0)),
        ],
        out_specs=pl.BlockSpec((XA_TM, D_MODEL), lambda i: (i, 0)),
        scratch_shapes=[pltpu.VMEM((XA_TM, MEM_WIDTH), BF16)],
        compiler_params=_params("parallel"),
        name="xattn",
    )(x, gain, wq, q_norm, kx, vx, wo)


def _pe_pair(x1, x2):
    z = jnp.zeros(x1.shape[:-1] + (LANES - MLA_ROPE,), x1.dtype)
    return jnp.concatenate([x1, x2, z], axis=-1), jnp.concatenate([x2, x1, z], axis=-1)


W_IN_TC = 512


def _w_in_prep_kernel(w_ref, o_ref):
    half = MLA_ROPE // 2
    kpe0 = MLA_Q_RANK + MLA_KV_RANK
    hg0 = kpe0 + MLA_ROPE
    zeros = jnp.zeros((LANES - MLA_ROPE, o_ref.shape[1]), BF16)
    o_ref[:kpe0, :] = w_ref[:kpe0, :].astype(BF16)
    o_ref[kpe0:COL_KPE, :] = w_ref[hg0:, :].astype(BF16)
    x1 = w_ref[kpe0:kpe0 + half, :].astype(BF16)
    x2 = w_ref[kpe0 + half:hg0, :].astype(BF16)
    o_ref[COL_KPE:COL_KPE + half, :] = x1
    o_ref[COL_KPE + half:COL_KPE + MLA_ROPE, :] = x2
    o_ref[COL_KPE + MLA_ROPE:COL_KPE_SW, :] = zeros
    o_ref[COL_KPE_SW:COL_KPE_SW + half, :] = x2
    o_ref[COL_KPE_SW + half:COL_KPE_SW + MLA_ROPE, :] = x1
    o_ref[COL_KPE_SW + MLA_ROPE:, :] = zeros


def _prep_w_in(w_in, layer):
    w_t = jnp.swapaxes(w_in, 1, 2)
    _, n, k = w_t.shape
    return pl.pallas_call(
        _w_in_prep_kernel,
        out_shape=jax.ShapeDtypeStruct((IN_COLS_PAD, k), BF16),
        grid=(k // W_IN_TC,),
        in_specs=[pl.BlockSpec((None, n, W_IN_TC), lambda i: (layer, 0, i))],
        out_specs=pl.BlockSpec((IN_COLS_PAD, W_IN_TC), lambda i: (0, i)),
        compiler_params=_params("parallel"),
        name="w_in_prep",
    )(w_t)


def _prep_w_q_up(w):
    w = w.astype(BF16).reshape(MLA_Q_RANK, MLA_HEADS, MLA_QK)
    half = MLA_ROPE // 2
    pe, pe_sw = _pe_pair(w[..., MLA_NOPE:MLA_NOPE + half], w[..., MLA_NOPE + half:])
    return jnp.concatenate([w[..., :MLA_NOPE], pe, pe_sw], axis=-1).reshape(MLA_Q_RANK, MLA_HEADS * Q_UP_PER_HEAD)


def _pe_gains(norm):
    half = MLA_ROPE // 2
    g1 = norm[MLA_NOPE:MLA_NOPE + half]
    g2 = norm[MLA_NOPE + half:]
    ga, gb = _pe_pair(g1, g2)
    sign = jnp.concatenate([-jnp.ones((half,), F32), jnp.ones((LANES - half,), F32)])
    return ga[None, :], (gb * sign)[None, :]


def kernel(x, mem, positions, ffn1_norm, ffn1_w_gate, ffn1_w_up, ffn1_w_down, mix_norm, w_in, mla_q_a_norm, mla_w_q_up, mla_kv_a_norm, mla_w_kv_up, mla_q_norm, mla_k_norm, mla_out_norm, hg_lb_logits, hg_out_norm, w_out, xattn_norm, mem_norm, xattn_w_q, xattn_w_kv, xattn_q_norm, xattn_k_norm, xattn_w_o, ffn2_norm, ffn2_w_gate, ffn2_w_up, ffn2_w_down):
    batch, seq, _ = x.shape
    mem_len = mem.shape[1]
    depth = ffn1_norm.shape[0]
    assert depth == 1 and seq % ATT_T == 0 and seq % XA_TM == 0
    t = batch * seq
    xt = x.reshape(t, D_MODEL)
    half = MLA_ROPE // 2
    quarters = LANES // half
    pos = positions.reshape(t // PREP_TM, quarters, PREP_TM // quarters).transpose(0, 2, 1)
    pos = jnp.repeat(pos, half, axis=-1).reshape(t // quarters, LANES)
    inv_freq = ROPE_BASE ** (-np.arange(half, dtype=np.float32) / half)
    invf = jnp.asarray(np.tile(inv_freq, quarters)[None, :])
    l = 0

    xt = _ffn(xt, ffn1_norm[l][None, :], ffn1_w_gate[l], ffn1_w_up[l], ffn1_w_down[l])

    proj = _in_proj(xt, mix_norm[l][None, :], _prep_w_in(w_in, l))
    qpa, qpb = _pe_gains(mla_q_norm[l])
    kpa, kpb = _pe_gains(mla_k_norm[l])
    q, k, v = _mla_prep(proj, pos, invf, mla_q_a_norm[l][None, :], mla_kv_a_norm[l][None, :],
                        _prep_w_q_up(mla_w_q_up[l]), mla_w_kv_up[l],
                        mla_q_norm[l][None, :MLA_NOPE], qpa, qpb,
                        mla_k_norm[l][None, :MLA_NOPE], kpa, kpb)
    a = _mla_attn(q, k, v, batch, seq)
    r = _hgrn(proj, hg_lb_logits, hg_out_norm[l][None, :], batch, seq)
    xt = _out_proj(a, r, xt, mla_out_norm[l][None, :], w_out[l])

    kx, vx = _mem_kv(mem.reshape(batch * mem_len, D_MODEL), mem_norm[l][None, :],
                     xattn_w_kv[l], xattn_k_norm[l][None, :], batch, mem_len)
    xt = _xattn(xt, xattn_norm[l][None, :], xattn_w_q[l], xattn_q_norm[l][None, :],
                kx, vx, xattn_w_o[l], seq, mem_len)

    xt = _ffn(xt, ffn2_norm[l][None, :], ffn2_w_gate[l], ffn2_w_up[l], ffn2_w_down[l])
    return xt.reshape(batch, seq, D_MODEL)
```

```python
import functools

import numpy as np
import jax
import jax.numpy as jnp
from jax import lax
from jax.experimental import pallas as pl
from jax.experimental.pallas import tpu as pltpu

F32 = jnp.float32
BF16 = jnp.bfloat16

EPS = 1e-6
ROPE_BASE = 10000.0
LANES = 128
SUBLANES = 8

D_MODEL = 2048
D_FF = 5504
FFN_TM = 1024
FFN_TF = 512

MLA_HEADS = 8
MLA_NOPE = 128
MLA_ROPE = 64
MLA_QK = MLA_NOPE + MLA_ROPE
MLA_V = 128
MLA_Q_RANK = 512
MLA_KV_RANK = 256
MLA_WIDTH = MLA_HEADS * MLA_V
QK_PAD = 256
Q_UP_PER_HEAD = 384
ATT_Q_SCALE = float(MLA_QK ** -0.5 * np.log2(np.e))

HG_HEADS = 8
HG_D = 128
HG_CHUNK = 64
HG_LEVELS = (32, 16, 8, 4, 2, 1)
HG_UNROLL = 8
HG_GROUP = HG_CHUNK * HG_UNROLL
HG_SLAB = 256
HG_WIDTH = HG_HEADS * HG_D

MEM_HEADS = 4
MEM_HD = 128
MEM_WIDTH = MEM_HEADS * MEM_HD
XATT_Q_SCALE = float(MEM_HD ** -0.5 * np.log2(np.e))

COL_CQ = 0
COL_CKV = 512
COL_HQ = 768
COL_HF = COL_HQ + HG_WIDTH
COL_HI = COL_HF + HG_WIDTH
COL_HG = COL_HI + HG_WIDTH
COL_KPE = COL_HG + HG_WIDTH
COL_KPE_SW = COL_KPE + LANES
IN_COLS_PAD = COL_KPE_SW + LANES

VMEM_LIMIT = 56 * 1024 * 1024
FFN_VMEM_LIMIT = 61 * 1024 * 1024


def _params(*sem, vmem_limit=VMEM_LIMIT):
    return pltpu.CompilerParams(dimension_semantics=sem, vmem_limit_bytes=vmem_limit)


def _rms_scale(x):
    return lax.rsqrt(jnp.mean(x * x, axis=-1, keepdims=True) + EPS)


def _sigmoid(x):
    return 1.0 / (1.0 + jnp.exp(-x))


def _dot(a, b):
    return jnp.dot(a, b, preferred_element_type=F32)


def _dot_nt(a, b):
    return lax.dot_general(a, b, (((1,), (1,)), ((), ())), preferred_element_type=F32)


def _dot_tn(a, b):
    return lax.dot_general(a, b, (((0,), (0,)), ((), ())), preferred_element_type=F32)


def _ffn_kernel(x_hbm, g_ref, wg_ref, wu_ref, wd_ref, o_ref, h_scr, x_scr, x_sem):
    i = pl.program_id(0)

    def x_copy(tile):
        rows = pl.ds(pl.multiple_of(tile * FFN_TM, FFN_TM), FFN_TM)
        return pltpu.make_async_copy(x_hbm.at[rows, :], x_scr, x_sem)

    @pl.when(pl.program_id(1) == 0)
    def _():
        @pl.when(i == 0)
        def _():
            x_copy(0).start()

        x_copy(i).wait()
        x = x_scr[...]
        h_scr[...] = (x * _rms_scale(x) * g_ref[...]).astype(BF16)
        o_ref[...] = x

        @pl.when(i + 1 < pl.num_programs(0))
        def _():
            x_copy(i + 1).start()

    h = h_scr[...]
    gate = _dot(h, wg_ref[...].astype(BF16))
    up = _dot(h, wu_ref[...].astype(BF16))
    a = 0.5 * gate * _sigmoid(gate) * up
    f0 = pl.program_id(1) * FFN_TF
    a_ok = f0 + lax.broadcasted_iota(jnp.int32, (1, FFN_TF), 1) < D_FF
    w_ok = f0 + lax.broadcasted_iota(jnp.int32, (FFN_TF, 1), 0) < D_FF
    a = jnp.where(a_ok, a, 0.0).astype(BF16)
    wd = jnp.where(w_ok, wd_ref[...], 0.0).astype(BF16)
    o_ref[...] += _dot(a, wd)


def _ffn(x, gain, wg, wu, wd):
    t = x.shape[0]
    return pl.pallas_call(
        _ffn_kernel,
        out_shape=jax.ShapeDtypeStruct((t, D_MODEL), F32),
        grid=(t // FFN_TM, pl.cdiv(D_FF, FFN_TF)),
        in_specs=[
            pl.BlockSpec(memory_space=pl.ANY),
            pl.BlockSpec((1, D_MODEL), lambda i, f: (0, 0)),
            pl.BlockSpec((D_MODEL, FFN_TF), lambda i, f: (0, f)),
            pl.BlockSpec((D_MODEL, FFN_TF), lambda i, f: (0, f)),
            pl.BlockSpec((FFN_TF, D_MODEL), lambda i, f: (f, 0)),
        ],
        out_specs=pl.BlockSpec((FFN_TM, D_MODEL), lambda i, f: (i, 0)),
        scratch_shapes=[pltpu.VMEM((FFN_TM, D_MODEL), BF16), pltpu.VMEM((FFN_TM, D_MODEL), F32),
                        pltpu.SemaphoreType.DMA],
        compiler_params=_params("arbitrary", "arbitrary", vmem_limit=FFN_VMEM_LIMIT),
        name="ffn",
    )(x, gain, wg, wu, wd)


PROJ_TM = 1024
PROJ_TN = 1024


def _in_proj_kernel(x_ref, g_ref, w_ref, o_ref, h_scr):
    @pl.when(pl.program_id(1) == 0)
    def _():
        x = x_ref[...]
        h_scr[...] = (x * _rms_scale(x) * g_ref[...]).astype(BF16)

    o_ref[...] = _dot_nt(h_scr[...], w_ref[...])


def _in_proj(x, gain, w_t):
    t = x.shape[0]
    n = w_t.shape[0]
    return pl.pallas_call(
        _in_proj_kernel,
        out_shape=jax.ShapeDtypeStruct((t, n), F32),
        grid=(t // PROJ_TM, n // PROJ_TN),
        in_specs=[
            pl.BlockSpec((PROJ_TM, D_MODEL), lambda i, j: (i, 0)),
            pl.BlockSpec((1, D_MODEL), lambda i, j: (0, 0)),
            pl.BlockSpec((PROJ_TN, D_MODEL), lambda i, j: (j, 0)),
        ],
        out_specs=pl.BlockSpec((PROJ_TM, PROJ_TN), lambda i, j: (i, j)),
        scratch_shapes=[pltpu.VMEM((PROJ_TM, D_MODEL), BF16)],
        compiler_params=_params("parallel", "arbitrary"),
        name="in_proj",
    )(x, gain, w_t)


PREP_TM = 512


def _mla_prep_kernel(cq_ref, ckv_ref, kpe_ref, kpesw_ref, pos_ref, invf_ref,
                     qa_ref, kva_ref, wq_ref, wkv_ref,
                     qn_ref, qpa_ref, qpb_ref, kn_ref, kpa_ref, kpb_ref,
                     q_ref, k_ref, vt_ref):
    ang = pos_ref[...].astype(F32) * invf_ref[...]
    cos4 = jnp.cos(ang)
    sin4 = jnp.sin(ang)
    lane = lax.broadcasted_iota(jnp.int32, (1, LANES), 1)
    half = MLA_ROPE // 2

    def expand(t4):
        parts = []
        for part in range(LANES // half):
            t = pltpu.roll(t4, (LANES - half * part) % LANES, 1) if part else t4
            parts.append(jnp.where(lane < half, t, pltpu.roll(t, half, 1)))
        return jnp.concatenate(parts, axis=0)

    cos = expand(cos4)
    sin = expand(sin4)

    cq = cq_ref[...]
    cqn = (cq * _rms_scale(cq) * qa_ref[...]).astype(BF16)
    qall = _dot(cqn, wq_ref[...])
    q_cos = qpa_ref[...] * cos
    q_sin = qpb_ref[...] * sin
    for h in range(MLA_HEADS):
        base = h * Q_UP_PER_HEAD
        qn = qall[:, base:base + LANES]
        y = qall[:, base + LANES:base + 2 * LANES]
        ysw = qall[:, base + 2 * LANES:base + 3 * LANES]
        ss = jnp.sum(qn * qn, axis=-1, keepdims=True) + jnp.sum(y * y, axis=-1, keepdims=True)
        r = lax.rsqrt(ss * (1.0 / MLA_QK) + EPS) * ATT_Q_SCALE
        q_ref[:, h * QK_PAD:h * QK_PAD + LANES] = (qn * r * qn_ref[...]).astype(BF16)
        q_ref[:, h * QK_PAD + LANES:(h + 1) * QK_PAD] = ((y * q_cos + ysw * q_sin) * r).astype(BF16)

    ckv = ckv_ref[...]
    ckvn = (ckv * _rms_scale(ckv) * kva_ref[...]).astype(BF16)
    kvall = _dot(ckvn, wkv_ref[...].astype(BF16))
    yk = kpe_ref[...]
    kr = yk * (kpa_ref[...] * cos) + kpesw_ref[...] * (kpb_ref[...] * sin)
    ss_pe = jnp.sum(yk * yk, axis=-1, keepdims=True)
    for h in range(MLA_HEADS):
        kn = kvall[:, h * 2 * LANES:h * 2 * LANES + LANES]
        ss = jnp.sum(kn * kn, axis=-1, keepdims=True) + ss_pe
        r = lax.rsqrt(ss * (1.0 / MLA_QK) + EPS)
        k_ref[:, h * QK_PAD:h * QK_PAD + LANES] = (kn * r * kn_ref[...]).astype(BF16)
        k_ref[:, h * QK_PAD + LANES:(h + 1) * QK_PAD] = (kr * r).astype(BF16)
        vt_ref[h * MLA_V:(h + 1) * MLA_V, :] = kvall[:, h * 2 * LANES + LANES:(h + 1) * 2 * LANES].T.astype(BF16)


def _mla_prep(proj, pos, invf, qa, kva, wq, wkv, qn, qpa, qpb, kn, kpa, kpb):
    t = proj.shape[0]
    tm = PREP_TM
    row = lambda i: (0, 0)
    vec = lambda n: pl.BlockSpec((1, n), row)
    return pl.pallas_call(
        _mla_prep_kernel,
        out_shape=(jax.ShapeDtypeStruct((t, MLA_HEADS * QK_PAD), BF16),
                   jax.ShapeDtypeStruct((t, MLA_HEADS * QK_PAD), BF16),
                   jax.ShapeDtypeStruct((MLA_WIDTH, t), BF16)),
        grid=(t // tm,),
        in_specs=[
            pl.BlockSpec((tm, MLA_Q_RANK), lambda i: (i, COL_CQ // MLA_Q_RANK)),
            pl.BlockSpec((tm, MLA_KV_RANK), lambda i: (i, COL_CKV // MLA_KV_RANK)),
            pl.BlockSpec((tm, LANES), lambda i: (i, COL_KPE // LANES)),
            pl.BlockSpec((tm, LANES), lambda i: (i, COL_KPE_SW // LANES)),
            pl.BlockSpec((tm // (LANES // (MLA_ROPE // 2)), LANES), lambda i: (i, 0)),
            vec(LANES),
            vec(MLA_Q_RANK), vec(MLA_KV_RANK),
            pl.BlockSpec(wq.shape, row), pl.BlockSpec(wkv.shape, row),
            vec(LANES), vec(LANES), vec(LANES), vec(LANES), vec(LANES), vec(LANES),
        ],
        out_specs=(pl.BlockSpec((tm, MLA_HEADS * QK_PAD), lambda i: (i, 0)),
                   pl.BlockSpec((tm, MLA_HEADS * QK_PAD), lambda i: (i, 0)),
                   pl.BlockSpec((MLA_WIDTH, tm), lambda i: (0, i))),
        compiler_params=_params("parallel"),
        name="mla_prep",
    )(proj, proj, proj, proj, pos, invf, qa, kva, wq, wkv, qn, qpa, qpb, kn, kpa, kpb)


ATT_T = 256
ATT_AHEAD = 3


def _mla_attn_kernel(q_ref, k_ref, vt_ref, o_ref):
    kpos = lax.broadcasted_iota(jnp.int32, (ATT_T, ATT_T), 0)
    qpos = lax.broadcasted_iota(jnp.int32, (ATT_T, ATT_T), 1)
    nq = q_ref.shape[0] // ATT_T

    def scores(i):
        lo, hi = i * ATT_T, (i + 1) * ATT_T
        q = q_ref[lo:hi, :]
        s_diag = jnp.where(kpos <= qpos, _dot_nt(k_ref[lo:hi, :], q), -jnp.inf)
        s_off = _dot_nt(k_ref[0:lo, :], q) if i > 0 else None
        return s_diag, s_off

    def finish(i, s_diag, s_off):
        lo, hi = i * ATT_T, (i + 1) * ATT_T
        m = jnp.max(s_diag, axis=0, keepdims=True)
        if i > 0:
            m = jnp.maximum(m, jnp.max(s_off, axis=0, keepdims=True))
        p = jnp.exp2(s_diag - m)
        l = jnp.sum(p, axis=0, keepdims=True)
        acc = _dot(vt_ref[:, lo:hi], p.astype(BF16))
        if i > 0:
            p = jnp.exp2(s_off - m)
            l = l + jnp.sum(p, axis=0, keepdims=True)
            acc = acc + _dot(vt_ref[:, 0:lo], p.astype(BF16))
        o_ref[lo:hi, :] = (acc / l).T

    order = list(reversed(range(nq)))
    pending = [scores(i) for i in order[:ATT_AHEAD]]
    for n, i in enumerate(order):
        if n + ATT_AHEAD < nq:
            pending.append(scores(order[n + ATT_AHEAD]))
        finish(i, *pending.pop(0))


def _mla_attn(q, k, vt, batch, seq):
    return pl.pallas_call(
        _mla_attn_kernel,
        out_shape=jax.ShapeDtypeStruct((batch * seq, MLA_WIDTH), F32),
        grid=(batch, MLA_HEADS),
        in_specs=[
            pl.BlockSpec((seq, QK_PAD), lambda b, h: (b, h)),
            pl.BlockSpec((seq, QK_PAD), lambda b, h: (b, h)),
            pl.BlockSpec((MLA_V, seq), lambda b, h: (h, b)),
        ],
        out_specs=pl.BlockSpec((seq, MLA_V), lambda b, h: (b, h)),
        compiler_params=_params("parallel", "parallel"),
        name="mla_attn",
    )(q, k, vt)


def _hgrn_masks():
    t = np.arange(HG_CHUNK)[:, None]
    s = np.arange(HG_CHUNK)[None, :]
    masks = [((t // c) % 2 == 1) & ((s // c) == (t // c) - 1) for c in HG_LEVELS]
    masks.append(t == s)
    return np.stack(masks).astype(np.float32)


def _hgrn_kernel(q_ref, f_ref, i_ref, g_ref, lbl_ref, gn_ref, msk_ref, tri_ref, o_ref, b_all):
    lbl = lbl_ref[...]
    e = jnp.exp(lbl - jnp.max(lbl, axis=0, keepdims=True))
    lb = e[0:1, :] / jnp.sum(e, axis=0, keepdims=True)
    gn = gn_ref[...]
    tri = tri_ref[...]
    sub = lax.broadcasted_iota(jnp.int32, (SUBLANES, LANES), 0)
    zeros8 = jnp.zeros((SUBLANES, LANES), F32)
    per_chunk = HG_CHUNK // SUBLANES
    chunk_rows = [slice(u * HG_CHUNK, (u + 1) * HG_CHUNK) for u in range(HG_UNROLL)]

    def sel(bounds, vals):
        out = vals[-1]
        for bound, val in zip(reversed(bounds), reversed(vals[:-1])):
            out = jnp.where(sub < bound, val, out)
        return out

    def head(gi):
        rows = slice(gi * HG_GROUP, (gi + 1) * HG_GROUP)
        q = q_ref[rows, :]
        v = i_ref[rows, :].astype(BF16)
        f = lb + (1.0 - lb) * _sigmoid(f_ref[rows, :])
        lg = jnp.log2(f)
        k = 1.0 - f

        l1 = lg.astype(BF16)
        l2 = (lg - l1.astype(F32)).astype(BF16)
        slabs = [slice(i, i + HG_SLAB) for i in range(0, HG_GROUP, HG_SLAB)]
        b = jnp.concatenate([_dot(tri, l1[r]) + _dot(tri, l2[r]) for r in slabs], axis=0)
        b_all[gi] = b
        return q, k, f, v, b

    def main(gi, hd, st):
        rows = slice(gi * HG_GROUP, (gi + 1) * HG_GROUP)
        q, k, f, v, b = hd
        b_scr = b_all.at[gi]
        cache = {}

        def rowb(u, r):
            if r < 0:
                return zeros8
            if (u, r) not in cache:
                cache[u, r] = jnp.broadcast_to(b_scr[u * HG_CHUNK + r:u * HG_CHUNK + r + 1, :], (SUBLANES, LANES))
            return cache[u, r]

        def build(fn):
            return jnp.concatenate([fn(u, j) for u in range(HG_UNROLL) for j in range(per_chunk)], axis=0)

        eye = msk_ref[len(HG_LEVELS)] != 0.0
        a = [jnp.where(eye, jnp.sum(q[r] * k[r], axis=-1, keepdims=True), 0.0) for r in chunk_rows]
        for ci, c in enumerate(HG_LEVELS):
            if c >= SUBLANES:
                m = c // SUBLANES
                bs = build(lambda u, j: rowb(u, SUBLANES * (j // m) * m - 1))
                be = build(lambda u, j: rowb(u, SUBLANES * ((j // m) * m + m) - 1))
            elif c == 4:
                bs = build(lambda u, j: sel((4,), (rowb(u, 8 * j - 1), rowb(u, 8 * j + 3))))
                be = build(lambda u, j: sel((4,), (rowb(u, 8 * j + 3), rowb(u, 8 * j + 7))))
            elif c == 2:
                bs = build(lambda u, j: sel((2, 4, 6), (rowb(u, 8 * j - 1), rowb(u, 8 * j + 1),
                                                         rowb(u, 8 * j + 3), rowb(u, 8 * j + 5))))
                be = build(lambda u, j: sel((2, 4, 6), (rowb(u, 8 * j + 1), rowb(u, 8 * j + 3),
                                                         rowb(u, 8 * j + 5), rowb(u, 8 * j + 7))))
            if c == 1:
                qd = (q * f).astype(BF16)
                kd = k.astype(BF16)
            else:
                qd = (q * jnp.exp2(b - bs)).astype(BF16)
                kd = (k * jnp.exp2(be - b)).astype(BF16)
            level = msk_ref[ci] != 0.0
            a = [jnp.where(level, _dot_nt(qd[r], kd[r]), a[u]) for u, r in enumerate(chunk_rows)]

        q_exp = (q * jnp.exp2(b)).astype(BF16)
        b_last = build(lambda u, j: rowb(u, HG_CHUNK - 1))
        k_dec = (k * jnp.exp2(b_last - b)).astype(BF16)
        o_intra = [_dot(a[u].astype(BF16), v[r]) for u, r in enumerate(chunk_rows)]
        st_add = [_dot_tn(v[r], k_dec[r]) for r in chunk_rows]

        o = []
        for u, r in enumerate(chunk_rows):
            o.append(o_intra[u] + _dot_nt(q_exp[r], st.astype(BF16)))
            st = jnp.exp2(rowb(u, HG_CHUNK - 1)[0:1, :]) * st + st_add[u]
        o = jnp.concatenate(o, axis=0)

        g = g_ref[rows, :]
        o_ref[rows, :] = o * _rms_scale(o) * gn * (g * _sigmoid(g))
        return st

    n_groups = q_ref.shape[0] // HG_GROUP
    st = jnp.zeros((HG_D, HG_D), F32)
    hd = head(0)
    for gi in range(n_groups):
        nxt = head(gi + 1) if gi + 1 < n_groups else None
        st = main(gi, hd, st)
        hd = nxt


def _hgrn(proj, lb_logits, out_norm, batch, seq):
    hblk = lambda col: pl.BlockSpec((seq, HG_D), lambda b, h: (b, col // HG_D + h))
    masks = jnp.asarray(_hgrn_masks())
    tri = np.kron(np.eye(HG_SLAB // HG_CHUNK), np.tril(np.ones((HG_CHUNK, HG_CHUNK))))
    tri = jnp.asarray(tri.astype(np.float32), dtype=BF16)
    nlev = masks.shape[0]
    return pl.pallas_call(
        _hgrn_kernel,
        out_shape=jax.ShapeDtypeStruct((batch * seq, HG_WIDTH), F32),
        grid=(batch, HG_HEADS),
        in_specs=[
            hblk(COL_HQ), hblk(COL_HF), hblk(COL_HI), hblk(COL_HG),
            pl.BlockSpec((lb_logits.shape[0], HG_D), lambda b, h: (0, h)),
            pl.BlockSpec((1, HG_D), lambda b, h: (0, 0)),
            pl.BlockSpec((nlev, HG_CHUNK, HG_CHUNK), lambda b, h: (0, 0, 0)),
            pl.BlockSpec((HG_SLAB, HG_SLAB), lambda b, h: (0, 0)),
        ],
        out_specs=pl.BlockSpec((seq, HG_D), lambda b, h: (b, h)),
        scratch_shapes=[pltpu.VMEM((seq // HG_GROUP, HG_GROUP, HG_D), F32)],
        compiler_params=_params("parallel", "parallel"),
        name="hgrn2",
    )(proj, proj, proj, proj, lb_logits, out_norm, masks, tri)


OUT_TM = 512


def _out_proj_kernel(a_ref, r_ref, x_ref, g_ref, w_ref, o_ref):
    a = a_ref[...]
    an = (a * _rms_scale(a) * g_ref[...]).astype(BF16)
    o_ref[...] = (x_ref[...] + _dot(an, w_ref[:MLA_WIDTH, :].astype(BF16))
                  + _dot(r_ref[...].astype(BF16), w_ref[MLA_WIDTH:, :].astype(BF16)))


def _out_proj(a, r, x, gain, w):
    t = x.shape[0]
    return pl.pallas_call(
        _out_proj_kernel,
        out_shape=jax.ShapeDtypeStruct((t, D_MODEL), F32),
        grid=(t // OUT_TM,),
        in_specs=[
            pl.BlockSpec((OUT_TM, MLA_WIDTH), lambda i: (i, 0)),
            pl.BlockSpec((OUT_TM, HG_WIDTH), lambda i: (i, 0)),
            pl.BlockSpec((OUT_TM, D_MODEL), lambda i: (i, 0)),
            pl.BlockSpec((1, MLA_WIDTH), lambda i: (0, 0)),
            pl.BlockSpec(w.shape, lambda i: (0, 0), pipeline_mode=pl.Buffered(1)),
        ],
        out_specs=pl.BlockSpec((OUT_TM, D_MODEL), lambda i: (i, 0)),
        compiler_params=_params("parallel"),
        name="out_proj",
    )(a, r, x, gain, w)


def _mem_kv_kernel(m_ref, g_ref, w_ref, kn_ref, k_ref, v_ref):
    m = m_ref[...]
    mn = (m * _rms_scale(m) * g_ref[...]).astype(BF16)
    kv = _dot(mn, w_ref[...].astype(BF16))
    for h in range(MEM_HEADS):
        k = kv[:, 2 * h * MEM_HD:(2 * h + 1) * MEM_HD]
        k_ref[:, h * MEM_HD:(h + 1) * MEM_HD] = (k * _rms_scale(k) * kn_ref[...]).astype(BF16)
        v_ref[:, h * MEM_HD:(h + 1) * MEM_HD] = kv[:, (2 * h + 1) * MEM_HD:(2 * h + 2) * MEM_HD].astype(BF16)


def _mem_kv(mem, gain, w, k_norm, batch, mem_len):
    return pl.pallas_call(
        _mem_kv_kernel,
        out_shape=(jax.ShapeDtypeStruct((batch * mem_len, MEM_WIDTH), BF16),
                   jax.ShapeDtypeStruct((batch * mem_len, MEM_WIDTH), BF16)),
        grid=(batch,),
        in_specs=[
            pl.BlockSpec((mem_len, D_MODEL), lambda b: (b, 0)),
            pl.BlockSpec((1, D_MODEL), lambda b: (0, 0)),
            pl.BlockSpec(w.shape, lambda b: (0, 0)),
            pl.BlockSpec((1, MEM_HD), lambda b: (0, 0)),
        ],
        out_specs=(pl.BlockSpec((mem_len, MEM_WIDTH), lambda b: (b, 0)),
                   pl.BlockSpec((mem_len, MEM_WIDTH), lambda b: (b, 0))),
        compiler_params=_params("parallel"),
        name="mem_kv",
    )(mem, gain, w, k_norm)


XA_TM = 512


def _xattn_kernel(x_ref, g_ref, wq_ref, qn_ref, k_ref, v_ref, wo_ref, o_ref, att_scr):
    x = x_ref[...]
    h = (x * _rms_scale(x) * g_ref[...]).astype(BF16)
    q = _dot(h, wq_ref[...].astype(BF16))
    heads = [slice(hd * MEM_HD, (hd + 1) * MEM_HD) for hd in range(MEM_HEADS)]
    scores = []
    for cols in heads:
        qh = q[:, cols]
        qh = (qh * (_rms_scale(qh) * XATT_Q_SCALE) * qn_ref[...]).astype(BF16)
        scores.append(_dot_nt(qh, k_ref[:, cols]))
    for cols, s in zip(heads, scores):
        p = jnp.exp2(s - jnp.max(s, axis=-1, keepdims=True))
        l = jnp.sum(p, axis=-1, keepdims=True)
        att_scr[:, cols] = (_dot(p.astype(BF16), v_ref[:, cols]) / l).astype(BF16)
    o_ref[...] = x + _dot(att_scr[...], wo_ref[...].astype(BF16))


def _xattn(x, gain, wq, q_norm, kx, vx, wo, seq, mem_len):
    t = x.shape[0]
    per_batch = seq // XA_TM
    return pl.pallas_call(
        _xattn_kernel,
        out_shape=jax.ShapeDtypeStruct((t, D_MODEL), F32),
        grid=(t // XA_TM,),
        in_specs=[
            pl.BlockSpec((XA_TM, D_MODEL), lambda i: (i, 0)),
            pl.BlockSpec((1, D_MODEL), lambda i: (0, 0)),
            pl.BlockSpec(wq.shape, lambda i: (0, 0)),
            pl.BlockSpec((1, MEM_HD), lambda i: (0, 0)),
            pl.BlockSpec((mem_len, MEM_WIDTH), lambda i: (i // per_batch, 0)),
            pl.BlockSpec((mem_len, MEM_WIDTH), lambda i: (i // per_batch, 0)),
            pl.BlockSpec(wo.shape, lambda i: (0, 0)),
        ],
        out_specs=pl.BlockSpec((XA_TM, D_MODEL), lambda i: (i, 0)),
        scratch_shapes=[pltpu.VMEM((XA_TM, MEM_WIDTH), BF16)],
        compiler_params=_params("parallel"),
        name="xattn",
    )(x, gain, wq, q_norm, kx, vx, wo)


def _pe_pair(x1, x2):
    z = jnp.zeros(x1.shape[:-1] + (LANES - MLA_ROPE,), x1.dtype)
    return jnp.concatenate([x1, x2, z], axis=-1), jnp.concatenate([x2, x1, z], axis=-1)


W_IN_TC = 512


def _w_in_prep_kernel(w_ref, o_ref):
    half = MLA_ROPE // 2
    kpe0 = MLA_Q_RANK + MLA_KV_RANK
    hg0 = kpe0 + MLA_ROPE
    zeros = jnp.zeros((LANES - MLA_ROPE, o_ref.shape[1]), BF16)
    o_ref[:kpe0, :] = w_ref[:kpe0, :].astype(BF16)
    o_ref[kpe0:COL_KPE, :] = w_ref[hg0:, :].astype(BF16)
    x1 = w_ref[kpe0:kpe0 + half, :].astype(BF16)
    x2 = w_ref[kpe0 + half:hg0, :].astype(BF16)
    o_ref[COL_KPE:COL_KPE + half, :] = x1
    o_ref[COL_KPE + half:COL_KPE + MLA_ROPE, :] = x2
    o_ref[COL_KPE + MLA_ROPE:COL_KPE_SW, :] = zeros
    o_ref[COL_KPE_SW:COL_KPE_SW + half, :] = x2
    o_ref[COL_KPE_SW + half:COL_KPE_SW + MLA_ROPE, :] = x1
    o_ref[COL_KPE_SW + MLA_ROPE:, :] = zeros


def _prep_w_in(w_in, layer):
    w_t = jnp.swapaxes(w_in, 1, 2)
    _, n, k = w_t.shape
    return pl.pallas_call(
        _w_in_prep_kernel,
        out_shape=jax.ShapeDtypeStruct((IN_COLS_PAD, k), BF16),
        grid=(k // W_IN_TC,),
        in_specs=[pl.BlockSpec((None, n, W_IN_TC), lambda i: (layer, 0, i))],
        out_specs=pl.BlockSpec((IN_COLS_PAD, W_IN_TC), lambda i: (0, i)),
        compiler_params=_params("parallel"),
        name="w_in_prep",
    )(w_t)


def _prep_w_q_up(w):
    w = w.astype(BF16).reshape(MLA_Q_RANK, MLA_HEADS, MLA_QK)
    half = MLA_ROPE // 2
    pe, pe_sw = _pe_pair(w[..., MLA_NOPE:MLA_NOPE + half], w[..., MLA_NOPE + half:])
    return jnp.concatenate([w[..., :MLA_NOPE], pe, pe_sw], axis=-1).reshape(MLA_Q_RANK, MLA_HEADS * Q_UP_PER_HEAD)


def _pe_gains(norm):
    half = MLA_ROPE // 2
    g1 = norm[MLA_NOPE:MLA_NOPE + half]
    g2 = norm[MLA_NOPE + half:]
    ga, gb = _pe_pair(g1, g2)
    sign = jnp.concatenate([-jnp.ones((half,), F32), jnp.ones((LANES - half,), F32)])
    return ga[None, :], (gb * sign)[None, :]


def kernel(x, mem, positions, ffn1_norm, ffn1_w_gate, ffn1_w_up, ffn1_w_down, mix_norm, w_in, mla_q_a_norm, mla_w_q_up, mla_kv_a_norm, mla_w_kv_up, mla_q_norm, mla_k_norm, mla_out_norm, hg_lb_logits, hg_out_norm, w_out, xattn_norm, mem_norm, xattn_w_q, xattn_w_kv, xattn_q_norm, xattn_k_norm, xattn_w_o, ffn2_norm, ffn2_w_gate, ffn2_w_up, ffn2_w_down):
    batch, seq, _ = x.shape
    mem_len = mem.shape[1]
    depth = ffn1_norm.shape[0]
    assert depth == 1 and seq % ATT_T == 0 and seq % XA_TM == 0
    t = batch * seq
    xt = x.reshape(t, D_MODEL)
    half = MLA_ROPE // 2
    quarters = LANES // half
    pos = positions.reshape(t // PREP_TM, quarters, PREP_TM // quarters).transpose(0, 2, 1)
    pos = jnp.repeat(pos, half, axis=-1).reshape(t // quarters, LANES)
    inv_freq = ROPE_BASE ** (-np.arange(half, dtype=np.float32) / half)
    invf = jnp.asarray(np.tile(inv_freq, quarters)[None, :])
    l = 0

    xt = _ffn(xt, ffn1_norm[l][None, :], ffn1_w_gate[l], ffn1_w_up[l], ffn1_w_down[l])

    proj = _in_proj(xt, mix_norm[l][None, :], _prep_w_in(w_in, l))
    qpa, qpb = _pe_gains(mla_q_norm[l])
    kpa, kpb = _pe_gains(mla_k_norm[l])
    q, k, v = _mla_prep(proj, pos, invf, mla_q_a_norm[l][None, :], mla_kv_a_norm[l][None, :],
                        _prep_w_q_up(mla_w_q_up[l]), mla_w_kv_up[l],
                        mla_q_norm[l][None, :MLA_NOPE], qpa, qpb,
                        mla_k_norm[l][None, :MLA_NOPE], kpa, kpb)
    a = _mla_attn(q, k, v, batch, seq)
    r = _hgrn(proj, hg_lb_logits, hg_out_norm[l][None, :], batch, seq)
    xt = _out_proj(a, r, xt, mla_out_norm[l][None, :], w_out[l])

    kx, vx = _mem_kv(mem.reshape(batch * mem_len, D_MODEL), mem_norm[l][None, :],
                     xattn_w_kv[l], xattn_k_norm[l][None, :], batch, mem_len)
    xt = _xattn(xt, xattn_norm[l][None, :], xattn_w_q[l], xattn_q_norm[l][None, :],
                kx, vx, xattn_w_o[l], seq, mem_len)

    xt = _ffn(xt, ffn2_norm[l][None, :], ffn2_w_gate[l], ffn2_w_up[l], ffn2_w_down[l])
    return xt.reshape(batch, seq, D_MODEL)
```

```python
import functools

import numpy as np
import jax
import jax.numpy as jnp
from jax import lax
from jax.experimental import pallas as pl
from jax.experimental.pallas import tpu as pltpu

F32 = jnp.float32
BF16 = jnp.bfloat16

EPS = 1e-6
ROPE_BASE = 10000.0
LANES = 128
SUBLANES = 8

D_MODEL = 2048
D_FF = 5504
FFN_TM = 1024
FFN_TF = 512

MLA_HEADS = 8
MLA_NOPE = 128
MLA_ROPE = 64
MLA_QK = MLA_NOPE + MLA_ROPE
MLA_V = 128
MLA_Q_RANK = 512
MLA_KV_RANK = 256
MLA_WIDTH = MLA_HEADS * MLA_V
QK_PAD = 256
Q_UP_PER_HEAD = 384
ATT_Q_SCALE = float(MLA_QK ** -0.5 * np.log2(np.e))

HG_HEADS = 8
HG_D = 128
HG_CHUNK = 64
HG_LEVELS = (32, 16, 8, 4, 2, 1)
HG_UNROLL = 8
HG_GROUP = HG_CHUNK * HG_UNROLL
HG_SLAB = 256
HG_WIDTH = HG_HEADS * HG_D

MEM_HEADS = 4
MEM_HD = 128
MEM_WIDTH = MEM_HEADS * MEM_HD
XATT_Q_SCALE = float(MEM_HD ** -0.5 * np.log2(np.e))

COL_CQ = 0
COL_CKV = 512
COL_HQ = 768
COL_HF = COL_HQ + HG_WIDTH
COL_HI = COL_HF + HG_WIDTH
COL_HG = COL_HI + HG_WIDTH
COL_KPE = COL_HG + HG_WIDTH
COL_KPE_SW = COL_KPE + LANES
IN_COLS_PAD = COL_KPE_SW + LANES

VMEM_LIMIT = 56 * 1024 * 1024
FFN_VMEM_LIMIT = 61 * 1024 * 1024


def _params(*sem, vmem_limit=VMEM_LIMIT):
    return pltpu.CompilerParams(dimension_semantics=sem, vmem_limit_bytes=vmem_limit)


def _rms_scale(x):
    return lax.rsqrt(jnp.mean(x * x, axis=-1, keepdims=True) + EPS)


def _sigmoid(x):
    return 1.0 / (1.0 + jnp.exp(-x))


def _dot(a, b):
    return jnp.dot(a, b, preferred_element_type=F32)


def _dot_nt(a, b):
    return lax.dot_general(a, b, (((1,), (1,)), ((), ())), preferred_element_type=F32)


def _dot_tn(a, b):
    return lax.dot_general(a, b, (((0,), (0,)), ((), ())), preferred_element_type=F32)


def _ffn_kernel(x_hbm, g_ref, wg_ref, wu_ref, wd_ref, o_ref, h_scr, x_scr, x_sem):
    i = pl.program_id(0)

    def x_copy(tile):
        rows = pl.ds(pl.multiple_of(tile * FFN_TM, FFN_TM), FFN_TM)
        return pltpu.make_async_copy(x_hbm.at[rows, :], x_scr, x_sem)

    def branch(h):
        gate = _dot(h, wg_ref[...].astype(BF16))
        up = _dot(h, wu_ref[...].astype(BF16))
        a = 0.5 * gate * _sigmoid(gate) * up
        f0 = pl.program_id(1) * FFN_TF
        a_ok = f0 + lax.broadcasted_iota(jnp.int32, (1, FFN_TF), 1) < D_FF
        w_ok = f0 + lax.broadcasted_iota(jnp.int32, (FFN_TF, 1), 0) < D_FF
        a = jnp.where(a_ok, a, 0.0).astype(BF16)
        wd = jnp.where(w_ok, wd_ref[...], 0.0).astype(BF16)
        return _dot(a, wd)

    @pl.when(pl.program_id(1) == 0)
    def _():
        @pl.when(i == 0)
        def _():
            x_copy(0).start()

        x_copy(i).wait()
        x = x_scr[...]
        h = (x * _rms_scale(x) * g_ref[...]).astype(BF16)
        h_scr[...] = h
        o_ref[...] = x + branch(h)

        @pl.when(i + 1 < pl.num_programs(0))
        def _():
            x_copy(i + 1).start()

    @pl.when(pl.program_id(1) > 0)
    def _():
        o_ref[...] += branch(h_scr[...])


def _ffn(x, gain, wg, wu, wd):
    t = x.shape[0]
    return pl.pallas_call(
        _ffn_kernel,
        out_shape=jax.ShapeDtypeStruct((t, D_MODEL), F32),
        grid=(t // FFN_TM, pl.cdiv(D_FF, FFN_TF)),
        in_specs=[
            pl.BlockSpec(memory_space=pl.ANY),
            pl.BlockSpec((1, D_MODEL), lambda i, f: (0, 0)),
            pl.BlockSpec((D_MODEL, FFN_TF), lambda i, f: (0, f)),
            pl.BlockSpec((D_MODEL, FFN_TF), lambda i, f: (0, f)),
            pl.BlockSpec((FFN_TF, D_MODEL), lambda i, f: (f, 0)),
        ],
        out_specs=pl.BlockSpec((FFN_TM, D_MODEL), lambda i, f: (i, 0)),
        scratch_shapes=[pltpu.VMEM((FFN_TM, D_MODEL), BF16), pltpu.VMEM((FFN_TM, D_MODEL), F32),
                        pltpu.SemaphoreType.DMA],
        compiler_params=_params("arbitrary", "arbitrary", vmem_limit=FFN_VMEM_LIMIT),
        name="ffn",
    )(x, gain, wg, wu, wd)


PROJ_TM = 1024
PROJ_TN = 1024


def _in_proj_kernel(x_ref, g_ref, w_ref, o_ref, h_scr):
    @pl.when(pl.program_id(1) == 0)
    def _():
        x = x_ref[...]
        h = (x * _rms_scale(x) * g_ref[...]).astype(BF16)
        h_scr[...] = h
        o_ref[...] = _dot_nt(h, w_ref[...])

    @pl.when(pl.program_id(1) > 0)
    def _():
        o_ref[...] = _dot_nt(h_scr[...], w_ref[...])


def _in_proj(x, gain, w_t):
    t = x.shape[0]
    n = w_t.shape[0]
    return pl.pallas_call(
        _in_proj_kernel,
        out_shape=jax.ShapeDtypeStruct((t, n), F32),
        grid=(t // PROJ_TM, n // PROJ_TN),
        in_specs=[
            pl.BlockSpec((PROJ_TM, D_MODEL), lambda i, j: (i, 0)),
            pl.BlockSpec((1, D_MODEL), lambda i, j: (0, 0)),
            pl.BlockSpec((PROJ_TN, D_MODEL), lambda i, j: (j, 0)),
        ],
        out_specs=pl.BlockSpec((PROJ_TM, PROJ_TN), lambda i, j: (i, j)),
        scratch_shapes=[pltpu.VMEM((PROJ_TM, D_MODEL), BF16)],
        compiler_params=_params("parallel", "arbitrary"),
        name="in_proj",
    )(x, gain, w_t)


PREP_TM = 512


def _mla_prep_kernel(cq_ref, ckv_ref, kpe_ref, kpesw_ref, pos_ref, invf_ref,
                     qa_ref, kva_ref, wq_ref, wkv_ref,
                     qn_ref, qpa_ref, qpb_ref, kn_ref, kpa_ref, kpb_ref,
                     q_ref, k_ref, vt_ref):
    ang = pos_ref[...].astype(F32) * invf_ref[...]
    cos4 = jnp.cos(ang)
    sin4 = jnp.sin(ang)
    lane = lax.broadcasted_iota(jnp.int32, (1, LANES), 1)
    half = MLA_ROPE // 2

    def expand(t4):
        parts = []
        for part in range(LANES // half):
            t = pltpu.roll(t4, (LANES - half * part) % LANES, 1) if part else t4
            parts.append(jnp.where(lane < half, t, pltpu.roll(t, half, 1)))
        return jnp.concatenate(parts, axis=0)

    cos = expand(cos4)
    sin = expand(sin4)

    cq = cq_ref[...]
    cqn = (cq * _rms_scale(cq) * qa_ref[...]).astype(BF16)
    qall = _dot(cqn, wq_ref[...])
    q_cos = qpa_ref[...] * cos
    q_sin = qpb_ref[...] * sin
    for h in range(MLA_HEADS):
        base = h * Q_UP_PER_HEAD
        qn = qall[:, base:base + LANES]
        y = qall[:, base + LANES:base + 2 * LANES]
        ysw = qall[:, base + 2 * LANES:base + 3 * LANES]
        ss = jnp.sum(qn * qn, axis=-1, keepdims=True) + jnp.sum(y * y, axis=-1, keepdims=True)
        r = lax.rsqrt(ss * (1.0 / MLA_QK) + EPS) * ATT_Q_SCALE
        q_ref[:, h * QK_PAD:h * QK_PAD + LANES] = (qn * r * qn_ref[...]).astype(BF16)
        q_ref[:, h * QK_PAD + LANES:(h + 1) * QK_PAD] = ((y * q_cos + ysw * q_sin) * r).astype(BF16)

    ckv = ckv_ref[...]
    ckvn = (ckv * _rms_scale(ckv) * kva_ref[...]).astype(BF16)
    kvall = _dot(ckvn, wkv_ref[...].astype(BF16))
    yk = kpe_ref[...]
    kr = yk * (kpa_ref[...] * cos) + kpesw_ref[...] * (kpb_ref[...] * sin)
    ss_pe = jnp.sum(yk * yk, axis=-1, keepdims=True)
    for h in range(MLA_HEADS):
        kn = kvall[:, h * 2 * LANES:h * 2 * LANES + LANES]
        ss = jnp.sum(kn * kn, axis=-1, keepdims=True) + ss_pe
        r = lax.rsqrt(ss * (1.0 / MLA_QK) + EPS)
        k_ref[:, h * QK_PAD:h * QK_PAD + LANES] = (kn * r * kn_ref[...]).astype(BF16)
        k_ref[:, h * QK_PAD + LANES:(h + 1) * QK_PAD] = (kr * r).astype(BF16)
        vt_ref[h * MLA_V:(h + 1) * MLA_V, :] = kvall[:, h * 2 * LANES + LANES:(h + 1) * 2 * LANES].T.astype(BF16)


def _mla_prep(proj, pos, invf, qa, kva, wq, wkv, qn, qpa, qpb, kn, kpa, kpb):
    t = proj.shape[0]
    tm = PREP_TM
    row = lambda i: (0, 0)
    vec = lambda n: pl.BlockSpec((1, n), row)
    return pl.pallas_call(
        _mla_prep_kernel,
        out_shape=(jax.ShapeDtypeStruct((t, MLA_HEADS * QK_PAD), BF16),
                   jax.ShapeDtypeStruct((t, MLA_HEADS * QK_PAD), BF16),
                   jax.ShapeDtypeStruct((MLA_WIDTH, t), BF16)),
        grid=(t // tm,),
        in_specs=[
            pl.BlockSpec((tm, MLA_Q_RANK), lambda i: (i, COL_CQ // MLA_Q_RANK)),
            pl.BlockSpec((tm, MLA_KV_RANK), lambda i: (i, COL_CKV // MLA_KV_RANK)),
            pl.BlockSpec((tm, LANES), lambda i: (i, COL_KPE // LANES)),
            pl.BlockSpec((tm, LANES), lambda i: (i, COL_KPE_SW // LANES)),
            pl.BlockSpec((tm // (LANES // (MLA_ROPE // 2)), LANES), lambda i: (i, 0)),
            vec(LANES),
            vec(MLA_Q_RANK), vec(MLA_KV_RANK),
            pl.BlockSpec(wq.shape, row), pl.BlockSpec(wkv.shape, row),
            vec(LANES), vec(LANES), vec(LANES), vec(LANES), vec(LANES), vec(LANES),
        ],
        out_specs=(pl.BlockSpec((tm, MLA_HEADS * QK_PAD), lambda i: (i, 0)),
                   pl.BlockSpec((tm, MLA_HEADS * QK_PAD), lambda i: (i, 0)),
                   pl.BlockSpec((MLA_WIDTH, tm), lambda i: (0, i))),
        compiler_params=_params("parallel"),
        name="mla_prep",
    )(proj, proj, proj, proj, pos, invf, qa, kva, wq, wkv, qn, qpa, qpb, kn, kpa, kpb)


ATT_T = 256
ATT_AHEAD = 3


def _mla_attn_kernel(q_ref, k_ref, vt_ref, o_ref):
    kpos = lax.broadcasted_iota(jnp.int32, (ATT_T, ATT_T), 0)
    qpos = lax.broadcasted_iota(jnp.int32, (ATT_T, ATT_T), 1)
    nq = q_ref.shape[0] // ATT_T

    def scores(i):
        lo, hi = i * ATT_T, (i + 1) * ATT_T
        q = q_ref[lo:hi, :]
        s_diag = jnp.where(kpos <= qpos, _dot_nt(k_ref[lo:hi, :], q), -jnp.inf)
        s_off = _dot_nt(k_ref[0:lo, :], q) if i > 0 else None
        return s_diag, s_off

    def finish(i, s_diag, s_off):
        lo, hi = i * ATT_T, (i + 1) * ATT_T
        m = jnp.max(s_diag, axis=0, keepdims=True)
        if i > 0:
            m = jnp.maximum(m, jnp.max(s_off, axis=0, keepdims=True))
        p = jnp.exp2(s_diag - m)
        l = jnp.sum(p, axis=0, keepdims=True)
        acc = _dot(vt_ref[:, lo:hi], p.astype(BF16))
        if i > 0:
            p = jnp.exp2(s_off - m)
            l = l + jnp.sum(p, axis=0, keepdims=True)
            acc = acc + _dot(vt_ref[:, 0:lo], p.astype(BF16))
        o_ref[lo:hi, :] = (acc / l).T

    order = list(reversed(range(nq)))
    pending = [scores(i) for i in order[:ATT_AHEAD]]
    for n, i in enumerate(order):
        if n + ATT_AHEAD < nq:
            pending.append(scores(order[n + ATT_AHEAD]))
        finish(i, *pending.pop(0))


def _mla_attn(q, k, vt, batch, seq):
    return pl.pallas_call(
        _mla_attn_kernel,
        out_shape=jax.ShapeDtypeStruct((batch * seq, MLA_WIDTH), F32),
        grid=(batch, MLA_HEADS),
        in_specs=[
            pl.BlockSpec((seq, QK_PAD), lambda b, h: (b, h)),
            pl.BlockSpec((seq, QK_PAD), lambda b, h: (b, h)),
            pl.BlockSpec((MLA_V, seq), lambda b, h: (h, b)),
        ],
        out_specs=pl.BlockSpec((seq, MLA_V), lambda b, h: (b, h)),
        compiler_params=_params("parallel", "parallel"),
        name="mla_attn",
    )(q, k, vt)


def _hgrn_masks():
    t = np.arange(HG_CHUNK)[:, None]
    s = np.arange(HG_CHUNK)[None, :]
    masks = [((t // c) % 2 == 1) & ((s // c) == (t // c) - 1) for c in HG_LEVELS]
    masks.append(t == s)
    return np.stack(masks).astype(np.float32)


def _hgrn_kernel(q_ref, f_ref, i_ref, g_ref, lbl_ref, gn_ref, msk_ref, tri_ref, o_ref, b_all):
    lbl = lbl_ref[...]
    e = jnp.exp(lbl - jnp.max(lbl, axis=0, keepdims=True))
    lb = e[0:1, :] / jnp.sum(e, axis=0, keepdims=True)
    gn = gn_ref[...]
    tri = tri_ref[...]
    sub = lax.broadcasted_iota(jnp.int32, (SUBLANES, LANES), 0)
    zeros8 = jnp.zeros((SUBLANES, LANES), F32)
    per_chunk = HG_CHUNK // SUBLANES
    chunk_rows = [slice(u * HG_CHUNK, (u + 1) * HG_CHUNK) for u in range(HG_UNROLL)]

    def sel(bounds, vals):
        out = vals[-1]
        for bound, val in zip(reversed(bounds), reversed(vals[:-1])):
            out = jnp.where(sub < bound, val, out)
        return out

    def head(gi):
        rows = slice(gi * HG_GROUP, (gi + 1) * HG_GROUP)
        q = q_ref[rows, :]
        v = i_ref[rows, :].astype(BF16)
        f = lb + (1.0 - lb) * _sigmoid(f_ref[rows, :])
        lg = jnp.log2(f)
        k = 1.0 - f

        l1 = lg.astype(BF16)
        l2 = (lg - l1.astype(F32)).astype(BF16)
        slabs = [slice(i, i + HG_SLAB) for i in range(0, HG_GROUP, HG_SLAB)]
        b = jnp.concatenate([_dot(tri, l1[r]) + _dot(tri, l2[r]) for r in slabs], axis=0)
        b_all[gi] = b
        return q, k, f, v, b

    def main(gi, hd, st):
        rows = slice(gi * HG_GROUP, (gi + 1) * HG_GROUP)
        q, k, f, v, b = hd
        b_scr = b_all.at[gi]
        cache = {}

        def rowb(u, r):
            if r < 0:
                return zeros8
            if (u, r) not in cache:
                cache[u, r] = jnp.broadcast_to(b_scr[u * HG_CHUNK + r:u * HG_CHUNK + r + 1, :], (SUBLANES, LANES))
            return cache[u, r]

        def build(fn):
            return jnp.concatenate([fn(u, j) for u in range(HG_UNROLL) for j in range(per_chunk)], axis=0)

        eye = msk_ref[len(HG_LEVELS)] != 0.0
        a = [jnp.where(eye, jnp.sum(q[r] * k[r], axis=-1, keepdims=True), 0.0) for r in chunk_rows]
        for ci, c in enumerate(HG_LEVELS):
            if c >= SUBLANES:
                m = c // SUBLANES
                bs = build(lambda u, j: rowb(u, SUBLANES * (j // m) * m - 1))
                be = build(lambda u, j: rowb(u, SUBLANES * ((j // m) * m + m) - 1))
            elif c == 4:
                bs = build(lambda u, j: sel((4,), (rowb(u, 8 * j - 1), rowb(u, 8 * j + 3))))
                be = build(lambda u, j: sel((4,), (rowb(u, 8 * j + 3), rowb(u, 8 * j + 7))))
            elif c == 2:
                bs = build(lambda u, j: sel((2, 4, 6), (rowb(u, 8 * j - 1), rowb(u, 8 * j + 1),
                                                         rowb(u, 8 * j + 3), rowb(u, 8 * j + 5))))
                be = build(lambda u, j: sel((2, 4, 6), (rowb(u, 8 * j + 1), rowb(u, 8 * j + 3),
                                                         rowb(u, 8 * j + 5), rowb(u, 8 * j + 7))))
            if c == 1:
                qd = (q * f).astype(BF16)
                kd = k.astype(BF16)
            else:
                qd = (q * jnp.exp2(b - bs)).astype(BF16)
                kd = (k * jnp.exp2(be - b)).astype(BF16)
            level = msk_ref[ci] != 0.0
            a = [jnp.where(level, _dot_nt(qd[r], kd[r]), a[u]) for u, r in enumerate(chunk_rows)]

        q_exp = (q * jnp.exp2(b)).astype(BF16)
        b_last = build(lambda u, j: rowb(u, HG_CHUNK - 1))
        k_dec = (k * jnp.exp2(b_last - b)).astype(BF16)
        o_intra = [_dot(a[u].astype(BF16), v[r]) for u, r in enumerate(chunk_rows)]
        st_add = [_dot_tn(v[r], k_dec[r]) for r in chunk_rows]

        o = []
        for u, r in enumerate(chunk_rows):
            o.append(o_intra[u] + _dot_nt(q_exp[r], st.astype(BF16)))
            st = jnp.exp2(rowb(u, HG_CHUNK - 1)[0:1, :]) * st + st_add[u]
        o = jnp.concatenate(o, axis=0)

        g = g_ref[rows, :]
        o_ref[rows, :] = o * _rms_scale(o) * gn * (g * _sigmoid(g))
        return st

    n_groups = q_ref.shape[0] // HG_GROUP
    st = jnp.zeros((HG_D, HG_D), F32)
    hd = head(0)
    for gi in range(n_groups):
        nxt = head(gi + 1) if gi + 1 < n_groups else None
        st = main(gi, hd, st)
        hd = nxt


def _hgrn(proj, lb_logits, out_norm, batch, seq):
    hblk = lambda col: pl.BlockSpec((seq, HG_D), lambda b, h: (b, col // HG_D + h))
    masks = jnp.asarray(_hgrn_masks())
    tri = np.kron(np.eye(HG_SLAB // HG_CHUNK), np.tril(np.ones((HG_CHUNK, HG_CHUNK))))
    tri = jnp.asarray(tri.astype(np.float32), dtype=BF16)
    nlev = masks.shape[0]
    return pl.pallas_call(
        _hgrn_kernel,
        out_shape=jax.ShapeDtypeStruct((batch * seq, HG_WIDTH), F32),
        grid=(batch, HG_HEADS),
        in_specs=[
            hblk(COL_HQ), hblk(COL_HF), hblk(COL_HI), hblk(COL_HG),
            pl.BlockSpec((lb_logits.shape[0], HG_D), lambda b, h: (0, h)),
            pl.BlockSpec((1, HG_D), lambda b, h: (0, 0)),
            pl.BlockSpec((nlev, HG_CHUNK, HG_CHUNK), lambda b, h: (0, 0, 0)),
            pl.BlockSpec((HG_SLAB, HG_SLAB), lambda b, h: (0, 0)),
        ],
        out_specs=pl.BlockSpec((seq, HG_D), lambda b, h: (b, h)),
        scratch_shapes=[pltpu.VMEM((seq // HG_GROUP, HG_GROUP, HG_D), F32)],
        compiler_params=_params("parallel", "parallel"),
        name="hgrn2",
    )(proj, proj, proj, proj, lb_logits, out_norm, masks, tri)


OUT_TM = 512


def _out_proj_kernel(a_ref, r_ref, x_ref, g_ref, w_ref, o_ref):
    a = a_ref[...]
    an = (a * _rms_scale(a) * g_ref[...]).astype(BF16)
    o_ref[...] = (x_ref[...] + _dot(an, w_ref[:MLA_WIDTH, :].astype(BF16))
                  + _dot(r_ref[...].astype(BF16), w_ref[MLA_WIDTH:, :].astype(BF16)))


def _out_proj(a, r, x, gain, w):
    t = x.shape[0]
    return pl.pallas_call(
        _out_proj_kernel,
        out_shape=jax.ShapeDtypeStruct((t, D_MODEL), F32),
        grid=(t // OUT_TM,),
        in_specs=[
            pl.BlockSpec((OUT_TM, MLA_WIDTH), lambda i: (i, 0)),
            pl.BlockSpec((OUT_TM, HG_WIDTH), lambda i: (i, 0)),
            pl.BlockSpec((OUT_TM, D_MODEL), lambda i: (i, 0)),
            pl.BlockSpec((1, MLA_WIDTH), lambda i: (0, 0)),
            pl.BlockSpec(w.shape, lambda i: (0, 0), pipeline_mode=pl.Buffered(1)),
        ],
        out_specs=pl.BlockSpec((OUT_TM, D_MODEL), lambda i: (i, 0)),
        compiler_params=_params("parallel"),
        name="out_proj",
    )(a, r, x, gain, w)


def _mem_kv_kernel(m_ref, g_ref, w_ref, kn_ref, k_ref, v_ref):
    m = m_ref[...]
    mn = (m * _rms_scale(m) * g_ref[...]).astype(BF16)
    kv = _dot(mn, w_ref[...].astype(BF16))
    for h in range(MEM_HEADS):
        k = kv[:, 2 * h * MEM_HD:(2 * h + 1) * MEM_HD]
        k_ref[:, h * MEM_HD:(h + 1) * MEM_HD] = (k * _rms_scale(k) * kn_ref[...]).astype(BF16)
        v_ref[:, h * MEM_HD:(h + 1) * MEM_HD] = kv[:, (2 * h + 1) * MEM_HD:(2 * h + 2) * MEM_HD].astype(BF16)


def _mem_kv(mem, gain, w, k_norm, batch, mem_len):
    return pl.pallas_call(
        _mem_kv_kernel,
        out_shape=(jax.ShapeDtypeStruct((batch * mem_len, MEM_WIDTH), BF16),
                   jax.ShapeDtypeStruct((batch * mem_len, MEM_WIDTH), BF16)),
        grid=(batch,),
        in_specs=[
            pl.BlockSpec((mem_len, D_MODEL), lambda b: (b, 0)),
            pl.BlockSpec((1, D_MODEL), lambda b: (0, 0)),
            pl.BlockSpec(w.shape, lambda b: (0, 0)),
            pl.BlockSpec((1, MEM_HD), lambda b: (0, 0)),
        ],
        out_specs=(pl.BlockSpec((mem_len, MEM_WIDTH), lambda b: (b, 0)),
                   pl.BlockSpec((mem_len, MEM_WIDTH), lambda b: (b, 0))),
        compiler_params=_params("parallel"),
        name="mem_kv",
    )(mem, gain, w, k_norm)


XA_TM = 512


def _xattn_kernel(x_ref, g_ref, wq_ref, qn_ref, k_ref, v_ref, wo_ref, o_ref, att_scr):
    x = x_ref[...]
    h = (x * _rms_scale(x) * g_ref[...]).astype(BF16)
    q = _dot(h, wq_ref[...].astype(BF16))
    heads = [slice(hd * MEM_HD, (hd + 1) * MEM_HD) for hd in range(MEM_HEADS)]
    scores = []
    for cols in heads:
        qh = q[:, cols]
        qh = (qh * (_rms_scale(qh) * XATT_Q_SCALE) * qn_ref[...]).astype(BF16)
        scores.append(_dot_nt(qh, k_ref[:, cols]))
    for cols, s in zip(heads, scores):
        p = jnp.exp2(s - jnp.max(s, axis=-1, keepdims=True))
        l = jnp.sum(p, axis=-1, keepdims=True)
        att_scr[:, cols] = (_dot(p.astype(BF16), v_ref[:, cols]) / l).astype(BF16)
    o_ref[...] = x + _dot(att_scr[...], wo_ref[...].astype(BF16))


def _xattn(x, gain, wq, q_norm, kx, vx, wo, seq, mem_len):
    t = x.shape[0]
    per_batch = seq // XA_TM
    return pl.pallas_call(
        _xattn_kernel,
        out_shape=jax.ShapeDtypeStruct((t, D_MODEL), F32),
        grid=(t // XA_TM,),
        in_specs=[
            pl.BlockSpec((XA_TM, D_MODEL), lambda i: (i, 0)),
            pl.BlockSpec((1, D_MODEL), lambda i: (0, 0)),
            pl.BlockSpec(wq.shape, lambda i: (0, 0)),
            pl.BlockSpec((1, MEM_HD), lambda i: (0, 0)),
            pl.BlockSpec((mem_len, MEM_WIDTH), lambda i: (i // per_batch, 0)),
            pl.BlockSpec((mem_len, MEM_WIDTH), lambda i: (i // per_batch, 0)),
            pl.BlockSpec(wo.shape, lambda i: (0, 0)),
        ],
        out_specs=pl.BlockSpec((XA_TM, D_MODEL), lambda i: (i, 0)),
        scratch_shapes=[pltpu.VMEM((XA_TM, MEM_WIDTH), BF16)],
        compiler_params=_params("parallel"),
        name="xattn",
    )(x, gain, wq, q_norm, kx, vx, wo)


def _pe_pair(x1, x2):
    z = jnp.zeros(x1.shape[:-1] + (LANES - MLA_ROPE,), x1.dtype)
    return jnp.concatenate([x1, x2, z], axis=-1), jnp.concatenate([x2, x1, z], axis=-1)


W_IN_TC = 512


def _w_in_prep_kernel(w_ref, o_ref):
    half = MLA_ROPE // 2
    kpe0 = MLA_Q_RANK + MLA_KV_RANK
    hg0 = kpe0 + MLA_ROPE
    zeros = jnp.zeros((LANES - MLA_ROPE, o_ref.shape[1]), BF16)
    o_ref[:kpe0, :] = w_ref[:kpe0, :].astype(BF16)
    o_ref[kpe0:COL_KPE, :] = w_ref[hg0:, :].astype(BF16)
    x1 = w_ref[kpe0:kpe0 + half, :].astype(BF16)
    x2 = w_ref[kpe0 + half:hg0, :].astype(BF16)
    o_ref[COL_KPE:COL_KPE + half, :] = x1
    o_ref[COL_KPE + half:COL_KPE + MLA_ROPE, :] = x2
    o_ref[COL_KPE + MLA_ROPE:COL_KPE_SW, :] = zeros
    o_ref[COL_KPE_SW:COL_KPE_SW + half, :] = x2
    o_ref[COL_KPE_SW + half:COL_KPE_SW + MLA_ROPE, :] = x1
    o_ref[COL_KPE_SW + MLA_ROPE:, :] = zeros


def _prep_w_in(w_in, layer):
    w_t = jnp.swapaxes(w_in, 1, 2)
    _, n, k = w_t.shape
    return pl.pallas_call(
        _w_in_prep_kernel,
        out_shape=jax.ShapeDtypeStruct((IN_COLS_PAD, k), BF16),
        grid=(k // W_IN_TC,),
        in_specs=[pl.BlockSpec((None, n, W_IN_TC), lambda i: (layer, 0, i))],
        out_specs=pl.BlockSpec((IN_COLS_PAD, W_IN_TC), lambda i: (0, i)),
        compiler_params=_params("parallel"),
        name="w_in_prep",
    )(w_t)


def _prep_w_q_up(w):
    w = w.astype(BF16).reshape(MLA_Q_RANK, MLA_HEADS, MLA_QK)
    half = MLA_ROPE // 2
    pe, pe_sw = _pe_pair(w[..., MLA_NOPE:MLA_NOPE + half], w[..., MLA_NOPE + half:])
    return jnp.concatenate([w[..., :MLA_NOPE], pe, pe_sw], axis=-1).reshape(MLA_Q_RANK, MLA_HEADS * Q_UP_PER_HEAD)


def _pe_gains(norm):
    half = MLA_ROPE // 2
    g1 = norm[MLA_NOPE:MLA_NOPE + half]
    g2 = norm[MLA_NOPE + half:]
    ga, gb = _pe_pair(g1, g2)
    sign = jnp.concatenate([-jnp.ones((half,), F32), jnp.ones((LANES - half,), F32)])
    return ga[None, :], (gb * sign)[None, :]


def kernel(x, mem, positions, ffn1_norm, ffn1_w_gate, ffn1_w_up, ffn1_w_down, mix_norm, w_in, mla_q_a_norm, mla_w_q_up, mla_kv_a_norm, mla_w_kv_up, mla_q_norm, mla_k_norm, mla_out_norm, hg_lb_logits, hg_out_norm, w_out, xattn_norm, mem_norm, xattn_w_q, xattn_w_kv, xattn_q_norm, xattn_k_norm, xattn_w_o, ffn2_norm, ffn2_w_gate, ffn2_w_up, ffn2_w_down):
    batch, seq, _ = x.shape
    mem_len = mem.shape[1]
    depth = ffn1_norm.shape[0]
    assert depth == 1 and seq % ATT_T == 0 and seq % XA_TM == 0
    t = batch * seq
    xt = x.reshape(t, D_MODEL)
    half = MLA_ROPE // 2
    quarters = LANES // half
    pos = positions.reshape(t // PREP_TM, quarters, PREP_TM // quarters).transpose(0, 2, 1)
    pos = jnp.repeat(pos, half, axis=-1).reshape(t // quarters, LANES)
    inv_freq = ROPE_BASE ** (-np.arange(half, dtype=np.float32) / half)
    invf = jnp.asarray(np.tile(inv_freq, quarters)[None, :])
    l = 0

    xt = _ffn(xt, ffn1_norm[l][None, :], ffn1_w_gate[l], ffn1_w_up[l], ffn1_w_down[l])

    proj = _in_proj(xt, mix_norm[l][None, :], _prep_w_in(w_in, l))
    qpa, qpb = _pe_gains(mla_q_norm[l])
    kpa, kpb = _pe_gains(mla_k_norm[l])
    q, k, v = _mla_prep(proj, pos, invf, mla_q_a_norm[l][None, :], mla_kv_a_norm[l][None, :],
                        _prep_w_q_up(mla_w_q_up[l]), mla_w_kv_up[l],
                        mla_q_norm[l][None, :MLA_NOPE], qpa, qpb,
                        mla_k_norm[l][None, :MLA_NOPE], kpa, kpb)
    a = _mla_attn(q, k, v, batch, seq)
    r = _hgrn(proj, hg_lb_logits, hg_out_norm[l][None, :], batch, seq)
    xt = _out_proj(a, r, xt, mla_out_norm[l][None, :], w_out[l])

    kx, vx = _mem_kv(mem.reshape(batch * mem_len, D_MODEL), mem_norm[l][None, :],
                     xattn_w_kv[l], xattn_k_norm[l][None, :], batch, mem_len)
    xt = _xattn(xt, xattn_norm[l][None, :], xattn_w_q[l], xattn_q_norm[l][None, :],
                kx, vx, xattn_w_o[l], seq, mem_len)

    xt = _ffn(xt, ffn2_norm[l][None, :], ffn2_w_gate[l], ffn2_w_up[l], ffn2_w_down[l])
    return xt.reshape(batch, seq, D_MODEL)
```

```python
import functools

import numpy as np
import jax
import jax.numpy as jnp
from jax import lax
from jax.experimental import pallas as pl
from jax.experimental.pallas import tpu as pltpu

F32 = jnp.float32
BF16 = jnp.bfloat16

EPS = 1e-6
ROPE_BASE = 10000.0
LANES = 128
SUBLANES = 8

D_MODEL = 2048
D_FF = 5504
FFN_TM = 1024
FFN_TF = 512

MLA_HEADS = 8
MLA_NOPE = 128
MLA_ROPE = 64
MLA_QK = MLA_NOPE + MLA_ROPE
MLA_V = 128
MLA_Q_RANK = 512
MLA_KV_RANK = 256
MLA_WIDTH = MLA_HEADS * MLA_V
QK_PAD = 256
Q_UP_PER_HEAD = 384
ATT_Q_SCALE = float(MLA_QK ** -0.5 * np.log2(np.e))

HG_HEADS = 8
HG_D = 128
HG_CHUNK = 64
HG_LEVELS = (32, 16, 8, 4, 2, 1)
HG_UNROLL = 8
HG_GROUP = HG_CHUNK * HG_UNROLL
HG_SLAB = 256
HG_WIDTH = HG_HEADS * HG_D

MEM_HEADS = 4
MEM_HD = 128
MEM_WIDTH = MEM_HEADS * MEM_HD
XATT_Q_SCALE = float(MEM_HD ** -0.5 * np.log2(np.e))

COL_CQ = 0
COL_CKV = 512
COL_HQ = 768
COL_HF = COL_HQ + HG_WIDTH
COL_HI = COL_HF + HG_WIDTH
COL_HG = COL_HI + HG_WIDTH
COL_KPE = COL_HG + HG_WIDTH
COL_KPE_SW = COL_KPE + LANES
IN_COLS_PAD = COL_KPE_SW + LANES

VMEM_LIMIT = 56 * 1024 * 1024
FFN_VMEM_LIMIT = 61 * 1024 * 1024


def _params(*sem, vmem_limit=VMEM_LIMIT):
    return pltpu.CompilerParams(dimension_semantics=sem, vmem_limit_bytes=vmem_limit)


def _rms_scale(x):
    return lax.rsqrt(jnp.mean(x * x, axis=-1, keepdims=True) + EPS)


def _sigmoid(x):
    return 1.0 / (1.0 + jnp.exp(-x))


def _dot(a, b):
    return jnp.dot(a, b, preferred_element_type=F32)


def _dot_nt(a, b):
    return lax.dot_general(a, b, (((1,), (1,)), ((), ())), preferred_element_type=F32)


def _dot_tn(a, b):
    return lax.dot_general(a, b, (((0,), (0,)), ((), ())), preferred_element_type=F32)


def _ffn_kernel(x_hbm, g_ref, wg_ref, wu_ref, wd_ref, o_ref, h_scr, x_scr, x_sem):
    i = pl.program_id(0)

    def x_copy(tile):
        rows = pl.ds(pl.multiple_of(tile * FFN_TM, FFN_TM), FFN_TM)
        return pltpu.make_async_copy(x_hbm.at[rows, :], x_scr, x_sem)

    def branch(h):
        gate = _dot(h, wg_ref[...].astype(BF16))
        up = _dot(h, wu_ref[...].astype(BF16))
        a = 0.5 * gate * _sigmoid(gate) * up
        f0 = pl.program_id(1) * FFN_TF
        a_ok = f0 + lax.broadcasted_iota(jnp.int32, (1, FFN_TF), 1) < D_FF
        w_ok = f0 + lax.broadcasted_iota(jnp.int32, (FFN_TF, 1), 0) < D_FF
        a = jnp.where(a_ok, a, 0.0).astype(BF16)
        wd = jnp.where(w_ok, wd_ref[...], 0.0).astype(BF16)
        return _dot(a, wd)

    @pl.when(pl.program_id(1) == 0)
    def _():
        @pl.when(i == 0)
        def _():
            x_copy(0).start()

        x_copy(i).wait()
        x = x_scr[...]
        h = (x * _rms_scale(x) * g_ref[...]).astype(BF16)
        h_scr[...] = h
        o_ref[...] = x + branch(h)

        @pl.when(i + 1 < pl.num_programs(0))
        def _():
            x_copy(i + 1).start()

    @pl.when(pl.program_id(1) > 0)
    def _():
        o_ref[...] += branch(h_scr[...])


def _ffn(x, gain, wg, wu, wd):
    t = x.shape[0]
    return pl.pallas_call(
        _ffn_kernel,
        out_shape=jax.ShapeDtypeStruct((t, D_MODEL), F32),
        grid=(t // FFN_TM, pl.cdiv(D_FF, FFN_TF)),
        in_specs=[
            pl.BlockSpec(memory_space=pl.ANY),
            pl.BlockSpec((1, D_MODEL), lambda i, f: (0, 0)),
            pl.BlockSpec((D_MODEL, FFN_TF), lambda i, f: (0, f)),
            pl.BlockSpec((D_MODEL, FFN_TF), lambda i, f: (0, f)),
            pl.BlockSpec((FFN_TF, D_MODEL), lambda i, f: (f, 0)),
        ],
        out_specs=pl.BlockSpec((FFN_TM, D_MODEL), lambda i, f: (i, 0)),
        scratch_shapes=[pltpu.VMEM((FFN_TM, D_MODEL), BF16), pltpu.VMEM((FFN_TM, D_MODEL), F32),
                        pltpu.SemaphoreType.DMA],
        compiler_params=_params("arbitrary", "arbitrary", vmem_limit=FFN_VMEM_LIMIT),
        name="ffn",
    )(x, gain, wg, wu, wd)


PROJ_TM = 1024
PROJ_TN = 1024


def _in_proj_kernel(x_ref, g_ref, w_ref, o_ref, h_scr):
    @pl.when(pl.program_id(1) == 0)
    def _():
        x = x_ref[...]
        h = (x * _rms_scale(x) * g_ref[...]).astype(BF16)
        h_scr[...] = h
        o_ref[...] = _dot_nt(h, w_ref[...])

    @pl.when(pl.program_id(1) > 0)
    def _():
        o_ref[...] = _dot_nt(h_scr[...], w_ref[...])


def _in_proj(x, gain, w_t):
    t = x.shape[0]
    n = w_t.shape[0]
    return pl.pallas_call(
        _in_proj_kernel,
        out_shape=jax.ShapeDtypeStruct((t, n), F32),
        grid=(t // PROJ_TM, n // PROJ_TN),
        in_specs=[
            pl.BlockSpec((PROJ_TM, D_MODEL), lambda i, j: (i, 0)),
            pl.BlockSpec((1, D_MODEL), lambda i, j: (0, 0)),
            pl.BlockSpec((PROJ_TN, D_MODEL), lambda i, j: (j, 0)),
        ],
        out_specs=pl.BlockSpec((PROJ_TM, PROJ_TN), lambda i, j: (i, j)),
        scratch_shapes=[pltpu.VMEM((PROJ_TM, D_MODEL), BF16)],
        compiler_params=_params("parallel", "arbitrary"),
        name="in_proj",
    )(x, gain, w_t)


PREP_TM = 512


def _mla_prep_kernel(cq_ref, ckv_ref, kpe_ref, kpesw_ref, pos_ref, invf_ref,
                     qa_ref, kva_ref, wq_ref, wkv_ref,
                     qn_ref, qpa_ref, qpb_ref, kn_ref, kpa_ref, kpb_ref,
                     q_ref, k_ref, vt_ref):
    ang = pos_ref[...].astype(F32) * invf_ref[...]
    cos4 = jnp.cos(ang)
    sin4 = jnp.sin(ang)
    lane = lax.broadcasted_iota(jnp.int32, (1, LANES), 1)
    half = MLA_ROPE // 2

    def expand(t4):
        parts = []
        for part in range(LANES // half):
            t = pltpu.roll(t4, (LANES - half * part) % LANES, 1) if part else t4
            parts.append(jnp.where(lane < half, t, pltpu.roll(t, half, 1)))
        return jnp.concatenate(parts, axis=0)

    cos = expand(cos4)
    sin = expand(sin4)

    cq = cq_ref[...]
    cqn = (cq * _rms_scale(cq) * qa_ref[...]).astype(BF16)
    qall = _dot(cqn, wq_ref[...])
    q_cos = qpa_ref[...] * cos
    q_sin = qpb_ref[...] * sin
    for h in range(MLA_HEADS):
        base = h * Q_UP_PER_HEAD
        qn = qall[:, base:base + LANES]
        y = qall[:, base + LANES:base + 2 * LANES]
        ysw = qall[:, base + 2 * LANES:base + 3 * LANES]
        ss = jnp.sum(qn * qn, axis=-1, keepdims=True) + jnp.sum(y * y, axis=-1, keepdims=True)
        r = lax.rsqrt(ss * (1.0 / MLA_QK) + EPS) * ATT_Q_SCALE
        q_ref[:, h * QK_PAD:h * QK_PAD + LANES] = (qn * r * qn_ref[...]).astype(BF16)
        q_ref[:, h * QK_PAD + LANES:(h + 1) * QK_PAD] = ((y * q_cos + ysw * q_sin) * r).astype(BF16)

    ckv = ckv_ref[...]
    ckvn = (ckv * _rms_scale(ckv) * kva_ref[...]).astype(BF16)
    kvall = _dot(ckvn, wkv_ref[...].astype(BF16))
    yk = kpe_ref[...]
    kr = yk * (kpa_ref[...] * cos) + kpesw_ref[...] * (kpb_ref[...] * sin)
    ss_pe = jnp.sum(yk * yk, axis=-1, keepdims=True)
    for h in range(MLA_HEADS):
        kn = kvall[:, h * 2 * LANES:h * 2 * LANES + LANES]
        ss = jnp.sum(kn * kn, axis=-1, keepdims=True) + ss_pe
        r = lax.rsqrt(ss * (1.0 / MLA_QK) + EPS)
        k_ref[:, h * QK_PAD:h * QK_PAD + LANES] = (kn * r * kn_ref[...]).astype(BF16)
        k_ref[:, h * QK_PAD + LANES:(h + 1) * QK_PAD] = (kr * r).astype(BF16)
        vt_ref[h * MLA_V:(h + 1) * MLA_V, :] = kvall[:, h * 2 * LANES + LANES:(h + 1) * 2 * LANES].T.astype(BF16)


def _mla_prep(proj, pos, invf, qa, kva, wq, wkv, qn, qpa, qpb, kn, kpa, kpb):
    t = proj.shape[0]
    tm = PREP_TM
    row = lambda i: (0, 0)
    vec = lambda n: pl.BlockSpec((1, n), row)
    return pl.pallas_call(
        _mla_prep_kernel,
        out_shape=(jax.ShapeDtypeStruct((t, MLA_HEADS * QK_PAD), BF16),
                   jax.ShapeDtypeStruct((t, MLA_HEADS * QK_PAD), BF16),
                   jax.ShapeDtypeStruct((MLA_WIDTH, t), BF16)),
        grid=(t // tm,),
        in_specs=[
            pl.BlockSpec((tm, MLA_Q_RANK), lambda i: (i, COL_CQ // MLA_Q_RANK)),
            pl.BlockSpec((tm, MLA_KV_RANK), lambda i: (i, COL_CKV // MLA_KV_RANK)),
            pl.BlockSpec((tm, LANES), lambda i: (i, COL_KPE // LANES)),
            pl.BlockSpec((tm, LANES), lambda i: (i, COL_KPE_SW // LANES)),
            pl.BlockSpec((tm // (LANES // (MLA_ROPE // 2)), LANES), lambda i: (i, 0)),
            vec(LANES),
            vec(MLA_Q_RANK), vec(MLA_KV_RANK),
            pl.BlockSpec(wq.shape, row), pl.BlockSpec(wkv.shape, row),
            vec(LANES), vec(LANES), vec(LANES), vec(LANES), vec(LANES), vec(LANES),
        ],
        out_specs=(pl.BlockSpec((tm, MLA_HEADS * QK_PAD), lambda i: (i, 0)),
                   pl.BlockSpec((tm, MLA_HEADS * QK_PAD), lambda i: (i, 0)),
                   pl.BlockSpec((MLA_WIDTH, tm), lambda i: (0, i))),
        compiler_params=_params("parallel"),
        name="mla_prep",
    )(proj, proj, proj, proj, pos, invf, qa, kva, wq, wkv, qn, qpa, qpb, kn, kpa, kpb)


ATT_T = 256
ATT_AHEAD = 3
ATT_HEADS_PER_STEP = 2


def _mla_attn_kernel(q_ref, k_ref, vt_ref, o_ref):
    kpos = lax.broadcasted_iota(jnp.int32, (ATT_T, ATT_T), 0)
    qpos = lax.broadcasted_iota(jnp.int32, (ATT_T, ATT_T), 1)
    nq = q_ref.shape[0] // ATT_T

    def scores(unit):
        h, i = unit
        lo, hi = i * ATT_T, (i + 1) * ATT_T
        cols = slice(h * QK_PAD, (h + 1) * QK_PAD)
        q = q_ref[lo:hi, cols]
        s_diag = jnp.where(kpos <= qpos, _dot_nt(k_ref[lo:hi, cols], q), -jnp.inf)
        s_off = _dot_nt(k_ref[0:lo, cols], q) if i > 0 else None
        return s_diag, s_off

    def finish(unit, s_diag, s_off):
        h, i = unit
        lo, hi = i * ATT_T, (i + 1) * ATT_T
        vt = vt_ref.at[h * MLA_V:(h + 1) * MLA_V, :]
        m = jnp.max(s_diag, axis=0, keepdims=True)
        if i > 0:
            m = jnp.maximum(m, jnp.max(s_off, axis=0, keepdims=True))
        p = jnp.exp2(s_diag - m)
        l = jnp.sum(p, axis=0, keepdims=True)
        acc = _dot(vt[:, lo:hi], p.astype(BF16))
        if i > 0:
            p = jnp.exp2(s_off - m)
            l = l + jnp.sum(p, axis=0, keepdims=True)
            acc = acc + _dot(vt[:, 0:lo], p.astype(BF16))
        o_ref[lo:hi, h * MLA_V:(h + 1) * MLA_V] = (acc / l).T

    order = [(h, i) for i in reversed(range(nq)) for h in range(ATT_HEADS_PER_STEP)]
    pending = [scores(u) for u in order[:ATT_AHEAD]]
    for n, unit in enumerate(order):
        if n + ATT_AHEAD < len(order):
            pending.append(scores(order[n + ATT_AHEAD]))
        finish(unit, *pending.pop(0))


def _mla_attn(q, k, vt, batch, seq):
    hs = ATT_HEADS_PER_STEP
    return pl.pallas_call(
        _mla_attn_kernel,
        out_shape=jax.ShapeDtypeStruct((batch * seq, MLA_WIDTH), F32),
        grid=(batch, MLA_HEADS // hs),
        in_specs=[
            pl.BlockSpec((seq, hs * QK_PAD), lambda b, h: (b, h)),
            pl.BlockSpec((seq, hs * QK_PAD), lambda b, h: (b, h)),
            pl.BlockSpec((hs * MLA_V, seq), lambda b, h: (h, b)),
        ],
        out_specs=pl.BlockSpec((seq, hs * MLA_V), lambda b, h: (b, h)),
        compiler_params=_params("parallel", "parallel"),
        name="mla_attn",
    )(q, k, vt)


def _hgrn_masks():
    t = np.arange(HG_CHUNK)[:, None]
    s = np.arange(HG_CHUNK)[None, :]
    masks = [((t // c) % 2 == 1) & ((s // c) == (t // c) - 1) for c in HG_LEVELS]
    masks.append(t == s)
    return np.stack(masks).astype(np.float32)


def _hgrn_kernel(q_ref, f_ref, i_ref, g_ref, lbl_ref, gn_ref, msk_ref, tri_ref, o_ref, b_all):
    lbl = lbl_ref[...]
    e = jnp.exp(lbl - jnp.max(lbl, axis=0, keepdims=True))
    lb = e[0:1, :] / jnp.sum(e, axis=0, keepdims=True)
    gn = gn_ref[...]
    tri = tri_ref[...]
    sub = lax.broadcasted_iota(jnp.int32, (SUBLANES, LANES), 0)
    zeros8 = jnp.zeros((SUBLANES, LANES), F32)
    per_chunk = HG_CHUNK // SUBLANES
    chunk_rows = [slice(u * HG_CHUNK, (u + 1) * HG_CHUNK) for u in range(HG_UNROLL)]

    def sel(bounds, vals):
        out = vals[-1]
        for bound, val in zip(reversed(bounds), reversed(vals[:-1])):
            out = jnp.where(sub < bound, val, out)
        return out

    def head(gi):
        rows = slice(gi * HG_GROUP, (gi + 1) * HG_GROUP)
        q = q_ref[rows, :]
        v = i_ref[rows, :].astype(BF16)
        f = lb + (1.0 - lb) * _sigmoid(f_ref[rows, :])
        lg = jnp.log2(f)
        k = 1.0 - f

        l1 = lg.astype(BF16)
        l2 = (lg - l1.astype(F32)).astype(BF16)
        slabs = [slice(i, i + HG_SLAB) for i in range(0, HG_GROUP, HG_SLAB)]
        b = jnp.concatenate([_dot(tri, l1[r]) + _dot(tri, l2[r]) for r in slabs], axis=0)
        b_all[gi] = b
        return q, k, f, v, b

    def main(gi, hd, st):
        rows = slice(gi * HG_GROUP, (gi + 1) * HG_GROUP)
        q, k, f, v, b = hd
        b_scr = b_all.at[gi]
        cache = {}

        def rowb(u, r):
            if r < 0:
                return zeros8
            if (u, r) not in cache:
                cache[u, r] = jnp.broadcast_to(b_scr[u * HG_CHUNK + r:u * HG_CHUNK + r + 1, :], (SUBLANES, LANES))
            return cache[u, r]

        def build(fn):
            return jnp.concatenate([fn(u, j) for u in range(HG_UNROLL) for j in range(per_chunk)], axis=0)

        def group_rows(x, u, j):
            r0 = u * HG_CHUNK + j * SUBLANES
            return x[r0:r0 + SUBLANES]

        def level_rows(ci, j):
            if ("m", ci, j) not in cache:
                cache["m", ci, j] = msk_ref[ci, j * SUBLANES:(j + 1) * SUBLANES, :] != 0.0
            return cache["m", ci, j]

        diag = jnp.sum(q * k, axis=-1, keepdims=True)
        a = [[jnp.where(level_rows(len(HG_LEVELS), j), group_rows(diag, u, j), 0.0) for j in range(per_chunk)]
             for u in range(HG_UNROLL)]
        for ci, c in enumerate(HG_LEVELS):
            if c >= SUBLANES:
                m = c // SUBLANES
                right = [j for j in range(per_chunk) if (j // m) % 2 == 1]
                pick = lambda x, js: jnp.concatenate([group_rows(x, u, j) for u in range(HG_UNROLL) for j in js], axis=0)
                bs = jnp.concatenate([rowb(u, SUBLANES * (j // m) * m - 1) for u in range(HG_UNROLL) for j in right], axis=0)
                qd = (pick(q, right) * jnp.exp2(pick(b, right) - bs)).astype(BF16)
                kd = jnp.concatenate(
                    [group_rows(k, u, j) * jnp.exp2(rowb(u, SUBLANES * ((j // m) * m + m) - 1) - group_rows(b, u, j))
                     if (j // m) % 2 == 0 else zeros8
                     for u in range(HG_UNROLL) for j in range(per_chunk)], axis=0).astype(BF16)
                rows_per_chunk = len(right) * SUBLANES
                for u, r in enumerate(chunk_rows):
                    prod = _dot_nt(qd[u * rows_per_chunk:(u + 1) * rows_per_chunk], kd[r])
                    for n, j in enumerate(right):
                        a[u][j] = jnp.where(level_rows(ci, j), prod[n * SUBLANES:(n + 1) * SUBLANES], a[u][j])
                continue
            if c == 4:
                bs = build(lambda u, j: sel((4,), (rowb(u, 8 * j - 1), rowb(u, 8 * j + 3))))
                be = build(lambda u, j: sel((4,), (rowb(u, 8 * j + 3), rowb(u, 8 * j + 7))))
            elif c == 2:
                bs = build(lambda u, j: sel((2, 4, 6), (rowb(u, 8 * j - 1), rowb(u, 8 * j + 1),
                                                         rowb(u, 8 * j + 3), rowb(u, 8 * j + 5))))
                be = build(lambda u, j: sel((2, 4, 6), (rowb(u, 8 * j + 1), rowb(u, 8 * j + 3),
                                                         rowb(u, 8 * j + 5), rowb(u, 8 * j + 7))))
            if c == 1:
                qd = (q * f).astype(BF16)
                kd = k.astype(BF16)
            else:
                qd = (q * jnp.exp2(b - bs)).astype(BF16)
                kd = (k * jnp.exp2(be - b)).astype(BF16)
            for u, r in enumerate(chunk_rows):
                prod = _dot_nt(qd[r], kd[r])
                for j in range(per_chunk):
                    a[u][j] = jnp.where(level_rows(ci, j), prod[j * SUBLANES:(j + 1) * SUBLANES], a[u][j])
        a = [jnp.concatenate(a[u], axis=0) for u in range(HG_UNROLL)]

        q_exp = (q * jnp.exp2(b)).astype(BF16)
        b_last = build(lambda u, j: rowb(u, HG_CHUNK - 1))
        k_dec = (k * jnp.exp2(b_last - b)).astype(BF16)
        o_intra = [_dot(a[u].astype(BF16), v[r]) for u, r in enumerate(chunk_rows)]
        st_add = [_dot_tn(v[r], k_dec[r]) for r in chunk_rows]

        o = []
        for u, r in enumerate(chunk_rows):
            o.append(o_intra[u] + _dot_nt(q_exp[r], st.astype(BF16)))
            st = jnp.exp2(rowb(u, HG_CHUNK - 1)[0:1, :]) * st + st_add[u]
        o = jnp.concatenate(o, axis=0)

        g = g_ref[rows, :]
        o_ref[rows, :] = (o * _rms_scale(o) * gn * (g * _sigmoid(g))).astype(BF16)
        return st

    n_groups = q_ref.shape[0] // HG_GROUP
    st = jnp.zeros((HG_D, HG_D), F32)
    hd = head(0)
    for gi in range(n_groups):
        nxt = head(gi + 1) if gi + 1 < n_groups else None
        st = main(gi, hd, st)
        hd = nxt


def _hgrn(proj, lb_logits, out_norm, batch, seq):
    hblk = lambda col: pl.BlockSpec((seq, HG_D), lambda b, h: (b, col // HG_D + h))
    masks = jnp.asarray(_hgrn_masks())
    tri = np.kron(np.eye(HG_SLAB // HG_CHUNK), np.tril(np.ones((HG_CHUNK, HG_CHUNK))))
    tri = jnp.asarray(tri.astype(np.float32), dtype=BF16)
    nlev = masks.shape[0]
    return pl.pallas_call(
        _hgrn_kernel,
        out_shape=jax.ShapeDtypeStruct((batch * seq, HG_WIDTH), BF16),
        grid=(batch, HG_HEADS),
        in_specs=[
            hblk(COL_HQ), hblk(COL_HF), hblk(COL_HI), hblk(COL_HG),
            pl.BlockSpec((lb_logits.shape[0], HG_D), lambda b, h: (0, h)),
            pl.BlockSpec((1, HG_D), lambda b, h: (0, 0)),
            pl.BlockSpec((nlev, HG_CHUNK, HG_CHUNK), lambda b, h: (0, 0, 0)),
            pl.BlockSpec((HG_SLAB, HG_SLAB), lambda b, h: (0, 0)),
        ],
        out_specs=pl.BlockSpec((seq, HG_D), lambda b, h: (b, h)),
        scratch_shapes=[pltpu.VMEM((seq // HG_GROUP, HG_GROUP, HG_D), F32)],
        compiler_params=_params("parallel", "parallel"),
        name="hgrn2",
    )(proj, proj, proj, proj, lb_logits, out_norm, masks, tri)


OUT_TM = 512


def _out_proj_kernel(a_ref, r_ref, x_ref, g_ref, w_ref, o_ref):
    a = a_ref[...]
    an = (a * _rms_scale(a) * g_ref[...]).astype(BF16)
    o_ref[...] = (x_ref[...] + _dot(an, w_ref[:MLA_WIDTH, :].astype(BF16))
                  + _dot(r_ref[...], w_ref[MLA_WIDTH:, :].astype(BF16)))


def _out_proj(a, r, x, gain, w):
    t = x.shape[0]
    return pl.pallas_call(
        _out_proj_kernel,
        out_shape=jax.ShapeDtypeStruct((t, D_MODEL), F32),
        grid=(t // OUT_TM,),
        in_specs=[
            pl.BlockSpec((OUT_TM, MLA_WIDTH), lambda i: (i, 0)),
            pl.BlockSpec((OUT_TM, HG_WIDTH), lambda i: (i, 0)),
            pl.BlockSpec((OUT_TM, D_MODEL), lambda i: (i, 0)),
            pl.BlockSpec((1, MLA_WIDTH), lambda i: (0, 0)),
            pl.BlockSpec(w.shape, lambda i: (0, 0), pipeline_mode=pl.Buffered(1)),
        ],
        out_specs=pl.BlockSpec((OUT_TM, D_MODEL), lambda i: (i, 0)),
        compiler_params=_params("parallel"),
        name="out_proj",
    )(a, r, x, gain, w)


def _mem_kv_kernel(m_ref, g_ref, w_ref, kn_ref, k_ref, v_ref):
    m = m_ref[...]
    mn = (m * _rms_scale(m) * g_ref[...]).astype(BF16)
    kv = _dot(mn, w_ref[...].astype(BF16))
    for h in range(MEM_HEADS):
        k = kv[:, 2 * h * MEM_HD:(2 * h + 1) * MEM_HD]
        k_ref[:, h * MEM_HD:(h + 1) * MEM_HD] = (k * _rms_scale(k) * kn_ref[...]).astype(BF16)
        v_ref[:, h * MEM_HD:(h + 1) * MEM_HD] = kv[:, (2 * h + 1) * MEM_HD:(2 * h + 2) * MEM_HD].astype(BF16)


def _mem_kv(mem, gain, w, k_norm, batch, mem_len):
    return pl.pallas_call(
        _mem_kv_kernel,
        out_shape=(jax.ShapeDtypeStruct((batch * mem_len, MEM_WIDTH), BF16),
                   jax.ShapeDtypeStruct((batch * mem_len, MEM_WIDTH), BF16)),
        grid=(batch,),
        in_specs=[
            pl.BlockSpec((mem_len, D_MODEL), lambda b: (b, 0)),
            pl.BlockSpec((1, D_MODEL), lambda b: (0, 0)),
            pl.BlockSpec(w.shape, lambda b: (0, 0)),
            pl.BlockSpec((1, MEM_HD), lambda b: (0, 0)),
        ],
        out_specs=(pl.BlockSpec((mem_len, MEM_WIDTH), lambda b: (b, 0)),
                   pl.BlockSpec((mem_len, MEM_WIDTH), lambda b: (b, 0))),
        compiler_params=_params("parallel"),
        name="mem_kv",
    )(mem, gain, w, k_norm)


XA_TM = 512


def _xattn_kernel(x_ref, g_ref, wq_ref, qn_ref, k_ref, v_ref, wo_ref, o_ref, att_scr):
    x = x_ref[...]
    h = (x * _rms_scale(x) * g_ref[...]).astype(BF16)
    q = _dot(h, wq_ref[...].astype(BF16))
    heads = [slice(hd * MEM_HD, (hd + 1) * MEM_HD) for hd in range(MEM_HEADS)]
    scores = []
    for cols in heads:
        qh = q[:, cols]
        qh = (qh * (_rms_scale(qh) * XATT_Q_SCALE) * qn_ref[...]).astype(BF16)
        scores.append(_dot_nt(qh, k_ref[:, cols]))
    for cols, s in zip(heads, scores):
        p = jnp.exp2(s - jnp.max(s, axis=-1, keepdims=True))
        l = jnp.sum(p, axis=-1, keepdims=True)
        att_scr[:, cols] = (_dot(p.astype(BF16), v_ref[:, cols]) / l).astype(BF16)
    o_ref[...] = x + _dot(att_scr[...], wo_ref[...].astype(BF16))


def _xattn(x, gain, wq, q_norm, kx, vx, wo, seq, mem_len):
    t = x.shape[0]
    per_batch = seq // XA_TM
    return pl.pallas_call(
        _xattn_kernel,
        out_shape=jax.ShapeDtypeStruct((t, D_MODEL), F32),
        grid=(t // XA_TM,),
        in_specs=[
            pl.BlockSpec((XA_TM, D_MODEL), lambda i: (i, 0)),
            pl.BlockSpec((1, D_MODEL), lambda i: (0, 0)),
            pl.BlockSpec(wq.shape, lambda i: (0, 0)),
            pl.BlockSpec((1, MEM_HD), lambda i: (0, 0)),
            pl.BlockSpec((mem_len, MEM_WIDTH), lambda i: (i // per_batch, 0)),
            pl.BlockSpec((mem_len, MEM_WIDTH), lambda i: (i // per_batch, 0)),
            pl.BlockSpec(wo.shape, lambda i: (0, 0)),
        ],
        out_specs=pl.BlockSpec((XA_TM, D_MODEL), lambda i: (i, 0)),
        scratch_shapes=[pltpu.VMEM((XA_TM, MEM_WIDTH), BF16)],
        compiler_params=_params("parallel"),
        name="xattn",
    )(x, gain, wq, q_norm, kx, vx, wo)


def _pe_pair(x1, x2):
    z = jnp.zeros(x1.shape[:-1] + (LANES - MLA_ROPE,), x1.dtype)
    return jnp.concatenate([x1, x2, z], axis=-1), jnp.concatenate([x2, x1, z], axis=-1)


W_IN_TC = 512


def _w_in_prep_kernel(w_ref, o_ref):
    half = MLA_ROPE // 2
    kpe0 = MLA_Q_RANK + MLA_KV_RANK
    hg0 = kpe0 + MLA_ROPE
    zeros = jnp.zeros((LANES - MLA_ROPE, o_ref.shape[1]), BF16)
    o_ref[:kpe0, :] = w_ref[:kpe0, :].astype(BF16)
    o_ref[kpe0:COL_KPE, :] = w_ref[hg0:, :].astype(BF16)
    x1 = w_ref[kpe0:kpe0 + half, :].astype(BF16)
    x2 = w_ref[kpe0 + half:hg0, :].astype(BF16)
    o_ref[COL_KPE:COL_KPE + half, :] = x1
    o_ref[COL_KPE + half:COL_KPE + MLA_ROPE, :] = x2
    o_ref[COL_KPE + MLA_ROPE:COL_KPE_SW, :] = zeros
    o_ref[COL_KPE_SW:COL_KPE_SW + half, :] = x2
    o_ref[COL_KPE_SW + half:COL_KPE_SW + MLA_ROPE, :] = x1
    o_ref[COL_KPE_SW + MLA_ROPE:, :] = zeros


def _prep_w_in(w_in, layer):
    w_t = jnp.swapaxes(w_in, 1, 2)
    _, n, k = w_t.shape
    return pl.pallas_call(
        _w_in_prep_kernel,
        out_shape=jax.ShapeDtypeStruct((IN_COLS_PAD, k), BF16),
        grid=(k // W_IN_TC,),
        in_specs=[pl.BlockSpec((None, n, W_IN_TC), lambda i: (layer, 0, i))],
        out_specs=pl.BlockSpec((IN_COLS_PAD, W_IN_TC), lambda i: (0, i)),
        compiler_params=_params("parallel"),
        name="w_in_prep",
    )(w_t)


def _prep_w_q_up(w):
    w = w.astype(BF16).reshape(MLA_Q_RANK, MLA_HEADS, MLA_QK)
    half = MLA_ROPE // 2
    pe, pe_sw = _pe_pair(w[..., MLA_NOPE:MLA_NOPE + half], w[..., MLA_NOPE + half:])
    return jnp.concatenate([w[..., :MLA_NOPE], pe, pe_sw], axis=-1).reshape(MLA_Q_RANK, MLA_HEADS * Q_UP_PER_HEAD)


def _pe_gains(norm):
    half = MLA_ROPE // 2
    g1 = norm[MLA_NOPE:MLA_NOPE + half]
    g2 = norm[MLA_NOPE + half:]
    ga, gb = _pe_pair(g1, g2)
    sign = jnp.concatenate([-jnp.ones((half,), F32), jnp.ones((LANES - half,), F32)])
    return ga[None, :], (gb * sign)[None, :]


def kernel(x, mem, positions, ffn1_norm, ffn1_w_gate, ffn1_w_up, ffn1_w_down, mix_norm, w_in, mla_q_a_norm, mla_w_q_up, mla_kv_a_norm, mla_w_kv_up, mla_q_norm, mla_k_norm, mla_out_norm, hg_lb_logits, hg_out_norm, w_out, xattn_norm, mem_norm, xattn_w_q, xattn_w_kv, xattn_q_norm, xattn_k_norm, xattn_w_o, ffn2_norm, ffn2_w_gate, ffn2_w_up, ffn2_w_down):
    batch, seq, _ = x.shape
    mem_len = mem.shape[1]
    depth = ffn1_norm.shape[0]
    assert depth == 1 and seq % ATT_T == 0 and seq % XA_TM == 0
    t = batch * seq
    xt = x.reshape(t, D_MODEL)
    half = MLA_ROPE // 2
    quarters = LANES // half
    pos = positions.reshape(t // PREP_TM, quarters, PREP_TM // quarters).transpose(0, 2, 1)
    pos = jnp.repeat(pos, half, axis=-1).reshape(t // quarters, LANES)
    inv_freq = ROPE_BASE ** (-np.arange(half, dtype=np.float32) / half)
    invf = jnp.asarray(np.tile(inv_freq, quarters)[None, :])
    l = 0

    xt = _ffn(xt, ffn1_norm[l][None, :], ffn1_w_gate[l], ffn1_w_up[l], ffn1_w_down[l])

    proj = _in_proj(xt, mix_norm[l][None, :], _prep_w_in(w_in, l))
    qpa, qpb = _pe_gains(mla_q_norm[l])
    kpa, kpb = _pe_gains(mla_k_norm[l])
    q, k, v = _mla_prep(proj, pos, invf, mla_q_a_norm[l][None, :], mla_kv_a_norm[l][None, :],
                        _prep_w_q_up(mla_w_q_up[l]), mla_w_kv_up[l],
                        mla_q_norm[l][None, :MLA_NOPE], qpa, qpb,
                        mla_k_norm[l][None, :MLA_NOPE], kpa, kpb)
    a = _mla_attn(q, k, v, batch, seq)
    r = _hgrn(proj, hg_lb_logits, hg_out_norm[l][None, :], batch, seq)
    xt = _out_proj(a, r, xt, mla_out_norm[l][None, :], w_out[l])

    kx, vx = _mem_kv(mem.reshape(batch * mem_len, D_MODEL), mem_norm[l][None, :],
                     xattn_w_kv[l], xattn_k_norm[l][None, :], batch, mem_len)
    xt = _xattn(xt, xattn_norm[l][None, :], xattn_w_q[l], xattn_q_norm[l][None, :],
                kx, vx, xattn_w_o[l], seq, mem_len)

    xt = _ffn(xt, ffn2_norm[l][None, :], ffn2_w_gate[l], ffn2_w_up[l], ffn2_w_down[l])
    return xt.reshape(batch, seq, D_MODEL)
```

```python
import functools

import numpy as np
import jax
import jax.numpy as jnp
from jax import lax
from jax.experimental import pallas as pl
from jax.experimental.pallas import tpu as pltpu

F32 = jnp.float32
BF16 = jnp.bfloat16

EPS = 1e-6
ROPE_BASE = 10000.0
LANES = 128
SUBLANES = 8

D_MODEL = 2048
D_FF = 5504
FFN_TM = 1024
FFN_TF = 512

MLA_HEADS = 8
MLA_NOPE = 128
MLA_ROPE = 64
MLA_QK = MLA_NOPE + MLA_ROPE
MLA_V = 128
MLA_Q_RANK = 512
MLA_KV_RANK = 256
MLA_WIDTH = MLA_HEADS * MLA_V
QK_PAD = 256
Q_UP_PER_HEAD = 384
ATT_Q_SCALE = float(MLA_QK ** -0.5 * np.log2(np.e))

HG_HEADS = 8
HG_D = 128
HG_CHUNK = 64
HG_LEVELS = (32, 16, 8, 4, 2, 1)
HG_UNROLL = 8
HG_GROUP = HG_CHUNK * HG_UNROLL
HG_SLAB = 256
HG_WIDTH = HG_HEADS * HG_D

MEM_HEADS = 4
MEM_HD = 128
MEM_WIDTH = MEM_HEADS * MEM_HD
XATT_Q_SCALE = float(MEM_HD ** -0.5 * np.log2(np.e))

COL_CQ = 0
COL_CKV = 512
COL_HQ = 768
COL_HF = COL_HQ + HG_WIDTH
COL_HI = COL_HF + HG_WIDTH
COL_HG = COL_HI + HG_WIDTH
COL_KPE = COL_HG + HG_WIDTH
COL_KPE_SW = COL_KPE + LANES
IN_COLS_PAD = COL_KPE_SW + LANES

VMEM_LIMIT = 56 * 1024 * 1024
FFN_VMEM_LIMIT = 61 * 1024 * 1024


def _params(*sem, vmem_limit=VMEM_LIMIT):
    return pltpu.CompilerParams(dimension_semantics=sem, vmem_limit_bytes=vmem_limit)


def _rms_scale(x):
    return lax.rsqrt(jnp.mean(x * x, axis=-1, keepdims=True) + EPS)


def _sigmoid(x):
    return 1.0 / (1.0 + jnp.exp(-x))


def _dot(a, b):
    return jnp.dot(a, b, preferred_element_type=F32)


def _dot_nt(a, b):
    return lax.dot_general(a, b, (((1,), (1,)), ((), ())), preferred_element_type=F32)


def _dot_tn(a, b):
    return lax.dot_general(a, b, (((0,), (0,)), ((), ())), preferred_element_type=F32)


def _ffn_kernel(x_hbm, g_ref, wg_ref, wu_ref, wd_ref, o_ref, h_scr, x_scr, x_sem):
    i = pl.program_id(0)

    def x_copy(tile):
        rows = pl.ds(pl.multiple_of(tile * FFN_TM, FFN_TM), FFN_TM)
        return pltpu.make_async_copy(x_hbm.at[rows, :], x_scr, x_sem)

    def branch(h):
        gate = _dot(h, wg_ref[...].astype(BF16))
        up = _dot(h, wu_ref[...].astype(BF16))
        a = 0.5 * gate * _sigmoid(gate) * up
        f0 = pl.program_id(1) * FFN_TF
        a_ok = f0 + lax.broadcasted_iota(jnp.int32, (1, FFN_TF), 1) < D_FF
        w_ok = f0 + lax.broadcasted_iota(jnp.int32, (FFN_TF, 1), 0) < D_FF
        a = jnp.where(a_ok, a, 0.0).astype(BF16)
        wd = jnp.where(w_ok, wd_ref[...], 0.0).astype(BF16)
        return _dot(a, wd)

    @pl.when(pl.program_id(1) == 0)
    def _():
        @pl.when(i == 0)
        def _():
            x_copy(0).start()

        x_copy(i).wait()
        x = x_scr[...]
        h = (x * _rms_scale(x) * g_ref[...]).astype(BF16)
        h_scr[...] = h
        o_ref[...] = x + branch(h)

        @pl.when(i + 1 < pl.num_programs(0))
        def _():
            x_copy(i + 1).start()

    @pl.when(pl.program_id(1) > 0)
    def _():
        o_ref[...] += branch(h_scr[...])


def _ffn(x, gain, wg, wu, wd):
    t = x.shape[0]
    return pl.pallas_call(
        _ffn_kernel,
        out_shape=jax.ShapeDtypeStruct((t, D_MODEL), F32),
        grid=(t // FFN_TM, pl.cdiv(D_FF, FFN_TF)),
        in_specs=[
            pl.BlockSpec(memory_space=pl.ANY),
            pl.BlockSpec((1, D_MODEL), lambda i, f: (0, 0)),
            pl.BlockSpec((D_MODEL, FFN_TF), lambda i, f: (0, f)),
            pl.BlockSpec((D_MODEL, FFN_TF), lambda i, f: (0, f)),
            pl.BlockSpec((FFN_TF, D_MODEL), lambda i, f: (f, 0)),
        ],
        out_specs=pl.BlockSpec((FFN_TM, D_MODEL), lambda i, f: (i, 0)),
        scratch_shapes=[pltpu.VMEM((FFN_TM, D_MODEL), BF16), pltpu.VMEM((FFN_TM, D_MODEL), F32),
                        pltpu.SemaphoreType.DMA],
        compiler_params=_params("arbitrary", "arbitrary", vmem_limit=FFN_VMEM_LIMIT),
        name="ffn",
    )(x, gain, wg, wu, wd)


PROJ_TM = 1024
PROJ_TN = 1024


def _in_proj_kernel(x_ref, g_ref, w_ref, o_ref, h_scr):
    @pl.when(pl.program_id(1) == 0)
    def _():
        x = x_ref[...]
        h = (x * _rms_scale(x) * g_ref[...]).astype(BF16)
        h_scr[...] = h
        o_ref[...] = _dot_nt(h, w_ref[...])

    @pl.when(pl.program_id(1) > 0)
    def _():
        o_ref[...] = _dot_nt(h_scr[...], w_ref[...])


def _in_proj(x, gain, w_t):
    t = x.shape[0]
    n = w_t.shape[0]
    return pl.pallas_call(
        _in_proj_kernel,
        out_shape=jax.ShapeDtypeStruct((t, n), F32),
        grid=(t // PROJ_TM, n // PROJ_TN),
        in_specs=[
            pl.BlockSpec((PROJ_TM, D_MODEL), lambda i, j: (i, 0)),
            pl.BlockSpec((1, D_MODEL), lambda i, j: (0, 0)),
            pl.BlockSpec((PROJ_TN, D_MODEL), lambda i, j: (j, 0)),
        ],
        out_specs=pl.BlockSpec((PROJ_TM, PROJ_TN), lambda i, j: (i, j)),
        scratch_shapes=[pltpu.VMEM((PROJ_TM, D_MODEL), BF16)],
        compiler_params=_params("parallel", "arbitrary"),
        name="in_proj",
    )(x, gain, w_t)


PREP_TM = 512


def _mla_prep_kernel(cq_ref, ckv_ref, kpe_ref, kpesw_ref, pos_ref, invf_ref,
                     qa_ref, kva_ref, wq_ref, wkv_ref,
                     qn_ref, qpa_ref, qpb_ref, kn_ref, kpa_ref, kpb_ref,
                     q_ref, k_ref, vt_ref):
    ang = pos_ref[...].astype(F32) * invf_ref[...]
    cos4 = jnp.cos(ang)
    sin4 = jnp.sin(ang)
    lane = lax.broadcasted_iota(jnp.int32, (1, LANES), 1)
    half = MLA_ROPE // 2

    def expand(t4):
        parts = []
        for part in range(LANES // half):
            t = pltpu.roll(t4, (LANES - half * part) % LANES, 1) if part else t4
            parts.append(jnp.where(lane < half, t, pltpu.roll(t, half, 1)))
        return jnp.concatenate(parts, axis=0)

    cos = expand(cos4)
    sin = expand(sin4)

    q_const = float(np.sqrt(MLA_QK)) * ATT_Q_SCALE
    k_const = float(np.sqrt(MLA_QK))

    cq = cq_ref[...]
    cqn = (cq * _rms_scale(cq) * qa_ref[...]).astype(BF16)
    qall = _dot(cqn, wq_ref[...])
    qn_gain = qn_ref[...] * q_const
    q_cos = (qpa_ref[...] * q_const) * cos
    q_sin = (qpb_ref[...] * q_const) * sin
    for h in range(MLA_HEADS):
        base = h * Q_UP_PER_HEAD
        qn = qall[:, base:base + LANES]
        y = qall[:, base + LANES:base + 2 * LANES]
        ysw = qall[:, base + 2 * LANES:base + 3 * LANES]
        r = lax.rsqrt(jnp.sum(qn * qn + y * y, axis=-1, keepdims=True) + MLA_QK * EPS)
        q_ref[:, h * QK_PAD:h * QK_PAD + LANES] = (qn * r * qn_gain).astype(BF16)
        q_ref[:, h * QK_PAD + LANES:(h + 1) * QK_PAD] = ((y * q_cos + ysw * q_sin) * r).astype(BF16)

    ckv = ckv_ref[...]
    ckvn = (ckv * _rms_scale(ckv) * kva_ref[...]).astype(BF16)
    kvall = _dot(ckvn, wkv_ref[...].astype(BF16))
    yk = kpe_ref[...]
    kn_gain = kn_ref[...] * k_const
    kr = yk * ((kpa_ref[...] * k_const) * cos) + kpesw_ref[...] * ((kpb_ref[...] * k_const) * sin)
    yk_sq = yk * yk
    for h in range(MLA_HEADS):
        kn = kvall[:, h * 2 * LANES:h * 2 * LANES + LANES]
        r = lax.rsqrt(jnp.sum(kn * kn + yk_sq, axis=-1, keepdims=True) + MLA_QK * EPS)
        k_ref[:, h * QK_PAD:h * QK_PAD + LANES] = (kn * r * kn_gain).astype(BF16)
        k_ref[:, h * QK_PAD + LANES:(h + 1) * QK_PAD] = (kr * r).astype(BF16)
        vt_ref[h * MLA_V:(h + 1) * MLA_V, :] = kvall[:, h * 2 * LANES + LANES:(h + 1) * 2 * LANES].T.astype(BF16)


def _mla_prep(proj, pos, invf, qa, kva, wq, wkv, qn, qpa, qpb, kn, kpa, kpb):
    t = proj.shape[0]
    tm = PREP_TM
    row = lambda i: (0, 0)
    vec = lambda n: pl.BlockSpec((1, n), row)
    return pl.pallas_call(
        _mla_prep_kernel,
        out_shape=(jax.ShapeDtypeStruct((t, MLA_HEADS * QK_PAD), BF16),
                   jax.ShapeDtypeStruct((t, MLA_HEADS * QK_PAD), BF16),
                   jax.ShapeDtypeStruct((MLA_WIDTH, t), BF16)),
        grid=(t // tm,),
        in_specs=[
            pl.BlockSpec((tm, MLA_Q_RANK), lambda i: (i, COL_CQ // MLA_Q_RANK)),
            pl.BlockSpec((tm, MLA_KV_RANK), lambda i: (i, COL_CKV // MLA_KV_RANK)),
            pl.BlockSpec((tm, LANES), lambda i: (i, COL_KPE // LANES)),
            pl.BlockSpec((tm, LANES), lambda i: (i, COL_KPE_SW // LANES)),
            pl.BlockSpec((tm // (LANES // (MLA_ROPE // 2)), LANES), lambda i: (i, 0)),
            vec(LANES),
            vec(MLA_Q_RANK), vec(MLA_KV_RANK),
            pl.BlockSpec(wq.shape, row), pl.BlockSpec(wkv.shape, row),
            vec(LANES), vec(LANES), vec(LANES), vec(LANES), vec(LANES), vec(LANES),
        ],
        out_specs=(pl.BlockSpec((tm, MLA_HEADS * QK_PAD), lambda i: (i, 0)),
                   pl.BlockSpec((tm, MLA_HEADS * QK_PAD), lambda i: (i, 0)),
                   pl.BlockSpec((MLA_WIDTH, tm), lambda i: (0, i))),
        compiler_params=_params("parallel"),
        name="mla_prep",
    )(proj, proj, proj, proj, pos, invf, qa, kva, wq, wkv, qn, qpa, qpb, kn, kpa, kpb)


ATT_T = 256
ATT_AHEAD = 5
ATT_HEADS_PER_STEP = 2


def _mla_attn_kernel(q_ref, k_ref, vt_ref, o_ref):
    kpos = lax.broadcasted_iota(jnp.int32, (ATT_T, ATT_T), 0)
    qpos = lax.broadcasted_iota(jnp.int32, (ATT_T, ATT_T), 1)
    nq = q_ref.shape[0] // ATT_T

    def scores(unit):
        h, i = unit
        lo, hi = i * ATT_T, (i + 1) * ATT_T
        cols = slice(h * QK_PAD, (h + 1) * QK_PAD)
        q = q_ref[lo:hi, cols]
        s_diag = jnp.where(kpos <= qpos, _dot_nt(k_ref[lo:hi, cols], q), -jnp.inf)
        s_off = _dot_nt(k_ref[0:lo, cols], q) if i > 0 else None
        return s_diag, s_off

    def finish(unit, s_diag, s_off):
        h, i = unit
        lo, hi = i * ATT_T, (i + 1) * ATT_T
        vt = vt_ref.at[h * MLA_V:(h + 1) * MLA_V, :]
        m = jnp.max(s_diag, axis=0, keepdims=True)
        if i > 0:
            m = jnp.maximum(m, jnp.max(s_off, axis=0, keepdims=True))
        p = jnp.exp2(s_diag - m)
        l = jnp.sum(p, axis=0, keepdims=True)
        acc = _dot(vt[:, lo:hi], p.astype(BF16))
        if i > 0:
            p = jnp.exp2(s_off - m)
            l = l + jnp.sum(p, axis=0, keepdims=True)
            acc = acc + _dot(vt[:, 0:lo], p.astype(BF16))
        o_ref[lo:hi, h * MLA_V:(h + 1) * MLA_V] = (acc / l).T.astype(BF16)

    order = [(h, i) for i in reversed(range(nq)) for h in range(ATT_HEADS_PER_STEP)]
    pending = [scores(u) for u in order[:ATT_AHEAD]]
    for n, unit in enumerate(order):
        if n + ATT_AHEAD < len(order):
            pending.append(scores(order[n + ATT_AHEAD]))
        finish(unit, *pending.pop(0))


def _mla_attn(q, k, vt, batch, seq):
    hs = ATT_HEADS_PER_STEP
    return pl.pallas_call(
        _mla_attn_kernel,
        out_shape=jax.ShapeDtypeStruct((batch * seq, MLA_WIDTH), BF16),
        grid=(batch, MLA_HEADS // hs),
        in_specs=[
            pl.BlockSpec((seq, hs * QK_PAD), lambda b, h: (b, h)),
            pl.BlockSpec((seq, hs * QK_PAD), lambda b, h: (b, h)),
            pl.BlockSpec((hs * MLA_V, seq), lambda b, h: (h, b)),
        ],
        out_specs=pl.BlockSpec((seq, hs * MLA_V), lambda b, h: (b, h)),
        compiler_params=_params("parallel", "parallel"),
        name="mla_attn",
    )(q, k, vt)


def _hgrn_masks():
    t = np.arange(HG_CHUNK)[:, None]
    s = np.arange(HG_CHUNK)[None, :]
    masks = [((t // c) % 2 == 1) & ((s // c) == (t // c) - 1) for c in HG_LEVELS]
    masks.append(t == s)
    return np.stack(masks).astype(np.float32)


def _hgrn_kernel(q_ref, f_ref, i_ref, g_ref, lbl_ref, gn_ref, msk_ref, tri_ref, o_ref, b_all):
    lbl = lbl_ref[...]
    e = jnp.exp(lbl - jnp.max(lbl, axis=0, keepdims=True))
    lb = e[0:1, :] / jnp.sum(e, axis=0, keepdims=True)
    gn = gn_ref[...]
    tri = tri_ref[...]
    sub = lax.broadcasted_iota(jnp.int32, (SUBLANES, LANES), 0)
    zeros8 = jnp.zeros((SUBLANES, LANES), F32)
    per_chunk = HG_CHUNK // SUBLANES
    chunk_rows = [slice(u * HG_CHUNK, (u + 1) * HG_CHUNK) for u in range(HG_UNROLL)]

    def sel(bounds, vals):
        out = vals[-1]
        for bound, val in zip(reversed(bounds), reversed(vals[:-1])):
            out = jnp.where(sub < bound, val, out)
        return out

    def head(gi):
        rows = slice(gi * HG_GROUP, (gi + 1) * HG_GROUP)
        q = q_ref[rows, :]
        v = i_ref[rows, :].astype(BF16)
        f = lb + (1.0 - lb) * _sigmoid(f_ref[rows, :])
        lg = jnp.log2(f)
        k = 1.0 - f

        l1 = lg.astype(BF16)
        l2 = (lg - l1.astype(F32)).astype(BF16)
        slabs = [slice(i, i + HG_SLAB) for i in range(0, HG_GROUP, HG_SLAB)]
        b = jnp.concatenate([_dot(tri, l1[r]) + _dot(tri, l2[r]) for r in slabs], axis=0)
        b_all[gi] = b
        return q, k, f, v, b

    def main(gi, hd, st):
        rows = slice(gi * HG_GROUP, (gi + 1) * HG_GROUP)
        q, k, f, v, b = hd
        b_scr = b_all.at[gi]
        cache = {}

        def rowb(u, r):
            if r < 0:
                return zeros8
            if (u, r) not in cache:
                cache[u, r] = jnp.broadcast_to(b_scr[u * HG_CHUNK + r:u * HG_CHUNK + r + 1, :], (SUBLANES, LANES))
            return cache[u, r]

        def build(fn):
            return jnp.concatenate([fn(u, j) for u in range(HG_UNROLL) for j in range(per_chunk)], axis=0)

        def group_rows(x, u, j):
            r0 = u * HG_CHUNK + j * SUBLANES
            return x[r0:r0 + SUBLANES]

        def level_rows(ci, j):
            if ("m", ci, j) not in cache:
                cache["m", ci, j] = msk_ref[ci, j * SUBLANES:(j + 1) * SUBLANES, :] != 0.0
            return cache["m", ci, j]

        diag = jnp.sum(q * k, axis=-1, keepdims=True)
        a = [[jnp.where(level_rows(len(HG_LEVELS), j), group_rows(diag, u, j), 0.0) for j in range(per_chunk)]
             for u in range(HG_UNROLL)]
        for ci, c in enumerate(HG_LEVELS):
            if c >= SUBLANES:
                m = c // SUBLANES
                right = [j for j in range(per_chunk) if (j // m) % 2 == 1]
                pick = lambda x, js: jnp.concatenate([group_rows(x, u, j) for u in range(HG_UNROLL) for j in js], axis=0)
                bs = jnp.concatenate([rowb(u, SUBLANES * (j // m) * m - 1) for u in range(HG_UNROLL) for j in right], axis=0)
                qd = (pick(q, right) * jnp.exp2(pick(b, right) - bs)).astype(BF16)
                kd = jnp.concatenate(
                    [group_rows(k, u, j) * jnp.exp2(rowb(u, SUBLANES * ((j // m) * m + m) - 1) - group_rows(b, u, j))
                     if (j // m) % 2 == 0 else zeros8
                     for u in range(HG_UNROLL) for j in range(per_chunk)], axis=0).astype(BF16)
                rows_per_chunk = len(right) * SUBLANES
                for u, r in enumerate(chunk_rows):
                    prod = _dot_nt(qd[u * rows_per_chunk:(u + 1) * rows_per_chunk], kd[r])
                    for n, j in enumerate(right):
                        a[u][j] = jnp.where(level_rows(ci, j), prod[n * SUBLANES:(n + 1) * SUBLANES], a[u][j])
                continue
            if c == 4:
                bs = build(lambda u, j: sel((4,), (rowb(u, 8 * j - 1), rowb(u, 8 * j + 3))))
                be = build(lambda u, j: sel((4,), (rowb(u, 8 * j + 3), rowb(u, 8 * j + 7))))
            elif c == 2:
                bs = build(lambda u, j: sel((2, 4, 6), (rowb(u, 8 * j - 1), rowb(u, 8 * j + 1),
                                                         rowb(u, 8 * j + 3), rowb(u, 8 * j + 5))))
                be = build(lambda u, j: sel((2, 4, 6), (rowb(u, 8 * j + 1), rowb(u, 8 * j + 3),
                                                         rowb(u, 8 * j + 5), rowb(u, 8 * j + 7))))
            if c == 1:
                qd = (q * f).astype(BF16)
                kd = k.astype(BF16)
            else:
                qd = (q * jnp.exp2(b - bs)).astype(BF16)
                kd = (k * jnp.exp2(be - b)).astype(BF16)
            for u, r in enumerate(chunk_rows):
                prod = _dot_nt(qd[r], kd[r])
                for j in range(per_chunk):
                    a[u][j] = jnp.where(level_rows(ci, j), prod[j * SUBLANES:(j + 1) * SUBLANES], a[u][j])
        a = [jnp.concatenate(a[u], axis=0) for u in range(HG_UNROLL)]

        q_exp = (q * jnp.exp2(b)).astype(BF16)
        b_last = build(lambda u, j: rowb(u, HG_CHUNK - 1))
        k_dec = (k * jnp.exp2(b_last - b)).astype(BF16)
        o_intra = [_dot(a[u].astype(BF16), v[r]) for u, r in enumerate(chunk_rows)]
        st_add = [_dot_tn(v[r], k_dec[r]) for r in chunk_rows]

        o = []
        for u, r in enumerate(chunk_rows):
            o.append(o_intra[u] + _dot_nt(q_exp[r], st.astype(BF16)))
            st = jnp.exp2(rowb(u, HG_CHUNK - 1)[0:1, :]) * st + st_add[u]
        o = jnp.concatenate(o, axis=0)

        g = g_ref[rows, :]
        o_ref[rows, :] = (o * _rms_scale(o) * gn * (g * _sigmoid(g))).astype(BF16)
        return st

    n_groups = q_ref.shape[0] // HG_GROUP
    st = jnp.zeros((HG_D, HG_D), F32)
    hd = head(0)
    for gi in range(n_groups):
        nxt = head(gi + 1) if gi + 1 < n_groups else None
        st = main(gi, hd, st)
        hd = nxt


def _hgrn(proj, lb_logits, out_norm, batch, seq):
    hblk = lambda col: pl.BlockSpec((seq, HG_D), lambda b, h: (b, col // HG_D + h))
    masks = jnp.asarray(_hgrn_masks())
    tri = np.kron(np.eye(HG_SLAB // HG_CHUNK), np.tril(np.ones((HG_CHUNK, HG_CHUNK))))
    tri = jnp.asarray(tri.astype(np.float32), dtype=BF16)
    nlev = masks.shape[0]
    return pl.pallas_call(
        _hgrn_kernel,
        out_shape=jax.ShapeDtypeStruct((batch * seq, HG_WIDTH), BF16),
        grid=(batch, HG_HEADS),
        in_specs=[
            hblk(COL_HQ), hblk(COL_HF), hblk(COL_HI), hblk(COL_HG),
            pl.BlockSpec((lb_logits.shape[0], HG_D), lambda b, h: (0, h)),
            pl.BlockSpec((1, HG_D), lambda b, h: (0, 0)),
            pl.BlockSpec((nlev, HG_CHUNK, HG_CHUNK), lambda b, h: (0, 0, 0)),
            pl.BlockSpec((HG_SLAB, HG_SLAB), lambda b, h: (0, 0)),
        ],
        out_specs=pl.BlockSpec((seq, HG_D), lambda b, h: (b, h)),
        scratch_shapes=[pltpu.VMEM((seq // HG_GROUP, HG_GROUP, HG_D), F32)],
        compiler_params=_params("parallel", "parallel"),
        name="hgrn2",
    )(proj, proj, proj, proj, lb_logits, out_norm, masks, tri)


OUT_TM = 512


def _out_proj_kernel(a_ref, r_ref, x_ref, g_ref, w_ref, o_ref):
    a = a_ref[...].astype(F32)
    an = (a * _rms_scale(a) * g_ref[...]).astype(BF16)
    o_ref[...] = (x_ref[...] + _dot(an, w_ref[:MLA_WIDTH, :].astype(BF16))
                  + _dot(r_ref[...], w_ref[MLA_WIDTH:, :].astype(BF16)))


def _out_proj(a, r, x, gain, w):
    t = x.shape[0]
    return pl.pallas_call(
        _out_proj_kernel,
        out_shape=jax.ShapeDtypeStruct((t, D_MODEL), F32),
        grid=(t // OUT_TM,),
        in_specs=[
            pl.BlockSpec((OUT_TM, MLA_WIDTH), lambda i: (i, 0)),
            pl.BlockSpec((OUT_TM, HG_WIDTH), lambda i: (i, 0)),
            pl.BlockSpec((OUT_TM, D_MODEL), lambda i: (i, 0)),
            pl.BlockSpec((1, MLA_WIDTH), lambda i: (0, 0)),
            pl.BlockSpec(w.shape, lambda i: (0, 0), pipeline_mode=pl.Buffered(1)),
        ],
        out_specs=pl.BlockSpec((OUT_TM, D_MODEL), lambda i: (i, 0)),
        compiler_params=_params("parallel"),
        name="out_proj",
    )(a, r, x, gain, w)


def _mem_kv_kernel(m_ref, g_ref, w_ref, kn_ref, k_ref, v_ref):
    m = m_ref[...]
    mn = (m * _rms_scale(m) * g_ref[...]).astype(BF16)
    kv = _dot(mn, w_ref[...].astype(BF16))
    for h in range(MEM_HEADS):
        k = kv[:, 2 * h * MEM_HD:(2 * h + 1) * MEM_HD]
        k_ref[:, h * MEM_HD:(h + 1) * MEM_HD] = (k * _rms_scale(k) * kn_ref[...]).astype(BF16)
        v_ref[:, h * MEM_HD:(h + 1) * MEM_HD] = kv[:, (2 * h + 1) * MEM_HD:(2 * h + 2) * MEM_HD].astype(BF16)


def _mem_kv(mem, gain, w, k_norm, batch, mem_len):
    return pl.pallas_call(
        _mem_kv_kernel,
        out_shape=(jax.ShapeDtypeStruct((batch * mem_len, MEM_WIDTH), BF16),
                   jax.ShapeDtypeStruct((batch * mem_len, MEM_WIDTH), BF16)),
        grid=(batch,),
        in_specs=[
            pl.BlockSpec((mem_len, D_MODEL), lambda b: (b, 0)),
            pl.BlockSpec((1, D_MODEL), lambda b: (0, 0)),
            pl.BlockSpec(w.shape, lambda b: (0, 0)),
            pl.BlockSpec((1, MEM_HD), lambda b: (0, 0)),
        ],
        out_specs=(pl.BlockSpec((mem_len, MEM_WIDTH), lambda b: (b, 0)),
                   pl.BlockSpec((mem_len, MEM_WIDTH), lambda b: (b, 0))),
        compiler_params=_params("parallel"),
        name="mem_kv",
    )(mem, gain, w, k_norm)


XA_TM = 512


def _xattn_kernel(x_ref, g_ref, wq_ref, qn_ref, k_ref, v_ref, wo_ref, o_ref, att_scr):
    x = x_ref[...]
    h = (x * _rms_scale(x) * g_ref[...]).astype(BF16)
    q = _dot(h, wq_ref[...].astype(BF16))
    heads = [slice(hd * MEM_HD, (hd + 1) * MEM_HD) for hd in range(MEM_HEADS)]
    q_gain = qn_ref[...] * (float(np.sqrt(MEM_HD)) * XATT_Q_SCALE)
    scores = []
    for cols in heads:
        qh = q[:, cols]
        r = lax.rsqrt(jnp.sum(qh * qh, axis=-1, keepdims=True) + MEM_HD * EPS)
        scores.append(_dot_nt((qh * r * q_gain).astype(BF16), k_ref[:, cols]))
    for cols, s in zip(heads, scores):
        p = jnp.exp2(s - jnp.max(s, axis=-1, keepdims=True))
        l = jnp.sum(p, axis=-1, keepdims=True)
        att_scr[:, cols] = (_dot(p.astype(BF16), v_ref[:, cols]) / l).astype(BF16)
    o_ref[...] = x + _dot(att_scr[...], wo_ref[...].astype(BF16))


def _xattn(x, gain, wq, q_norm, kx, vx, wo, seq, mem_len):
    t = x.shape[0]
    per_batch = seq // XA_TM
    return pl.pallas_call(
        _xattn_kernel,
        out_shape=jax.ShapeDtypeStruct((t, D_MODEL), F32),
        grid=(t // XA_TM,),
        in_specs=[
            pl.BlockSpec((XA_TM, D_MODEL), lambda i: (i, 0)),
            pl.BlockSpec((1, D_MODEL), lambda i: (0, 0)),
            pl.BlockSpec(wq.shape, lambda i: (0, 0)),
            pl.BlockSpec((1, MEM_HD), lambda i: (0, 0)),
            pl.BlockSpec((mem_len, MEM_WIDTH), lambda i: (i // per_batch, 0)),
            pl.BlockSpec((mem_len, MEM_WIDTH), lambda i: (i // per_batch, 0)),
            pl.BlockSpec(wo.shape, lambda i: (0, 0)),
        ],
        out_specs=pl.BlockSpec((XA_TM, D_MODEL), lambda i: (i, 0)),
        scratch_shapes=[pltpu.VMEM((XA_TM, MEM_WIDTH), BF16)],
        compiler_params=_params("parallel"),
        name="xattn",
    )(x, gain, wq, q_norm, kx, vx, wo)


def _pe_pair(x1, x2):
    z = jnp.zeros(x1.shape[:-1] + (LANES - MLA_ROPE,), x1.dtype)
    return jnp.concatenate([x1, x2, z], axis=-1), jnp.concatenate([x2, x1, z], axis=-1)


W_IN_TC = 512


def _w_in_prep_kernel(w_ref, o_ref):
    half = MLA_ROPE // 2
    kpe0 = MLA_Q_RANK + MLA_KV_RANK
    hg0 = kpe0 + MLA_ROPE
    zeros = jnp.zeros((LANES - MLA_ROPE, o_ref.shape[1]), BF16)
    o_ref[:kpe0, :] = w_ref[:kpe0, :].astype(BF16)
    o_ref[kpe0:COL_KPE, :] = w_ref[hg0:, :].astype(BF16)
    x1 = w_ref[kpe0:kpe0 + half, :].astype(BF16)
    x2 = w_ref[kpe0 + half:hg0, :].astype(BF16)
    o_ref[COL_KPE:COL_KPE + half, :] = x1
    o_ref[COL_KPE + half:COL_KPE + MLA_ROPE, :] = x2
    o_ref[COL_KPE + MLA_ROPE:COL_KPE_SW, :] = zeros
    o_ref[COL_KPE_SW:COL_KPE_SW + half, :] = x2
    o_ref[COL_KPE_SW + half:COL_KPE_SW + MLA_ROPE, :] = x1
    o_ref[COL_KPE_SW + MLA_ROPE:, :] = zeros


def _prep_w_in(w_in, layer):
    w_t = jnp.swapaxes(w_in, 1, 2)
    _, n, k = w_t.shape
    return pl.pallas_call(
        _w_in_prep_kernel,
        out_shape=jax.ShapeDtypeStruct((IN_COLS_PAD, k), BF16),
        grid=(k // W_IN_TC,),
        in_specs=[pl.BlockSpec((None, n, W_IN_TC), lambda i: (layer, 0, i))],
        out_specs=pl.BlockSpec((IN_COLS_PAD, W_IN_TC), lambda i: (0, i)),
        compiler_params=_params("parallel"),
        name="w_in_prep",
    )(w_t)


def _prep_w_q_up(w):
    w = w.astype(BF16).reshape(MLA_Q_RANK, MLA_HEADS, MLA_QK)
    half = MLA_ROPE // 2
    pe, pe_sw = _pe_pair(w[..., MLA_NOPE:MLA_NOPE + half], w[..., MLA_NOPE + half:])
    return jnp.concatenate([w[..., :MLA_NOPE], pe, pe_sw], axis=-1).reshape(MLA_Q_RANK, MLA_HEADS * Q_UP_PER_HEAD)


def _pe_gains(norm):
    half = MLA_ROPE // 2
    g1 = norm[MLA_NOPE:MLA_NOPE + half]
    g2 = norm[MLA_NOPE + half:]
    ga, gb = _pe_pair(g1, g2)
    sign = jnp.concatenate([-jnp.ones((half,), F32), jnp.ones((LANES - half,), F32)])
    return ga[None, :], (gb * sign)[None, :]


def kernel(x, mem, positions, ffn1_norm, ffn1_w_gate, ffn1_w_up, ffn1_w_down, mix_norm, w_in, mla_q_a_norm, mla_w_q_up, mla_kv_a_norm, mla_w_kv_up, mla_q_norm, mla_k_norm, mla_out_norm, hg_lb_logits, hg_out_norm, w_out, xattn_norm, mem_norm, xattn_w_q, xattn_w_kv, xattn_q_norm, xattn_k_norm, xattn_w_o, ffn2_norm, ffn2_w_gate, ffn2_w_up, ffn2_w_down):
    batch, seq, _ = x.shape
    mem_len = mem.shape[1]
    depth = ffn1_norm.shape[0]
    assert depth == 1 and seq % ATT_T == 0 and seq % XA_TM == 0
    t = batch * seq
    xt = x.reshape(t, D_MODEL)
    half = MLA_ROPE // 2
    quarters = LANES // half
    pos = positions.reshape(t // PREP_TM, quarters, PREP_TM // quarters).transpose(0, 2, 1)
    pos = jnp.repeat(pos, half, axis=-1).reshape(t // quarters, LANES)
    inv_freq = ROPE_BASE ** (-np.arange(half, dtype=np.float32) / half)
    invf = jnp.asarray(np.tile(inv_freq, quarters)[None, :])
    l = 0

    xt = _ffn(xt, ffn1_norm[l][None, :], ffn1_w_gate[l], ffn1_w_up[l], ffn1_w_down[l])

    proj = _in_proj(xt, mix_norm[l][None, :], _prep_w_in(w_in, l))
    qpa, qpb = _pe_gains(mla_q_norm[l])
    kpa, kpb = _pe_gains(mla_k_norm[l])
    q, k, v = _mla_prep(proj, pos, invf, mla_q_a_norm[l][None, :], mla_kv_a_norm[l][None, :],
                        _prep_w_q_up(mla_w_q_up[l]), mla_w_kv_up[l],
                        mla_q_norm[l][None, :MLA_NOPE], qpa, qpb,
                        mla_k_norm[l][None, :MLA_NOPE], kpa, kpb)
    a = _mla_attn(q, k, v, batch, seq)
    r = _hgrn(proj, hg_lb_logits, hg_out_norm[l][None, :], batch, seq)
    xt = _out_proj(a, r, xt, mla_out_norm[l][None, :], w_out[l])

    kx, vx = _mem_kv(mem.reshape(batch * mem_len, D_MODEL), mem_norm[l][None, :],
                     xattn_w_kv[l], xattn_k_norm[l][None, :], batch, mem_len)
    xt = _xattn(xt, xattn_norm[l][None, :], xattn_w_q[l], xattn_q_norm[l][None, :],
                kx, vx, xattn_w_o[l], seq, mem_len)

    xt = _ffn(xt, ffn2_norm[l][None, :], ffn2_w_gate[l], ffn2_w_up[l], ffn2_w_down[l])
    return xt.reshape(batch, seq, D_MODEL)
```

```python
import numpy as np
import jax
import jax.numpy as jnp
from jax import lax
from jax.experimental import pallas as pl
from jax.experimental.pallas import tpu as pltpu

F32 = jnp.float32
BF16 = jnp.bfloat16

EPS = 1e-6
ROPE_BASE = 10000.0
LANES = 128
SUBLANES = 8

D_MODEL = 2048
D_FF = 5504
FFN_TM = 1024
FFN_TF = 512

MLA_HEADS = 8
MLA_NOPE = 128
MLA_ROPE = 64
MLA_QK = MLA_NOPE + MLA_ROPE
MLA_V = 128
MLA_Q_RANK = 512
MLA_KV_RANK = 256
MLA_WIDTH = MLA_HEADS * MLA_V
QK_PAD = 256
Q_UP_PER_HEAD = 384
ATT_Q_SCALE = float(MLA_QK ** -0.5 * np.log2(np.e))

HG_HEADS = 8
HG_D = 128
HG_CHUNK = 64
HG_LEVELS = (32, 16, 8, 4, 2, 1)
HG_UNROLL = 16
HG_GROUP = HG_CHUNK * HG_UNROLL
HG_SLAB = 256
HG_WIDTH = HG_HEADS * HG_D

MEM_HEADS = 4
MEM_HD = 128
MEM_WIDTH = MEM_HEADS * MEM_HD
XATT_Q_SCALE = float(MEM_HD ** -0.5 * np.log2(np.e))

COL_CQ = 0
COL_CKV = MLA_Q_RANK
COL_HQ = MLA_Q_RANK + MLA_KV_RANK
COL_HF = COL_HQ + HG_WIDTH
COL_HI = COL_HF + HG_WIDTH
COL_HG = COL_HI + HG_WIDTH
COL_KPE = COL_HG + HG_WIDTH
COL_KPE_SW = COL_KPE + LANES
IN_COLS_PAD = COL_KPE_SW + LANES

VMEM_LIMIT = 56 * 1024 * 1024
FFN_VMEM_LIMIT = 61 * 1024 * 1024


def _params(*sem, vmem_limit=VMEM_LIMIT):
    return pltpu.CompilerParams(dimension_semantics=sem, vmem_limit_bytes=vmem_limit)


def _rms_scale(x):
    return lax.rsqrt(jnp.mean(x * x, axis=-1, keepdims=True) + EPS)


def _sigmoid(x):
    return 1.0 / (1.0 + jnp.exp(-x))


def _dot(a, b):
    return jnp.dot(a, b, preferred_element_type=F32)


def _dot_nt(a, b):
    return lax.dot_general(a, b, (((1,), (1,)), ((), ())), preferred_element_type=F32)


def _dot_tn(a, b):
    return lax.dot_general(a, b, (((0,), (0,)), ((), ())), preferred_element_type=F32)


def _ffn_kernel(x_hbm, g_ref, wg_ref, wu_ref, wd_ref, o_ref, h_scr, x_scr, x_sem):
    i = pl.program_id(0)

    def x_copy(tile):
        rows = pl.ds(pl.multiple_of(tile * FFN_TM, FFN_TM), FFN_TM)
        return pltpu.make_async_copy(x_hbm.at[rows, :], x_scr, x_sem)

    def branch(h):
        gate = _dot(h, wg_ref[...].astype(BF16))
        up = _dot(h, wu_ref[...].astype(BF16))
        a = 0.5 * gate * _sigmoid(gate) * up
        f0 = pl.program_id(1) * FFN_TF
        a_ok = f0 + lax.broadcasted_iota(jnp.int32, (1, FFN_TF), 1) < D_FF
        w_ok = f0 + lax.broadcasted_iota(jnp.int32, (FFN_TF, 1), 0) < D_FF
        a = jnp.where(a_ok, a, 0.0).astype(BF16)
        wd = jnp.where(w_ok, wd_ref[...], 0.0).astype(BF16)
        return _dot(a, wd)

    @pl.when(pl.program_id(1) == 0)
    def _():
        @pl.when(i == 0)
        def _():
            x_copy(0).start()

        x_copy(i).wait()
        x = x_scr[...]
        h = (x * _rms_scale(x) * g_ref[...]).astype(BF16)
        h_scr[...] = h
        o_ref[...] = x + branch(h)

        @pl.when(i + 1 < pl.num_programs(0))
        def _():
            x_copy(i + 1).start()

    @pl.when(pl.program_id(1) > 0)
    def _():
        o_ref[...] += branch(h_scr[...])


def _ffn(x, gain, wg, wu, wd):
    t = x.shape[0]
    return pl.pallas_call(
        _ffn_kernel,
        out_shape=jax.ShapeDtypeStruct((t, D_MODEL), F32),
        grid=(t // FFN_TM, pl.cdiv(D_FF, FFN_TF)),
        in_specs=[
            pl.BlockSpec(memory_space=pl.ANY),
            pl.BlockSpec((1, D_MODEL), lambda i, f: (0, 0)),
            pl.BlockSpec((D_MODEL, FFN_TF), lambda i, f: (0, f)),
            pl.BlockSpec((D_MODEL, FFN_TF), lambda i, f: (0, f)),
            pl.BlockSpec((FFN_TF, D_MODEL), lambda i, f: (f, 0)),
        ],
        out_specs=pl.BlockSpec((FFN_TM, D_MODEL), lambda i, f: (i, 0)),
        scratch_shapes=[pltpu.VMEM((FFN_TM, D_MODEL), BF16), pltpu.VMEM((FFN_TM, D_MODEL), F32),
                        pltpu.SemaphoreType.DMA],
        compiler_params=_params("arbitrary", "arbitrary", vmem_limit=FFN_VMEM_LIMIT),
        name="ffn",
    )(x, gain, wg, wu, wd)


PROJ_TM = 1024
PROJ_TN = 1024


def _in_proj_kernel(x_ref, g_ref, w_ref, o_ref, h_scr):
    @pl.when(pl.program_id(1) == 0)
    def _():
        x = x_ref[...]
        h = (x * _rms_scale(x) * g_ref[...]).astype(BF16)
        h_scr[...] = h
        o_ref[...] = _dot_nt(h, w_ref[...])

    @pl.when(pl.program_id(1) > 0)
    def _():
        o_ref[...] = _dot_nt(h_scr[...], w_ref[...])


def _in_proj(x, gain, w_t):
    t = x.shape[0]
    n = w_t.shape[0]
    return pl.pallas_call(
        _in_proj_kernel,
        out_shape=jax.ShapeDtypeStruct((t, n), F32),
        grid=(t // PROJ_TM, n // PROJ_TN),
        in_specs=[
            pl.BlockSpec((PROJ_TM, D_MODEL), lambda i, j: (i, 0)),
            pl.BlockSpec((1, D_MODEL), lambda i, j: (0, 0)),
            pl.BlockSpec((PROJ_TN, D_MODEL), lambda i, j: (j, 0)),
        ],
        out_specs=pl.BlockSpec((PROJ_TM, PROJ_TN), lambda i, j: (i, j)),
        scratch_shapes=[pltpu.VMEM((PROJ_TM, D_MODEL), BF16)],
        compiler_params=_params("parallel", "arbitrary"),
        name="in_proj",
    )(x, gain, w_t)


PREP_TM = 1024


def _mla_prep_kernel(cq_ref, ckv_ref, kpe_ref, kpesw_ref, pos_ref, invf_ref,
                     qa_ref, kva_ref, wq_ref, wkv_ref,
                     qn_ref, qpa_ref, qpb_ref, kn_ref, kpa_ref, kpb_ref,
                     q_ref, k_ref, vt_ref):
    ang = pos_ref[...].astype(F32) * invf_ref[...]
    cos4 = jnp.cos(ang)
    sin4 = jnp.sin(ang)
    lane = lax.broadcasted_iota(jnp.int32, (1, LANES), 1)
    half = MLA_ROPE // 2

    def expand(t4):
        parts = []
        for part in range(LANES // half):
            t = pltpu.roll(t4, (LANES - half * part) % LANES, 1) if part else t4
            parts.append(jnp.where(lane < half, t, pltpu.roll(t, half, 1)))
        return jnp.concatenate(parts, axis=0)

    cos = expand(cos4)
    sin = expand(sin4)

    q_const = float(np.sqrt(MLA_QK)) * ATT_Q_SCALE
    k_const = float(np.sqrt(MLA_QK))

    cq = cq_ref[...]
    cqn = (cq * _rms_scale(cq) * qa_ref[...]).astype(BF16)
    qall = _dot(cqn, wq_ref[...])
    qn_gain = qn_ref[...] * q_const
    q_cos = (qpa_ref[...] * q_const) * cos
    q_sin = (qpb_ref[...] * q_const) * sin
    for h in range(MLA_HEADS):
        base = h * Q_UP_PER_HEAD
        qn = qall[:, base:base + LANES]
        y = qall[:, base + LANES:base + 2 * LANES]
        ysw = qall[:, base + 2 * LANES:base + 3 * LANES]
        r = lax.rsqrt(jnp.sum(qn * qn + y * y, axis=-1, keepdims=True) + MLA_QK * EPS)
        q_ref[:, h * QK_PAD:h * QK_PAD + LANES] = (qn * r * qn_gain).astype(BF16)
        q_ref[:, h * QK_PAD + LANES:(h + 1) * QK_PAD] = ((y * q_cos + ysw * q_sin) * r).astype(BF16)

    ckv = ckv_ref[...]
    ckvn = (ckv * _rms_scale(ckv) * kva_ref[...]).astype(BF16)
    kvall = _dot(ckvn, wkv_ref[...].astype(BF16))
    yk = kpe_ref[...]
    kn_gain = kn_ref[...] * k_const
    kr = yk * ((kpa_ref[...] * k_const) * cos) + kpesw_ref[...] * ((kpb_ref[...] * k_const) * sin)
    yk_sq = yk * yk
    for h in range(MLA_HEADS):
        kn = kvall[:, h * 2 * LANES:h * 2 * LANES + LANES]
        r = lax.rsqrt(jnp.sum(kn * kn + yk_sq, axis=-1, keepdims=True) + MLA_QK * EPS)
        k_ref[:, h * QK_PAD:h * QK_PAD + LANES] = (kn * r * kn_gain).astype(BF16)
        k_ref[:, h * QK_PAD + LANES:(h + 1) * QK_PAD] = (kr * r).astype(BF16)
        vt_ref[h * MLA_V:(h + 1) * MLA_V, :] = kvall[:, h * 2 * LANES + LANES:(h + 1) * 2 * LANES].T.astype(BF16)


def _mla_prep(proj, pos, invf, qa, kva, wq, wkv, qn, qpa, qpb, kn, kpa, kpb):
    t = proj.shape[0]
    tm = PREP_TM
    row = lambda i: (0, 0)
    vec = lambda n: pl.BlockSpec((1, n), row)
    return pl.pallas_call(
        _mla_prep_kernel,
        out_shape=(jax.ShapeDtypeStruct((t, MLA_HEADS * QK_PAD), BF16),
                   jax.ShapeDtypeStruct((t, MLA_HEADS * QK_PAD), BF16),
                   jax.ShapeDtypeStruct((MLA_WIDTH, t), BF16)),
        grid=(t // tm,),
        in_specs=[
            pl.BlockSpec((tm, MLA_Q_RANK), lambda i: (i, COL_CQ // MLA_Q_RANK)),
            pl.BlockSpec((tm, MLA_KV_RANK), lambda i: (i, COL_CKV // MLA_KV_RANK)),
            pl.BlockSpec((tm, LANES), lambda i: (i, COL_KPE // LANES)),
            pl.BlockSpec((tm, LANES), lambda i: (i, COL_KPE_SW // LANES)),
            pl.BlockSpec((tm // (LANES // (MLA_ROPE // 2)), LANES), lambda i: (i, 0)),
            vec(LANES),
            vec(MLA_Q_RANK), vec(MLA_KV_RANK),
            pl.BlockSpec(wq.shape, row), pl.BlockSpec(wkv.shape, row),
            vec(LANES), vec(LANES), vec(LANES), vec(LANES), vec(LANES), vec(LANES),
        ],
        out_specs=(pl.BlockSpec((tm, MLA_HEADS * QK_PAD), lambda i: (i, 0)),
                   pl.BlockSpec((tm, MLA_HEADS * QK_PAD), lambda i: (i, 0)),
                   pl.BlockSpec((MLA_WIDTH, tm), lambda i: (0, i))),
        compiler_params=_params("parallel"),
        name="mla_prep",
    )(proj, proj, proj, proj, pos, invf, qa, kva, wq, wkv, qn, qpa, qpb, kn, kpa, kpb)


ATT_T = 256
ATT_AHEAD = 5
ATT_HEADS_PER_STEP = 2


def _mla_attn_kernel(q_ref, k_ref, vt_ref, o_ref):
    kpos = lax.broadcasted_iota(jnp.int32, (ATT_T, ATT_T), 0)
    qpos = lax.broadcasted_iota(jnp.int32, (ATT_T, ATT_T), 1)
    nq = q_ref.shape[0] // ATT_T

    def scores(unit):
        h, i = unit
        lo, hi = i * ATT_T, (i + 1) * ATT_T
        cols = slice(h * QK_PAD, (h + 1) * QK_PAD)
        q = q_ref[lo:hi, cols]
        s_diag = jnp.where(kpos <= qpos, _dot_nt(k_ref[lo:hi, cols], q), -jnp.inf)
        s_off = _dot_nt(k_ref[0:lo, cols], q) if i > 0 else None
        return s_diag, s_off

    def finish(unit, s_diag, s_off):
        h, i = unit
        lo, hi = i * ATT_T, (i + 1) * ATT_T
        vt = vt_ref.at[h * MLA_V:(h + 1) * MLA_V, :]
        m = jnp.max(s_diag, axis=0, keepdims=True)
        if i > 0:
            m = jnp.maximum(m, jnp.max(s_off, axis=0, keepdims=True))
        p = jnp.exp2(s_diag - m)
        l = jnp.sum(p, axis=0, keepdims=True)
        acc = _dot(vt[:, lo:hi], p.astype(BF16))
        if i > 0:
            p = jnp.exp2(s_off - m)
            l = l + jnp.sum(p, axis=0, keepdims=True)
            acc = acc + _dot(vt[:, 0:lo], p.astype(BF16))
        o_ref[lo:hi, h * MLA_V:(h + 1) * MLA_V] = (acc / l).T.astype(BF16)

    order = [(h, i) for i in reversed(range(nq)) for h in range(ATT_HEADS_PER_STEP)]
    pending = [scores(u) for u in order[:ATT_AHEAD]]
    for n, unit in enumerate(order):
        if n + ATT_AHEAD < len(order):
            pending.append(scores(order[n + ATT_AHEAD]))
        finish(unit, *pending.pop(0))


def _mla_attn(q, k, vt, batch, seq):
    hs = ATT_HEADS_PER_STEP
    return pl.pallas_call(
        _mla_attn_kernel,
        out_shape=jax.ShapeDtypeStruct((batch * seq, MLA_WIDTH), BF16),
        grid=(batch, MLA_HEADS // hs),
        in_specs=[
            pl.BlockSpec((seq, hs * QK_PAD), lambda b, h: (b, h)),
            pl.BlockSpec((seq, hs * QK_PAD), lambda b, h: (b, h)),
            pl.BlockSpec((hs * MLA_V, seq), lambda b, h: (h, b)),
        ],
        out_specs=pl.BlockSpec((seq, hs * MLA_V), lambda b, h: (b, h)),
        compiler_params=_params("parallel", "parallel"),
        name="mla_attn",
    )(q, k, vt)


def _hgrn_masks():
    t = np.arange(HG_CHUNK)[:, None]
    s = np.arange(HG_CHUNK)[None, :]
    masks = [((t // c) % 2 == 1) & ((s // c) == (t // c) - 1) for c in HG_LEVELS]
    masks.append(t == s)
    return np.stack(masks).astype(np.float32)


def _hgrn_kernel(q_ref, f_ref, i_ref, g_ref, lbl_ref, gn_ref, msk_ref, tri_ref, o_ref, b_all):
    lbl = lbl_ref[...]
    e = jnp.exp(lbl - jnp.max(lbl, axis=0, keepdims=True))
    lb = e[0:1, :] / jnp.sum(e, axis=0, keepdims=True)
    gn = gn_ref[...]
    tri = tri_ref[...]
    sub = lax.broadcasted_iota(jnp.int32, (SUBLANES, LANES), 0)
    zeros8 = jnp.zeros((SUBLANES, LANES), F32)
    per_chunk = HG_CHUNK // SUBLANES
    chunk_rows = [slice(u * HG_CHUNK, (u + 1) * HG_CHUNK) for u in range(HG_UNROLL)]

    def sel(bounds, vals):
        out = vals[-1]
        for bound, val in zip(reversed(bounds), reversed(vals[:-1])):
            out = jnp.where(sub < bound, val, out)
        return out

    def head(gi):
        rows = slice(gi * HG_GROUP, (gi + 1) * HG_GROUP)
        q = q_ref[rows, :]
        v = i_ref[rows, :].astype(BF16)
        f = lb + (1.0 - lb) * _sigmoid(f_ref[rows, :])
        lg = jnp.log2(f)
        k = 1.0 - f

        l1 = lg.astype(BF16)
        l2 = (lg - l1.astype(F32)).astype(BF16)
        slabs = [slice(i, i + HG_SLAB) for i in range(0, HG_GROUP, HG_SLAB)]
        b = jnp.concatenate([_dot(tri, l1[r]) + _dot(tri, l2[r]) for r in slabs], axis=0)
        b_all[gi] = b
        return q, k, f, v, b

    def main(gi, hd, st):
        rows = slice(gi * HG_GROUP, (gi + 1) * HG_GROUP)
        q, k, f, v, b = hd
        b_scr = b_all.at[gi]
        cache = {}

        def rowb(u, r):
            if r < 0:
                return zeros8
            if (u, r) not in cache:
                cache[u, r] = jnp.broadcast_to(b_scr[u * HG_CHUNK + r:u * HG_CHUNK + r + 1, :], (SUBLANES, LANES))
            return cache[u, r]

        def build(fn):
            return jnp.concatenate([fn(u, j) for u in range(HG_UNROLL) for j in range(per_chunk)], axis=0)

        def group_rows(x, u, j):
            r0 = u * HG_CHUNK + j * SUBLANES
            return x[r0:r0 + SUBLANES]

        def level_rows(ci, j):
            if ("m", ci, j) not in cache:
                cache["m", ci, j] = msk_ref[ci, j * SUBLANES:(j + 1) * SUBLANES, :] != 0.0
            return cache["m", ci, j]

        diag = jnp.sum(q * k, axis=-1, keepdims=True)
        a = [[jnp.where(level_rows(len(HG_LEVELS), j), group_rows(diag, u, j), 0.0) for j in range(per_chunk)]
             for u in range(HG_UNROLL)]
        for ci, c in enumerate(HG_LEVELS):
            if c >= SUBLANES:
                m = c // SUBLANES
                right = [j for j in range(per_chunk) if (j // m) % 2 == 1]
                pick = lambda x, js: jnp.concatenate([group_rows(x, u, j) for u in range(HG_UNROLL) for j in js], axis=0)
                bs = jnp.concatenate([rowb(u, SUBLANES * (j // m) * m - 1) for u in range(HG_UNROLL) for j in right], axis=0)
                qd = (pick(q, right) * jnp.exp2(pick(b, right) - bs)).astype(BF16)
                kd = jnp.concatenate(
                    [group_rows(k, u, j) * jnp.exp2(rowb(u, SUBLANES * ((j // m) * m + m) - 1) - group_rows(b, u, j))
                     if (j // m) % 2 == 0 else zeros8
                     for u in range(HG_UNROLL) for j in range(per_chunk)], axis=0).astype(BF16)
                rows_per_chunk = len(right) * SUBLANES
                for u, r in enumerate(chunk_rows):
                    prod = _dot_nt(qd[u * rows_per_chunk:(u + 1) * rows_per_chunk], kd[r])
                    for n, j in enumerate(right):
                        a[u][j] = jnp.where(level_rows(ci, j), prod[n * SUBLANES:(n + 1) * SUBLANES], a[u][j])
                continue
            if c == 4:
                bs = build(lambda u, j: sel((4,), (rowb(u, 8 * j - 1), rowb(u, 8 * j + 3))))
                be = build(lambda u, j: sel((4,), (rowb(u, 8 * j + 3), rowb(u, 8 * j + 7))))
            elif c == 2:
                bs = build(lambda u, j: sel((2, 4, 6), (rowb(u, 8 * j - 1), rowb(u, 8 * j + 1),
                                                         rowb(u, 8 * j + 3), rowb(u, 8 * j + 5))))
                be = build(lambda u, j: sel((2, 4, 6), (rowb(u, 8 * j + 1), rowb(u, 8 * j + 3),
                                                         rowb(u, 8 * j + 5), rowb(u, 8 * j + 7))))
            if c == 1:
                qd = (q * f).astype(BF16)
                kd = k.astype(BF16)
            else:
                qd = (q * jnp.exp2(b - bs)).astype(BF16)
                kd = (k * jnp.exp2(be - b)).astype(BF16)
            for u, r in enumerate(chunk_rows):
                prod = _dot_nt(qd[r], kd[r])
                for j in range(per_chunk):
                    a[u][j] = jnp.where(level_rows(ci, j), prod[j * SUBLANES:(j + 1) * SUBLANES], a[u][j])
        a = [jnp.concatenate(a[u], axis=0) for u in range(HG_UNROLL)]

        q_exp = (q * jnp.exp2(b)).astype(BF16)
        b_last = build(lambda u, j: rowb(u, HG_CHUNK - 1))
        k_dec = (k * jnp.exp2(b_last - b)).astype(BF16)
        o_intra = [_dot(a[u].astype(BF16), v[r]) for u, r in enumerate(chunk_rows)]
        st_add = [_dot_tn(v[r], k_dec[r]) for r in chunk_rows]

        o = []
        for u, r in enumerate(chunk_rows):
            o.append(o_intra[u] + _dot_nt(q_exp[r], st.astype(BF16)))
            st = jnp.exp2(rowb(u, HG_CHUNK - 1)[0:1, :]) * st + st_add[u]
        o = jnp.concatenate(o, axis=0)

        g = g_ref[rows, :]
        o_ref[rows, :] = (o * _rms_scale(o) * gn * (g * _sigmoid(g))).astype(BF16)
        return st

    n_groups = q_ref.shape[0] // HG_GROUP
    st = jnp.zeros((HG_D, HG_D), F32)
    hd = head(0)
    for gi in range(n_groups):
        nxt = head(gi + 1) if gi + 1 < n_groups else None
        st = main(gi, hd, st)
        hd = nxt


def _hgrn(proj, lb_logits, out_norm, batch, seq):
    hblk = lambda col: pl.BlockSpec((seq, HG_D), lambda b, h: (b, col // HG_D + h))
    masks = jnp.asarray(_hgrn_masks())
    tri = np.kron(np.eye(HG_SLAB // HG_CHUNK), np.tril(np.ones((HG_CHUNK, HG_CHUNK))))
    tri = jnp.asarray(tri.astype(np.float32), dtype=BF16)
    nlev = masks.shape[0]
    return pl.pallas_call(
        _hgrn_kernel,
        out_shape=jax.ShapeDtypeStruct((batch * seq, HG_WIDTH), BF16),
        grid=(batch, HG_HEADS),
        in_specs=[
            hblk(COL_HQ), hblk(COL_HF), hblk(COL_HI), hblk(COL_HG),
            pl.BlockSpec((lb_logits.shape[0], HG_D), lambda b, h: (0, h)),
            pl.BlockSpec((1, HG_D), lambda b, h: (0, 0)),
            pl.BlockSpec((nlev, HG_CHUNK, HG_CHUNK), lambda b, h: (0, 0, 0)),
            pl.BlockSpec((HG_SLAB, HG_SLAB), lambda b, h: (0, 0)),
        ],
        out_specs=pl.BlockSpec((seq, HG_D), lambda b, h: (b, h)),
        scratch_shapes=[pltpu.VMEM((seq // HG_GROUP, HG_GROUP, HG_D), F32)],
        compiler_params=_params("parallel", "parallel"),
        name="hgrn2",
    )(proj, proj, proj, proj, lb_logits, out_norm, masks, tri)


OUT_TM = 512


def _out_proj_kernel(a_ref, r_ref, x_ref, g_ref, w_ref, o_ref):
    a = a_ref[...].astype(F32)
    an = (a * _rms_scale(a) * g_ref[...]).astype(BF16)
    o_ref[...] = (x_ref[...] + _dot(an, w_ref[:MLA_WIDTH, :].astype(BF16))
                  + _dot(r_ref[...], w_ref[MLA_WIDTH:, :].astype(BF16)))


def _out_proj(a, r, x, gain, w):
    t = x.shape[0]
    return pl.pallas_call(
        _out_proj_kernel,
        out_shape=jax.ShapeDtypeStruct((t, D_MODEL), F32),
        grid=(t // OUT_TM,),
        in_specs=[
            pl.BlockSpec((OUT_TM, MLA_WIDTH), lambda i: (i, 0)),
            pl.BlockSpec((OUT_TM, HG_WIDTH), lambda i: (i, 0)),
            pl.BlockSpec((OUT_TM, D_MODEL), lambda i: (i, 0)),
            pl.BlockSpec((1, MLA_WIDTH), lambda i: (0, 0)),
            pl.BlockSpec(w.shape, lambda i: (0, 0), pipeline_mode=pl.Buffered(1)),
        ],
        out_specs=pl.BlockSpec((OUT_TM, D_MODEL), lambda i: (i, 0)),
        compiler_params=_params("parallel"),
        name="out_proj",
    )(a, r, x, gain, w)


def _mem_kv_kernel(m_ref, g_ref, w_ref, kn_ref, k_ref, v_ref):
    m = m_ref[...]
    mn = (m * _rms_scale(m) * g_ref[...]).astype(BF16)
    kv = _dot(mn, w_ref[...].astype(BF16))
    for h in range(MEM_HEADS):
        k = kv[:, 2 * h * MEM_HD:(2 * h + 1) * MEM_HD]
        k_ref[:, h * MEM_HD:(h + 1) * MEM_HD] = (k * _rms_scale(k) * kn_ref[...]).astype(BF16)
        v_ref[:, h * MEM_HD:(h + 1) * MEM_HD] = kv[:, (2 * h + 1) * MEM_HD:(2 * h + 2) * MEM_HD].astype(BF16)


def _mem_kv(mem, gain, w, k_norm, batch, mem_len):
    return pl.pallas_call(
        _mem_kv_kernel,
        out_shape=(jax.ShapeDtypeStruct((batch * mem_len, MEM_WIDTH), BF16),
                   jax.ShapeDtypeStruct((batch * mem_len, MEM_WIDTH), BF16)),
        grid=(batch,),
        in_specs=[
            pl.BlockSpec((mem_len, D_MODEL), lambda b: (b, 0)),
            pl.BlockSpec((1, D_MODEL), lambda b: (0, 0)),
            pl.BlockSpec(w.shape, lambda b: (0, 0)),
            pl.BlockSpec((1, MEM_HD), lambda b: (0, 0)),
        ],
        out_specs=(pl.BlockSpec((mem_len, MEM_WIDTH), lambda b: (b, 0)),
                   pl.BlockSpec((mem_len, MEM_WIDTH), lambda b: (b, 0))),
        compiler_params=_params("parallel"),
        name="mem_kv",
    )(mem, gain, w, k_norm)


XA_TM = 1024


def _xattn_kernel(x_ref, g_ref, wq_ref, qn_ref, k_ref, v_ref, wo_ref, o_ref, att_scr):
    x = x_ref[...]
    h = (x * _rms_scale(x) * g_ref[...]).astype(BF16)
    q = _dot(h, wq_ref[...].astype(BF16))
    heads = [slice(hd * MEM_HD, (hd + 1) * MEM_HD) for hd in range(MEM_HEADS)]
    q_gain = qn_ref[...] * (float(np.sqrt(MEM_HD)) * XATT_Q_SCALE)
    scores = []
    for cols in heads:
        qh = q[:, cols]
        r = lax.rsqrt(jnp.sum(qh * qh, axis=-1, keepdims=True) + MEM_HD * EPS)
        scores.append(_dot_nt((qh * r * q_gain).astype(BF16), k_ref[:, cols]))
    for cols, s in zip(heads, scores):
        p = jnp.exp2(s - jnp.max(s, axis=-1, keepdims=True))
        l = jnp.sum(p, axis=-1, keepdims=True)
        att_scr[:, cols] = (_dot(p.astype(BF16), v_ref[:, cols]) / l).astype(BF16)
    o_ref[...] = x + _dot(att_scr[...], wo_ref[...].astype(BF16))


def _xattn(x, gain, wq, q_norm, kx, vx, wo, seq, mem_len):
    t = x.shape[0]
    per_batch = seq // XA_TM
    return pl.pallas_call(
        _xattn_kernel,
        out_shape=jax.ShapeDtypeStruct((t, D_MODEL), F32),
        grid=(t // XA_TM,),
        in_specs=[
            pl.BlockSpec((XA_TM, D_MODEL), lambda i: (i, 0)),
            pl.BlockSpec((1, D_MODEL), lambda i: (0, 0)),
            pl.BlockSpec(wq.shape, lambda i: (0, 0)),
            pl.BlockSpec((1, MEM_HD), lambda i: (0, 0)),
            pl.BlockSpec((mem_len, MEM_WIDTH), lambda i: (i // per_batch, 0)),
            pl.BlockSpec((mem_len, MEM_WIDTH), lambda i: (i // per_batch, 0)),
            pl.BlockSpec(wo.shape, lambda i: (0, 0)),
        ],
        out_specs=pl.BlockSpec((XA_TM, D_MODEL), lambda i: (i, 0)),
        scratch_shapes=[pltpu.VMEM((XA_TM, MEM_WIDTH), BF16)],
        compiler_params=_params("parallel"),
        name="xattn",
    )(x, gain, wq, q_norm, kx, vx, wo)


def _pe_pair(x1, x2):
    z = jnp.zeros(x1.shape[:-1] + (LANES - MLA_ROPE,), x1.dtype)
    return jnp.concatenate([x1, x2, z], axis=-1), jnp.concatenate([x2, x1, z], axis=-1)


W_IN_TC = 512


def _w_in_prep_kernel(w_ref, o_ref):
    half = MLA_ROPE // 2
    kpe0 = MLA_Q_RANK + MLA_KV_RANK
    hg0 = kpe0 + MLA_ROPE
    zeros = jnp.zeros((LANES - MLA_ROPE, o_ref.shape[1]), BF16)
    o_ref[:kpe0, :] = w_ref[:kpe0, :].astype(BF16)
    o_ref[kpe0:COL_KPE, :] = w_ref[hg0:, :].astype(BF16)
    x1 = w_ref[kpe0:kpe0 + half, :].astype(BF16)
    x2 = w_ref[kpe0 + half:hg0, :].astype(BF16)
    o_ref[COL_KPE:COL_KPE + half, :] = x1
    o_ref[COL_KPE + half:COL_KPE + MLA_ROPE, :] = x2
    o_ref[COL_KPE + MLA_ROPE:COL_KPE_SW, :] = zeros
    o_ref[COL_KPE_SW:COL_KPE_SW + half, :] = x2
    o_ref[COL_KPE_SW + half:COL_KPE_SW + MLA_ROPE, :] = x1
    o_ref[COL_KPE_SW + MLA_ROPE:, :] = zeros


def _prep_w_in(w_in, layer):
    w_t = jnp.swapaxes(w_in, 1, 2)
    _, n, k = w_t.shape
    return pl.pallas_call(
        _w_in_prep_kernel,
        out_shape=jax.ShapeDtypeStruct((IN_COLS_PAD, k), BF16),
        grid=(k // W_IN_TC,),
        in_specs=[pl.BlockSpec((None, n, W_IN_TC), lambda i: (layer, 0, i))],
        out_specs=pl.BlockSpec((IN_COLS_PAD, W_IN_TC), lambda i: (0, i)),
        compiler_params=_params("parallel"),
        name="w_in_prep",
    )(w_t)


def _prep_w_q_up(w):
    w = w.astype(BF16).reshape(MLA_Q_RANK, MLA_HEADS, MLA_QK)
    half = MLA_ROPE // 2
    pe, pe_sw = _pe_pair(w[..., MLA_NOPE:MLA_NOPE + half], w[..., MLA_NOPE + half:])
    return jnp.concatenate([w[..., :MLA_NOPE], pe, pe_sw], axis=-1).reshape(MLA_Q_RANK, MLA_HEADS * Q_UP_PER_HEAD)


def _pe_gains(norm):
    half = MLA_ROPE // 2
    g1 = norm[MLA_NOPE:MLA_NOPE + half]
    g2 = norm[MLA_NOPE + half:]
    ga, gb = _pe_pair(g1, g2)
    sign = jnp.concatenate([-jnp.ones((half,), F32), jnp.ones((LANES - half,), F32)])
    return ga[None, :], (gb * sign)[None, :]


def kernel(x, mem, positions, ffn1_norm, ffn1_w_gate, ffn1_w_up, ffn1_w_down, mix_norm, w_in, mla_q_a_norm, mla_w_q_up, mla_kv_a_norm, mla_w_kv_up, mla_q_norm, mla_k_norm, mla_out_norm, hg_lb_logits, hg_out_norm, w_out, xattn_norm, mem_norm, xattn_w_q, xattn_w_kv, xattn_q_norm, xattn_k_norm, xattn_w_o, ffn2_norm, ffn2_w_gate, ffn2_w_up, ffn2_w_down):
    batch, seq, _ = x.shape
    mem_len = mem.shape[1]
    depth = ffn1_norm.shape[0]
    assert depth == 1 and seq % ATT_T == 0 and seq % XA_TM == 0
    t = batch * seq
    xt = x.reshape(t, D_MODEL)
    half = MLA_ROPE // 2
    quarters = LANES // half
    pos = positions.reshape(t // PREP_TM, quarters, PREP_TM // quarters).transpose(0, 2, 1)
    pos = jnp.repeat(pos, half, axis=-1).reshape(t // quarters, LANES)
    inv_freq = ROPE_BASE ** (-np.arange(half, dtype=np.float32) / half)
    invf = jnp.asarray(np.tile(inv_freq, quarters)[None, :])
    l = 0

    xt = _ffn(xt, ffn1_norm[l][None, :], ffn1_w_gate[l], ffn1_w_up[l], ffn1_w_down[l])

    proj = _in_proj(xt, mix_norm[l][None, :], _prep_w_in(w_in, l))
    qpa, qpb = _pe_gains(mla_q_norm[l])
    kpa, kpb = _pe_gains(mla_k_norm[l])
    q, k, v = _mla_prep(proj, pos, invf, mla_q_a_norm[l][None, :], mla_kv_a_norm[l][None, :],
                        _prep_w_q_up(mla_w_q_up[l]), mla_w_kv_up[l],
                        mla_q_norm[l][None, :MLA_NOPE], qpa, qpb,
                        mla_k_norm[l][None, :MLA_NOPE], kpa, kpb)
    a = _mla_attn(q, k, v, batch, seq)
    r = _hgrn(proj, hg_lb_logits, hg_out_norm[l][None, :], batch, seq)
    xt = _out_proj(a, r, xt, mla_out_norm[l][None, :], w_out[l])

    kx, vx = _mem_kv(mem.reshape(batch * mem_len, D_MODEL), mem_norm[l][None, :],
                     xattn_w_kv[l], xattn_k_norm[l][None, :], batch, mem_len)
    xt = _xattn(xt, xattn_norm[l][None, :], xattn_w_q[l], xattn_q_norm[l][None, :],
                kx, vx, xattn_w_o[l], seq, mem_len)

    xt = _ffn(xt, ffn2_norm[l][None, :], ffn2_w_gate[l], ffn2_w_up[l], ffn2_w_down[l])
    return xt.reshape(batch, seq, D_MODEL)
```

```python
import numpy as np
import jax
import jax.numpy as jnp
from jax import lax
from jax.experimental import pallas as pl
from jax.experimental.pallas import tpu as pltpu

F32 = jnp.float32
BF16 = jnp.bfloat16

EPS = 1e-6
ROPE_BASE = 10000.0
LANES = 128
SUBLANES = 8

D_MODEL = 2048
D_FF = 5504
FFN_TM = 1024
FFN_TF = 512

MLA_HEADS = 8
MLA_NOPE = 128
MLA_ROPE = 64
MLA_QK = MLA_NOPE + MLA_ROPE
MLA_V = 128
MLA_Q_RANK = 512
MLA_KV_RANK = 256
MLA_WIDTH = MLA_HEADS * MLA_V
QK_PAD = 256
Q_UP_PER_HEAD = 384
ATT_Q_SCALE = float(MLA_QK ** -0.5 * np.log2(np.e))

HG_HEADS = 8
HG_D = 128
HG_CHUNK = 64
HG_LEVELS = (32, 16, 8, 4, 2, 1)
HG_UNROLL = 16
HG_GROUP = HG_CHUNK * HG_UNROLL
HG_SLAB = 256
HG_WIDTH = HG_HEADS * HG_D

MEM_HEADS = 4
MEM_HD = 128
MEM_WIDTH = MEM_HEADS * MEM_HD
XATT_Q_SCALE = float(MEM_HD ** -0.5 * np.log2(np.e))

COL_CQ = 0
COL_CKV = MLA_Q_RANK
COL_HQ = MLA_Q_RANK + MLA_KV_RANK
COL_HF = COL_HQ + HG_WIDTH
COL_HI = COL_HF + HG_WIDTH
COL_HG = COL_HI + HG_WIDTH
COL_KPE = COL_HG + HG_WIDTH
COL_KPE_SW = COL_KPE + LANES
IN_COLS_PAD = COL_KPE_SW + LANES

VMEM_LIMIT = 56 * 1024 * 1024
WIDE_VMEM_LIMIT = 61 * 1024 * 1024


def _params(*sem, vmem_limit=VMEM_LIMIT):
    return pltpu.CompilerParams(dimension_semantics=sem, vmem_limit_bytes=vmem_limit)


def _rms_scale(x):
    return lax.rsqrt(jnp.mean(x * x, axis=-1, keepdims=True) + EPS)


def _sigmoid(x):
    return 1.0 / (1.0 + jnp.exp(-x))


def _dot(a, b):
    return jnp.dot(a, b, preferred_element_type=F32)


def _dot_nt(a, b):
    return lax.dot_general(a, b, (((1,), (1,)), ((), ())), preferred_element_type=F32)


def _dot_tn(a, b):
    return lax.dot_general(a, b, (((0,), (0,)), ((), ())), preferred_element_type=F32)


def _ffn_kernel(x_hbm, g_ref, wg_ref, wu_ref, wd_ref, o_ref, h_scr, x_scr, x_sem):
    i = pl.program_id(0)

    def x_copy(tile):
        rows = pl.ds(pl.multiple_of(tile * FFN_TM, FFN_TM), FFN_TM)
        return pltpu.make_async_copy(x_hbm.at[rows, :], x_scr, x_sem)

    def branch(h):
        gate = _dot(h, wg_ref[...].astype(BF16))
        up = _dot(h, wu_ref[...].astype(BF16))
        a = 0.5 * gate * _sigmoid(gate) * up
        f0 = pl.program_id(1) * FFN_TF
        a_ok = f0 + lax.broadcasted_iota(jnp.int32, (1, FFN_TF), 1) < D_FF
        w_ok = f0 + lax.broadcasted_iota(jnp.int32, (FFN_TF, 1), 0) < D_FF
        a = jnp.where(a_ok, a, 0.0).astype(BF16)
        wd = jnp.where(w_ok, wd_ref[...], 0.0).astype(BF16)
        return _dot(a, wd)

    @pl.when(pl.program_id(1) == 0)
    def _():
        @pl.when(i == 0)
        def _():
            x_copy(0).start()

        x_copy(i).wait()
        x = x_scr[...]
        h = (x * _rms_scale(x) * g_ref[...]).astype(BF16)
        h_scr[...] = h
        o_ref[...] = x + branch(h)

        @pl.when(i + 1 < pl.num_programs(0))
        def _():
            x_copy(i + 1).start()

    @pl.when(pl.program_id(1) > 0)
    def _():
        o_ref[...] += branch(h_scr[...])


def _ffn(x, gain, wg, wu, wd):
    t = x.shape[0]
    return pl.pallas_call(
        _ffn_kernel,
        out_shape=jax.ShapeDtypeStruct((t, D_MODEL), F32),
        grid=(t // FFN_TM, pl.cdiv(D_FF, FFN_TF)),
        in_specs=[
            pl.BlockSpec(memory_space=pl.ANY),
            pl.BlockSpec((1, D_MODEL), lambda i, f: (0, 0)),
            pl.BlockSpec((D_MODEL, FFN_TF), lambda i, f: (0, f)),
            pl.BlockSpec((D_MODEL, FFN_TF), lambda i, f: (0, f)),
            pl.BlockSpec((FFN_TF, D_MODEL), lambda i, f: (f, 0)),
        ],
        out_specs=pl.BlockSpec((FFN_TM, D_MODEL), lambda i, f: (i, 0)),
        scratch_shapes=[pltpu.VMEM((FFN_TM, D_MODEL), BF16), pltpu.VMEM((FFN_TM, D_MODEL), F32),
                        pltpu.SemaphoreType.DMA],
        compiler_params=_params("arbitrary", "arbitrary", vmem_limit=WIDE_VMEM_LIMIT),
        name="ffn",
    )(x, gain, wg, wu, wd)


PROJ_TM = 1024
PROJ_TN = 2560


def _in_proj_kernel(x_hbm, g_ref, w_ref, o_ref, h_scr, x_scr, x_sem):
    i = pl.program_id(0)

    def x_copy(tile):
        rows = pl.ds(pl.multiple_of(tile * PROJ_TM, PROJ_TM), PROJ_TM)
        return pltpu.make_async_copy(x_hbm.at[rows, :], x_scr, x_sem)

    @pl.when(pl.program_id(1) == 0)
    def _():
        @pl.when(i == 0)
        def _():
            x_copy(0).start()

        x_copy(i).wait()
        x = x_scr[...]
        h = (x * _rms_scale(x) * g_ref[...]).astype(BF16)
        h_scr[...] = h
        o_ref[...] = _dot_nt(h, w_ref[...])

        @pl.when(i + 1 < pl.num_programs(0))
        def _():
            x_copy(i + 1).start()

    @pl.when(pl.program_id(1) > 0)
    def _():
        o_ref[...] = _dot_nt(h_scr[...], w_ref[...])


def _in_proj(x, gain, w_t):
    t = x.shape[0]
    n = w_t.shape[0]
    return pl.pallas_call(
        _in_proj_kernel,
        out_shape=jax.ShapeDtypeStruct((t, n), F32),
        grid=(t // PROJ_TM, n // PROJ_TN),
        in_specs=[
            pl.BlockSpec(memory_space=pl.ANY),
            pl.BlockSpec((1, D_MODEL), lambda i, j: (0, 0)),
            pl.BlockSpec((PROJ_TN, D_MODEL), lambda i, j: (j, 0)),
        ],
        out_specs=pl.BlockSpec((PROJ_TM, PROJ_TN), lambda i, j: (i, j)),
        scratch_shapes=[pltpu.VMEM((PROJ_TM, D_MODEL), BF16), pltpu.VMEM((PROJ_TM, D_MODEL), F32),
                        pltpu.SemaphoreType.DMA],
        compiler_params=_params("arbitrary", "arbitrary", vmem_limit=WIDE_VMEM_LIMIT),
        name="in_proj",
    )(x, gain, w_t)


PREP_TM = 1024


def _mla_prep_kernel(cq_ref, ckv_ref, kpe_ref, kpesw_ref, pos_ref, invf_ref,
                     qa_ref, kva_ref, wq_ref, wkv_ref,
                     qn_ref, qpa_ref, qpb_ref, kn_ref, kpa_ref, kpb_ref,
                     q_ref, k_ref, vt_ref):
    ang = pos_ref[...].astype(F32) * invf_ref[...]
    cos4 = jnp.cos(ang)
    sin4 = jnp.sin(ang)
    lane = lax.broadcasted_iota(jnp.int32, (1, LANES), 1)
    half = MLA_ROPE // 2

    def expand(t4):
        parts = []
        for part in range(LANES // half):
            t = pltpu.roll(t4, (LANES - half * part) % LANES, 1) if part else t4
            parts.append(jnp.where(lane < half, t, pltpu.roll(t, half, 1)))
        return jnp.concatenate(parts, axis=0)

    cos = expand(cos4)
    sin = expand(sin4)

    q_const = float(np.sqrt(MLA_QK)) * ATT_Q_SCALE
    k_const = float(np.sqrt(MLA_QK))

    cq = cq_ref[...]
    cqn = (cq * _rms_scale(cq) * qa_ref[...]).astype(BF16)
    qall = _dot(cqn, wq_ref[...])
    qn_gain = qn_ref[...] * q_const
    q_cos = (qpa_ref[...] * q_const) * cos
    q_sin = (qpb_ref[...] * q_const) * sin
    for h in range(MLA_HEADS):
        base = h * Q_UP_PER_HEAD
        qn = qall[:, base:base + LANES]
        y = qall[:, base + LANES:base + 2 * LANES]
        ysw = qall[:, base + 2 * LANES:base + 3 * LANES]
        r = lax.rsqrt(jnp.sum(qn * qn + y * y, axis=-1, keepdims=True) + MLA_QK * EPS)
        q_ref[:, h * QK_PAD:h * QK_PAD + LANES] = (qn * r * qn_gain).astype(BF16)
        q_ref[:, h * QK_PAD + LANES:(h + 1) * QK_PAD] = ((y * q_cos + ysw * q_sin) * r).astype(BF16)

    ckv = ckv_ref[...]
    ckvn = (ckv * _rms_scale(ckv) * kva_ref[...]).astype(BF16)
    kvall = _dot(ckvn, wkv_ref[...].astype(BF16))
    yk = kpe_ref[...]
    kn_gain = kn_ref[...] * k_const
    kr = yk * ((kpa_ref[...] * k_const) * cos) + kpesw_ref[...] * ((kpb_ref[...] * k_const) * sin)
    yk_sq = yk * yk
    for h in range(MLA_HEADS):
        kn = kvall[:, h * 2 * LANES:h * 2 * LANES + LANES]
        r = lax.rsqrt(jnp.sum(kn * kn + yk_sq, axis=-1, keepdims=True) + MLA_QK * EPS)
        k_ref[:, h * QK_PAD:h * QK_PAD + LANES] = (kn * r * kn_gain).astype(BF16)
        k_ref[:, h * QK_PAD + LANES:(h + 1) * QK_PAD] = (kr * r).astype(BF16)
        vt_ref[h * MLA_V:(h + 1) * MLA_V, :] = kvall[:, h * 2 * LANES + LANES:(h + 1) * 2 * LANES].T.astype(BF16)


def _mla_prep(proj, pos, invf, qa, kva, wq, wkv, qn, qpa, qpb, kn, kpa, kpb):
    t = proj.shape[0]
    tm = PREP_TM
    row = lambda i: (0, 0)
    vec = lambda n: pl.BlockSpec((1, n), row)
    return pl.pallas_call(
        _mla_prep_kernel,
        out_shape=(jax.ShapeDtypeStruct((t, MLA_HEADS * QK_PAD), BF16),
                   jax.ShapeDtypeStruct((t, MLA_HEADS * QK_PAD), BF16),
                   jax.ShapeDtypeStruct((MLA_WIDTH, t), BF16)),
        grid=(t // tm,),
        in_specs=[
            pl.BlockSpec((tm, MLA_Q_RANK), lambda i: (i, COL_CQ // MLA_Q_RANK)),
            pl.BlockSpec((tm, MLA_KV_RANK), lambda i: (i, COL_CKV // MLA_KV_RANK)),
            pl.BlockSpec((tm, LANES), lambda i: (i, COL_KPE // LANES)),
            pl.BlockSpec((tm, LANES), lambda i: (i, COL_KPE_SW // LANES)),
            pl.BlockSpec((tm // (LANES // (MLA_ROPE // 2)), LANES), lambda i: (i, 0)),
            vec(LANES),
            vec(MLA_Q_RANK), vec(MLA_KV_RANK),
            pl.BlockSpec(wq.shape, row), pl.BlockSpec(wkv.shape, row),
            vec(LANES), vec(LANES), vec(LANES), vec(LANES), vec(LANES), vec(LANES),
        ],
        out_specs=(pl.BlockSpec((tm, MLA_HEADS * QK_PAD), lambda i: (i, 0)),
                   pl.BlockSpec((tm, MLA_HEADS * QK_PAD), lambda i: (i, 0)),
                   pl.BlockSpec((MLA_WIDTH, tm), lambda i: (0, i))),
        compiler_params=_params("parallel"),
        name="mla_prep",
    )(proj, proj, proj, proj, pos, invf, qa, kva, wq, wkv, qn, qpa, qpb, kn, kpa, kpb)


ATT_T = 256
ATT_AHEAD = 5
ATT_HEADS_PER_STEP = 2


def _mla_attn_kernel(q_ref, k_ref, vt_ref, o_ref):
    kpos = lax.broadcasted_iota(jnp.int32, (ATT_T, ATT_T), 0)
    qpos = lax.broadcasted_iota(jnp.int32, (ATT_T, ATT_T), 1)
    nq = q_ref.shape[0] // ATT_T

    def scores(unit):
        h, i = unit
        lo, hi = i * ATT_T, (i + 1) * ATT_T
        cols = slice(h * QK_PAD, (h + 1) * QK_PAD)
        q = q_ref[lo:hi, cols]
        s_diag = jnp.where(kpos <= qpos, _dot_nt(k_ref[lo:hi, cols], q), -jnp.inf)
        s_off = _dot_nt(k_ref[0:lo, cols], q) if i > 0 else None
        return s_diag, s_off

    def finish(unit, s_diag, s_off):
        h, i = unit
        lo, hi = i * ATT_T, (i + 1) * ATT_T
        vt = vt_ref.at[h * MLA_V:(h + 1) * MLA_V, :]
        m = jnp.max(s_diag, axis=0, keepdims=True)
        if i > 0:
            m = jnp.maximum(m, jnp.max(s_off, axis=0, keepdims=True))
        p = jnp.exp2(s_diag - m)
        l = jnp.sum(p, axis=0, keepdims=True)
        acc = _dot(vt[:, lo:hi], p.astype(BF16))
        if i > 0:
            p = jnp.exp2(s_off - m)
            l = l + jnp.sum(p, axis=0, keepdims=True)
            acc = acc + _dot(vt[:, 0:lo], p.astype(BF16))
        o_ref[lo:hi, h * MLA_V:(h + 1) * MLA_V] = (acc / l).T.astype(BF16)

    order = [(h, i) for i in reversed(range(nq)) for h in range(ATT_HEADS_PER_STEP)]
    pending = [scores(u) for u in order[:ATT_AHEAD]]
    for n, unit in enumerate(order):
        if n + ATT_AHEAD < len(order):
            pending.append(scores(order[n + ATT_AHEAD]))
        finish(unit, *pending.pop(0))


def _mla_attn(q, k, vt, batch, seq):
    hs = ATT_HEADS_PER_STEP
    return pl.pallas_call(
        _mla_attn_kernel,
        out_shape=jax.ShapeDtypeStruct((batch * seq, MLA_WIDTH), BF16),
        grid=(batch, MLA_HEADS // hs),
        in_specs=[
            pl.BlockSpec((seq, hs * QK_PAD), lambda b, h: (b, h)),
            pl.BlockSpec((seq, hs * QK_PAD), lambda b, h: (b, h)),
            pl.BlockSpec((hs * MLA_V, seq), lambda b, h: (h, b)),
        ],
        out_specs=pl.BlockSpec((seq, hs * MLA_V), lambda b, h: (b, h)),
        compiler_params=_params("parallel", "parallel"),
        name="mla_attn",
    )(q, k, vt)


def _hgrn_masks():
    t = np.arange(HG_CHUNK)[:, None]
    s = np.arange(HG_CHUNK)[None, :]
    masks = [((t // c) % 2 == 1) & ((s // c) == (t // c) - 1) for c in HG_LEVELS]
    masks.append(t == s)
    return np.stack(masks).astype(np.float32)


def _hgrn_kernel(q_ref, f_ref, i_ref, g_ref, lbl_ref, gn_ref, msk_ref, tri_ref, o_ref, b_all):
    lbl = lbl_ref[...]
    e = jnp.exp(lbl - jnp.max(lbl, axis=0, keepdims=True))
    lb = e[0:1, :] / jnp.sum(e, axis=0, keepdims=True)
    gn = gn_ref[...]
    tri = tri_ref[...]
    sub = lax.broadcasted_iota(jnp.int32, (SUBLANES, LANES), 0)
    zeros8 = jnp.zeros((SUBLANES, LANES), F32)
    per_chunk = HG_CHUNK // SUBLANES
    chunk_rows = [slice(u * HG_CHUNK, (u + 1) * HG_CHUNK) for u in range(HG_UNROLL)]

    def sel(bounds, vals):
        out = vals[-1]
        for bound, val in zip(reversed(bounds), reversed(vals[:-1])):
            out = jnp.where(sub < bound, val, out)
        return out

    def head(gi):
        rows = slice(gi * HG_GROUP, (gi + 1) * HG_GROUP)
        q = q_ref[rows, :]
        v = i_ref[rows, :].astype(BF16)
        f = lb + (1.0 - lb) * _sigmoid(f_ref[rows, :])
        lg = jnp.log2(f)
        k = 1.0 - f

        l1 = lg.astype(BF16)
        l2 = (lg - l1.astype(F32)).astype(BF16)
        slabs = [slice(i, i + HG_SLAB) for i in range(0, HG_GROUP, HG_SLAB)]
        b = jnp.concatenate([_dot(tri, l1[r]) + _dot(tri, l2[r]) for r in slabs], axis=0)
        b_all[gi] = b
        return q, k, f, v, b

    def main(gi, hd, st):
        rows = slice(gi * HG_GROUP, (gi + 1) * HG_GROUP)
        q, k, f, v, b = hd
        b_scr = b_all.at[gi]
        cache = {}

        def rowb(u, r):
            if r < 0:
                return zeros8
            if (u, r) not in cache:
                cache[u, r] = jnp.broadcast_to(b_scr[u * HG_CHUNK + r:u * HG_CHUNK + r + 1, :], (SUBLANES, LANES))
            return cache[u, r]

        def build(fn):
            return jnp.concatenate([fn(u, j) for u in range(HG_UNROLL) for j in range(per_chunk)], axis=0)

        def group_rows(x, u, j):
            r0 = u * HG_CHUNK + j * SUBLANES
            return x[r0:r0 + SUBLANES]

        def level_rows(ci, j):
            if ("m", ci, j) not in cache:
                cache["m", ci, j] = msk_ref[ci, j * SUBLANES:(j + 1) * SUBLANES, :] != 0.0
            return cache["m", ci, j]

        diag = jnp.sum(q * k, axis=-1, keepdims=True)
        a = [[jnp.where(level_rows(len(HG_LEVELS), j), group_rows(diag, u, j), 0.0) for j in range(per_chunk)]
             for u in range(HG_UNROLL)]
        for ci, c in enumerate(HG_LEVELS):
            if c >= SUBLANES:
                m = c // SUBLANES
                right = [j for j in range(per_chunk) if (j // m) % 2 == 1]
                pick = lambda x, js: jnp.concatenate([group_rows(x, u, j) for u in range(HG_UNROLL) for j in js], axis=0)
                bs = jnp.concatenate([rowb(u, SUBLANES * (j // m) * m - 1) for u in range(HG_UNROLL) for j in right], axis=0)
                qd = (pick(q, right) * jnp.exp2(pick(b, right) - bs)).astype(BF16)
                kd = jnp.concatenate(
                    [group_rows(k, u, j) * jnp.exp2(rowb(u, SUBLANES * ((j // m) * m + m) - 1) - group_rows(b, u, j))
                     if (j // m) % 2 == 0 else zeros8
                     for u in range(HG_UNROLL) for j in range(per_chunk)], axis=0).astype(BF16)
                rows_per_chunk = len(right) * SUBLANES
                for u, r in enumerate(chunk_rows):
                    prod = _dot_nt(qd[u * rows_per_chunk:(u + 1) * rows_per_chunk], kd[r])
                    for n, j in enumerate(right):
                        a[u][j] = jnp.where(level_rows(ci, j), prod[n * SUBLANES:(n + 1) * SUBLANES], a[u][j])
                continue
            if c == 4:
                bs = build(lambda u, j: sel((4,), (rowb(u, 8 * j - 1), rowb(u, 8 * j + 3))))
                be = build(lambda u, j: sel((4,), (rowb(u, 8 * j + 3), rowb(u, 8 * j + 7))))
            elif c == 2:
                bs = build(lambda u, j: sel((2, 4, 6), (rowb(u, 8 * j - 1), rowb(u, 8 * j + 1),
                                                         rowb(u, 8 * j + 3), rowb(u, 8 * j + 5))))
                be = build(lambda u, j: sel((2, 4, 6), (rowb(u, 8 * j + 1), rowb(u, 8 * j + 3),
                                                         rowb(u, 8 * j + 5), rowb(u, 8 * j + 7))))
            if c == 1:
                qd = (q * f).astype(BF16)
                kd = k.astype(BF16)
            else:
                qd = (q * jnp.exp2(b - bs)).astype(BF16)
                kd = (k * jnp.exp2(be - b)).astype(BF16)
            for u, r in enumerate(chunk_rows):
                prod = _dot_nt(qd[r], kd[r])
                for j in range(per_chunk):
                    a[u][j] = jnp.where(level_rows(ci, j), prod[j * SUBLANES:(j + 1) * SUBLANES], a[u][j])
        a = [jnp.concatenate(a[u], axis=0) for u in range(HG_UNROLL)]

        q_exp = (q * jnp.exp2(b)).astype(BF16)
        b_last = build(lambda u, j: rowb(u, HG_CHUNK - 1))
        k_dec = (k * jnp.exp2(b_last - b)).astype(BF16)
        o_intra = [_dot(a[u].astype(BF16), v[r]) for u, r in enumerate(chunk_rows)]
        st_add = [_dot_tn(v[r], k_dec[r]) for r in chunk_rows]

        o = []
        for u, r in enumerate(chunk_rows):
            o.append(o_intra[u] + _dot_nt(q_exp[r], st.astype(BF16)))
            st = jnp.exp2(rowb(u, HG_CHUNK - 1)[0:1, :]) * st + st_add[u]
        o = jnp.concatenate(o, axis=0)

        g = g_ref[rows, :]
        o_ref[rows, :] = (o * _rms_scale(o) * gn * (g * _sigmoid(g))).astype(BF16)
        return st

    n_groups = q_ref.shape[0] // HG_GROUP
    st = jnp.zeros((HG_D, HG_D), F32)
    hd = head(0)
    for gi in range(n_groups):
        nxt = head(gi + 1) if gi + 1 < n_groups else None
        st = main(gi, hd, st)
        hd = nxt


def _hgrn(proj, lb_logits, out_norm, batch, seq):
    hblk = lambda col: pl.BlockSpec((seq, HG_D), lambda b, h: (b, col // HG_D + h))
    masks = jnp.asarray(_hgrn_masks())
    tri = np.kron(np.eye(HG_SLAB // HG_CHUNK), np.tril(np.ones((HG_CHUNK, HG_CHUNK))))
    tri = jnp.asarray(tri.astype(np.float32), dtype=BF16)
    nlev = masks.shape[0]
    return pl.pallas_call(
        _hgrn_kernel,
        out_shape=jax.ShapeDtypeStruct((batch * seq, HG_WIDTH), BF16),
        grid=(batch, HG_HEADS),
        in_specs=[
            hblk(COL_HQ), hblk(COL_HF), hblk(COL_HI), hblk(COL_HG),
            pl.BlockSpec((lb_logits.shape[0], HG_D), lambda b, h: (0, h)),
            pl.BlockSpec((1, HG_D), lambda b, h: (0, 0)),
            pl.BlockSpec((nlev, HG_CHUNK, HG_CHUNK), lambda b, h: (0, 0, 0)),
            pl.BlockSpec((HG_SLAB, HG_SLAB), lambda b, h: (0, 0)),
        ],
        out_specs=pl.BlockSpec((seq, HG_D), lambda b, h: (b, h)),
        scratch_shapes=[pltpu.VMEM((seq // HG_GROUP, HG_GROUP, HG_D), F32)],
        compiler_params=_params("parallel", "parallel"),
        name="hgrn2",
    )(proj, proj, proj, proj, lb_logits, out_norm, masks, tri)


OUT_TM = 512


def _out_proj_kernel(a_ref, r_ref, x_ref, g_ref, w_ref, o_ref):
    a = a_ref[...].astype(F32)
    an = (a * _rms_scale(a) * g_ref[...]).astype(BF16)
    o_ref[...] = (x_ref[...] + _dot(an, w_ref[:MLA_WIDTH, :].astype(BF16))
                  + _dot(r_ref[...], w_ref[MLA_WIDTH:, :].astype(BF16)))


def _out_proj(a, r, x, gain, w):
    t = x.shape[0]
    return pl.pallas_call(
        _out_proj_kernel,
        out_shape=jax.ShapeDtypeStruct((t, D_MODEL), F32),
        grid=(t // OUT_TM,),
        in_specs=[
            pl.BlockSpec((OUT_TM, MLA_WIDTH), lambda i: (i, 0)),
            pl.BlockSpec((OUT_TM, HG_WIDTH), lambda i: (i, 0)),
            pl.BlockSpec((OUT_TM, D_MODEL), lambda i: (i, 0)),
            pl.BlockSpec((1, MLA_WIDTH), lambda i: (0, 0)),
            pl.BlockSpec(w.shape, lambda i: (0, 0), pipeline_mode=pl.Buffered(1)),
        ],
        out_specs=pl.BlockSpec((OUT_TM, D_MODEL), lambda i: (i, 0)),
        compiler_params=_params("parallel"),
        name="out_proj",
    )(a, r, x, gain, w)


def _mem_kv_kernel(m_ref, g_ref, w_ref, kn_ref, k_ref, v_ref):
    m = m_ref[...]
    mn = (m * _rms_scale(m) * g_ref[...]).astype(BF16)
    kv = _dot(mn, w_ref[...].astype(BF16))
    for h in range(MEM_HEADS):
        k = kv[:, 2 * h * MEM_HD:(2 * h + 1) * MEM_HD]
        k_ref[:, h * MEM_HD:(h + 1) * MEM_HD] = (k * _rms_scale(k) * kn_ref[...]).astype(BF16)
        v_ref[:, h * MEM_HD:(h + 1) * MEM_HD] = kv[:, (2 * h + 1) * MEM_HD:(2 * h + 2) * MEM_HD].astype(BF16)


def _mem_kv(mem, gain, w, k_norm, batch, mem_len):
    return pl.pallas_call(
        _mem_kv_kernel,
        out_shape=(jax.ShapeDtypeStruct((batch * mem_len, MEM_WIDTH), BF16),
                   jax.ShapeDtypeStruct((batch * mem_len, MEM_WIDTH), BF16)),
        grid=(batch,),
        in_specs=[
            pl.BlockSpec((mem_len, D_MODEL), lambda b: (b, 0)),
            pl.BlockSpec((1, D_MODEL), lambda b: (0, 0)),
            pl.BlockSpec(w.shape, lambda b: (0, 0)),
            pl.BlockSpec((1, MEM_HD), lambda b: (0, 0)),
        ],
        out_specs=(pl.BlockSpec((mem_len, MEM_WIDTH), lambda b: (b, 0)),
                   pl.BlockSpec((mem_len, MEM_WIDTH), lambda b: (b, 0))),
        compiler_params=_params("parallel"),
        name="mem_kv",
    )(mem, gain, w, k_norm)


XA_TM = 1024


def _xattn_kernel(x_ref, g_ref, wq_ref, qn_ref, k_ref, v_ref, wo_ref, o_ref, att_scr):
    x = x_ref[...]
    h = (x * _rms_scale(x) * g_ref[...]).astype(BF16)
    q = _dot(h, wq_ref[...].astype(BF16))
    heads = [slice(hd * MEM_HD, (hd + 1) * MEM_HD) for hd in range(MEM_HEADS)]
    q_gain = qn_ref[...] * (float(np.sqrt(MEM_HD)) * XATT_Q_SCALE)
    scores = []
    for cols in heads:
        qh = q[:, cols]
        r = lax.rsqrt(jnp.sum(qh * qh, axis=-1, keepdims=True) + MEM_HD * EPS)
        scores.append(_dot_nt((qh * r * q_gain).astype(BF16), k_ref[:, cols]))
    for cols, s in zip(heads, scores):
        p = jnp.exp2(s - jnp.max(s, axis=-1, keepdims=True))
        l = jnp.sum(p, axis=-1, keepdims=True)
        att_scr[:, cols] = (_dot(p.astype(BF16), v_ref[:, cols]) / l).astype(BF16)
    o_ref[...] = x + _dot(att_scr[...], wo_ref[...].astype(BF16))


def _xattn(x, gain, wq, q_norm, kx, vx, wo, seq, mem_len):
    t = x.shape[0]
    per_batch = seq // XA_TM
    return pl.pallas_call(
        _xattn_kernel,
        out_shape=jax.ShapeDtypeStruct((t, D_MODEL), F32),
        grid=(t // XA_TM,),
        in_specs=[
            pl.BlockSpec((XA_TM, D_MODEL), lambda i: (i, 0)),
            pl.BlockSpec((1, D_MODEL), lambda i: (0, 0)),
            pl.BlockSpec(wq.shape, lambda i: (0, 0)),
            pl.BlockSpec((1, MEM_HD), lambda i: (0, 0)),
            pl.BlockSpec((mem_len, MEM_WIDTH), lambda i: (i // per_batch, 0)),
            pl.BlockSpec((mem_len, MEM_WIDTH), lambda i: (i // per_batch, 0)),
            pl.BlockSpec(wo.shape, lambda i: (0, 0)),
        ],
        out_specs=pl.BlockSpec((XA_TM, D_MODEL), lambda i: (i, 0)),
        scratch_shapes=[pltpu.VMEM((XA_TM, MEM_WIDTH), BF16)],
        compiler_params=_params("parallel"),
        name="xattn",
    )(x, gain, wq, q_norm, kx, vx, wo)


def _pe_pair(x1, x2):
    z = jnp.zeros(x1.shape[:-1] + (LANES - MLA_ROPE,), x1.dtype)
    return jnp.concatenate([x1, x2, z], axis=-1), jnp.concatenate([x2, x1, z], axis=-1)


W_IN_TC = 512


def _w_in_prep_kernel(w_ref, o_ref):
    half = MLA_ROPE // 2
    kpe0 = MLA_Q_RANK + MLA_KV_RANK
    hg0 = kpe0 + MLA_ROPE
    zeros = jnp.zeros((LANES - MLA_ROPE, o_ref.shape[1]), BF16)
    o_ref[:kpe0, :] = w_ref[:kpe0, :].astype(BF16)
    o_ref[kpe0:COL_KPE, :] = w_ref[hg0:, :].astype(BF16)
    x1 = w_ref[kpe0:kpe0 + half, :].astype(BF16)
    x2 = w_ref[kpe0 + half:hg0, :].astype(BF16)
    o_ref[COL_KPE:COL_KPE + half, :] = x1
    o_ref[COL_KPE + half:COL_KPE + MLA_ROPE, :] = x2
    o_ref[COL_KPE + MLA_ROPE:COL_KPE_SW, :] = zeros
    o_ref[COL_KPE_SW:COL_KPE_SW + half, :] = x2
    o_ref[COL_KPE_SW + half:COL_KPE_SW + MLA_ROPE, :] = x1
    o_ref[COL_KPE_SW + MLA_ROPE:, :] = zeros


def _prep_w_in(w_in, layer):
    w_t = jnp.swapaxes(w_in, 1, 2)
    _, n, k = w_t.shape
    return pl.pallas_call(
        _w_in_prep_kernel,
        out_shape=jax.ShapeDtypeStruct((IN_COLS_PAD, k), BF16),
        grid=(k // W_IN_TC,),
        in_specs=[pl.BlockSpec((None, n, W_IN_TC), lambda i: (layer, 0, i))],
        out_specs=pl.BlockSpec((IN_COLS_PAD, W_IN_TC), lambda i: (0, i)),
        compiler_params=_params("parallel"),
        name="w_in_prep",
    )(w_t)


def _prep_w_q_up(w):
    w = w.astype(BF16).reshape(MLA_Q_RANK, MLA_HEADS, MLA_QK)
    half = MLA_ROPE // 2
    pe, pe_sw = _pe_pair(w[..., MLA_NOPE:MLA_NOPE + half], w[..., MLA_NOPE + half:])
    return jnp.concatenate([w[..., :MLA_NOPE], pe, pe_sw], axis=-1).reshape(MLA_Q_RANK, MLA_HEADS * Q_UP_PER_HEAD)


def _pe_gains(norm):
    half = MLA_ROPE // 2
    g1 = norm[MLA_NOPE:MLA_NOPE + half]
    g2 = norm[MLA_NOPE + half:]
    ga, gb = _pe_pair(g1, g2)
    sign = jnp.concatenate([-jnp.ones((half,), F32), jnp.ones((LANES - half,), F32)])
    return ga[None, :], (gb * sign)[None, :]


def kernel(x, mem, positions, ffn1_norm, ffn1_w_gate, ffn1_w_up, ffn1_w_down, mix_norm, w_in, mla_q_a_norm, mla_w_q_up, mla_kv_a_norm, mla_w_kv_up, mla_q_norm, mla_k_norm, mla_out_norm, hg_lb_logits, hg_out_norm, w_out, xattn_norm, mem_norm, xattn_w_q, xattn_w_kv, xattn_q_norm, xattn_k_norm, xattn_w_o, ffn2_norm, ffn2_w_gate, ffn2_w_up, ffn2_w_down):
    batch, seq, _ = x.shape
    mem_len = mem.shape[1]
    depth = ffn1_norm.shape[0]
    assert depth == 1 and seq % ATT_T == 0 and seq % XA_TM == 0
    t = batch * seq
    xt = x.reshape(t, D_MODEL)
    half = MLA_ROPE // 2
    quarters = LANES // half
    pos = positions.reshape(t // PREP_TM, quarters, PREP_TM // quarters).transpose(0, 2, 1)
    pos = jnp.repeat(pos, half, axis=-1).reshape(t // quarters, LANES)
    inv_freq = ROPE_BASE ** (-np.arange(half, dtype=np.float32) / half)
    invf = jnp.asarray(np.tile(inv_freq, quarters)[None, :])
    l = 0

    xt = _ffn(xt, ffn1_norm[l][None, :], ffn1_w_gate[l], ffn1_w_up[l], ffn1_w_down[l])

    proj = _in_proj(xt, mix_norm[l][None, :], _prep_w_in(w_in, l))
    qpa, qpb = _pe_gains(mla_q_norm[l])
    kpa, kpb = _pe_gains(mla_k_norm[l])
    q, k, v = _mla_prep(proj, pos, invf, mla_q_a_norm[l][None, :], mla_kv_a_norm[l][None, :],
                        _prep_w_q_up(mla_w_q_up[l]), mla_w_kv_up[l],
                        mla_q_norm[l][None, :MLA_NOPE], qpa, qpb,
                        mla_k_norm[l][None, :MLA_NOPE], kpa, kpb)
    a = _mla_attn(q, k, v, batch, seq)
    r = _hgrn(proj, hg_lb_logits, hg_out_norm[l][None, :], batch, seq)
    xt = _out_proj(a, r, xt, mla_out_norm[l][None, :], w_out[l])

    kx, vx = _mem_kv(mem.reshape(batch * mem_len, D_MODEL), mem_norm[l][None, :],
                     xattn_w_kv[l], xattn_k_norm[l][None, :], batch, mem_len)
    xt = _xattn(xt, xattn_norm[l][None, :], xattn_w_q[l], xattn_q_norm[l][None, :],
                kx, vx, xattn_w_o[l], seq, mem_len)

    xt = _ffn(xt, ffn2_norm[l][None, :], ffn2_w_gate[l], ffn2_w_up[l], ffn2_w_down[l])
    return xt.reshape(batch, seq, D_MODEL)
```

```python
import numpy as np
import jax
import jax.numpy as jnp
from jax import lax
from jax.experimental import pallas as pl
from jax.experimental.pallas import tpu as pltpu

F32 = jnp.float32
BF16 = jnp.bfloat16

EPS = 1e-6
ROPE_BASE = 10000.0
LANES = 128
SUBLANES = 8

D_MODEL = 2048
D_FF = 5504
FFN_TM = 1024
FFN_TF = 512

MLA_HEADS = 8
MLA_NOPE = 128
MLA_ROPE = 64
MLA_QK = MLA_NOPE + MLA_ROPE
MLA_V = 128
MLA_Q_RANK = 512
MLA_KV_RANK = 256
MLA_WIDTH = MLA_HEADS * MLA_V
QK_PAD = 256
Q_UP_PER_HEAD = 384
ATT_Q_SCALE = float(MLA_QK ** -0.5 * np.log2(np.e))

HG_HEADS = 8
HG_D = 128
HG_CHUNK = 64
HG_LEVELS = (32, 16, 8, 4, 2, 1)
HG_UNROLL = 16
HG_GROUP = HG_CHUNK * HG_UNROLL
HG_SLAB = 256
HG_WIDTH = HG_HEADS * HG_D

MEM_HEADS = 4
MEM_HD = 128
MEM_WIDTH = MEM_HEADS * MEM_HD
XATT_Q_SCALE = float(MEM_HD ** -0.5 * np.log2(np.e))

COL_CQ = 0
COL_CKV = MLA_Q_RANK
COL_HQ = MLA_Q_RANK + MLA_KV_RANK
COL_HF = COL_HQ + HG_WIDTH
COL_HI = COL_HF + HG_WIDTH
COL_HG = COL_HI + HG_WIDTH
COL_KPE = COL_HG + HG_WIDTH
COL_KPE_SW = COL_KPE + LANES
IN_COLS_PAD = COL_KPE_SW + LANES

VMEM_LIMIT = 56 * 1024 * 1024
WIDE_VMEM_LIMIT = 61 * 1024 * 1024


def _params(*sem, vmem_limit=VMEM_LIMIT):
    return pltpu.CompilerParams(dimension_semantics=sem, vmem_limit_bytes=vmem_limit)


def _rms_scale(x):
    return lax.rsqrt(jnp.mean(x * x, axis=-1, keepdims=True) + EPS)


def _sigmoid(x):
    return 1.0 / (1.0 + jnp.exp(-x))


def _dot(a, b):
    return jnp.dot(a, b, preferred_element_type=F32)


def _dot_nt(a, b):
    return lax.dot_general(a, b, (((1,), (1,)), ((), ())), preferred_element_type=F32)


def _dot_tn(a, b):
    return lax.dot_general(a, b, (((0,), (0,)), ((), ())), preferred_element_type=F32)


def _ffn_kernel(x_hbm, g_ref, wg_ref, wu_ref, wd_ref, o_ref, h_scr, x_scr, x_sem):
    i = pl.program_id(0)

    def x_copy(tile):
        rows = pl.ds(pl.multiple_of(tile * FFN_TM, FFN_TM), FFN_TM)
        return pltpu.make_async_copy(x_hbm.at[rows, :], x_scr, x_sem)

    def branch(h):
        gate = _dot(h, wg_ref[...].astype(BF16))
        up = _dot(h, wu_ref[...].astype(BF16))
        a = 0.5 * gate * _sigmoid(gate) * up
        f0 = pl.program_id(1) * FFN_TF
        a_ok = f0 + lax.broadcasted_iota(jnp.int32, (1, FFN_TF), 1) < D_FF
        w_ok = f0 + lax.broadcasted_iota(jnp.int32, (FFN_TF, 1), 0) < D_FF
        a = jnp.where(a_ok, a, 0.0).astype(BF16)
        wd = jnp.where(w_ok, wd_ref[...], 0.0).astype(BF16)
        return _dot(a, wd)

    @pl.when(pl.program_id(1) == 0)
    def _():
        @pl.when(i == 0)
        def _():
            x_copy(0).start()

        x_copy(i).wait()
        x = x_scr[...]
        h = (x * _rms_scale(x) * g_ref[...]).astype(BF16)
        h_scr[...] = h
        o_ref[...] = x + branch(h)

        @pl.when(i + 1 < pl.num_programs(0))
        def _():
            x_copy(i + 1).start()

    @pl.when(pl.program_id(1) > 0)
    def _():
        o_ref[...] += branch(h_scr[...])


def _ffn(x, gain, wg, wu, wd):
    t = x.shape[0]
    return pl.pallas_call(
        _ffn_kernel,
        out_shape=jax.ShapeDtypeStruct((t, D_MODEL), F32),
        grid=(t // FFN_TM, pl.cdiv(D_FF, FFN_TF)),
        in_specs=[
            pl.BlockSpec(memory_space=pl.ANY),
            pl.BlockSpec((1, D_MODEL), lambda i, f: (0, 0)),
            pl.BlockSpec((D_MODEL, FFN_TF), lambda i, f: (0, f)),
            pl.BlockSpec((D_MODEL, FFN_TF), lambda i, f: (0, f)),
            pl.BlockSpec((FFN_TF, D_MODEL), lambda i, f: (f, 0)),
        ],
        out_specs=pl.BlockSpec((FFN_TM, D_MODEL), lambda i, f: (i, 0)),
        scratch_shapes=[pltpu.VMEM((FFN_TM, D_MODEL), BF16), pltpu.VMEM((FFN_TM, D_MODEL), F32),
                        pltpu.SemaphoreType.DMA],
        compiler_params=_params("arbitrary", "arbitrary", vmem_limit=WIDE_VMEM_LIMIT),
        name="ffn",
    )(x, gain, wg, wu, wd)


PROJ_TM = 1024
PROJ_TN = 2560


def _in_proj_kernel(x_hbm, g_ref, w_ref, o_ref, h_scr, x_scr, x_sem):
    i = pl.program_id(0)

    def x_copy(tile):
        rows = pl.ds(pl.multiple_of(tile * PROJ_TM, PROJ_TM), PROJ_TM)
        return pltpu.make_async_copy(x_hbm.at[rows, :], x_scr, x_sem)

    @pl.when(pl.program_id(1) == 0)
    def _():
        @pl.when(i == 0)
        def _():
            x_copy(0).start()

        x_copy(i).wait()
        x = x_scr[...]
        h = (x * _rms_scale(x) * g_ref[...]).astype(BF16)
        h_scr[...] = h
        o_ref[...] = _dot_nt(h, w_ref[...])

        @pl.when(i + 1 < pl.num_programs(0))
        def _():
            x_copy(i + 1).start()

    @pl.when(pl.program_id(1) > 0)
    def _():
        o_ref[...] = _dot_nt(h_scr[...], w_ref[...])


def _in_proj(x, gain, w_t):
    t = x.shape[0]
    n = w_t.shape[0]
    return pl.pallas_call(
        _in_proj_kernel,
        out_shape=jax.ShapeDtypeStruct((t, n), F32),
        grid=(t // PROJ_TM, n // PROJ_TN),
        in_specs=[
            pl.BlockSpec(memory_space=pl.ANY),
            pl.BlockSpec((1, D_MODEL), lambda i, j: (0, 0)),
            pl.BlockSpec((PROJ_TN, D_MODEL), lambda i, j: (j, 0)),
        ],
        out_specs=pl.BlockSpec((PROJ_TM, PROJ_TN), lambda i, j: (i, j)),
        scratch_shapes=[pltpu.VMEM((PROJ_TM, D_MODEL), BF16), pltpu.VMEM((PROJ_TM, D_MODEL), F32),
                        pltpu.SemaphoreType.DMA],
        compiler_params=_params("arbitrary", "arbitrary", vmem_limit=WIDE_VMEM_LIMIT),
        name="in_proj",
    )(x, gain, w_t)


PREP_TM = 1024


def _mla_prep_kernel(cq_ref, ckv_ref, kpe_ref, kpesw_ref, pos_ref, invf_ref,
                     qa_ref, kva_ref, wq_ref, wkv_ref,
                     qn_ref, qpa_ref, qpb_ref, kn_ref, kpa_ref, kpb_ref,
                     q_ref, k_ref, vt_ref):
    ang = pos_ref[...].astype(F32) * invf_ref[...]
    cos4 = jnp.cos(ang)
    sin4 = jnp.sin(ang)
    lane = lax.broadcasted_iota(jnp.int32, (1, LANES), 1)
    half = MLA_ROPE // 2

    def expand(t4):
        parts = []
        for part in range(LANES // half):
            t = pltpu.roll(t4, (LANES - half * part) % LANES, 1) if part else t4
            parts.append(jnp.where(lane < half, t, pltpu.roll(t, half, 1)))
        return jnp.concatenate(parts, axis=0)

    cos = expand(cos4)
    sin = expand(sin4)

    q_const = float(np.sqrt(MLA_QK)) * ATT_Q_SCALE
    k_const = float(np.sqrt(MLA_QK))

    cq = cq_ref[...]
    cqn = (cq * _rms_scale(cq) * qa_ref[...]).astype(BF16)
    qall = _dot(cqn, wq_ref[...])
    qn_gain = qn_ref[...] * q_const
    q_cos = (qpa_ref[...] * q_const) * cos
    q_sin = (qpb_ref[...] * q_const) * sin
    for h in range(MLA_HEADS):
        base = h * Q_UP_PER_HEAD
        qn = qall[:, base:base + LANES]
        y = qall[:, base + LANES:base + 2 * LANES]
        ysw = qall[:, base + 2 * LANES:base + 3 * LANES]
        r = lax.rsqrt(jnp.sum(qn * qn + y * y, axis=-1, keepdims=True) + MLA_QK * EPS)
        q_ref[:, h * QK_PAD:h * QK_PAD + LANES] = (qn * r * qn_gain).astype(BF16)
        q_ref[:, h * QK_PAD + LANES:(h + 1) * QK_PAD] = ((y * q_cos + ysw * q_sin) * r).astype(BF16)

    ckv = ckv_ref[...]
    ckvn = (ckv * _rms_scale(ckv) * kva_ref[...]).astype(BF16)
    kvall = _dot(ckvn, wkv_ref[...].astype(BF16))
    yk = kpe_ref[...]
    kn_gain = kn_ref[...] * k_const
    kr = yk * ((kpa_ref[...] * k_const) * cos) + kpesw_ref[...] * ((kpb_ref[...] * k_const) * sin)
    yk_sq = yk * yk
    for h in range(MLA_HEADS):
        kn = kvall[:, h * 2 * LANES:h * 2 * LANES + LANES]
        r = lax.rsqrt(jnp.sum(kn * kn + yk_sq, axis=-1, keepdims=True) + MLA_QK * EPS)
        k_ref[:, h * QK_PAD:h * QK_PAD + LANES] = (kn * r * kn_gain).astype(BF16)
        k_ref[:, h * QK_PAD + LANES:(h + 1) * QK_PAD] = (kr * r).astype(BF16)
        vt_ref[h * MLA_V:(h + 1) * MLA_V, :] = kvall[:, h * 2 * LANES + LANES:(h + 1) * 2 * LANES].T.astype(BF16)


def _mla_prep(proj, pos, invf, qa, kva, wq, wkv, qn, qpa, qpb, kn, kpa, kpb):
    t = proj.shape[0]
    tm = PREP_TM
    row = lambda i: (0, 0)
    vec = lambda n: pl.BlockSpec((1, n), row)
    return pl.pallas_call(
        _mla_prep_kernel,
        out_shape=(jax.ShapeDtypeStruct((t, MLA_HEADS * QK_PAD), BF16),
                   jax.ShapeDtypeStruct((t, MLA_HEADS * QK_PAD), BF16),
                   jax.ShapeDtypeStruct((MLA_WIDTH, t), BF16)),
        grid=(t // tm,),
        in_specs=[
            pl.BlockSpec((tm, MLA_Q_RANK), lambda i: (i, COL_CQ // MLA_Q_RANK)),
            pl.BlockSpec((tm, MLA_KV_RANK), lambda i: (i, COL_CKV // MLA_KV_RANK)),
            pl.BlockSpec((tm, LANES), lambda i: (i, COL_KPE // LANES)),
            pl.BlockSpec((tm, LANES), lambda i: (i, COL_KPE_SW // LANES)),
            pl.BlockSpec((tm // (LANES // (MLA_ROPE // 2)), LANES), lambda i: (i, 0)),
            vec(LANES),
            vec(MLA_Q_RANK), vec(MLA_KV_RANK),
            pl.BlockSpec(wq.shape, row), pl.BlockSpec(wkv.shape, row),
            vec(LANES), vec(LANES), vec(LANES), vec(LANES), vec(LANES), vec(LANES),
        ],
        out_specs=(pl.BlockSpec((tm, MLA_HEADS * QK_PAD), lambda i: (i, 0)),
                   pl.BlockSpec((tm, MLA_HEADS * QK_PAD), lambda i: (i, 0)),
                   pl.BlockSpec((MLA_WIDTH, tm), lambda i: (0, i))),
        compiler_params=_params("parallel"),
        name="mla_prep",
    )(proj, proj, proj, proj, pos, invf, qa, kva, wq, wkv, qn, qpa, qpb, kn, kpa, kpb)


ATT_T = 256
ATT_AHEAD = 5
ATT_HEADS_PER_STEP = 2


def _mla_attn_kernel(q_ref, k_ref, vt_ref, o_ref):
    kpos = lax.broadcasted_iota(jnp.int32, (ATT_T, ATT_T), 0)
    qpos = lax.broadcasted_iota(jnp.int32, (ATT_T, ATT_T), 1)
    nq = q_ref.shape[0] // ATT_T

    def scores(unit):
        h, i = unit
        lo, hi = i * ATT_T, (i + 1) * ATT_T
        cols = slice(h * QK_PAD, (h + 1) * QK_PAD)
        q = q_ref[lo:hi, cols]
        s_diag = jnp.where(kpos <= qpos, _dot_nt(k_ref[lo:hi, cols], q), -jnp.inf)
        s_off = _dot_nt(k_ref[0:lo, cols], q) if i > 0 else None
        return s_diag, s_off

    def finish(unit, s_diag, s_off):
        h, i = unit
        lo, hi = i * ATT_T, (i + 1) * ATT_T
        vt = vt_ref.at[h * MLA_V:(h + 1) * MLA_V, :]
        m = jnp.max(s_diag, axis=0, keepdims=True)
        if i > 0:
            m = jnp.maximum(m, jnp.max(s_off, axis=0, keepdims=True))
        p = jnp.exp2(s_diag - m)
        l = jnp.sum(p, axis=0, keepdims=True)
        acc = _dot(vt[:, lo:hi], p.astype(BF16))
        if i > 0:
            p = jnp.exp2(s_off - m)
            l = l + jnp.sum(p, axis=0, keepdims=True)
            acc = acc + _dot(vt[:, 0:lo], p.astype(BF16))
        o_ref[lo:hi, h * MLA_V:(h + 1) * MLA_V] = (acc / l).T.astype(BF16)

    order = [(h, i) for i in reversed(range(nq)) for h in range(ATT_HEADS_PER_STEP)]
    pending = [scores(u) for u in order[:ATT_AHEAD]]
    for n, unit in enumerate(order):
        if n + ATT_AHEAD < len(order):
            pending.append(scores(order[n + ATT_AHEAD]))
        finish(unit, *pending.pop(0))


def _mla_attn(q, k, vt, batch, seq):
    hs = ATT_HEADS_PER_STEP
    return pl.pallas_call(
        _mla_attn_kernel,
        out_shape=jax.ShapeDtypeStruct((batch * seq, MLA_WIDTH), BF16),
        grid=(batch, MLA_HEADS // hs),
        in_specs=[
            pl.BlockSpec((seq, hs * QK_PAD), lambda b, h: (b, h)),
            pl.BlockSpec((seq, hs * QK_PAD), lambda b, h: (b, h)),
            pl.BlockSpec((hs * MLA_V, seq), lambda b, h: (h, b)),
        ],
        out_specs=pl.BlockSpec((seq, hs * MLA_V), lambda b, h: (b, h)),
        compiler_params=_params("parallel", "parallel"),
        name="mla_attn",
    )(q, k, vt)


def _hgrn_masks():
    t = np.arange(HG_CHUNK)[:, None]
    s = np.arange(HG_CHUNK)[None, :]
    masks = [((t // c) % 2 == 1) & ((s // c) == (t // c) - 1) for c in HG_LEVELS]
    masks.append(t == s)
    return np.stack(masks).astype(np.float32)


def _hgrn_kernel(q_ref, f_ref, i_ref, g_ref, lbl_ref, gn_ref, msk_ref, tri_ref, o_ref, b_all):
    lbl = lbl_ref[...]
    e = jnp.exp(lbl - jnp.max(lbl, axis=0, keepdims=True))
    lb = e[0:1, :] / jnp.sum(e, axis=0, keepdims=True)
    gn = gn_ref[...]
    tri = tri_ref[...]
    sub = lax.broadcasted_iota(jnp.int32, (SUBLANES, LANES), 0)
    zeros8 = jnp.zeros((SUBLANES, LANES), F32)
    per_chunk = HG_CHUNK // SUBLANES
    chunk_rows = [slice(u * HG_CHUNK, (u + 1) * HG_CHUNK) for u in range(HG_UNROLL)]

    def sel(bounds, vals):
        out = vals[-1]
        for bound, val in zip(reversed(bounds), reversed(vals[:-1])):
            out = jnp.where(sub < bound, val, out)
        return out

    def head(gi):
        rows = slice(gi * HG_GROUP, (gi + 1) * HG_GROUP)
        q = q_ref[rows, :]
        v = i_ref[rows, :].astype(BF16)
        f = lb + (1.0 - lb) * _sigmoid(f_ref[rows, :])
        lg = jnp.log2(f)
        k = 1.0 - f

        l1 = lg.astype(BF16)
        l2 = (lg - l1.astype(F32)).astype(BF16)
        slabs = [slice(i, i + HG_SLAB) for i in range(0, HG_GROUP, HG_SLAB)]
        b = jnp.concatenate([_dot(tri, l1[r]) + _dot(tri, l2[r]) for r in slabs], axis=0)
        b_all[gi] = b
        return q, k, f, v, b

    def main(gi, hd, st):
        rows = slice(gi * HG_GROUP, (gi + 1) * HG_GROUP)
        q, k, f, v, b = hd
        b_scr = b_all.at[gi]
        cache = {}

        def rowb(u, r):
            if r < 0:
                return zeros8
            if (u, r) not in cache:
                cache[u, r] = jnp.broadcast_to(b_scr[u * HG_CHUNK + r:u * HG_CHUNK + r + 1, :], (SUBLANES, LANES))
            return cache[u, r]

        def build(fn):
            return jnp.concatenate([fn(u, j) for u in range(HG_UNROLL) for j in range(per_chunk)], axis=0)

        def group_rows(x, u, j):
            r0 = u * HG_CHUNK + j * SUBLANES
            return x[r0:r0 + SUBLANES]

        def level_rows(ci, j):
            if ("m", ci, j) not in cache:
                cache["m", ci, j] = msk_ref[ci, j * SUBLANES:(j + 1) * SUBLANES, :] != 0.0
            return cache["m", ci, j]

        diag = jnp.sum(q * k, axis=-1, keepdims=True)
        a = [[jnp.where(level_rows(len(HG_LEVELS), j), group_rows(diag, u, j), 0.0) for j in range(per_chunk)]
             for u in range(HG_UNROLL)]
        for ci, c in enumerate(HG_LEVELS):
            if c >= SUBLANES:
                m = c // SUBLANES
                right = [j for j in range(per_chunk) if (j // m) % 2 == 1]
                pick = lambda x, js: jnp.concatenate([group_rows(x, u, j) for u in range(HG_UNROLL) for j in js], axis=0)
                bs = jnp.concatenate([rowb(u, SUBLANES * (j // m) * m - 1) for u in range(HG_UNROLL) for j in right], axis=0)
                qd = (pick(q, right) * jnp.exp2(pick(b, right) - bs)).astype(BF16)
                kd = jnp.concatenate(
                    [group_rows(k, u, j) * jnp.exp2(rowb(u, SUBLANES * ((j // m) * m + m) - 1) - group_rows(b, u, j))
                     if (j // m) % 2 == 0 else zeros8
                     for u in range(HG_UNROLL) for j in range(per_chunk)], axis=0).astype(BF16)
                rows_per_chunk = len(right) * SUBLANES
                for u, r in enumerate(chunk_rows):
                    prod = _dot_nt(qd[u * rows_per_chunk:(u + 1) * rows_per_chunk], kd[r])
                    for n, j in enumerate(right):
                        a[u][j] = jnp.where(level_rows(ci, j), prod[n * SUBLANES:(n + 1) * SUBLANES], a[u][j])
                continue
            if c == 4:
                bs = build(lambda u, j: sel((4,), (rowb(u, 8 * j - 1), rowb(u, 8 * j + 3))))
                be = build(lambda u, j: sel((4,), (rowb(u, 8 * j + 3), rowb(u, 8 * j + 7))))
            elif c == 2:
                bs = build(lambda u, j: sel((2, 4, 6), (rowb(u, 8 * j - 1), rowb(u, 8 * j + 1),
                                                         rowb(u, 8 * j + 3), rowb(u, 8 * j + 5))))
                be = build(lambda u, j: sel((2, 4, 6), (rowb(u, 8 * j + 1), rowb(u, 8 * j + 3),
                                                         rowb(u, 8 * j + 5), rowb(u, 8 * j + 7))))
            if c == 1:
                qd = (q * f).astype(BF16)
                kd = k.astype(BF16)
            else:
                qd = (q * jnp.exp2(b - bs)).astype(BF16)
                kd = (k * jnp.exp2(be - b)).astype(BF16)
            for u, r in enumerate(chunk_rows):
                prod = _dot_nt(qd[r], kd[r])
                for j in range(per_chunk):
                    a[u][j] = jnp.where(level_rows(ci, j), prod[j * SUBLANES:(j + 1) * SUBLANES], a[u][j])
        a = [jnp.concatenate(a[u], axis=0) for u in range(HG_UNROLL)]

        q_exp = (q * jnp.exp2(b)).astype(BF16)
        b_last = build(lambda u, j: rowb(u, HG_CHUNK - 1))
        k_dec = (k * jnp.exp2(b_last - b)).astype(BF16)
        o_intra = [_dot(a[u].astype(BF16), v[r]) for u, r in enumerate(chunk_rows)]
        st_add = [_dot_tn(v[r], k_dec[r]) for r in chunk_rows]

        o = []
        for u, r in enumerate(chunk_rows):
            o.append(o_intra[u] + _dot_nt(q_exp[r], st.astype(BF16)))
            st = jnp.exp2(rowb(u, HG_CHUNK - 1)[0:1, :]) * st + st_add[u]
        o = jnp.concatenate(o, axis=0)

        g = g_ref[rows, :]
        o_ref[rows, :] = (o * _rms_scale(o) * gn * (g * _sigmoid(g))).astype(BF16)
        return st

    n_groups = q_ref.shape[0] // HG_GROUP
    st = jnp.zeros((HG_D, HG_D), F32)
    hd = head(0)
    for gi in range(n_groups):
        nxt = head(gi + 1) if gi + 1 < n_groups else None
        st = main(gi, hd, st)
        hd = nxt


def _hgrn(proj, lb_logits, out_norm, batch, seq):
    hblk = lambda col: pl.BlockSpec((seq, HG_D), lambda b, h: (b, col // HG_D + h))
    masks = jnp.asarray(_hgrn_masks())
    tri = np.kron(np.eye(HG_SLAB // HG_CHUNK), np.tril(np.ones((HG_CHUNK, HG_CHUNK))))
    tri = jnp.asarray(tri.astype(np.float32), dtype=BF16)
    nlev = masks.shape[0]
    return pl.pallas_call(
        _hgrn_kernel,
        out_shape=jax.ShapeDtypeStruct((batch * seq, HG_WIDTH), BF16),
        grid=(batch, HG_HEADS),
        in_specs=[
            hblk(COL_HQ), hblk(COL_HF), hblk(COL_HI), hblk(COL_HG),
            pl.BlockSpec((lb_logits.shape[0], HG_D), lambda b, h: (0, h)),
            pl.BlockSpec((1, HG_D), lambda b, h: (0, 0)),
            pl.BlockSpec((nlev, HG_CHUNK, HG_CHUNK), lambda b, h: (0, 0, 0)),
            pl.BlockSpec((HG_SLAB, HG_SLAB), lambda b, h: (0, 0)),
        ],
        out_specs=pl.BlockSpec((seq, HG_D), lambda b, h: (b, h)),
        scratch_shapes=[pltpu.VMEM((seq // HG_GROUP, HG_GROUP, HG_D), F32)],
        compiler_params=_params("parallel", "parallel"),
        name="hgrn2",
    )(proj, proj, proj, proj, lb_logits, out_norm, masks, tri)


def _mem_kv_kernel(m_ref, g_ref, w_ref, kn_ref, k_ref, v_ref):
    m = m_ref[...]
    mn = (m * _rms_scale(m) * g_ref[...]).astype(BF16)
    kv = _dot(mn, w_ref[...].astype(BF16))
    for h in range(MEM_HEADS):
        k = kv[:, 2 * h * MEM_HD:(2 * h + 1) * MEM_HD]
        k_ref[:, h * MEM_HD:(h + 1) * MEM_HD] = (k * _rms_scale(k) * kn_ref[...]).astype(BF16)
        v_ref[:, h * MEM_HD:(h + 1) * MEM_HD] = kv[:, (2 * h + 1) * MEM_HD:(2 * h + 2) * MEM_HD].astype(BF16)


def _mem_kv(mem, gain, w, k_norm, batch, mem_len):
    return pl.pallas_call(
        _mem_kv_kernel,
        out_shape=(jax.ShapeDtypeStruct((batch * mem_len, MEM_WIDTH), BF16),
                   jax.ShapeDtypeStruct((batch * mem_len, MEM_WIDTH), BF16)),
        grid=(batch,),
        in_specs=[
            pl.BlockSpec((mem_len, D_MODEL), lambda b: (b, 0)),
            pl.BlockSpec((1, D_MODEL), lambda b: (0, 0)),
            pl.BlockSpec(w.shape, lambda b: (0, 0)),
            pl.BlockSpec((1, MEM_HD), lambda b: (0, 0)),
        ],
        out_specs=(pl.BlockSpec((mem_len, MEM_WIDTH), lambda b: (b, 0)),
                   pl.BlockSpec((mem_len, MEM_WIDTH), lambda b: (b, 0))),
        compiler_params=_params("parallel"),
        name="mem_kv",
    )(mem, gain, w, k_norm)


XA_TM = 512


def _out_xattn_kernel(a_ref, r_ref, x_ref, og_ref, wout_ref, g_ref, wq_ref, qn_ref, k_ref, v_ref, wo_ref,
                      o_ref, att_scr):
    a = a_ref[...].astype(F32)
    an = (a * _rms_scale(a) * og_ref[...]).astype(BF16)
    x = (x_ref[...] + _dot(an, wout_ref[:MLA_WIDTH, :].astype(BF16))
         + _dot(r_ref[...], wout_ref[MLA_WIDTH:, :].astype(BF16)))

    h = (x * _rms_scale(x) * g_ref[...]).astype(BF16)
    q = _dot(h, wq_ref[...].astype(BF16))
    heads = [slice(hd * MEM_HD, (hd + 1) * MEM_HD) for hd in range(MEM_HEADS)]
    q_gain = qn_ref[...] * (float(np.sqrt(MEM_HD)) * XATT_Q_SCALE)
    scores = []
    for cols in heads:
        qh = q[:, cols]
        r = lax.rsqrt(jnp.sum(qh * qh, axis=-1, keepdims=True) + MEM_HD * EPS)
        scores.append(_dot_nt((qh * r * q_gain).astype(BF16), k_ref[:, cols]))
    for cols, s in zip(heads, scores):
        p = jnp.exp2(s - jnp.max(s, axis=-1, keepdims=True))
        l = jnp.sum(p, axis=-1, keepdims=True)
        att_scr[:, cols] = (_dot(p.astype(BF16), v_ref[:, cols]) / l).astype(BF16)
    o_ref[...] = x + _dot(att_scr[...], wo_ref[...].astype(BF16))


def _out_xattn(a, r, x, out_gain, w_out, gain, wq, q_norm, kx, vx, wo, seq, mem_len):
    t = x.shape[0]
    per_batch = seq // XA_TM
    tile = lambda n: pl.BlockSpec((XA_TM, n), lambda i: (i, 0))
    whole = lambda w: pl.BlockSpec(w.shape, lambda i: (0, 0), pipeline_mode=pl.Buffered(1))
    vec = lambda n: pl.BlockSpec((1, n), lambda i: (0, 0))
    mem = pl.BlockSpec((mem_len, MEM_WIDTH), lambda i: (i // per_batch, 0))
    return pl.pallas_call(
        _out_xattn_kernel,
        out_shape=jax.ShapeDtypeStruct((t, D_MODEL), F32),
        grid=(t // XA_TM,),
        in_specs=[tile(MLA_WIDTH), tile(HG_WIDTH), tile(D_MODEL), vec(MLA_WIDTH), whole(w_out),
                  vec(D_MODEL), whole(wq), vec(MEM_HD), mem, mem, whole(wo)],
        out_specs=tile(D_MODEL),
        scratch_shapes=[pltpu.VMEM((XA_TM, MEM_WIDTH), BF16)],
        compiler_params=_params("parallel", vmem_limit=WIDE_VMEM_LIMIT),
        name="out_xattn",
    )(a, r, x, out_gain, w_out, gain, wq, q_norm, kx, vx, wo)


def _pe_pair(x1, x2):
    z = jnp.zeros(x1.shape[:-1] + (LANES - MLA_ROPE,), x1.dtype)
    return jnp.concatenate([x1, x2, z], axis=-1), jnp.concatenate([x2, x1, z], axis=-1)


W_IN_TC = 512


def _w_in_prep_kernel(w_ref, o_ref):
    half = MLA_ROPE // 2
    kpe0 = MLA_Q_RANK + MLA_KV_RANK
    hg0 = kpe0 + MLA_ROPE
    zeros = jnp.zeros((LANES - MLA_ROPE, o_ref.shape[1]), BF16)
    o_ref[:kpe0, :] = w_ref[:kpe0, :].astype(BF16)
    o_ref[kpe0:COL_KPE, :] = w_ref[hg0:, :].astype(BF16)
    x1 = w_ref[kpe0:kpe0 + half, :].astype(BF16)
    x2 = w_ref[kpe0 + half:hg0, :].astype(BF16)
    o_ref[COL_KPE:COL_KPE + half, :] = x1
    o_ref[COL_KPE + half:COL_KPE + MLA_ROPE, :] = x2
    o_ref[COL_KPE + MLA_ROPE:COL_KPE_SW, :] = zeros
    o_ref[COL_KPE_SW:COL_KPE_SW + half, :] = x2
    o_ref[COL_KPE_SW + half:COL_KPE_SW + MLA_ROPE, :] = x1
    o_ref[COL_KPE_SW + MLA_ROPE:, :] = zeros


def _prep_w_in(w_in, layer):
    w_t = jnp.swapaxes(w_in, 1, 2)
    _, n, k = w_t.shape
    return pl.pallas_call(
        _w_in_prep_kernel,
        out_shape=jax.ShapeDtypeStruct((IN_COLS_PAD, k), BF16),
        grid=(k // W_IN_TC,),
        in_specs=[pl.BlockSpec((None, n, W_IN_TC), lambda i: (layer, 0, i))],
        out_specs=pl.BlockSpec((IN_COLS_PAD, W_IN_TC), lambda i: (0, i)),
        compiler_params=_params("parallel"),
        name="w_in_prep",
    )(w_t)


def _prep_w_q_up(w):
    w = w.astype(BF16).reshape(MLA_Q_RANK, MLA_HEADS, MLA_QK)
    half = MLA_ROPE // 2
    pe, pe_sw = _pe_pair(w[..., MLA_NOPE:MLA_NOPE + half], w[..., MLA_NOPE + half:])
    return jnp.concatenate([w[..., :MLA_NOPE], pe, pe_sw], axis=-1).reshape(MLA_Q_RANK, MLA_HEADS * Q_UP_PER_HEAD)


def _pe_gains(norm):
    half = MLA_ROPE // 2
    g1 = norm[MLA_NOPE:MLA_NOPE + half]
    g2 = norm[MLA_NOPE + half:]
    ga, gb = _pe_pair(g1, g2)
    sign = jnp.concatenate([-jnp.ones((half,), F32), jnp.ones((LANES - half,), F32)])
    return ga[None, :], (gb * sign)[None, :]


def kernel(x, mem, positions, ffn1_norm, ffn1_w_gate, ffn1_w_up, ffn1_w_down, mix_norm, w_in, mla_q_a_norm, mla_w_q_up, mla_kv_a_norm, mla_w_kv_up, mla_q_norm, mla_k_norm, mla_out_norm, hg_lb_logits, hg_out_norm, w_out, xattn_norm, mem_norm, xattn_w_q, xattn_w_kv, xattn_q_norm, xattn_k_norm, xattn_w_o, ffn2_norm, ffn2_w_gate, ffn2_w_up, ffn2_w_down):
    batch, seq, _ = x.shape
    mem_len = mem.shape[1]
    depth = ffn1_norm.shape[0]
    assert depth == 1 and seq % ATT_T == 0 and seq % XA_TM == 0
    t = batch * seq
    xt = x.reshape(t, D_MODEL)
    half = MLA_ROPE // 2
    quarters = LANES // half
    pos = positions.reshape(t // PREP_TM, quarters, PREP_TM // quarters).transpose(0, 2, 1)
    pos = jnp.repeat(pos, half, axis=-1).reshape(t // quarters, LANES)
    inv_freq = ROPE_BASE ** (-np.arange(half, dtype=np.float32) / half)
    invf = jnp.asarray(np.tile(inv_freq, quarters)[None, :])
    l = 0

    xt = _ffn(xt, ffn1_norm[l][None, :], ffn1_w_gate[l], ffn1_w_up[l], ffn1_w_down[l])

    proj = _in_proj(xt, mix_norm[l][None, :], _prep_w_in(w_in, l))
    qpa, qpb = _pe_gains(mla_q_norm[l])
    kpa, kpb = _pe_gains(mla_k_norm[l])
    q, k, v = _mla_prep(proj, pos, invf, mla_q_a_norm[l][None, :], mla_kv_a_norm[l][None, :],
                        _prep_w_q_up(mla_w_q_up[l]), mla_w_kv_up[l],
                        mla_q_norm[l][None, :MLA_NOPE], qpa, qpb,
                        mla_k_norm[l][None, :MLA_NOPE], kpa, kpb)
    a = _mla_attn(q, k, v, batch, seq)
    r = _hgrn(proj, hg_lb_logits, hg_out_norm[l][None, :], batch, seq)
    kx, vx = _mem_kv(mem.reshape(batch * mem_len, D_MODEL), mem_norm[l][None, :],
                     xattn_w_kv[l], xattn_k_norm[l][None, :], batch, mem_len)
    xt = _out_xattn(a, r, xt, mla_out_norm[l][None, :], w_out[l],
                    xattn_norm[l][None, :], xattn_w_q[l], xattn_q_norm[l][None, :],
                    kx, vx, xattn_w_o[l], seq, mem_len)

    xt = _ffn(xt, ffn2_norm[l][None, :], ffn2_w_gate[l], ffn2_w_up[l], ffn2_w_down[l])
    return xt.reshape(batch, seq, D_MODEL)
```

```python
import numpy as np
import jax
import jax.numpy as jnp
from jax import lax
from jax.experimental import pallas as pl
from jax.experimental.pallas import tpu as pltpu

F32 = jnp.float32
BF16 = jnp.bfloat16

EPS = 1e-6
ROPE_BASE = 10000.0
LANES = 128
SUBLANES = 8

D_MODEL = 2048
D_FF = 5504
FFN_TM = 1024
FFN_TF = 512

MLA_HEADS = 8
MLA_NOPE = 128
MLA_ROPE = 64
MLA_QK = MLA_NOPE + MLA_ROPE
MLA_V = 128
MLA_Q_RANK = 512
MLA_KV_RANK = 256
MLA_WIDTH = MLA_HEADS * MLA_V
QK_PAD = 256
Q_UP_PER_HEAD = 384
ATT_Q_SCALE = float(MLA_QK ** -0.5 * np.log2(np.e))
ATT_HEADS_PER_STEP = 2
ATT_GROUPS = MLA_HEADS // ATT_HEADS_PER_STEP

HG_HEADS = 8
HG_D = 128
HG_CHUNK = 64
HG_LEVELS = (32, 16, 8, 4, 2, 1)
HG_UNROLL = 16
HG_GROUP = HG_CHUNK * HG_UNROLL
HG_SLAB = 256
HG_WIDTH = HG_HEADS * HG_D

MEM_HEADS = 4
MEM_HD = 128
MEM_WIDTH = MEM_HEADS * MEM_HD
XATT_Q_SCALE = float(MEM_HD ** -0.5 * np.log2(np.e))

COL_CQ = 0
COL_CKV = MLA_Q_RANK
COL_HQ = MLA_Q_RANK + MLA_KV_RANK
COL_HF = COL_HQ + HG_WIDTH
COL_HI = COL_HF + HG_WIDTH
COL_HG = COL_HI + HG_WIDTH
COL_KPE = COL_HG + HG_WIDTH
COL_KPE_SW = COL_KPE + LANES
IN_COLS_PAD = COL_KPE_SW + LANES

VMEM_LIMIT = 56 * 1024 * 1024
WIDE_VMEM_LIMIT = 61 * 1024 * 1024


def _params(*sem, vmem_limit=VMEM_LIMIT):
    return pltpu.CompilerParams(dimension_semantics=sem, vmem_limit_bytes=vmem_limit)


def _rms_scale(x):
    return lax.rsqrt(jnp.mean(x * x, axis=-1, keepdims=True) + EPS)


def _sigmoid(x):
    return 1.0 / (1.0 + jnp.exp(-x))


def _dot(a, b):
    return jnp.dot(a, b, preferred_element_type=F32)


def _dot_nt(a, b):
    return lax.dot_general(a, b, (((1,), (1,)), ((), ())), preferred_element_type=F32)


def _dot_tn(a, b):
    return lax.dot_general(a, b, (((0,), (0,)), ((), ())), preferred_element_type=F32)


def _ffn_kernel(x_hbm, g_ref, wg_ref, wu_ref, wd_ref, o_ref, h_scr, x_scr, x_sem):
    i = pl.program_id(0)

    def x_copy(tile):
        rows = pl.ds(pl.multiple_of(tile * FFN_TM, FFN_TM), FFN_TM)
        return pltpu.make_async_copy(x_hbm.at[rows, :], x_scr, x_sem)

    def branch(h):
        gate = _dot(h, wg_ref[...].astype(BF16))
        up = _dot(h, wu_ref[...].astype(BF16))
        a = 0.5 * gate * _sigmoid(gate) * up
        f0 = pl.program_id(1) * FFN_TF
        a_ok = f0 + lax.broadcasted_iota(jnp.int32, (1, FFN_TF), 1) < D_FF
        w_ok = f0 + lax.broadcasted_iota(jnp.int32, (FFN_TF, 1), 0) < D_FF
        a = jnp.where(a_ok, a, 0.0).astype(BF16)
        wd = jnp.where(w_ok, wd_ref[...], 0.0).astype(BF16)
        return _dot(a, wd)

    @pl.when(pl.program_id(1) == 0)
    def _():
        @pl.when(i == 0)
        def _():
            x_copy(0).start()

        x_copy(i).wait()
        x = x_scr[...]
        h = (x * _rms_scale(x) * g_ref[...]).astype(BF16)
        h_scr[...] = h
        o_ref[...] = x + branch(h)

        @pl.when(i + 1 < pl.num_programs(0))
        def _():
            x_copy(i + 1).start()

    @pl.when(pl.program_id(1) > 0)
    def _():
        o_ref[...] += branch(h_scr[...])


def _ffn(x, gain, wg, wu, wd):
    t = x.shape[0]
    return pl.pallas_call(
        _ffn_kernel,
        out_shape=jax.ShapeDtypeStruct((t, D_MODEL), F32),
        grid=(t // FFN_TM, pl.cdiv(D_FF, FFN_TF)),
        in_specs=[
            pl.BlockSpec(memory_space=pl.ANY),
            pl.BlockSpec((1, D_MODEL), lambda i, f: (0, 0)),
            pl.BlockSpec((D_MODEL, FFN_TF), lambda i, f: (0, f)),
            pl.BlockSpec((D_MODEL, FFN_TF), lambda i, f: (0, f)),
            pl.BlockSpec((FFN_TF, D_MODEL), lambda i, f: (f, 0)),
        ],
        out_specs=pl.BlockSpec((FFN_TM, D_MODEL), lambda i, f: (i, 0)),
        scratch_shapes=[pltpu.VMEM((FFN_TM, D_MODEL), BF16), pltpu.VMEM((FFN_TM, D_MODEL), F32),
                        pltpu.SemaphoreType.DMA],
        compiler_params=_params("arbitrary", "arbitrary", vmem_limit=WIDE_VMEM_LIMIT),
        name="ffn",
    )(x, gain, wg, wu, wd)


PROJ_TM = 1024
PROJ_TN = 2560


def _in_proj_kernel(x_hbm, g_ref, w_ref, o_ref, h_scr, x_scr, x_sem):
    i = pl.program_id(0)

    def x_copy(tile):
        rows = pl.ds(pl.multiple_of(tile * PROJ_TM, PROJ_TM), PROJ_TM)
        return pltpu.make_async_copy(x_hbm.at[rows, :], x_scr, x_sem)

    @pl.when(pl.program_id(1) == 0)
    def _():
        @pl.when(i == 0)
        def _():
            x_copy(0).start()

        x_copy(i).wait()
        x = x_scr[...]
        h = (x * _rms_scale(x) * g_ref[...]).astype(BF16)
        h_scr[...] = h
        o_ref[...] = _dot_nt(h, w_ref[...])

        @pl.when(i + 1 < pl.num_programs(0))
        def _():
            x_copy(i + 1).start()

    @pl.when(pl.program_id(1) > 0)
    def _():
        o_ref[...] = _dot_nt(h_scr[...], w_ref[...])


def _in_proj(x, gain, w_t):
    t = x.shape[0]
    n = w_t.shape[0]
    return pl.pallas_call(
        _in_proj_kernel,
        out_shape=jax.ShapeDtypeStruct((t, n), F32),
        grid=(t // PROJ_TM, n // PROJ_TN),
        in_specs=[
            pl.BlockSpec(memory_space=pl.ANY),
            pl.BlockSpec((1, D_MODEL), lambda i, j: (0, 0)),
            pl.BlockSpec((PROJ_TN, D_MODEL), lambda i, j: (j, 0)),
        ],
        out_specs=pl.BlockSpec((PROJ_TM, PROJ_TN), lambda i, j: (i, j)),
        scratch_shapes=[pltpu.VMEM((PROJ_TM, D_MODEL), BF16), pltpu.VMEM((PROJ_TM, D_MODEL), F32),
                        pltpu.SemaphoreType.DMA],
        compiler_params=_params("arbitrary", "arbitrary", vmem_limit=WIDE_VMEM_LIMIT),
        name="in_proj",
    )(x, gain, w_t)


PREP_TM = 1024


def _mla_prep_kernel(cq_ref, ckv_ref, kpe_ref, kpesw_ref, pos_ref, invf_ref,
                     qa_ref, kva_ref, wq_ref, wkv_ref,
                     qn_ref, qpa_ref, qpb_ref, kn_ref, kpa_ref, kpb_ref,
                     q_ref, k_ref, vt_ref):
    ang = pos_ref[...].astype(F32) * invf_ref[...]
    cos4 = jnp.cos(ang)
    sin4 = jnp.sin(ang)
    lane = lax.broadcasted_iota(jnp.int32, (1, LANES), 1)
    half = MLA_ROPE // 2

    def expand(t4):
        parts = []
        for part in range(LANES // half):
            t = pltpu.roll(t4, (LANES - half * part) % LANES, 1) if part else t4
            parts.append(jnp.where(lane < half, t, pltpu.roll(t, half, 1)))
        return jnp.concatenate(parts, axis=0)

    cos = expand(cos4)
    sin = expand(sin4)

    q_const = float(np.sqrt(MLA_QK)) * ATT_Q_SCALE
    k_const = float(np.sqrt(MLA_QK))

    cq = cq_ref[...]
    cqn = (cq * _rms_scale(cq) * qa_ref[...]).astype(BF16)
    qall = _dot(cqn, wq_ref[...])
    qn_gain = qn_ref[...] * q_const
    q_cos = (qpa_ref[...] * q_const) * cos
    q_sin = (qpb_ref[...] * q_const) * sin
    for h in range(MLA_HEADS):
        base = h * Q_UP_PER_HEAD
        qn = qall[:, base:base + LANES]
        y = qall[:, base + LANES:base + 2 * LANES]
        ysw = qall[:, base + 2 * LANES:base + 3 * LANES]
        r = lax.rsqrt(jnp.sum(qn * qn + y * y, axis=-1, keepdims=True) + MLA_QK * EPS)
        g, c = divmod(h, ATT_HEADS_PER_STEP)
        c *= QK_PAD
        q_ref[g, :, c:c + LANES] = (qn * r * qn_gain).astype(BF16)
        q_ref[g, :, c + LANES:c + QK_PAD] = ((y * q_cos + ysw * q_sin) * r).astype(BF16)

    ckv = ckv_ref[...]
    ckvn = (ckv * _rms_scale(ckv) * kva_ref[...]).astype(BF16)
    kvall = _dot(ckvn, wkv_ref[...].astype(BF16))
    yk = kpe_ref[...]
    kn_gain = kn_ref[...] * k_const
    kr = yk * ((kpa_ref[...] * k_const) * cos) + kpesw_ref[...] * ((kpb_ref[...] * k_const) * sin)
    yk_sq = yk * yk
    for h in range(MLA_HEADS):
        kn = kvall[:, h * 2 * LANES:h * 2 * LANES + LANES]
        r = lax.rsqrt(jnp.sum(kn * kn + yk_sq, axis=-1, keepdims=True) + MLA_QK * EPS)
        g, c = divmod(h, ATT_HEADS_PER_STEP)
        c *= QK_PAD
        k_ref[g, :, c:c + LANES] = (kn * r * kn_gain).astype(BF16)
        k_ref[g, :, c + LANES:c + QK_PAD] = (kr * r).astype(BF16)
        vt_ref[h * MLA_V:(h + 1) * MLA_V, :] = kvall[:, h * 2 * LANES + LANES:(h + 1) * 2 * LANES].T.astype(BF16)


def _mla_prep(proj, pos, invf, qa, kva, wq, wkv, qn, qpa, qpb, kn, kpa, kpb):
    t = proj.shape[0]
    tm = PREP_TM
    row = lambda i: (0, 0)
    vec = lambda n: pl.BlockSpec((1, n), row)
    return pl.pallas_call(
        _mla_prep_kernel,
        out_shape=(jax.ShapeDtypeStruct((ATT_GROUPS, t, ATT_HEADS_PER_STEP * QK_PAD), BF16),
                   jax.ShapeDtypeStruct((ATT_GROUPS, t, ATT_HEADS_PER_STEP * QK_PAD), BF16),
                   jax.ShapeDtypeStruct((MLA_WIDTH, t), BF16)),
        grid=(t // tm,),
        in_specs=[
            pl.BlockSpec((tm, MLA_Q_RANK), lambda i: (i, COL_CQ // MLA_Q_RANK)),
            pl.BlockSpec((tm, MLA_KV_RANK), lambda i: (i, COL_CKV // MLA_KV_RANK)),
            pl.BlockSpec((tm, LANES), lambda i: (i, COL_KPE // LANES)),
            pl.BlockSpec((tm, LANES), lambda i: (i, COL_KPE_SW // LANES)),
            pl.BlockSpec((tm // (LANES // (MLA_ROPE // 2)), LANES), lambda i: (i, 0)),
            vec(LANES),
            vec(MLA_Q_RANK), vec(MLA_KV_RANK),
            pl.BlockSpec(wq.shape, row), pl.BlockSpec(wkv.shape, row),
            vec(LANES), vec(LANES), vec(LANES), vec(LANES), vec(LANES), vec(LANES),
        ],
        out_specs=(pl.BlockSpec((ATT_GROUPS, tm, ATT_HEADS_PER_STEP * QK_PAD), lambda i: (0, i, 0)),
                   pl.BlockSpec((ATT_GROUPS, tm, ATT_HEADS_PER_STEP * QK_PAD), lambda i: (0, i, 0)),
                   pl.BlockSpec((MLA_WIDTH, tm), lambda i: (0, i))),
        compiler_params=_params("parallel"),
        name="mla_prep",
    )(proj, proj, proj, proj, pos, invf, qa, kva, wq, wkv, qn, qpa, qpb, kn, kpa, kpb)


ATT_T = 256
ATT_AHEAD = 5


def _mla_attn_kernel(q_ref, k_ref, vt_ref, o_ref):
    kpos = lax.broadcasted_iota(jnp.int32, (ATT_T, ATT_T), 0)
    qpos = lax.broadcasted_iota(jnp.int32, (ATT_T, ATT_T), 1)
    nq = q_ref.shape[0] // ATT_T

    def scores(unit):
        h, i = unit
        lo, hi = i * ATT_T, (i + 1) * ATT_T
        cols = slice(h * QK_PAD, (h + 1) * QK_PAD)
        q = q_ref[lo:hi, cols]
        s_diag = jnp.where(kpos <= qpos, _dot_nt(k_ref[lo:hi, cols], q), -jnp.inf)
        s_off = _dot_nt(k_ref[0:lo, cols], q) if i > 0 else None
        return s_diag, s_off

    def finish(unit, s_diag, s_off):
        h, i = unit
        lo, hi = i * ATT_T, (i + 1) * ATT_T
        vt = vt_ref.at[h * MLA_V:(h + 1) * MLA_V, :]
        m = jnp.max(s_diag, axis=0, keepdims=True)
        if i > 0:
            m = jnp.maximum(m, jnp.max(s_off, axis=0, keepdims=True))
        p = jnp.exp2(s_diag - m)
        l = jnp.sum(p, axis=0, keepdims=True)
        acc = _dot(vt[:, lo:hi], p.astype(BF16))
        if i > 0:
            p = jnp.exp2(s_off - m)
            l = l + jnp.sum(p, axis=0, keepdims=True)
            acc = acc + _dot(vt[:, 0:lo], p.astype(BF16))
        o_ref[lo:hi, h * MLA_V:(h + 1) * MLA_V] = (acc / l).T.astype(BF16)

    order = [(h, i) for i in reversed(range(nq)) for h in range(ATT_HEADS_PER_STEP)]
    pending = [scores(u) for u in order[:ATT_AHEAD]]
    for n, unit in enumerate(order):
        if n + ATT_AHEAD < len(order):
            pending.append(scores(order[n + ATT_AHEAD]))
        finish(unit, *pending.pop(0))


def _mla_attn(q, k, vt, batch, seq):
    hs = ATT_HEADS_PER_STEP
    return pl.pallas_call(
        _mla_attn_kernel,
        out_shape=jax.ShapeDtypeStruct((ATT_GROUPS, batch * seq, hs * MLA_V), BF16),
        grid=(batch, ATT_GROUPS),
        in_specs=[
            pl.BlockSpec((None, seq, hs * QK_PAD), lambda b, h: (h, b, 0)),
            pl.BlockSpec((None, seq, hs * QK_PAD), lambda b, h: (h, b, 0)),
            pl.BlockSpec((hs * MLA_V, seq), lambda b, h: (h, b)),
        ],
        out_specs=pl.BlockSpec((None, seq, hs * MLA_V), lambda b, h: (h, b, 0)),
        compiler_params=_params("parallel", "parallel"),
        name="mla_attn",
    )(q, k, vt)


def _hgrn_masks():
    t = np.arange(HG_CHUNK)[:, None]
    s = np.arange(HG_CHUNK)[None, :]
    masks = [((t // c) % 2 == 1) & ((s // c) == (t // c) - 1) for c in HG_LEVELS]
    masks.append(t == s)
    return np.stack(masks).astype(np.float32)


def _hgrn_kernel(q_ref, f_ref, i_ref, g_ref, lbl_ref, gn_ref, msk_ref, tri_ref, o_ref, b_all):
    lbl = lbl_ref[...]
    e = jnp.exp(lbl - jnp.max(lbl, axis=0, keepdims=True))
    lb = e[0:1, :] / jnp.sum(e, axis=0, keepdims=True)
    gn = gn_ref[...]
    tri = tri_ref[...]
    sub = lax.broadcasted_iota(jnp.int32, (SUBLANES, LANES), 0)
    zeros8 = jnp.zeros((SUBLANES, LANES), F32)
    per_chunk = HG_CHUNK // SUBLANES
    chunk_rows = [slice(u * HG_CHUNK, (u + 1) * HG_CHUNK) for u in range(HG_UNROLL)]

    def sel(bounds, vals):
        out = vals[-1]
        for bound, val in zip(reversed(bounds), reversed(vals[:-1])):
            out = jnp.where(sub < bound, val, out)
        return out

    def head(gi):
        rows = slice(gi * HG_GROUP, (gi + 1) * HG_GROUP)
        q = q_ref[rows, :]
        v = i_ref[rows, :].astype(BF16)
        f = lb + (1.0 - lb) * _sigmoid(f_ref[rows, :])
        lg = jnp.log2(f)
        k = 1.0 - f

        l1 = lg.astype(BF16)
        l2 = (lg - l1.astype(F32)).astype(BF16)
        slabs = [slice(i, i + HG_SLAB) for i in range(0, HG_GROUP, HG_SLAB)]
        b = jnp.concatenate([_dot(tri, l1[r]) + _dot(tri, l2[r]) for r in slabs], axis=0)
        b_all[gi] = b
        return q, k, f, v, b

    def main(gi, hd, st):
        rows = slice(gi * HG_GROUP, (gi + 1) * HG_GROUP)
        q, k, f, v, b = hd
        b_scr = b_all.at[gi]
        cache = {}

        def rowb(u, r):
            if r < 0:
                return zeros8
            if (u, r) not in cache:
                cache[u, r] = jnp.broadcast_to(b_scr[u * HG_CHUNK + r:u * HG_CHUNK + r + 1, :], (SUBLANES, LANES))
            return cache[u, r]

        def build(fn):
            return jnp.concatenate([fn(u, j) for u in range(HG_UNROLL) for j in range(per_chunk)], axis=0)

        def group_rows(x, u, j):
            r0 = u * HG_CHUNK + j * SUBLANES
            return x[r0:r0 + SUBLANES]

        def level_rows(ci, j):
            if ("m", ci, j) not in cache:
                cache["m", ci, j] = msk_ref[ci, j * SUBLANES:(j + 1) * SUBLANES, :] != 0.0
            return cache["m", ci, j]

        diag = jnp.sum(q * k, axis=-1, keepdims=True)
        a = [[jnp.where(level_rows(len(HG_LEVELS), j), group_rows(diag, u, j), 0.0) for j in range(per_chunk)]
             for u in range(HG_UNROLL)]
        for ci, c in enumerate(HG_LEVELS):
            if c >= SUBLANES:
                m = c // SUBLANES
                right = [j for j in range(per_chunk) if (j // m) % 2 == 1]
                pick = lambda x, js: jnp.concatenate([group_rows(x, u, j) for u in range(HG_UNROLL) for j in js], axis=0)
                bs = jnp.concatenate([rowb(u, SUBLANES * (j // m) * m - 1) for u in range(HG_UNROLL) for j in right], axis=0)
                qd = (pick(q, right) * jnp.exp2(pick(b, right) - bs)).astype(BF16)
                kd = jnp.concatenate(
                    [group_rows(k, u, j) * jnp.exp2(rowb(u, SUBLANES * ((j // m) * m + m) - 1) - group_rows(b, u, j))
                     if (j // m) % 2 == 0 else zeros8
                     for u in range(HG_UNROLL) for j in range(per_chunk)], axis=0).astype(BF16)
                rows_per_chunk = len(right) * SUBLANES
                for u, r in enumerate(chunk_rows):
                    prod = _dot_nt(qd[u * rows_per_chunk:(u + 1) * rows_per_chunk], kd[r])
                    for n, j in enumerate(right):
                        a[u][j] = jnp.where(level_rows(ci, j), prod[n * SUBLANES:(n + 1) * SUBLANES], a[u][j])
                continue
            if c == 4:
                bs = build(lambda u, j: sel((4,), (rowb(u, 8 * j - 1), rowb(u, 8 * j + 3))))
                be = build(lambda u, j: sel((4,), (rowb(u, 8 * j + 3), rowb(u, 8 * j + 7))))
            elif c == 2:
                bs = build(lambda u, j: sel((2, 4, 6), (rowb(u, 8 * j - 1), rowb(u, 8 * j + 1),
                                                         rowb(u, 8 * j + 3), rowb(u, 8 * j + 5))))
                be = build(lambda u, j: sel((2, 4, 6), (rowb(u, 8 * j + 1), rowb(u, 8 * j + 3),
                                                         rowb(u, 8 * j + 5), rowb(u, 8 * j + 7))))
            if c == 1:
                qd = (q * f).astype(BF16)
                kd = k.astype(BF16)
            else:
                qd = (q * jnp.exp2(b - bs)).astype(BF16)
                kd = (k * jnp.exp2(be - b)).astype(BF16)
            for u, r in enumerate(chunk_rows):
                prod = _dot_nt(qd[r], kd[r])
                for j in range(per_chunk):
                    a[u][j] = jnp.where(level_rows(ci, j), prod[j * SUBLANES:(j + 1) * SUBLANES], a[u][j])
        a = [jnp.concatenate(a[u], axis=0) for u in range(HG_UNROLL)]

        q_exp = (q * jnp.exp2(b)).astype(BF16)
        b_last = build(lambda u, j: rowb(u, HG_CHUNK - 1))
        k_dec = (k * jnp.exp2(b_last - b)).astype(BF16)
        o_intra = [_dot(a[u].astype(BF16), v[r]) for u, r in enumerate(chunk_rows)]
        st_add = [_dot_tn(v[r], k_dec[r]) for r in chunk_rows]

        o = []
        for u, r in enumerate(chunk_rows):
            o.append(o_intra[u] + _dot_nt(q_exp[r], st.astype(BF16)))
            st = jnp.exp2(rowb(u, HG_CHUNK - 1)[0:1, :]) * st + st_add[u]
        o = jnp.concatenate(o, axis=0)

        g = g_ref[rows, :]
        o_ref[rows, :] = (o * _rms_scale(o) * gn * (g * _sigmoid(g))).astype(BF16)
        return st

    n_groups = q_ref.shape[0] // HG_GROUP
    st = jnp.zeros((HG_D, HG_D), F32)
    hd = head(0)
    for gi in range(n_groups):
        nxt = head(gi + 1) if gi + 1 < n_groups else None
        st = main(gi, hd, st)
        hd = nxt


def _hgrn(proj, lb_logits, out_norm, batch, seq):
    hblk = lambda col: pl.BlockSpec((seq, HG_D), lambda b, h: (b, col // HG_D + h))
    masks = jnp.asarray(_hgrn_masks())
    tri = np.kron(np.eye(HG_SLAB // HG_CHUNK), np.tril(np.ones((HG_CHUNK, HG_CHUNK))))
    tri = jnp.asarray(tri.astype(np.float32), dtype=BF16)
    nlev = masks.shape[0]
    return pl.pallas_call(
        _hgrn_kernel,
        out_shape=jax.ShapeDtypeStruct((batch * seq, HG_WIDTH), BF16),
        grid=(batch, HG_HEADS),
        in_specs=[
            hblk(COL_HQ), hblk(COL_HF), hblk(COL_HI), hblk(COL_HG),
            pl.BlockSpec((lb_logits.shape[0], HG_D), lambda b, h: (0, h)),
            pl.BlockSpec((1, HG_D), lambda b, h: (0, 0)),
            pl.BlockSpec((nlev, HG_CHUNK, HG_CHUNK), lambda b, h: (0, 0, 0)),
            pl.BlockSpec((HG_SLAB, HG_SLAB), lambda b, h: (0, 0)),
        ],
        out_specs=pl.BlockSpec((seq, HG_D), lambda b, h: (b, h)),
        scratch_shapes=[pltpu.VMEM((seq // HG_GROUP, HG_GROUP, HG_D), F32)],
        compiler_params=_params("parallel", "parallel"),
        name="hgrn2",
    )(proj, proj, proj, proj, lb_logits, out_norm, masks, tri)


OUT_TM = 512


def _out_proj_kernel(a_ref, r_ref, x_ref, g_ref, w_ref, o_ref):
    a = jnp.concatenate([a_ref[g] for g in range(ATT_GROUPS)], axis=-1).astype(F32)
    an = (a * _rms_scale(a) * g_ref[...]).astype(BF16)
    o_ref[...] = (x_ref[...] + _dot(an, w_ref[:MLA_WIDTH, :].astype(BF16))
                  + _dot(r_ref[...], w_ref[MLA_WIDTH:, :].astype(BF16)))


def _out_proj(a, r, x, gain, w):
    t = x.shape[0]
    return pl.pallas_call(
        _out_proj_kernel,
        out_shape=jax.ShapeDtypeStruct((t, D_MODEL), F32),
        grid=(t // OUT_TM,),
        in_specs=[
            pl.BlockSpec((ATT_GROUPS, OUT_TM, MLA_WIDTH // ATT_GROUPS), lambda i: (0, i, 0)),
            pl.BlockSpec((OUT_TM, HG_WIDTH), lambda i: (i, 0)),
            pl.BlockSpec((OUT_TM, D_MODEL), lambda i: (i, 0)),
            pl.BlockSpec((1, MLA_WIDTH), lambda i: (0, 0)),
            pl.BlockSpec(w.shape, lambda i: (0, 0), pipeline_mode=pl.Buffered(1)),
        ],
        out_specs=pl.BlockSpec((OUT_TM, D_MODEL), lambda i: (i, 0)),
        compiler_params=_params("parallel"),
        name="out_proj",
    )(a, r, x, gain, w)


def _mem_kv_kernel(m_ref, g_ref, w_ref, kn_ref, k_ref, v_ref):
    m = m_ref[...]
    mn = (m * _rms_scale(m) * g_ref[...]).astype(BF16)
    kv = _dot(mn, w_ref[...].astype(BF16))
    for h in range(MEM_HEADS):
        k = kv[:, 2 * h * MEM_HD:(2 * h + 1) * MEM_HD]
        k_ref[:, h * MEM_HD:(h + 1) * MEM_HD] = (k * _rms_scale(k) * kn_ref[...]).astype(BF16)
        v_ref[:, h * MEM_HD:(h + 1) * MEM_HD] = kv[:, (2 * h + 1) * MEM_HD:(2 * h + 2) * MEM_HD].astype(BF16)


def _mem_kv(mem, gain, w, k_norm, batch, mem_len):
    return pl.pallas_call(
        _mem_kv_kernel,
        out_shape=(jax.ShapeDtypeStruct((batch * mem_len, MEM_WIDTH), BF16),
                   jax.ShapeDtypeStruct((batch * mem_len, MEM_WIDTH), BF16)),
        grid=(batch,),
        in_specs=[
            pl.BlockSpec((mem_len, D_MODEL), lambda b: (b, 0)),
            pl.BlockSpec((1, D_MODEL), lambda b: (0, 0)),
            pl.BlockSpec(w.shape, lambda b: (0, 0)),
            pl.BlockSpec((1, MEM_HD), lambda b: (0, 0)),
        ],
        out_specs=(pl.BlockSpec((mem_len, MEM_WIDTH), lambda b: (b, 0)),
                   pl.BlockSpec((mem_len, MEM_WIDTH), lambda b: (b, 0))),
        compiler_params=_params("parallel"),
        name="mem_kv",
    )(mem, gain, w, k_norm)


XA_TM = 1024


def _xattn_kernel(x_ref, g_ref, wq_ref, qn_ref, k_ref, v_ref, wo_ref, o_ref, att_scr):
    x = x_ref[...]
    h = (x * _rms_scale(x) * g_ref[...]).astype(BF16)
    q = _dot(h, wq_ref[...].astype(BF16))
    heads = [slice(hd * MEM_HD, (hd + 1) * MEM_HD) for hd in range(MEM_HEADS)]
    q_gain = qn_ref[...] * (float(np.sqrt(MEM_HD)) * XATT_Q_SCALE)
    scores = []
    for cols in heads:
        qh = q[:, cols]
        r = lax.rsqrt(jnp.sum(qh * qh, axis=-1, keepdims=True) + MEM_HD * EPS)
        scores.append(_dot_nt((qh * r * q_gain).astype(BF16), k_ref[:, cols]))
    for cols, s in zip(heads, scores):
        p = jnp.exp2(s - jnp.max(s, axis=-1, keepdims=True))
        l = jnp.sum(p, axis=-1, keepdims=True)
        att_scr[:, cols] = (_dot(p.astype(BF16), v_ref[:, cols]) / l).astype(BF16)
    o_ref[...] = x + _dot(att_scr[...], wo_ref[...].astype(BF16))


def _xattn(x, gain, wq, q_norm, kx, vx, wo, seq, mem_len):
    t = x.shape[0]
    per_batch = seq // XA_TM
    return pl.pallas_call(
        _xattn_kernel,
        out_shape=jax.ShapeDtypeStruct((t, D_MODEL), F32),
        grid=(t // XA_TM,),
        in_specs=[
            pl.BlockSpec((XA_TM, D_MODEL), lambda i: (i, 0)),
            pl.BlockSpec((1, D_MODEL), lambda i: (0, 0)),
            pl.BlockSpec(wq.shape, lambda i: (0, 0)),
            pl.BlockSpec((1, MEM_HD), lambda i: (0, 0)),
            pl.BlockSpec((mem_len, MEM_WIDTH), lambda i: (i // per_batch, 0)),
            pl.BlockSpec((mem_len, MEM_WIDTH), lambda i: (i // per_batch, 0)),
            pl.BlockSpec(wo.shape, lambda i: (0, 0)),
        ],
        out_specs=pl.BlockSpec((XA_TM, D_MODEL), lambda i: (i, 0)),
        scratch_shapes=[pltpu.VMEM((XA_TM, MEM_WIDTH), BF16)],
        compiler_params=_params("parallel"),
        name="xattn",
    )(x, gain, wq, q_norm, kx, vx, wo)


def _pe_pair(x1, x2):
    z = jnp.zeros(x1.shape[:-1] + (LANES - MLA_ROPE,), x1.dtype)
    return jnp.concatenate([x1, x2, z], axis=-1), jnp.concatenate([x2, x1, z], axis=-1)


W_IN_TC = 512


def _w_in_prep_kernel(w_ref, o_ref):
    half = MLA_ROPE // 2
    kpe0 = MLA_Q_RANK + MLA_KV_RANK
    hg0 = kpe0 + MLA_ROPE
    zeros = jnp.zeros((LANES - MLA_ROPE, o_ref.shape[1]), BF16)
    o_ref[:kpe0, :] = w_ref[:kpe0, :].astype(BF16)
    o_ref[kpe0:COL_KPE, :] = w_ref[hg0:, :].astype(BF16)
    x1 = w_ref[kpe0:kpe0 + half, :].astype(BF16)
    x2 = w_ref[kpe0 + half:hg0, :].astype(BF16)
    o_ref[COL_KPE:COL_KPE + half, :] = x1
    o_ref[COL_KPE + half:COL_KPE + MLA_ROPE, :] = x2
    o_ref[COL_KPE + MLA_ROPE:COL_KPE_SW, :] = zeros
    o_ref[COL_KPE_SW:COL_KPE_SW + half, :] = x2
    o_ref[COL_KPE_SW + half:COL_KPE_SW + MLA_ROPE, :] = x1
    o_ref[COL_KPE_SW + MLA_ROPE:, :] = zeros


def _prep_w_in(w_in, layer):
    w_t = jnp.swapaxes(w_in, 1, 2)
    _, n, k = w_t.shape
    return pl.pallas_call(
        _w_in_prep_kernel,
        out_shape=jax.ShapeDtypeStruct((IN_COLS_PAD, k), BF16),
        grid=(k // W_IN_TC,),
        in_specs=[pl.BlockSpec((None, n, W_IN_TC), lambda i: (layer, 0, i))],
        out_specs=pl.BlockSpec((IN_COLS_PAD, W_IN_TC), lambda i: (0, i)),
        compiler_params=_params("parallel"),
        name="w_in_prep",
    )(w_t)


def _prep_w_q_up(w):
    w = w.astype(BF16).reshape(MLA_Q_RANK, MLA_HEADS, MLA_QK)
    half = MLA_ROPE // 2
    pe, pe_sw = _pe_pair(w[..., MLA_NOPE:MLA_NOPE + half], w[..., MLA_NOPE + half:])
    return jnp.concatenate([w[..., :MLA_NOPE], pe, pe_sw], axis=-1).reshape(MLA_Q_RANK, MLA_HEADS * Q_UP_PER_HEAD)


def _pe_gains(norm):
    half = MLA_ROPE // 2
    g1 = norm[MLA_NOPE:MLA_NOPE + half]
    g2 = norm[MLA_NOPE + half:]
    ga, gb = _pe_pair(g1, g2)
    sign = jnp.concatenate([-jnp.ones((half,), F32), jnp.ones((LANES - half,), F32)])
    return ga[None, :], (gb * sign)[None, :]


def kernel(x, mem, positions, ffn1_norm, ffn1_w_gate, ffn1_w_up, ffn1_w_down, mix_norm, w_in, mla_q_a_norm, mla_w_q_up, mla_kv_a_norm, mla_w_kv_up, mla_q_norm, mla_k_norm, mla_out_norm, hg_lb_logits, hg_out_norm, w_out, xattn_norm, mem_norm, xattn_w_q, xattn_w_kv, xattn_q_norm, xattn_k_norm, xattn_w_o, ffn2_norm, ffn2_w_gate, ffn2_w_up, ffn2_w_down):
    batch, seq, _ = x.shape
    mem_len = mem.shape[1]
    depth = ffn1_norm.shape[0]
    assert depth == 1 and seq % ATT_T == 0 and seq % XA_TM == 0
    t = batch * seq
    xt = x.reshape(t, D_MODEL)
    half = MLA_ROPE // 2
    quarters = LANES // half
    pos = positions.reshape(t // PREP_TM, quarters, PREP_TM // quarters).transpose(0, 2, 1)
    pos = jnp.repeat(pos, half, axis=-1).reshape(t // quarters, LANES)
    inv_freq = ROPE_BASE ** (-np.arange(half, dtype=np.float32) / half)
    invf = jnp.asarray(np.tile(inv_freq, quarters)[None, :])
    l = 0

    xt = _ffn(xt, ffn1_norm[l][None, :], ffn1_w_gate[l], ffn1_w_up[l], ffn1_w_down[l])

    proj = _in_proj(xt, mix_norm[l][None, :], _prep_w_in(w_in, l))
    qpa, qpb = _pe_gains(mla_q_norm[l])
    kpa, kpb = _pe_gains(mla_k_norm[l])
    q, k, v = _mla_prep(proj, pos, invf, mla_q_a_norm[l][None, :], mla_kv_a_norm[l][None, :],
                        _prep_w_q_up(mla_w_q_up[l]), mla_w_kv_up[l],
                        mla_q_norm[l][None, :MLA_NOPE], qpa, qpb,
                        mla_k_norm[l][None, :MLA_NOPE], kpa, kpb)
    a = _mla_attn(q, k, v, batch, seq)
    r = _hgrn(proj, hg_lb_logits, hg_out_norm[l][None, :], batch, seq)
    xt = _out_proj(a, r, xt, mla_out_norm[l][None, :], w_out[l])

    kx, vx = _mem_kv(mem.reshape(batch * mem_len, D_MODEL), mem_norm[l][None, :],
                     xattn_w_kv[l], xattn_k_norm[l][None, :], batch, mem_len)
    xt = _xattn(xt, xattn_norm[l][None, :], xattn_w_q[l], xattn_q_norm[l][None, :],
                kx, vx, xattn_w_o[l], seq, mem_len)

    xt = _ffn(xt, ffn2_norm[l][None, :], ffn2_w_gate[l], ffn2_w_up[l], ffn2_w_down[l])
    return xt.reshape(batch, seq, D_MODEL)
```

```python
import numpy as np
import jax
import jax.numpy as jnp
from jax import lax
from jax.experimental import pallas as pl
from jax.experimental.pallas import tpu as pltpu

F32 = jnp.float32
BF16 = jnp.bfloat16

EPS = 1e-6
ROPE_BASE = 10000.0
LANES = 128
SUBLANES = 8

D_MODEL = 2048
D_FF = 5504
FFN_TM = 1024
FFN_TF = 512

MLA_HEADS = 8
MLA_NOPE = 128
MLA_ROPE = 64
MLA_QK = MLA_NOPE + MLA_ROPE
MLA_V = 128
MLA_Q_RANK = 512
MLA_KV_RANK = 256
MLA_WIDTH = MLA_HEADS * MLA_V
QK_PAD = 256
Q_UP_PER_HEAD = 384
ATT_Q_SCALE = float(MLA_QK ** -0.5 * np.log2(np.e))
ATT_HEADS_PER_STEP = 2
ATT_GROUPS = MLA_HEADS // ATT_HEADS_PER_STEP

HG_HEADS = 8
HG_D = 128
HG_CHUNK = 64
HG_LEVELS = (32, 16, 8, 4, 2, 1)
HG_UNROLL = 16
HG_GROUP = HG_CHUNK * HG_UNROLL
HG_SLAB = 256
HG_WIDTH = HG_HEADS * HG_D

MEM_HEADS = 4
MEM_HD = 128
MEM_WIDTH = MEM_HEADS * MEM_HD
XATT_Q_SCALE = float(MEM_HD ** -0.5 * np.log2(np.e))

COL_CQ = 0
COL_CKV = MLA_Q_RANK
COL_HQ = MLA_Q_RANK + MLA_KV_RANK
COL_HF = COL_HQ + HG_WIDTH
COL_HI = COL_HF + HG_WIDTH
COL_HG = COL_HI + HG_WIDTH
COL_KPE = COL_HG + HG_WIDTH
COL_KPE_SW = COL_KPE + LANES
IN_COLS_PAD = COL_KPE_SW + LANES

VMEM_LIMIT = 56 * 1024 * 1024
WIDE_VMEM_LIMIT = 61 * 1024 * 1024


def _params(*sem, vmem_limit=VMEM_LIMIT):
    return pltpu.CompilerParams(dimension_semantics=sem, vmem_limit_bytes=vmem_limit)


def _rms_scale(x):
    return lax.rsqrt(jnp.mean(x * x, axis=-1, keepdims=True) + EPS)


def _sigmoid(x):
    return 1.0 / (1.0 + jnp.exp(-x))


def _dot(a, b):
    return jnp.dot(a, b, preferred_element_type=F32)


def _dot_nt(a, b):
    return lax.dot_general(a, b, (((1,), (1,)), ((), ())), preferred_element_type=F32)


def _dot_tn(a, b):
    return lax.dot_general(a, b, (((0,), (0,)), ((), ())), preferred_element_type=F32)


def _ffn_kernel(x_hbm, g_ref, wg_ref, wu_ref, wd_ref, o_ref, h_scr, x_scr, x_sem):
    i = pl.program_id(0)

    def x_copy(tile):
        rows = pl.ds(pl.multiple_of(tile * FFN_TM, FFN_TM), FFN_TM)
        return pltpu.make_async_copy(x_hbm.at[rows, :], x_scr, x_sem)

    def branch(h):
        gate = _dot(h, wg_ref[...].astype(BF16))
        up = _dot(h, wu_ref[...].astype(BF16))
        a = 0.5 * gate * _sigmoid(gate) * up
        f0 = pl.program_id(1) * FFN_TF
        a_ok = f0 + lax.broadcasted_iota(jnp.int32, (1, FFN_TF), 1) < D_FF
        w_ok = f0 + lax.broadcasted_iota(jnp.int32, (FFN_TF, 1), 0) < D_FF
        a = jnp.where(a_ok, a, 0.0).astype(BF16)
        wd = jnp.where(w_ok, wd_ref[...], 0.0).astype(BF16)
        return _dot(a, wd)

    @pl.when(pl.program_id(1) == 0)
    def _():
        @pl.when(i == 0)
        def _():
            x_copy(0).start()

        x_copy(i).wait()
        x = x_scr[...]
        h = (x * _rms_scale(x) * g_ref[...]).astype(BF16)
        h_scr[...] = h
        o_ref[...] = x + branch(h)

        @pl.when(i + 1 < pl.num_programs(0))
        def _():
            x_copy(i + 1).start()

    @pl.when(pl.program_id(1) > 0)
    def _():
        o_ref[...] += branch(h_scr[...])


def _ffn(x, gain, wg, wu, wd):
    t = x.shape[0]
    return pl.pallas_call(
        _ffn_kernel,
        out_shape=jax.ShapeDtypeStruct((t, D_MODEL), F32),
        grid=(t // FFN_TM, pl.cdiv(D_FF, FFN_TF)),
        in_specs=[
            pl.BlockSpec(memory_space=pl.ANY),
            pl.BlockSpec((1, D_MODEL), lambda i, f: (0, 0)),
            pl.BlockSpec((D_MODEL, FFN_TF), lambda i, f: (0, f)),
            pl.BlockSpec((D_MODEL, FFN_TF), lambda i, f: (0, f)),
            pl.BlockSpec((FFN_TF, D_MODEL), lambda i, f: (f, 0)),
        ],
        out_specs=pl.BlockSpec((FFN_TM, D_MODEL), lambda i, f: (i, 0)),
        scratch_shapes=[pltpu.VMEM((FFN_TM, D_MODEL), BF16), pltpu.VMEM((FFN_TM, D_MODEL), F32),
                        pltpu.SemaphoreType.DMA],
        compiler_params=_params("arbitrary", "arbitrary", vmem_limit=WIDE_VMEM_LIMIT),
        name="ffn",
    )(x, gain, wg, wu, wd)


PROJ_TM = 1024
PROJ_TN = 2560


def _in_proj_kernel(x_hbm, g_ref, w_ref, o_ref, h_scr, x_scr, x_sem):
    i = pl.program_id(0)

    def x_copy(tile):
        rows = pl.ds(pl.multiple_of(tile * PROJ_TM, PROJ_TM), PROJ_TM)
        return pltpu.make_async_copy(x_hbm.at[rows, :], x_scr, x_sem)

    @pl.when(pl.program_id(1) == 0)
    def _():
        @pl.when(i == 0)
        def _():
            x_copy(0).start()

        x_copy(i).wait()
        x = x_scr[...]
        h = (x * _rms_scale(x) * g_ref[...]).astype(BF16)
        h_scr[...] = h
        o_ref[...] = _dot_nt(h, w_ref[...])

        @pl.when(i + 1 < pl.num_programs(0))
        def _():
            x_copy(i + 1).start()

    @pl.when(pl.program_id(1) > 0)
    def _():
        o_ref[...] = _dot_nt(h_scr[...], w_ref[...])


def _in_proj(x, gain, w_t):
    t = x.shape[0]
    n = w_t.shape[0]
    return pl.pallas_call(
        _in_proj_kernel,
        out_shape=jax.ShapeDtypeStruct((t, n), F32),
        grid=(t // PROJ_TM, n // PROJ_TN),
        in_specs=[
            pl.BlockSpec(memory_space=pl.ANY),
            pl.BlockSpec((1, D_MODEL), lambda i, j: (0, 0)),
            pl.BlockSpec((PROJ_TN, D_MODEL), lambda i, j: (j, 0)),
        ],
        out_specs=pl.BlockSpec((PROJ_TM, PROJ_TN), lambda i, j: (i, j)),
        scratch_shapes=[pltpu.VMEM((PROJ_TM, D_MODEL), BF16), pltpu.VMEM((PROJ_TM, D_MODEL), F32),
                        pltpu.SemaphoreType.DMA],
        compiler_params=_params("arbitrary", "arbitrary", vmem_limit=WIDE_VMEM_LIMIT),
        name="in_proj",
    )(x, gain, w_t)


PREP_TM = 1024


def _mla_prep_kernel(cq_ref, ckv_ref, kpe_ref, kpesw_ref, pos_ref, invf_ref,
                     qa_ref, kva_ref, wq_ref, wkv_ref,
                     qn_ref, qpa_ref, qpb_ref, kn_ref, kpa_ref, kpb_ref,
                     q_ref, k_ref, vt_ref):
    ang = pos_ref[...].astype(F32) * invf_ref[...]
    cos4 = jnp.cos(ang)
    sin4 = jnp.sin(ang)
    lane = lax.broadcasted_iota(jnp.int32, (1, LANES), 1)
    half = MLA_ROPE // 2

    def expand(t4):
        parts = []
        for part in range(LANES // half):
            t = pltpu.roll(t4, (LANES - half * part) % LANES, 1) if part else t4
            parts.append(jnp.where(lane < half, t, pltpu.roll(t, half, 1)))
        return jnp.concatenate(parts, axis=0)

    cos = expand(cos4)
    sin = expand(sin4)

    q_const = float(np.sqrt(MLA_QK)) * ATT_Q_SCALE
    k_const = float(np.sqrt(MLA_QK))

    cq = cq_ref[...]
    cqn = (cq * _rms_scale(cq) * qa_ref[...]).astype(BF16)
    qall = _dot(cqn, wq_ref[...])
    qn_gain = qn_ref[...] * q_const
    q_cos = (qpa_ref[...] * q_const) * cos
    q_sin = (qpb_ref[...] * q_const) * sin
    for h in range(MLA_HEADS):
        base = h * Q_UP_PER_HEAD
        qn = qall[:, base:base + LANES]
        y = qall[:, base + LANES:base + 2 * LANES]
        ysw = qall[:, base + 2 * LANES:base + 3 * LANES]
        r = lax.rsqrt(jnp.sum(qn * qn + y * y, axis=-1, keepdims=True) + MLA_QK * EPS)
        g, c = divmod(h, ATT_HEADS_PER_STEP)
        c *= QK_PAD
        q_ref[g, :, c:c + LANES] = (qn * r * qn_gain).astype(BF16)
        q_ref[g, :, c + LANES:c + QK_PAD] = ((y * q_cos + ysw * q_sin) * r).astype(BF16)

    ckv = ckv_ref[...]
    ckvn = (ckv * _rms_scale(ckv) * kva_ref[...]).astype(BF16)
    wkv = wkv_ref[...]
    per_head = MLA_NOPE + MLA_V
    wk = jnp.concatenate([wkv[:, h * per_head:h * per_head + MLA_NOPE] for h in range(MLA_HEADS)], axis=1)
    wv = jnp.concatenate([wkv[:, h * per_head + MLA_NOPE:(h + 1) * per_head] for h in range(MLA_HEADS)], axis=1)
    kn_all = _dot(ckvn, wk.astype(BF16))
    vt_ref[...] = _dot_nt(wv.T.astype(BF16), ckvn).astype(BF16)
    yk = kpe_ref[...]
    kn_gain = kn_ref[...] * k_const
    kr = yk * ((kpa_ref[...] * k_const) * cos) + kpesw_ref[...] * ((kpb_ref[...] * k_const) * sin)
    yk_sq = yk * yk
    for h in range(MLA_HEADS):
        kn = kn_all[:, h * MLA_NOPE:(h + 1) * MLA_NOPE]
        r = lax.rsqrt(jnp.sum(kn * kn + yk_sq, axis=-1, keepdims=True) + MLA_QK * EPS)
        g, c = divmod(h, ATT_HEADS_PER_STEP)
        c *= QK_PAD
        k_ref[g, :, c:c + LANES] = (kn * r * kn_gain).astype(BF16)
        k_ref[g, :, c + LANES:c + QK_PAD] = (kr * r).astype(BF16)


def _mla_prep(proj, pos, invf, qa, kva, wq, wkv, qn, qpa, qpb, kn, kpa, kpb):
    t = proj.shape[0]
    tm = PREP_TM
    row = lambda i: (0, 0)
    vec = lambda n: pl.BlockSpec((1, n), row)
    return pl.pallas_call(
        _mla_prep_kernel,
        out_shape=(jax.ShapeDtypeStruct((ATT_GROUPS, t, ATT_HEADS_PER_STEP * QK_PAD), BF16),
                   jax.ShapeDtypeStruct((ATT_GROUPS, t, ATT_HEADS_PER_STEP * QK_PAD), BF16),
                   jax.ShapeDtypeStruct((MLA_WIDTH, t), BF16)),
        grid=(t // tm,),
        in_specs=[
            pl.BlockSpec((tm, MLA_Q_RANK), lambda i: (i, COL_CQ // MLA_Q_RANK)),
            pl.BlockSpec((tm, MLA_KV_RANK), lambda i: (i, COL_CKV // MLA_KV_RANK)),
            pl.BlockSpec((tm, LANES), lambda i: (i, COL_KPE // LANES)),
            pl.BlockSpec((tm, LANES), lambda i: (i, COL_KPE_SW // LANES)),
            pl.BlockSpec((tm // (LANES // (MLA_ROPE // 2)), LANES), lambda i: (i, 0)),
            vec(LANES),
            vec(MLA_Q_RANK), vec(MLA_KV_RANK),
            pl.BlockSpec(wq.shape, row), pl.BlockSpec(wkv.shape, row),
            vec(LANES), vec(LANES), vec(LANES), vec(LANES), vec(LANES), vec(LANES),
        ],
        out_specs=(pl.BlockSpec((ATT_GROUPS, tm, ATT_HEADS_PER_STEP * QK_PAD), lambda i: (0, i, 0)),
                   pl.BlockSpec((ATT_GROUPS, tm, ATT_HEADS_PER_STEP * QK_PAD), lambda i: (0, i, 0)),
                   pl.BlockSpec((MLA_WIDTH, tm), lambda i: (0, i))),
        compiler_params=_params("parallel"),
        name="mla_prep",
    )(proj, proj, proj, proj, pos, invf, qa, kva, wq, wkv, qn, qpa, qpb, kn, kpa, kpb)


ATT_T = 256
ATT_AHEAD = 5


def _mla_attn_kernel(q_ref, k_ref, vt_ref, o_ref):
    kpos = lax.broadcasted_iota(jnp.int32, (ATT_T, ATT_T), 0)
    qpos = lax.broadcasted_iota(jnp.int32, (ATT_T, ATT_T), 1)
    nq = q_ref.shape[0] // ATT_T

    def scores(unit):
        h, i = unit
        lo, hi = i * ATT_T, (i + 1) * ATT_T
        cols = slice(h * QK_PAD, (h + 1) * QK_PAD)
        q = q_ref[lo:hi, cols]
        s_diag = jnp.where(kpos <= qpos, _dot_nt(k_ref[lo:hi, cols], q), -jnp.inf)
        s_off = _dot_nt(k_ref[0:lo, cols], q) if i > 0 else None
        return s_diag, s_off

    def finish(unit, s_diag, s_off):
        h, i = unit
        lo, hi = i * ATT_T, (i + 1) * ATT_T
        vt = vt_ref.at[h * MLA_V:(h + 1) * MLA_V, :]
        m = jnp.max(s_diag, axis=0, keepdims=True)
        if i > 0:
            m = jnp.maximum(m, jnp.max(s_off, axis=0, keepdims=True))
        p = jnp.exp2(s_diag - m)
        l = jnp.sum(p, axis=0, keepdims=True)
        acc = _dot(vt[:, lo:hi], p.astype(BF16))
        if i > 0:
            p = jnp.exp2(s_off - m)
            l = l + jnp.sum(p, axis=0, keepdims=True)
            acc = acc + _dot(vt[:, 0:lo], p.astype(BF16))
        o_ref[lo:hi, h * MLA_V:(h + 1) * MLA_V] = (acc / l).T.astype(BF16)

    order = [(h, i) for i in reversed(range(nq)) for h in range(ATT_HEADS_PER_STEP)]
    pending = [scores(u) for u in order[:ATT_AHEAD]]
    for n, unit in enumerate(order):
        if n + ATT_AHEAD < len(order):
            pending.append(scores(order[n + ATT_AHEAD]))
        finish(unit, *pending.pop(0))


def _mla_attn(q, k, vt, batch, seq):
    hs = ATT_HEADS_PER_STEP
    return pl.pallas_call(
        _mla_attn_kernel,
        out_shape=jax.ShapeDtypeStruct((ATT_GROUPS, batch * seq, hs * MLA_V), BF16),
        grid=(batch, ATT_GROUPS),
        in_specs=[
            pl.BlockSpec((None, seq, hs * QK_PAD), lambda b, h: (h, b, 0)),
            pl.BlockSpec((None, seq, hs * QK_PAD), lambda b, h: (h, b, 0)),
            pl.BlockSpec((hs * MLA_V, seq), lambda b, h: (h, b)),
        ],
        out_specs=pl.BlockSpec((None, seq, hs * MLA_V), lambda b, h: (h, b, 0)),
        compiler_params=_params("parallel", "parallel"),
        name="mla_attn",
    )(q, k, vt)


def _hgrn_masks():
    t = np.arange(HG_CHUNK)[:, None]
    s = np.arange(HG_CHUNK)[None, :]
    masks = [((t // c) % 2 == 1) & ((s // c) == (t // c) - 1) for c in HG_LEVELS]
    masks.append(t == s)
    return np.stack(masks).astype(np.float32)


def _hgrn_kernel(q_ref, f_ref, i_ref, g_ref, lbl_ref, gn_ref, msk_ref, tri_ref, o_ref, b_all):
    lbl = lbl_ref[...]
    e = jnp.exp(lbl - jnp.max(lbl, axis=0, keepdims=True))
    lb = e[0:1, :] / jnp.sum(e, axis=0, keepdims=True)
    gn = gn_ref[...]
    tri = tri_ref[...]
    sub = lax.broadcasted_iota(jnp.int32, (SUBLANES, LANES), 0)
    zeros8 = jnp.zeros((SUBLANES, LANES), F32)
    per_chunk = HG_CHUNK // SUBLANES
    chunk_rows = [slice(u * HG_CHUNK, (u + 1) * HG_CHUNK) for u in range(HG_UNROLL)]

    def sel(bounds, vals):
        out = vals[-1]
        for bound, val in zip(reversed(bounds), reversed(vals[:-1])):
            out = jnp.where(sub < bound, val, out)
        return out

    def head(gi):
        rows = slice(gi * HG_GROUP, (gi + 1) * HG_GROUP)
        q = q_ref[rows, :]
        v = i_ref[rows, :].astype(BF16)
        f = lb + (1.0 - lb) * _sigmoid(f_ref[rows, :])
        lg = jnp.log2(f)
        k = 1.0 - f

        l1 = lg.astype(BF16)
        l2 = (lg - l1.astype(F32)).astype(BF16)
        slabs = [slice(i, i + HG_SLAB) for i in range(0, HG_GROUP, HG_SLAB)]
        b = jnp.concatenate([_dot(tri, l1[r]) + _dot(tri, l2[r]) for r in slabs], axis=0)
        b_all[gi] = b
        return q, k, f, v, b

    def main(gi, hd, st):
        rows = slice(gi * HG_GROUP, (gi + 1) * HG_GROUP)
        q, k, f, v, b = hd
        b_scr = b_all.at[gi]
        cache = {}

        def rowb(u, r):
            if r < 0:
                return zeros8
            if (u, r) not in cache:
                cache[u, r] = jnp.broadcast_to(b_scr[u * HG_CHUNK + r:u * HG_CHUNK + r + 1, :], (SUBLANES, LANES))
            return cache[u, r]

        def build(fn):
            return jnp.concatenate([fn(u, j) for u in range(HG_UNROLL) for j in range(per_chunk)], axis=0)

        def group_rows(x, u, j):
            r0 = u * HG_CHUNK + j * SUBLANES
            return x[r0:r0 + SUBLANES]

        def level_rows(ci, j):
            if ("m", ci, j) not in cache:
                cache["m", ci, j] = msk_ref[ci, j * SUBLANES:(j + 1) * SUBLANES, :] != 0.0
            return cache["m", ci, j]

        diag = jnp.sum(q * k, axis=-1, keepdims=True)
        a = [[jnp.where(level_rows(len(HG_LEVELS), j), group_rows(diag, u, j), 0.0) for j in range(per_chunk)]
             for u in range(HG_UNROLL)]
        for ci, c in enumerate(HG_LEVELS):
            if c >= SUBLANES:
                m = c // SUBLANES
                right = [j for j in range(per_chunk) if (j // m) % 2 == 1]
                pick = lambda x, js: jnp.concatenate([group_rows(x, u, j) for u in range(HG_UNROLL) for j in js], axis=0)
                bs = jnp.concatenate([rowb(u, SUBLANES * (j // m) * m - 1) for u in range(HG_UNROLL) for j in right], axis=0)
                qd = (pick(q, right) * jnp.exp2(pick(b, right) - bs)).astype(BF16)
                kd = jnp.concatenate(
                    [group_rows(k, u, j) * jnp.exp2(rowb(u, SUBLANES * ((j // m) * m + m) - 1) - group_rows(b, u, j))
                     if (j // m) % 2 == 0 else zeros8
                     for u in range(HG_UNROLL) for j in range(per_chunk)], axis=0).astype(BF16)
                rows_per_chunk = len(right) * SUBLANES
                for u, r in enumerate(chunk_rows):
                    prod = _dot_nt(qd[u * rows_per_chunk:(u + 1) * rows_per_chunk], kd[r])
                    for n, j in enumerate(right):
                        a[u][j] = jnp.where(level_rows(ci, j), prod[n * SUBLANES:(n + 1) * SUBLANES], a[u][j])
                continue
            if c == 4:
                bs = build(lambda u, j: sel((4,), (rowb(u, 8 * j - 1), rowb(u, 8 * j + 3))))
                be = build(lambda u, j: sel((4,), (rowb(u, 8 * j + 3), rowb(u, 8 * j + 7))))
            elif c == 2:
                bs = build(lambda u, j: sel((2, 4, 6), (rowb(u, 8 * j - 1), rowb(u, 8 * j + 1),
                                                         rowb(u, 8 * j + 3), rowb(u, 8 * j + 5))))
                be = build(lambda u, j: sel((2, 4, 6), (rowb(u, 8 * j + 1), rowb(u, 8 * j + 3),
                                                         rowb(u, 8 * j + 5), rowb(u, 8 * j + 7))))
            if c == 1:
                qd = (q * f).astype(BF16)
                kd = k.astype(BF16)
            else:
                qd = (q * jnp.exp2(b - bs)).astype(BF16)
                kd = (k * jnp.exp2(be - b)).astype(BF16)
            for u, r in enumerate(chunk_rows):
                prod = _dot_nt(qd[r], kd[r])
                for j in range(per_chunk):
                    a[u][j] = jnp.where(level_rows(ci, j), prod[j * SUBLANES:(j + 1) * SUBLANES], a[u][j])
        a = [jnp.concatenate(a[u], axis=0) for u in range(HG_UNROLL)]

        q_exp = (q * jnp.exp2(b)).astype(BF16)
        b_last = build(lambda u, j: rowb(u, HG_CHUNK - 1))
        k_dec = (k * jnp.exp2(b_last - b)).astype(BF16)
        o_intra = [_dot(a[u].astype(BF16), v[r]) for u, r in enumerate(chunk_rows)]
        st_add = [_dot_tn(v[r], k_dec[r]) for r in chunk_rows]

        o = []
        for u, r in enumerate(chunk_rows):
            o.append(o_intra[u] + _dot_nt(q_exp[r], st.astype(BF16)))
            st = jnp.exp2(rowb(u, HG_CHUNK - 1)[0:1, :]) * st + st_add[u]
        o = jnp.concatenate(o, axis=0)

        g = g_ref[rows, :]
        o_ref[rows, :] = (o * _rms_scale(o) * gn * (g * _sigmoid(g))).astype(BF16)
        return st

    n_groups = q_ref.shape[0] // HG_GROUP
    st = jnp.zeros((HG_D, HG_D), F32)
    hd = head(0)
    for gi in range(n_groups):
        nxt = head(gi + 1) if gi + 1 < n_groups else None
        st = main(gi, hd, st)
        hd = nxt


def _hgrn(proj, lb_logits, out_norm, batch, seq):
    hblk = lambda col: pl.BlockSpec((seq, HG_D), lambda b, h: (b, col // HG_D + h))
    masks = jnp.asarray(_hgrn_masks())
    tri = np.kron(np.eye(HG_SLAB // HG_CHUNK), np.tril(np.ones((HG_CHUNK, HG_CHUNK))))
    tri = jnp.asarray(tri.astype(np.float32), dtype=BF16)
    nlev = masks.shape[0]
    return pl.pallas_call(
        _hgrn_kernel,
        out_shape=jax.ShapeDtypeStruct((batch * seq, HG_WIDTH), BF16),
        grid=(batch, HG_HEADS),
        in_specs=[
            hblk(COL_HQ), hblk(COL_HF), hblk(COL_HI), hblk(COL_HG),
            pl.BlockSpec((lb_logits.shape[0], HG_D), lambda b, h: (0, h)),
            pl.BlockSpec((1, HG_D), lambda b, h: (0, 0)),
            pl.BlockSpec((nlev, HG_CHUNK, HG_CHUNK), lambda b, h: (0, 0, 0)),
            pl.BlockSpec((HG_SLAB, HG_SLAB), lambda b, h: (0, 0)),
        ],
        out_specs=pl.BlockSpec((seq, HG_D), lambda b, h: (b, h)),
        scratch_shapes=[pltpu.VMEM((seq // HG_GROUP, HG_GROUP, HG_D), F32)],
        compiler_params=_params("parallel", "parallel"),
        name="hgrn2",
    )(proj, proj, proj, proj, lb_logits, out_norm, masks, tri)


def _mem_kv_kernel(m_ref, g_ref, w_ref, kn_ref, k_ref, v_ref):
    m = m_ref[...]
    mn = (m * _rms_scale(m) * g_ref[...]).astype(BF16)
    kv = _dot(mn, w_ref[...].astype(BF16))
    for h in range(MEM_HEADS):
        k = kv[:, 2 * h * MEM_HD:(2 * h + 1) * MEM_HD]
        k_ref[:, h * MEM_HD:(h + 1) * MEM_HD] = (k * _rms_scale(k) * kn_ref[...]).astype(BF16)
        v_ref[:, h * MEM_HD:(h + 1) * MEM_HD] = kv[:, (2 * h + 1) * MEM_HD:(2 * h + 2) * MEM_HD].astype(BF16)


def _mem_kv(mem, gain, w, k_norm, batch, mem_len):
    return pl.pallas_call(
        _mem_kv_kernel,
        out_shape=(jax.ShapeDtypeStruct((batch * mem_len, MEM_WIDTH), BF16),
                   jax.ShapeDtypeStruct((batch * mem_len, MEM_WIDTH), BF16)),
        grid=(batch,),
        in_specs=[
            pl.BlockSpec((mem_len, D_MODEL), lambda b: (b, 0)),
            pl.BlockSpec((1, D_MODEL), lambda b: (0, 0)),
            pl.BlockSpec(w.shape, lambda b: (0, 0)),
            pl.BlockSpec((1, MEM_HD), lambda b: (0, 0)),
        ],
        out_specs=(pl.BlockSpec((mem_len, MEM_WIDTH), lambda b: (b, 0)),
                   pl.BlockSpec((mem_len, MEM_WIDTH), lambda b: (b, 0))),
        compiler_params=_params("parallel"),
        name="mem_kv",
    )(mem, gain, w, k_norm)


XA_TM = 512


def _out_xattn_kernel(a_ref, r_ref, x_ref, og_ref, wout_ref, g_ref, wq_ref, qn_ref, k_ref, v_ref, wo_ref,
                      o_ref, att_scr):
    a = jnp.concatenate([a_ref[g] for g in range(ATT_GROUPS)], axis=-1).astype(F32)
    an = (a * _rms_scale(a) * og_ref[...]).astype(BF16)
    x = (x_ref[...] + _dot(an, wout_ref[:MLA_WIDTH, :].astype(BF16))
         + _dot(r_ref[...], wout_ref[MLA_WIDTH:, :].astype(BF16)))

    h = (x * _rms_scale(x) * g_ref[...]).astype(BF16)
    q = _dot(h, wq_ref[...].astype(BF16))
    heads = [slice(hd * MEM_HD, (hd + 1) * MEM_HD) for hd in range(MEM_HEADS)]
    q_gain = qn_ref[...] * (float(np.sqrt(MEM_HD)) * XATT_Q_SCALE)
    scores = []
    for cols in heads:
        qh = q[:, cols]
        r = lax.rsqrt(jnp.sum(qh * qh, axis=-1, keepdims=True) + MEM_HD * EPS)
        scores.append(_dot_nt((qh * r * q_gain).astype(BF16), k_ref[:, cols]))
    for cols, s in zip(heads, scores):
        p = jnp.exp2(s - jnp.max(s, axis=-1, keepdims=True))
        l = jnp.sum(p, axis=-1, keepdims=True)
        att_scr[:, cols] = (_dot(p.astype(BF16), v_ref[:, cols]) / l).astype(BF16)
    o_ref[...] = x + _dot(att_scr[...], wo_ref[...].astype(BF16))


def _out_xattn(a, r, x, out_gain, w_out, gain, wq, q_norm, kx, vx, wo, seq, mem_len):
    t = x.shape[0]
    per_batch = seq // XA_TM
    tile = lambda n: pl.BlockSpec((XA_TM, n), lambda i: (i, 0))
    whole = lambda w: pl.BlockSpec(w.shape, lambda i: (0, 0), pipeline_mode=pl.Buffered(1))
    vec = lambda n: pl.BlockSpec((1, n), lambda i: (0, 0))
    mem = pl.BlockSpec((mem_len, MEM_WIDTH), lambda i: (i // per_batch, 0))
    groups = pl.BlockSpec((ATT_GROUPS, XA_TM, MLA_WIDTH // ATT_GROUPS), lambda i: (0, i, 0))
    return pl.pallas_call(
        _out_xattn_kernel,
        out_shape=jax.ShapeDtypeStruct((t, D_MODEL), F32),
        grid=(t // XA_TM,),
        in_specs=[groups, tile(HG_WIDTH), tile(D_MODEL), vec(MLA_WIDTH), whole(w_out),
                  vec(D_MODEL), whole(wq), vec(MEM_HD), mem, mem, whole(wo)],
        out_specs=tile(D_MODEL),
        scratch_shapes=[pltpu.VMEM((XA_TM, MEM_WIDTH), BF16)],
        compiler_params=_params("parallel", vmem_limit=WIDE_VMEM_LIMIT),
        name="out_xattn",
    )(a, r, x, out_gain, w_out, gain, wq, q_norm, kx, vx, wo)


def _pe_pair(x1, x2):
    z = jnp.zeros(x1.shape[:-1] + (LANES - MLA_ROPE,), x1.dtype)
    return jnp.concatenate([x1, x2, z], axis=-1), jnp.concatenate([x2, x1, z], axis=-1)


W_IN_TC = 512


def _w_in_prep_kernel(w_ref, o_ref):
    half = MLA_ROPE // 2
    kpe0 = MLA_Q_RANK + MLA_KV_RANK
    hg0 = kpe0 + MLA_ROPE
    zeros = jnp.zeros((LANES - MLA_ROPE, o_ref.shape[1]), BF16)
    o_ref[:kpe0, :] = w_ref[:kpe0, :].astype(BF16)
    o_ref[kpe0:COL_KPE, :] = w_ref[hg0:, :].astype(BF16)
    x1 = w_ref[kpe0:kpe0 + half, :].astype(BF16)
    x2 = w_ref[kpe0 + half:hg0, :].astype(BF16)
    o_ref[COL_KPE:COL_KPE + half, :] = x1
    o_ref[COL_KPE + half:COL_KPE + MLA_ROPE, :] = x2
    o_ref[COL_KPE + MLA_ROPE:COL_KPE_SW, :] = zeros
    o_ref[COL_KPE_SW:COL_KPE_SW + half, :] = x2
    o_ref[COL_KPE_SW + half:COL_KPE_SW + MLA_ROPE, :] = x1
    o_ref[COL_KPE_SW + MLA_ROPE:, :] = zeros


def _prep_w_in(w_in, layer):
    w_t = jnp.swapaxes(w_in, 1, 2)
    _, n, k = w_t.shape
    return pl.pallas_call(
        _w_in_prep_kernel,
        out_shape=jax.ShapeDtypeStruct((IN_COLS_PAD, k), BF16),
        grid=(k // W_IN_TC,),
        in_specs=[pl.BlockSpec((None, n, W_IN_TC), lambda i: (layer, 0, i))],
        out_specs=pl.BlockSpec((IN_COLS_PAD, W_IN_TC), lambda i: (0, i)),
        compiler_params=_params("parallel"),
        name="w_in_prep",
    )(w_t)


def _prep_w_q_up(w):
    w = w.astype(BF16).reshape(MLA_Q_RANK, MLA_HEADS, MLA_QK)
    half = MLA_ROPE // 2
    pe, pe_sw = _pe_pair(w[..., MLA_NOPE:MLA_NOPE + half], w[..., MLA_NOPE + half:])
    return jnp.concatenate([w[..., :MLA_NOPE], pe, pe_sw], axis=-1).reshape(MLA_Q_RANK, MLA_HEADS * Q_UP_PER_HEAD)


def _pe_gains(norm):
    half = MLA_ROPE // 2
    g1 = norm[MLA_NOPE:MLA_NOPE + half]
    g2 = norm[MLA_NOPE + half:]
    ga, gb = _pe_pair(g1, g2)
    sign = jnp.concatenate([-jnp.ones((half,), F32), jnp.ones((LANES - half,), F32)])
    return ga[None, :], (gb * sign)[None, :]


def kernel(x, mem, positions, ffn1_norm, ffn1_w_gate, ffn1_w_up, ffn1_w_down, mix_norm, w_in, mla_q_a_norm, mla_w_q_up, mla_kv_a_norm, mla_w_kv_up, mla_q_norm, mla_k_norm, mla_out_norm, hg_lb_logits, hg_out_norm, w_out, xattn_norm, mem_norm, xattn_w_q, xattn_w_kv, xattn_q_norm, xattn_k_norm, xattn_w_o, ffn2_norm, ffn2_w_gate, ffn2_w_up, ffn2_w_down):
    batch, seq, _ = x.shape
    mem_len = mem.shape[1]
    depth = ffn1_norm.shape[0]
    assert depth == 1 and seq % ATT_T == 0 and seq % XA_TM == 0
    t = batch * seq
    xt = x.reshape(t, D_MODEL)
    half = MLA_ROPE // 2
    quarters = LANES // half
    pos = positions.reshape(t // PREP_TM, quarters, PREP_TM // quarters).transpose(0, 2, 1)
    pos = jnp.repeat(pos, half, axis=-1).reshape(t // quarters, LANES)
    inv_freq = ROPE_BASE ** (-np.arange(half, dtype=np.float32) / half)
    invf = jnp.asarray(np.tile(inv_freq, quarters)[None, :])
    l = 0

    xt = _ffn(xt, ffn1_norm[l][None, :], ffn1_w_gate[l], ffn1_w_up[l], ffn1_w_down[l])

    proj = _in_proj(xt, mix_norm[l][None, :], _prep_w_in(w_in, l))
    qpa, qpb = _pe_gains(mla_q_norm[l])
    kpa, kpb = _pe_gains(mla_k_norm[l])
    q, k, v = _mla_prep(proj, pos, invf, mla_q_a_norm[l][None, :], mla_kv_a_norm[l][None, :],
                        _prep_w_q_up(mla_w_q_up[l]), mla_w_kv_up[l],
                        mla_q_norm[l][None, :MLA_NOPE], qpa, qpb,
                        mla_k_norm[l][None, :MLA_NOPE], kpa, kpb)
    a = _mla_attn(q, k, v, batch, seq)
    r = _hgrn(proj, hg_lb_logits, hg_out_norm[l][None, :], batch, seq)
    kx, vx = _mem_kv(mem.reshape(batch * mem_len, D_MODEL), mem_norm[l][None, :],
                     xattn_w_kv[l], xattn_k_norm[l][None, :], batch, mem_len)
    xt = _out_xattn(a, r, xt, mla_out_norm[l][None, :], w_out[l],
                    xattn_norm[l][None, :], xattn_w_q[l], xattn_q_norm[l][None, :],
                    kx, vx, xattn_w_o[l], seq, mem_len)

    xt = _ffn(xt, ffn2_norm[l][None, :], ffn2_w_gate[l], ffn2_w_up[l], ffn2_w_down[l])
    return xt.reshape(batch, seq, D_MODEL)
```

```python
import numpy as np
import jax
import jax.numpy as jnp
from jax import lax
from jax.experimental import pallas as pl
from jax.experimental.pallas import tpu as pltpu

F32 = jnp.float32
BF16 = jnp.bfloat16

EPS = 1e-6
ROPE_BASE = 10000.0
LANES = 128
SUBLANES = 8

D_MODEL = 2048
D_FF = 5504
FFN_TM = 1024
FFN_TF = 512

MLA_HEADS = 8
MLA_NOPE = 128
MLA_ROPE = 64
MLA_QK = MLA_NOPE + MLA_ROPE
MLA_V = 128
MLA_Q_RANK = 512
MLA_KV_RANK = 256
MLA_WIDTH = MLA_HEADS * MLA_V
QK_PAD = 256
Q_UP_PER_HEAD = 384
ATT_Q_SCALE = float(MLA_QK ** -0.5 * np.log2(np.e))
ATT_HEADS_PER_STEP = 2
ATT_GROUPS = MLA_HEADS // ATT_HEADS_PER_STEP

HG_HEADS = 8
HG_D = 128
HG_CHUNK = 64
HG_LEVELS = (32, 16, 8, 4, 2, 1)
HG_UNROLL = 16
HG_GROUP = HG_CHUNK * HG_UNROLL
HG_SLAB = 256
HG_WIDTH = HG_HEADS * HG_D

MEM_HEADS = 4
MEM_HD = 128
MEM_WIDTH = MEM_HEADS * MEM_HD
XATT_Q_SCALE = float(MEM_HD ** -0.5 * np.log2(np.e))

COL_CQ = 0
COL_CKV = MLA_Q_RANK
COL_HQ = MLA_Q_RANK + MLA_KV_RANK
COL_HF = COL_HQ + HG_WIDTH
COL_HI = COL_HF + HG_WIDTH
COL_HG = COL_HI + HG_WIDTH
COL_KPE = COL_HG + HG_WIDTH
COL_KPE_SW = COL_KPE + LANES
IN_COLS_PAD = COL_KPE_SW + LANES

VMEM_LIMIT = 56 * 1024 * 1024
WIDE_VMEM_LIMIT = 61 * 1024 * 1024


def _params(*sem, vmem_limit=VMEM_LIMIT):
    return pltpu.CompilerParams(dimension_semantics=sem, vmem_limit_bytes=vmem_limit)


def _rms_scale(x):
    return lax.rsqrt(jnp.mean(x * x, axis=-1, keepdims=True) + EPS)


def _sigmoid(x):
    return 1.0 / (1.0 + jnp.exp(-x))


def _dot(a, b):
    return jnp.dot(a, b, preferred_element_type=F32)


def _dot_nt(a, b):
    return lax.dot_general(a, b, (((1,), (1,)), ((), ())), preferred_element_type=F32)


def _dot_tn(a, b):
    return lax.dot_general(a, b, (((0,), (0,)), ((), ())), preferred_element_type=F32)


def _ffn_kernel(x_hbm, g_ref, wg_ref, wu_ref, wd_ref, o_ref, h_scr, x_scr, x_sem):
    i = pl.program_id(0)

    def x_copy(tile):
        rows = pl.ds(pl.multiple_of(tile * FFN_TM, FFN_TM), FFN_TM)
        return pltpu.make_async_copy(x_hbm.at[rows, :], x_scr, x_sem)

    def branch(h):
        gate = _dot(h, wg_ref[...].astype(BF16))
        up = _dot(h, wu_ref[...].astype(BF16))
        a = 0.5 * gate * _sigmoid(gate) * up
        f0 = pl.program_id(1) * FFN_TF
        a_ok = f0 + lax.broadcasted_iota(jnp.int32, (1, FFN_TF), 1) < D_FF
        w_ok = f0 + lax.broadcasted_iota(jnp.int32, (FFN_TF, 1), 0) < D_FF
        a = jnp.where(a_ok, a, 0.0).astype(BF16)
        wd = jnp.where(w_ok, wd_ref[...], 0.0).astype(BF16)
        return _dot(a, wd)

    @pl.when(pl.program_id(1) == 0)
    def _():
        @pl.when(i == 0)
        def _():
            x_copy(0).start()

        x_copy(i).wait()
        x = x_scr[...]
        h = (x * _rms_scale(x) * g_ref[...]).astype(BF16)
        h_scr[...] = h
        o_ref[...] = x + branch(h)

        @pl.when(i + 1 < pl.num_programs(0))
        def _():
            x_copy(i + 1).start()

    @pl.when(pl.program_id(1) > 0)
    def _():
        o_ref[...] += branch(h_scr[...])


def _ffn(x, gain, wg, wu, wd):
    t = x.shape[0]
    return pl.pallas_call(
        _ffn_kernel,
        out_shape=jax.ShapeDtypeStruct((t, D_MODEL), F32),
        grid=(t // FFN_TM, pl.cdiv(D_FF, FFN_TF)),
        in_specs=[
            pl.BlockSpec(memory_space=pl.ANY),
            pl.BlockSpec((1, D_MODEL), lambda i, f: (0, 0)),
            pl.BlockSpec((D_MODEL, FFN_TF), lambda i, f: (0, f)),
            pl.BlockSpec((D_MODEL, FFN_TF), lambda i, f: (0, f)),
            pl.BlockSpec((FFN_TF, D_MODEL), lambda i, f: (f, 0)),
        ],
        out_specs=pl.BlockSpec((FFN_TM, D_MODEL), lambda i, f: (i, 0)),
        scratch_shapes=[pltpu.VMEM((FFN_TM, D_MODEL), BF16), pltpu.VMEM((FFN_TM, D_MODEL), F32),
                        pltpu.SemaphoreType.DMA],
        compiler_params=_params("arbitrary", "arbitrary", vmem_limit=WIDE_VMEM_LIMIT),
        name="ffn",
    )(x, gain, wg, wu, wd)


PROJ_TM = 1024
PROJ_TN = 1280


def _in_proj_kernel(x_hbm, g_ref, w_ref, o_ref, h_scr, x_scr, x_sem):
    i = pl.program_id(0)

    def x_copy(tile):
        rows = pl.ds(pl.multiple_of(tile * PROJ_TM, PROJ_TM), PROJ_TM)
        return pltpu.make_async_copy(x_hbm.at[rows, :], x_scr, x_sem)

    @pl.when(pl.program_id(1) == 0)
    def _():
        @pl.when(i == 0)
        def _():
            x_copy(0).start()

        x_copy(i).wait()
        x = x_scr[...]
        h = (x * _rms_scale(x) * g_ref[...]).astype(BF16)
        h_scr[...] = h
        o_ref[...] = _dot_nt(h, w_ref[...])

        @pl.when(i + 1 < pl.num_programs(0))
        def _():
            x_copy(i + 1).start()

    @pl.when(pl.program_id(1) > 0)
    def _():
        o_ref[...] = _dot_nt(h_scr[...], w_ref[...])


def _in_proj(x, gain, w_t):
    t = x.shape[0]
    n = w_t.shape[0]
    return pl.pallas_call(
        _in_proj_kernel,
        out_shape=jax.ShapeDtypeStruct((t, n), F32),
        grid=(t // PROJ_TM, n // PROJ_TN),
        in_specs=[
            pl.BlockSpec(memory_space=pl.ANY),
            pl.BlockSpec((1, D_MODEL), lambda i, j: (0, 0)),
            pl.BlockSpec((PROJ_TN, D_MODEL), lambda i, j: (j, 0)),
        ],
        out_specs=pl.BlockSpec((PROJ_TM, PROJ_TN), lambda i, j: (i, j)),
        scratch_shapes=[pltpu.VMEM((PROJ_TM, D_MODEL), BF16), pltpu.VMEM((PROJ_TM, D_MODEL), F32),
                        pltpu.SemaphoreType.DMA],
        compiler_params=_params("arbitrary", "arbitrary", vmem_limit=WIDE_VMEM_LIMIT),
        name="in_proj",
    )(x, gain, w_t)


PREP_TM = 1024


def _mla_prep_kernel(cq_ref, ckv_ref, kpe_ref, kpesw_ref, pos_ref, invf_ref,
                     qa_ref, kva_ref, wq_ref, wkv_ref,
                     qn_ref, qpa_ref, qpb_ref, kn_ref, kpa_ref, kpb_ref,
                     q_ref, k_ref, vt_ref):
    ang = pos_ref[...].astype(F32) * invf_ref[...]
    cos4 = jnp.cos(ang)
    sin4 = jnp.sin(ang)
    lane = lax.broadcasted_iota(jnp.int32, (1, LANES), 1)
    half = MLA_ROPE // 2

    def expand(t4):
        parts = []
        for part in range(LANES // half):
            t = pltpu.roll(t4, (LANES - half * part) % LANES, 1) if part else t4
            parts.append(jnp.where(lane < half, t, pltpu.roll(t, half, 1)))
        return jnp.concatenate(parts, axis=0)

    cos = expand(cos4)
    sin = expand(sin4)

    q_const = float(np.sqrt(MLA_QK)) * ATT_Q_SCALE
    k_const = float(np.sqrt(MLA_QK))

    cq = cq_ref[...]
    cqn = (cq * _rms_scale(cq) * qa_ref[...]).astype(BF16)
    qall = _dot(cqn, wq_ref[...])
    qn_gain = qn_ref[...] * q_const
    q_cos = (qpa_ref[...] * q_const) * cos
    q_sin = (qpb_ref[...] * q_const) * sin
    for h in range(MLA_HEADS):
        base = h * Q_UP_PER_HEAD
        qn = qall[:, base:base + LANES]
        y = qall[:, base + LANES:base + 2 * LANES]
        ysw = qall[:, base + 2 * LANES:base + 3 * LANES]
        r = lax.rsqrt(jnp.sum(qn * qn + y * y, axis=-1, keepdims=True) + MLA_QK * EPS)
        g, c = divmod(h, ATT_HEADS_PER_STEP)
        c *= QK_PAD
        q_ref[g, :, c:c + LANES] = (qn * r * qn_gain).astype(BF16)
        q_ref[g, :, c + LANES:c + QK_PAD] = ((y * q_cos + ysw * q_sin) * r).astype(BF16)

    ckv = ckv_ref[...]
    ckvn = (ckv * _rms_scale(ckv) * kva_ref[...]).astype(BF16)
    wkv = wkv_ref[...]
    per_head = MLA_NOPE + MLA_V
    wk = jnp.concatenate([wkv[:, h * per_head:h * per_head + MLA_NOPE] for h in range(MLA_HEADS)], axis=1)
    wv = jnp.concatenate([wkv[:, h * per_head + MLA_NOPE:(h + 1) * per_head] for h in range(MLA_HEADS)], axis=1)
    kn_all = _dot(ckvn, wk.astype(BF16))
    vt_ref[...] = _dot_nt(wv.T.astype(BF16), ckvn).astype(BF16)
    yk = kpe_ref[...]
    kn_gain = kn_ref[...] * k_const
    kr = yk * ((kpa_ref[...] * k_const) * cos) + kpesw_ref[...] * ((kpb_ref[...] * k_const) * sin)
    yk_sq = yk * yk
    for h in range(MLA_HEADS):
        kn = kn_all[:, h * MLA_NOPE:(h + 1) * MLA_NOPE]
        r = lax.rsqrt(jnp.sum(kn * kn + yk_sq, axis=-1, keepdims=True) + MLA_QK * EPS)
        g, c = divmod(h, ATT_HEADS_PER_STEP)
        c *= QK_PAD
        k_ref[g, :, c:c + LANES] = (kn * r * kn_gain).astype(BF16)
        k_ref[g, :, c + LANES:c + QK_PAD] = (kr * r).astype(BF16)


def _mla_prep(proj, pos, invf, qa, kva, wq, wkv, qn, qpa, qpb, kn, kpa, kpb):
    t = proj.shape[0]
    tm = PREP_TM
    row = lambda i: (0, 0)
    vec = lambda n: pl.BlockSpec((1, n), row)
    return pl.pallas_call(
        _mla_prep_kernel,
        out_shape=(jax.ShapeDtypeStruct((ATT_GROUPS, t, ATT_HEADS_PER_STEP * QK_PAD), BF16),
                   jax.ShapeDtypeStruct((ATT_GROUPS, t, ATT_HEADS_PER_STEP * QK_PAD), BF16),
                   jax.ShapeDtypeStruct((MLA_WIDTH, t), BF16)),
        grid=(t // tm,),
        in_specs=[
            pl.BlockSpec((tm, MLA_Q_RANK), lambda i: (i, COL_CQ // MLA_Q_RANK)),
            pl.BlockSpec((tm, MLA_KV_RANK), lambda i: (i, COL_CKV // MLA_KV_RANK)),
            pl.BlockSpec((tm, LANES), lambda i: (i, COL_KPE // LANES)),
            pl.BlockSpec((tm, LANES), lambda i: (i, COL_KPE_SW // LANES)),
            pl.BlockSpec((tm // (LANES // (MLA_ROPE // 2)), LANES), lambda i: (i, 0)),
            vec(LANES),
            vec(MLA_Q_RANK), vec(MLA_KV_RANK),
            pl.BlockSpec(wq.shape, row), pl.BlockSpec(wkv.shape, row),
            vec(LANES), vec(LANES), vec(LANES), vec(LANES), vec(LANES), vec(LANES),
        ],
        out_specs=(pl.BlockSpec((ATT_GROUPS, tm, ATT_HEADS_PER_STEP * QK_PAD), lambda i: (0, i, 0)),
                   pl.BlockSpec((ATT_GROUPS, tm, ATT_HEADS_PER_STEP * QK_PAD), lambda i: (0, i, 0)),
                   pl.BlockSpec((MLA_WIDTH, tm), lambda i: (0, i))),
        compiler_params=_params("parallel"),
        name="mla_prep",
    )(proj, proj, proj, proj, pos, invf, qa, kva, wq, wkv, qn, qpa, qpb, kn, kpa, kpb)


ATT_T = 256
ATT_AHEAD = 5


def _mla_attn_kernel(q_ref, k_ref, vt_ref, o_ref):
    kpos = lax.broadcasted_iota(jnp.int32, (ATT_T, ATT_T), 0)
    qpos = lax.broadcasted_iota(jnp.int32, (ATT_T, ATT_T), 1)
    nq = q_ref.shape[0] // ATT_T

    def scores(unit):
        h, i = unit
        lo, hi = i * ATT_T, (i + 1) * ATT_T
        cols = slice(h * QK_PAD, (h + 1) * QK_PAD)
        q = q_ref[lo:hi, cols]
        s_diag = jnp.where(kpos <= qpos, _dot_nt(k_ref[lo:hi, cols], q), -jnp.inf)
        s_off = _dot_nt(k_ref[0:lo, cols], q) if i > 0 else None
        return s_diag, s_off

    def finish(unit, s_diag, s_off):
        h, i = unit
        lo, hi = i * ATT_T, (i + 1) * ATT_T
        vt = vt_ref.at[h * MLA_V:(h + 1) * MLA_V, :]
        m = jnp.max(s_diag, axis=0, keepdims=True)
        if i > 0:
            m = jnp.maximum(m, jnp.max(s_off, axis=0, keepdims=True))
        p = jnp.exp2(s_diag - m)
        l = jnp.sum(p, axis=0, keepdims=True)
        acc = _dot(vt[:, lo:hi], p.astype(BF16))
        if i > 0:
            p = jnp.exp2(s_off - m)
            l = l + jnp.sum(p, axis=0, keepdims=True)
            acc = acc + _dot(vt[:, 0:lo], p.astype(BF16))
        o_ref[lo:hi, h * MLA_V:(h + 1) * MLA_V] = (acc / l).T.astype(BF16)

    order = [(h, i) for i in reversed(range(nq)) for h in range(ATT_HEADS_PER_STEP)]
    pending = [scores(u) for u in order[:ATT_AHEAD]]
    for n, unit in enumerate(order):
        if n + ATT_AHEAD < len(order):
            pending.append(scores(order[n + ATT_AHEAD]))
        finish(unit, *pending.pop(0))


def _mla_attn(q, k, vt, batch, seq):
    hs = ATT_HEADS_PER_STEP
    return pl.pallas_call(
        _mla_attn_kernel,
        out_shape=jax.ShapeDtypeStruct((ATT_GROUPS, batch * seq, hs * MLA_V), BF16),
        grid=(batch, ATT_GROUPS),
        in_specs=[
            pl.BlockSpec((None, seq, hs * QK_PAD), lambda b, h: (h, b, 0)),
            pl.BlockSpec((None, seq, hs * QK_PAD), lambda b, h: (h, b, 0)),
            pl.BlockSpec((hs * MLA_V, seq), lambda b, h: (h, b)),
        ],
        out_specs=pl.BlockSpec((None, seq, hs * MLA_V), lambda b, h: (h, b, 0)),
        compiler_params=_params("parallel", "parallel"),
        name="mla_attn",
    )(q, k, vt)


def _hgrn_masks():
    t = np.arange(HG_CHUNK)[:, None]
    s = np.arange(HG_CHUNK)[None, :]
    masks = [((t // c) % 2 == 1) & ((s // c) == (t // c) - 1) for c in HG_LEVELS]
    masks.append(t == s)
    return np.stack(masks).astype(np.float32)


def _hgrn_kernel(q_ref, f_ref, i_ref, g_ref, lbl_ref, gn_ref, msk_ref, tri_ref, o_ref, b_all):
    lbl = lbl_ref[...]
    e = jnp.exp(lbl - jnp.max(lbl, axis=0, keepdims=True))
    lb = e[0:1, :] / jnp.sum(e, axis=0, keepdims=True)
    gn = gn_ref[...]
    tri = tri_ref[...]
    sub = lax.broadcasted_iota(jnp.int32, (SUBLANES, LANES), 0)
    zeros8 = jnp.zeros((SUBLANES, LANES), F32)
    per_chunk = HG_CHUNK // SUBLANES
    chunk_rows = [slice(u * HG_CHUNK, (u + 1) * HG_CHUNK) for u in range(HG_UNROLL)]

    def sel(bounds, vals):
        out = vals[-1]
        for bound, val in zip(reversed(bounds), reversed(vals[:-1])):
            out = jnp.where(sub < bound, val, out)
        return out

    def head(gi):
        rows = slice(gi * HG_GROUP, (gi + 1) * HG_GROUP)
        q = q_ref[rows, :]
        v = i_ref[rows, :].astype(BF16)
        f = lb + (1.0 - lb) * _sigmoid(f_ref[rows, :])
        lg = jnp.log2(f)
        k = 1.0 - f

        l1 = lg.astype(BF16)
        l2 = (lg - l1.astype(F32)).astype(BF16)
        slabs = [slice(i, i + HG_SLAB) for i in range(0, HG_GROUP, HG_SLAB)]
        b = jnp.concatenate([_dot(tri, l1[r]) + _dot(tri, l2[r]) for r in slabs], axis=0)
        b_all[gi] = b
        return q, k, f, v, b

    def main(gi, hd, st):
        rows = slice(gi * HG_GROUP, (gi + 1) * HG_GROUP)
        q, k, f, v, b = hd
        b_scr = b_all.at[gi]
        cache = {}

        def rowb(u, r):
            if r < 0:
                return zeros8
            if (u, r) not in cache:
                cache[u, r] = jnp.broadcast_to(b_scr[u * HG_CHUNK + r:u * HG_CHUNK + r + 1, :], (SUBLANES, LANES))
            return cache[u, r]

        def build(fn):
            return jnp.concatenate([fn(u, j) for u in range(HG_UNROLL) for j in range(per_chunk)], axis=0)

        def group_rows(x, u, j):
            r0 = u * HG_CHUNK + j * SUBLANES
            return x[r0:r0 + SUBLANES]

        def level_rows(ci, j):
            if ("m", ci, j) not in cache:
                cache["m", ci, j] = msk_ref[ci, j * SUBLANES:(j + 1) * SUBLANES, :] != 0.0
            return cache["m", ci, j]

        diag = jnp.sum(q * k, axis=-1, keepdims=True)
        a = [[jnp.where(level_rows(len(HG_LEVELS), j), group_rows(diag, u, j), 0.0) for j in range(per_chunk)]
             for u in range(HG_UNROLL)]
        for ci, c in enumerate(HG_LEVELS):
            if c >= SUBLANES:
                m = c // SUBLANES
                right = [j for j in range(per_chunk) if (j // m) % 2 == 1]
                pick = lambda x, js: jnp.concatenate([group_rows(x, u, j) for u in range(HG_UNROLL) for j in js], axis=0)
                bs = jnp.concatenate([rowb(u, SUBLANES * (j // m) * m - 1) for u in range(HG_UNROLL) for j in right], axis=0)
                qd = (pick(q, right) * jnp.exp2(pick(b, right) - bs)).astype(BF16)
                kd = jnp.concatenate(
                    [group_rows(k, u, j) * jnp.exp2(rowb(u, SUBLANES * ((j // m) * m + m) - 1) - group_rows(b, u, j))
                     if (j // m) % 2 == 0 else zeros8
                     for u in range(HG_UNROLL) for j in range(per_chunk)], axis=0).astype(BF16)
                rows_per_chunk = len(right) * SUBLANES
                for u, r in enumerate(chunk_rows):
                    prod = _dot_nt(qd[u * rows_per_chunk:(u + 1) * rows_per_chunk], kd[r])
                    for n, j in enumerate(right):
                        a[u][j] = jnp.where(level_rows(ci, j), prod[n * SUBLANES:(n + 1) * SUBLANES], a[u][j])
                continue
            if c == 4:
                bs = build(lambda u, j: sel((4,), (rowb(u, 8 * j - 1), rowb(u, 8 * j + 3))))
                be = build(lambda u, j: sel((4,), (rowb(u, 8 * j + 3), rowb(u, 8 * j + 7))))
            elif c == 2:
                bs = build(lambda u, j: sel((2, 4, 6), (rowb(u, 8 * j - 1), rowb(u, 8 * j + 1),
                                                         rowb(u, 8 * j + 3), rowb(u, 8 * j + 5))))
                be = build(lambda u, j: sel((2, 4, 6), (rowb(u, 8 * j + 1), rowb(u, 8 * j + 3),
                                                         rowb(u, 8 * j + 5), rowb(u, 8 * j + 7))))
            if c == 1:
                qd = (q * f).astype(BF16)
                kd = k.astype(BF16)
            else:
                qd = (q * jnp.exp2(b - bs)).astype(BF16)
                kd = (k * jnp.exp2(be - b)).astype(BF16)
            for u, r in enumerate(chunk_rows):
                prod = _dot_nt(qd[r], kd[r])
                for j in range(per_chunk):
                    a[u][j] = jnp.where(level_rows(ci, j), prod[j * SUBLANES:(j + 1) * SUBLANES], a[u][j])
        a = [jnp.concatenate(a[u], axis=0) for u in range(HG_UNROLL)]

        q_exp = (q * jnp.exp2(b)).astype(BF16)
        b_last = build(lambda u, j: rowb(u, HG_CHUNK - 1))
        k_dec = (k * jnp.exp2(b_last - b)).astype(BF16)
        o_intra = [_dot(a[u].astype(BF16), v[r]) for u, r in enumerate(chunk_rows)]
        st_add = [_dot_tn(v[r], k_dec[r]) for r in chunk_rows]

        o = []
        for u, r in enumerate(chunk_rows):
            o.append(o_intra[u] + _dot_nt(q_exp[r], st.astype(BF16)))
            st = jnp.exp2(rowb(u, HG_CHUNK - 1)[0:1, :]) * st + st_add[u]
        o = jnp.concatenate(o, axis=0)

        g = g_ref[rows, :]
        o_ref[rows, :] = (o * _rms_scale(o) * gn * (g * _sigmoid(g))).astype(BF16)
        return st

    n_groups = q_ref.shape[0] // HG_GROUP
    st = jnp.zeros((HG_D, HG_D), F32)
    hd = head(0)
    for gi in range(n_groups):
        nxt = head(gi + 1) if gi + 1 < n_groups else None
        st = main(gi, hd, st)
        hd = nxt


def _hgrn(proj, lb_logits, out_norm, batch, seq):
    hblk = lambda col: pl.BlockSpec((seq, HG_D), lambda b, h: (b, col // HG_D + h))
    masks = jnp.asarray(_hgrn_masks())
    tri = np.kron(np.eye(HG_SLAB // HG_CHUNK), np.tril(np.ones((HG_CHUNK, HG_CHUNK))))
    tri = jnp.asarray(tri.astype(np.float32), dtype=BF16)
    nlev = masks.shape[0]
    return pl.pallas_call(
        _hgrn_kernel,
        out_shape=jax.ShapeDtypeStruct((batch * seq, HG_WIDTH), BF16),
        grid=(batch, HG_HEADS),
        in_specs=[
            hblk(COL_HQ), hblk(COL_HF), hblk(COL_HI), hblk(COL_HG),
            pl.BlockSpec((lb_logits.shape[0], HG_D), lambda b, h: (0, h)),
            pl.BlockSpec((1, HG_D), lambda b, h: (0, 0)),
            pl.BlockSpec((nlev, HG_CHUNK, HG_CHUNK), lambda b, h: (0, 0, 0)),
            pl.BlockSpec((HG_SLAB, HG_SLAB), lambda b, h: (0, 0)),
        ],
        out_specs=pl.BlockSpec((seq, HG_D), lambda b, h: (b, h)),
        scratch_shapes=[pltpu.VMEM((seq // HG_GROUP, HG_GROUP, HG_D), F32)],
        compiler_params=_params("parallel", "parallel"),
        name="hgrn2",
    )(proj, proj, proj, proj, lb_logits, out_norm, masks, tri)


def _mem_kv_kernel(m_ref, g_ref, w_ref, kn_ref, k_ref, v_ref):
    m = m_ref[...]
    mn = (m * _rms_scale(m) * g_ref[...]).astype(BF16)
    kv = _dot(mn, w_ref[...].astype(BF16))
    for h in range(MEM_HEADS):
        k = kv[:, 2 * h * MEM_HD:(2 * h + 1) * MEM_HD]
        k_ref[:, h * MEM_HD:(h + 1) * MEM_HD] = (k * _rms_scale(k) * kn_ref[...]).astype(BF16)
        v_ref[:, h * MEM_HD:(h + 1) * MEM_HD] = kv[:, (2 * h + 1) * MEM_HD:(2 * h + 2) * MEM_HD].astype(BF16)


def _mem_kv(mem, gain, w, k_norm, batch, mem_len):
    return pl.pallas_call(
        _mem_kv_kernel,
        out_shape=(jax.ShapeDtypeStruct((batch * mem_len, MEM_WIDTH), BF16),
                   jax.ShapeDtypeStruct((batch * mem_len, MEM_WIDTH), BF16)),
        grid=(batch,),
        in_specs=[
            pl.BlockSpec((mem_len, D_MODEL), lambda b: (b, 0)),
            pl.BlockSpec((1, D_MODEL), lambda b: (0, 0)),
            pl.BlockSpec(w.shape, lambda b: (0, 0)),
            pl.BlockSpec((1, MEM_HD), lambda b: (0, 0)),
        ],
        out_specs=(pl.BlockSpec((mem_len, MEM_WIDTH), lambda b: (b, 0)),
                   pl.BlockSpec((mem_len, MEM_WIDTH), lambda b: (b, 0))),
        compiler_params=_params("parallel"),
        name="mem_kv",
    )(mem, gain, w, k_norm)


XA_TM = 512


def _out_xattn_kernel(a_ref, r_ref, x_ref, og_ref, wout_ref, g_ref, wq_ref, qn_ref, k_ref, v_ref, wo_ref,
                      o_ref, att_scr):
    a = jnp.concatenate([a_ref[g] for g in range(ATT_GROUPS)], axis=-1).astype(F32)
    an = (a * _rms_scale(a) * og_ref[...]).astype(BF16)
    x = (x_ref[...] + _dot(an, wout_ref[:MLA_WIDTH, :].astype(BF16))
         + _dot(r_ref[...], wout_ref[MLA_WIDTH:, :].astype(BF16)))

    h = (x * _rms_scale(x) * g_ref[...]).astype(BF16)
    q = _dot(h, wq_ref[...].astype(BF16))
    heads = [slice(hd * MEM_HD, (hd + 1) * MEM_HD) for hd in range(MEM_HEADS)]
    q_gain = qn_ref[...] * (float(np.sqrt(MEM_HD)) * XATT_Q_SCALE)
    scores = []
    for cols in heads:
        qh = q[:, cols]
        r = lax.rsqrt(jnp.sum(qh * qh, axis=-1, keepdims=True) + MEM_HD * EPS)
        scores.append(_dot_nt((qh * r * q_gain).astype(BF16), k_ref[:, cols]))
    for cols, s in zip(heads, scores):
        p = jnp.exp2(s - jnp.max(s, axis=-1, keepdims=True))
        l = jnp.sum(p, axis=-1, keepdims=True)
        att_scr[:, cols] = (_dot(p.astype(BF16), v_ref[:, cols]) / l).astype(BF16)
    o_ref[...] = x + _dot(att_scr[...], wo_ref[...].astype(BF16))


def _out_xattn(a, r, x, out_gain, w_out, gain, wq, q_norm, kx, vx, wo, seq, mem_len):
    t = x.shape[0]
    per_batch = seq // XA_TM
    tile = lambda n: pl.BlockSpec((XA_TM, n), lambda i: (i, 0))
    whole = lambda w: pl.BlockSpec(w.shape, lambda i: (0, 0), pipeline_mode=pl.Buffered(1))
    vec = lambda n: pl.BlockSpec((1, n), lambda i: (0, 0))
    mem = pl.BlockSpec((mem_len, MEM_WIDTH), lambda i: (i // per_batch, 0))
    groups = pl.BlockSpec((ATT_GROUPS, XA_TM, MLA_WIDTH // ATT_GROUPS), lambda i: (0, i, 0))
    return pl.pallas_call(
        _out_xattn_kernel,
        out_shape=jax.ShapeDtypeStruct((t, D_MODEL), F32),
        grid=(t // XA_TM,),
        in_specs=[groups, tile(HG_WIDTH), tile(D_MODEL), vec(MLA_WIDTH), whole(w_out),
                  vec(D_MODEL), whole(wq), vec(MEM_HD), mem, mem, whole(wo)],
        out_specs=tile(D_MODEL),
        scratch_shapes=[pltpu.VMEM((XA_TM, MEM_WIDTH), BF16)],
        compiler_params=_params("parallel", vmem_limit=WIDE_VMEM_LIMIT),
        name="out_xattn",
    )(a, r, x, out_gain, w_out, gain, wq, q_norm, kx, vx, wo)


def _pe_pair(x1, x2):
    z = jnp.zeros(x1.shape[:-1] + (LANES - MLA_ROPE,), x1.dtype)
    return jnp.concatenate([x1, x2, z], axis=-1), jnp.concatenate([x2, x1, z], axis=-1)


W_IN_TC = 512


def _w_in_prep_kernel(w_ref, o_ref):
    half = MLA_ROPE // 2
    kpe0 = MLA_Q_RANK + MLA_KV_RANK
    hg0 = kpe0 + MLA_ROPE
    zeros = jnp.zeros((LANES - MLA_ROPE, o_ref.shape[1]), BF16)
    o_ref[:kpe0, :] = w_ref[:kpe0, :].astype(BF16)
    o_ref[kpe0:COL_KPE, :] = w_ref[hg0:, :].astype(BF16)
    x1 = w_ref[kpe0:kpe0 + half, :].astype(BF16)
    x2 = w_ref[kpe0 + half:hg0, :].astype(BF16)
    o_ref[COL_KPE:COL_KPE + half, :] = x1
    o_ref[COL_KPE + half:COL_KPE + MLA_ROPE, :] = x2
    o_ref[COL_KPE + MLA_ROPE:COL_KPE_SW, :] = zeros
    o_ref[COL_KPE_SW:COL_KPE_SW + half, :] = x2
    o_ref[COL_KPE_SW + half:COL_KPE_SW + MLA_ROPE, :] = x1
    o_ref[COL_KPE_SW + MLA_ROPE:, :] = zeros


def _prep_w_in(w_in, layer):
    w_t = jnp.swapaxes(w_in, 1, 2)
    _, n, k = w_t.shape
    return pl.pallas_call(
        _w_in_prep_kernel,
        out_shape=jax.ShapeDtypeStruct((IN_COLS_PAD, k), BF16),
        grid=(k // W_IN_TC,),
        in_specs=[pl.BlockSpec((None, n, W_IN_TC), lambda i: (layer, 0, i))],
        out_specs=pl.BlockSpec((IN_COLS_PAD, W_IN_TC), lambda i: (0, i)),
        compiler_params=_params("parallel"),
        name="w_in_prep",
    )(w_t)


def _prep_w_q_up(w):
    w = w.astype(BF16).reshape(MLA_Q_RANK, MLA_HEADS, MLA_QK)
    half = MLA_ROPE // 2
    pe, pe_sw = _pe_pair(w[..., MLA_NOPE:MLA_NOPE + half], w[..., MLA_NOPE + half:])
    return jnp.concatenate([w[..., :MLA_NOPE], pe, pe_sw], axis=-1).reshape(MLA_Q_RANK, MLA_HEADS * Q_UP_PER_HEAD)


def _pe_gains(norm):
    half = MLA_ROPE // 2
    g1 = norm[MLA_NOPE:MLA_NOPE + half]
    g2 = norm[MLA_NOPE + half:]
    ga, gb = _pe_pair(g1, g2)
    sign = jnp.concatenate([-jnp.ones((half,), F32), jnp.ones((LANES - half,), F32)])
    return ga[None, :], (gb * sign)[None, :]


def kernel(x, mem, positions, ffn1_norm, ffn1_w_gate, ffn1_w_up, ffn1_w_down, mix_norm, w_in, mla_q_a_norm, mla_w_q_up, mla_kv_a_norm, mla_w_kv_up, mla_q_norm, mla_k_norm, mla_out_norm, hg_lb_logits, hg_out_norm, w_out, xattn_norm, mem_norm, xattn_w_q, xattn_w_kv, xattn_q_norm, xattn_k_norm, xattn_w_o, ffn2_norm, ffn2_w_gate, ffn2_w_up, ffn2_w_down):
    batch, seq, _ = x.shape
    mem_len = mem.shape[1]
    depth = ffn1_norm.shape[0]
    assert depth == 1 and seq % ATT_T == 0 and seq % XA_TM == 0
    t = batch * seq
    xt = x.reshape(t, D_MODEL)
    half = MLA_ROPE // 2
    quarters = LANES // half
    pos = positions.reshape(t // PREP_TM, quarters, PREP_TM // quarters).transpose(0, 2, 1)
    pos = jnp.repeat(pos, half, axis=-1).reshape(t // quarters, LANES)
    inv_freq = ROPE_BASE ** (-np.arange(half, dtype=np.float32) / half)
    invf = jnp.asarray(np.tile(inv_freq, quarters)[None, :])
    l = 0

    xt = _ffn(xt, ffn1_norm[l][None, :], ffn1_w_gate[l], ffn1_w_up[l], ffn1_w_down[l])

    proj = _in_proj(xt, mix_norm[l][None, :], _prep_w_in(w_in, l))
    qpa, qpb = _pe_gains(mla_q_norm[l])
    kpa, kpb = _pe_gains(mla_k_norm[l])
    q, k, v = _mla_prep(proj, pos, invf, mla_q_a_norm[l][None, :], mla_kv_a_norm[l][None, :],
                        _prep_w_q_up(mla_w_q_up[l]), mla_w_kv_up[l],
                        mla_q_norm[l][None, :MLA_NOPE], qpa, qpb,
                        mla_k_norm[l][None, :MLA_NOPE], kpa, kpb)
    a = _mla_attn(q, k, v, batch, seq)
    r = _hgrn(proj, hg_lb_logits, hg_out_norm[l][None, :], batch, seq)
    kx, vx = _mem_kv(mem.reshape(batch * mem_len, D_MODEL), mem_norm[l][None, :],
                     xattn_w_kv[l], xattn_k_norm[l][None, :], batch, mem_len)
    xt = _out_xattn(a, r, xt, mla_out_norm[l][None, :], w_out[l],
                    xattn_norm[l][None, :], xattn_w_q[l], xattn_q_norm[l][None, :],
                    kx, vx, xattn_w_o[l], seq, mem_len)

    xt = _ffn(xt, ffn2_norm[l][None, :], ffn2_w_gate[l], ffn2_w_up[l], ffn2_w_down[l])
    return xt.reshape(batch, seq, D_MODEL)
```

```python
import numpy as np
import jax
import jax.numpy as jnp
from jax import lax
from jax.experimental import pallas as pl
from jax.experimental.pallas import tpu as pltpu

F32 = jnp.float32
BF16 = jnp.bfloat16

EPS = 1e-6
ROPE_BASE = 10000.0
LANES = 128
SUBLANES = 8

D_MODEL = 2048
D_FF = 5504
FFN_TM = 1024
FFN_TF = 512

MLA_HEADS = 8
MLA_NOPE = 128
MLA_ROPE = 64
MLA_QK = MLA_NOPE + MLA_ROPE
MLA_V = 128
MLA_Q_RANK = 512
MLA_KV_RANK = 256
MLA_WIDTH = MLA_HEADS * MLA_V
QK_PAD = 256
Q_UP_PER_HEAD = 384
ATT_Q_SCALE = float(MLA_QK ** -0.5 * np.log2(np.e))
ATT_HEADS_PER_STEP = 2
ATT_GROUPS = MLA_HEADS // ATT_HEADS_PER_STEP

HG_HEADS = 8
HG_D = 128
HG_CHUNK = 64
HG_LEVELS = (32, 16, 8, 4, 2, 1)
HG_UNROLL = 16
HG_GROUP = HG_CHUNK * HG_UNROLL
HG_SLAB = 256
HG_WIDTH = HG_HEADS * HG_D

MEM_HEADS = 4
MEM_HD = 128
MEM_WIDTH = MEM_HEADS * MEM_HD
XATT_Q_SCALE = float(MEM_HD ** -0.5 * np.log2(np.e))

COL_CQ = 0
COL_CKV = MLA_Q_RANK
COL_HQ = MLA_Q_RANK + MLA_KV_RANK
COL_HF = COL_HQ + HG_WIDTH
COL_HI = COL_HF + HG_WIDTH
COL_HG = COL_HI + HG_WIDTH
COL_KPE = COL_HG + HG_WIDTH
COL_KPE_SW = COL_KPE + LANES
IN_COLS_PAD = COL_KPE_SW + LANES

VMEM_LIMIT = 56 * 1024 * 1024
WIDE_VMEM_LIMIT = 61 * 1024 * 1024


def _params(*sem, vmem_limit=VMEM_LIMIT):
    return pltpu.CompilerParams(dimension_semantics=sem, vmem_limit_bytes=vmem_limit)


def _rms_scale(x):
    return lax.rsqrt(jnp.mean(x * x, axis=-1, keepdims=True) + EPS)


def _sigmoid(x):
    return 1.0 / (1.0 + jnp.exp(-x))


def _dot(a, b):
    return jnp.dot(a, b, preferred_element_type=F32)


def _dot_nt(a, b):
    return lax.dot_general(a, b, (((1,), (1,)), ((), ())), preferred_element_type=F32)


def _dot_tn(a, b):
    return lax.dot_general(a, b, (((0,), (0,)), ((), ())), preferred_element_type=F32)


def _ffn_kernel(x_hbm, g_ref, wg_ref, wu_ref, wd_ref, o_ref, h_scr, x_scr, x_sem):
    i = pl.program_id(0)

    def x_copy(tile):
        rows = pl.ds(pl.multiple_of(tile * FFN_TM, FFN_TM), FFN_TM)
        return pltpu.make_async_copy(x_hbm.at[rows, :], x_scr, x_sem)

    def branch(h):
        gate = _dot(h, wg_ref[...].astype(BF16))
        up = _dot(h, wu_ref[...].astype(BF16))
        a = 0.5 * gate * _sigmoid(gate) * up
        f0 = pl.program_id(1) * FFN_TF
        a_ok = f0 + lax.broadcasted_iota(jnp.int32, (1, FFN_TF), 1) < D_FF
        w_ok = f0 + lax.broadcasted_iota(jnp.int32, (FFN_TF, 1), 0) < D_FF
        a = jnp.where(a_ok, a, 0.0).astype(BF16)
        wd = jnp.where(w_ok, wd_ref[...], 0.0).astype(BF16)
        return _dot(a, wd)

    @pl.when(pl.program_id(1) == 0)
    def _():
        @pl.when(i == 0)
        def _():
            x_copy(0).start()

        x_copy(i).wait()
        x = x_scr[...]
        h = (x * _rms_scale(x) * g_ref[...]).astype(BF16)
        h_scr[...] = h
        o_ref[...] = x + branch(h)

        @pl.when(i + 1 < pl.num_programs(0))
        def _():
            x_copy(i + 1).start()

    @pl.when(pl.program_id(1) > 0)
    def _():
        o_ref[...] += branch(h_scr[...])


def _ffn(x, gain, wg, wu, wd):
    t = x.shape[0]
    return pl.pallas_call(
        _ffn_kernel,
        out_shape=jax.ShapeDtypeStruct((t, D_MODEL), F32),
        grid=(t // FFN_TM, pl.cdiv(D_FF, FFN_TF)),
        in_specs=[
            pl.BlockSpec(memory_space=pl.ANY),
            pl.BlockSpec((1, D_MODEL), lambda i, f: (0, 0)),
            pl.BlockSpec((D_MODEL, FFN_TF), lambda i, f: (0, f)),
            pl.BlockSpec((D_MODEL, FFN_TF), lambda i, f: (0, f)),
            pl.BlockSpec((FFN_TF, D_MODEL), lambda i, f: (f, 0)),
        ],
        out_specs=pl.BlockSpec((FFN_TM, D_MODEL), lambda i, f: (i, 0)),
        scratch_shapes=[pltpu.VMEM((FFN_TM, D_MODEL), BF16), pltpu.VMEM((FFN_TM, D_MODEL), F32),
                        pltpu.SemaphoreType.DMA],
        compiler_params=_params("arbitrary", "arbitrary", vmem_limit=WIDE_VMEM_LIMIT),
        name="ffn",
    )(x, gain, wg, wu, wd)


PROJ_TM = 1024
PROJ_TN = 1024


def _in_proj_kernel(x_hbm, g_ref, w_ref, o_ref, h_scr, x_scr, x_sem):
    i = pl.program_id(0)

    def x_copy(tile):
        rows = pl.ds(pl.multiple_of(tile * PROJ_TM, PROJ_TM), PROJ_TM)
        return pltpu.make_async_copy(x_hbm.at[rows, :], x_scr, x_sem)

    @pl.when(pl.program_id(1) == 0)
    def _():
        @pl.when(i == 0)
        def _():
            x_copy(0).start()

        x_copy(i).wait()
        x = x_scr[...]
        h = (x * _rms_scale(x) * g_ref[...]).astype(BF16)
        h_scr[...] = h
        o_ref[...] = _dot_nt(h, w_ref[...])

        @pl.when(i + 1 < pl.num_programs(0))
        def _():
            x_copy(i + 1).start()

    @pl.when(pl.program_id(1) > 0)
    def _():
        o_ref[...] = _dot_nt(h_scr[...], w_ref[...])


def _in_proj(x, gain, w_t):
    t = x.shape[0]
    n = w_t.shape[0]
    return pl.pallas_call(
        _in_proj_kernel,
        out_shape=jax.ShapeDtypeStruct((t, n), F32),
        grid=(t // PROJ_TM, n // PROJ_TN),
        in_specs=[
            pl.BlockSpec(memory_space=pl.ANY),
            pl.BlockSpec((1, D_MODEL), lambda i, j: (0, 0)),
            pl.BlockSpec((PROJ_TN, D_MODEL), lambda i, j: (j, 0)),
        ],
        out_specs=pl.BlockSpec((PROJ_TM, PROJ_TN), lambda i, j: (i, j)),
        scratch_shapes=[pltpu.VMEM((PROJ_TM, D_MODEL), BF16), pltpu.VMEM((PROJ_TM, D_MODEL), F32),
                        pltpu.SemaphoreType.DMA],
        compiler_params=_params("arbitrary", "arbitrary", vmem_limit=WIDE_VMEM_LIMIT),
        name="in_proj",
    )(x, gain, w_t)


PREP_TM = 1024


def _mla_prep_kernel(cq_ref, ckv_ref, kpe_ref, kpesw_ref, pos_ref, invf_ref,
                     qa_ref, kva_ref, wq_ref, wkv_ref,
                     qn_ref, qpa_ref, qpb_ref, kn_ref, kpa_ref, kpb_ref,
                     q_ref, k_ref, vt_ref):
    ang = pos_ref[...].astype(F32) * invf_ref[...]
    cos4 = jnp.cos(ang)
    sin4 = jnp.sin(ang)
    lane = lax.broadcasted_iota(jnp.int32, (1, LANES), 1)
    half = MLA_ROPE // 2

    def expand(t4):
        parts = []
        for part in range(LANES // half):
            t = pltpu.roll(t4, (LANES - half * part) % LANES, 1) if part else t4
            parts.append(jnp.where(lane < half, t, pltpu.roll(t, half, 1)))
        return jnp.concatenate(parts, axis=0)

    cos = expand(cos4)
    sin = expand(sin4)

    q_const = float(np.sqrt(MLA_QK)) * ATT_Q_SCALE
    k_const = float(np.sqrt(MLA_QK))

    cq = cq_ref[...]
    cqn = (cq * _rms_scale(cq) * qa_ref[...]).astype(BF16)
    qall = _dot(cqn, wq_ref[...])
    qn_gain = qn_ref[...] * q_const
    q_cos = (qpa_ref[...] * q_const) * cos
    q_sin = (qpb_ref[...] * q_const) * sin
    for h in range(MLA_HEADS):
        base = h * Q_UP_PER_HEAD
        qn = qall[:, base:base + LANES]
        y = qall[:, base + LANES:base + 2 * LANES]
        ysw = qall[:, base + 2 * LANES:base + 3 * LANES]
        r = lax.rsqrt(jnp.sum(qn * qn + y * y, axis=-1, keepdims=True) + MLA_QK * EPS)
        g, c = divmod(h, ATT_HEADS_PER_STEP)
        c *= QK_PAD
        q_ref[g, :, c:c + LANES] = (qn * r * qn_gain).astype(BF16)
        q_ref[g, :, c + LANES:c + QK_PAD] = ((y * q_cos + ysw * q_sin) * r).astype(BF16)

    ckv = ckv_ref[...]
    ckvn = (ckv * _rms_scale(ckv) * kva_ref[...]).astype(BF16)
    wkv = wkv_ref[...]
    per_head = MLA_NOPE + MLA_V
    wk = jnp.concatenate([wkv[:, h * per_head:h * per_head + MLA_NOPE] for h in range(MLA_HEADS)], axis=1)
    wv = jnp.concatenate([wkv[:, h * per_head + MLA_NOPE:(h + 1) * per_head] for h in range(MLA_HEADS)], axis=1)
    kn_all = _dot(ckvn, wk.astype(BF16))
    vt_ref[...] = _dot_nt(wv.T.astype(BF16), ckvn).astype(BF16)
    yk = kpe_ref[...]
    kn_gain = kn_ref[...] * k_const
    kr = yk * ((kpa_ref[...] * k_const) * cos) + kpesw_ref[...] * ((kpb_ref[...] * k_const) * sin)
    yk_sq = yk * yk
    for h in range(MLA_HEADS):
        kn = kn_all[:, h * MLA_NOPE:(h + 1) * MLA_NOPE]
        r = lax.rsqrt(jnp.sum(kn * kn + yk_sq, axis=-1, keepdims=True) + MLA_QK * EPS)
        g, c = divmod(h, ATT_HEADS_PER_STEP)
        c *= QK_PAD
        k_ref[g, :, c:c + LANES] = (kn * r * kn_gain).astype(BF16)
        k_ref[g, :, c + LANES:c + QK_PAD] = (kr * r).astype(BF16)


def _mla_prep(proj, pos, invf, qa, kva, wq, wkv, qn, qpa, qpb, kn, kpa, kpb):
    t = proj.shape[0]
    tm = PREP_TM
    row = lambda i: (0, 0)
    vec = lambda n: pl.BlockSpec((1, n), row)
    return pl.pallas_call(
        _mla_prep_kernel,
        out_shape=(jax.ShapeDtypeStruct((ATT_GROUPS, t, ATT_HEADS_PER_STEP * QK_PAD), BF16),
                   jax.ShapeDtypeStruct((ATT_GROUPS, t, ATT_HEADS_PER_STEP * QK_PAD), BF16),
                   jax.ShapeDtypeStruct((MLA_WIDTH, t), BF16)),
        grid=(t // tm,),
        in_specs=[
            pl.BlockSpec((tm, MLA_Q_RANK), lambda i: (i, COL_CQ // MLA_Q_RANK)),
            pl.BlockSpec((tm, MLA_KV_RANK), lambda i: (i, COL_CKV // MLA_KV_RANK)),
            pl.BlockSpec((tm, LANES), lambda i: (i, COL_KPE // LANES)),
            pl.BlockSpec((tm, LANES), lambda i: (i, COL_KPE_SW // LANES)),
            pl.BlockSpec((tm // (LANES // (MLA_ROPE // 2)), LANES), lambda i: (i, 0)),
            vec(LANES),
            vec(MLA_Q_RANK), vec(MLA_KV_RANK),
            pl.BlockSpec(wq.shape, row), pl.BlockSpec(wkv.shape, row),
            vec(LANES), vec(LANES), vec(LANES), vec(LANES), vec(LANES), vec(LANES),
        ],
        out_specs=(pl.BlockSpec((ATT_GROUPS, tm, ATT_HEADS_PER_STEP * QK_PAD), lambda i: (0, i, 0)),
                   pl.BlockSpec((ATT_GROUPS, tm, ATT_HEADS_PER_STEP * QK_PAD), lambda i: (0, i, 0)),
                   pl.BlockSpec((MLA_WIDTH, tm), lambda i: (0, i))),
        compiler_params=_params("parallel"),
        name="mla_prep",
    )(proj, proj, proj, proj, pos, invf, qa, kva, wq, wkv, qn, qpa, qpb, kn, kpa, kpb)


ATT_T = 256
ATT_AHEAD = 5


def _mla_attn_kernel(q_ref, k_ref, vt_ref, o_ref):
    kpos = lax.broadcasted_iota(jnp.int32, (ATT_T, ATT_T), 0)
    qpos = lax.broadcasted_iota(jnp.int32, (ATT_T, ATT_T), 1)
    nq = q_ref.shape[0] // ATT_T

    def scores(unit):
        h, i = unit
        lo, hi = i * ATT_T, (i + 1) * ATT_T
        cols = slice(h * QK_PAD, (h + 1) * QK_PAD)
        q = q_ref[lo:hi, cols]
        s_diag = jnp.where(kpos <= qpos, _dot_nt(k_ref[lo:hi, cols], q), -jnp.inf)
        s_off = _dot_nt(k_ref[0:lo, cols], q) if i > 0 else None
        return s_diag, s_off

    def finish(unit, s_diag, s_off):
        h, i = unit
        lo, hi = i * ATT_T, (i + 1) * ATT_T
        vt = vt_ref.at[h * MLA_V:(h + 1) * MLA_V, :]
        m = jnp.max(s_diag, axis=0, keepdims=True)
        if i > 0:
            m = jnp.maximum(m, jnp.max(s_off, axis=0, keepdims=True))
        p = jnp.exp2(s_diag - m)
        l = jnp.sum(p, axis=0, keepdims=True)
        acc = _dot(vt[:, lo:hi], p.astype(BF16))
        if i > 0:
            p = jnp.exp2(s_off - m)
            l = l + jnp.sum(p, axis=0, keepdims=True)
            acc = acc + _dot(vt[:, 0:lo], p.astype(BF16))
        o_ref[lo:hi, h * MLA_V:(h + 1) * MLA_V] = (acc / l).T.astype(BF16)

    order = [(h, i) for i in reversed(range(nq)) for h in range(ATT_HEADS_PER_STEP)]
    pending = [scores(u) for u in order[:ATT_AHEAD]]
    for n, unit in enumerate(order):
        if n + ATT_AHEAD < len(order):
            pending.append(scores(order[n + ATT_AHEAD]))
        finish(unit, *pending.pop(0))


def _mla_attn(q, k, vt, batch, seq):
    hs = ATT_HEADS_PER_STEP
    return pl.pallas_call(
        _mla_attn_kernel,
        out_shape=jax.ShapeDtypeStruct((ATT_GROUPS, batch * seq, hs * MLA_V), BF16),
        grid=(batch, ATT_GROUPS),
        in_specs=[
            pl.BlockSpec((None, seq, hs * QK_PAD), lambda b, h: (h, b, 0)),
            pl.BlockSpec((None, seq, hs * QK_PAD), lambda b, h: (h, b, 0)),
            pl.BlockSpec((hs * MLA_V, seq), lambda b, h: (h, b)),
        ],
        out_specs=pl.BlockSpec((None, seq, hs * MLA_V), lambda b, h: (h, b, 0)),
        compiler_params=_params("parallel", "parallel"),
        name="mla_attn",
    )(q, k, vt)


def _hgrn_masks():
    t = np.arange(HG_CHUNK)[:, None]
    s = np.arange(HG_CHUNK)[None, :]
    masks = [((t // c) % 2 == 1) & ((s // c) == (t // c) - 1) for c in HG_LEVELS]
    masks.append(t == s)
    return np.stack(masks).astype(np.float32)


def _hgrn_kernel(q_ref, f_ref, i_ref, g_ref, lbl_ref, gn_ref, msk_ref, tri_ref, o_ref, b_all):
    lbl = lbl_ref[...]
    e = jnp.exp(lbl - jnp.max(lbl, axis=0, keepdims=True))
    lb = e[0:1, :] / jnp.sum(e, axis=0, keepdims=True)
    gn = gn_ref[...]
    tri = tri_ref[...]
    sub = lax.broadcasted_iota(jnp.int32, (SUBLANES, LANES), 0)
    zeros8 = jnp.zeros((SUBLANES, LANES), F32)
    per_chunk = HG_CHUNK // SUBLANES
    chunk_rows = [slice(u * HG_CHUNK, (u + 1) * HG_CHUNK) for u in range(HG_UNROLL)]

    def sel(bounds, vals):
        out = vals[-1]
        for bound, val in zip(reversed(bounds), reversed(vals[:-1])):
            out = jnp.where(sub < bound, val, out)
        return out

    def head(gi):
        rows = slice(gi * HG_GROUP, (gi + 1) * HG_GROUP)
        q = q_ref[rows, :]
        v = i_ref[rows, :].astype(BF16)
        f = lb + (1.0 - lb) * _sigmoid(f_ref[rows, :])
        lg = jnp.log2(f)
        k = 1.0 - f

        l1 = lg.astype(BF16)
        l2 = (lg - l1.astype(F32)).astype(BF16)
        slabs = [slice(i, i + HG_SLAB) for i in range(0, HG_GROUP, HG_SLAB)]
        b = jnp.concatenate([_dot(tri, l1[r]) + _dot(tri, l2[r]) for r in slabs], axis=0)
        b_all[gi] = b
        return q, k, f, v, b

    def main(gi, hd, st):
        rows = slice(gi * HG_GROUP, (gi + 1) * HG_GROUP)
        q, k, f, v, b = hd
        b_scr = b_all.at[gi]
        cache = {}

        def rowb(u, r):
            if r < 0:
                return zeros8
            if (u, r) not in cache:
                cache[u, r] = jnp.broadcast_to(b_scr[u * HG_CHUNK + r:u * HG_CHUNK + r + 1, :], (SUBLANES, LANES))
            return cache[u, r]

        def build(fn):
            return jnp.concatenate([fn(u, j) for u in range(HG_UNROLL) for j in range(per_chunk)], axis=0)

        def group_rows(x, u, j):
            r0 = u * HG_CHUNK + j * SUBLANES
            return x[r0:r0 + SUBLANES]

        def level_rows(ci, j):
            if ("m", ci, j) not in cache:
                cache["m", ci, j] = msk_ref[ci, j * SUBLANES:(j + 1) * SUBLANES, :] != 0.0
            return cache["m", ci, j]

        diag = jnp.sum(q * k, axis=-1, keepdims=True)
        a = [[jnp.where(level_rows(len(HG_LEVELS), j), group_rows(diag, u, j), 0.0) for j in range(per_chunk)]
             for u in range(HG_UNROLL)]
        for ci, c in enumerate(HG_LEVELS):
            if c >= SUBLANES:
                m = c // SUBLANES
                right = [j for j in range(per_chunk) if (j // m) % 2 == 1]
                pick = lambda x, js: jnp.concatenate([group_rows(x, u, j) for u in range(HG_UNROLL) for j in js], axis=0)
                bs = jnp.concatenate([rowb(u, SUBLANES * (j // m) * m - 1) for u in range(HG_UNROLL) for j in right], axis=0)
                qd = (pick(q, right) * jnp.exp2(pick(b, right) - bs)).astype(BF16)
                kd = jnp.concatenate(
                    [group_rows(k, u, j) * jnp.exp2(rowb(u, SUBLANES * ((j // m) * m + m) - 1) - group_rows(b, u, j))
                     if (j // m) % 2 == 0 else zeros8
                     for u in range(HG_UNROLL) for j in range(per_chunk)], axis=0).astype(BF16)
                rows_per_chunk = len(right) * SUBLANES
                for u, r in enumerate(chunk_rows):
                    prod = _dot_nt(qd[u * rows_per_chunk:(u + 1) * rows_per_chunk], kd[r])
                    for n, j in enumerate(right):
                        a[u][j] = jnp.where(level_rows(ci, j), prod[n * SUBLANES:(n + 1) * SUBLANES], a[u][j])
                continue
            if c == 4:
                bs = build(lambda u, j: sel((4,), (rowb(u, 8 * j - 1), rowb(u, 8 * j + 3))))
                be = build(lambda u, j: sel((4,), (rowb(u, 8 * j + 3), rowb(u, 8 * j + 7))))
            elif c == 2:
                bs = build(lambda u, j: sel((2, 4, 6), (rowb(u, 8 * j - 1), rowb(u, 8 * j + 1),
                                                         rowb(u, 8 * j + 3), rowb(u, 8 * j + 5))))
                be = build(lambda u, j: sel((2, 4, 6), (rowb(u, 8 * j + 1), rowb(u, 8 * j + 3),
                                                         rowb(u, 8 * j + 5), rowb(u, 8 * j + 7))))
            if c == 1:
                qd = (q * f).astype(BF16)
                kd = k.astype(BF16)
            else:
                qd = (q * jnp.exp2(b - bs)).astype(BF16)
                kd = (k * jnp.exp2(be - b)).astype(BF16)
            for u, r in enumerate(chunk_rows):
                prod = _dot_nt(qd[r], kd[r])
                for j in range(per_chunk):
                    a[u][j] = jnp.where(level_rows(ci, j), prod[j * SUBLANES:(j + 1) * SUBLANES], a[u][j])
        a = [jnp.concatenate(a[u], axis=0) for u in range(HG_UNROLL)]

        q_exp = (q * jnp.exp2(b)).astype(BF16)
        b_last = build(lambda u, j: rowb(u, HG_CHUNK - 1))
        k_dec = (k * jnp.exp2(b_last - b)).astype(BF16)
        o_intra = [_dot(a[u].astype(BF16), v[r]) for u, r in enumerate(chunk_rows)]
        st_add = [_dot_tn(v[r], k_dec[r]) for r in chunk_rows]

        o = []
        for u, r in enumerate(chunk_rows):
            o.append(o_intra[u] + _dot_nt(q_exp[r], st.astype(BF16)))
            st = jnp.exp2(rowb(u, HG_CHUNK - 1)[0:1, :]) * st + st_add[u]
        o = jnp.concatenate(o, axis=0)

        g = g_ref[rows, :]
        o_ref[rows, :] = (o * _rms_scale(o) * gn * (g * _sigmoid(g))).astype(BF16)
        return st

    n_groups = q_ref.shape[0] // HG_GROUP
    st = jnp.zeros((HG_D, HG_D), F32)
    hd = head(0)
    for gi in range(n_groups):
        nxt = head(gi + 1) if gi + 1 < n_groups else None
        st = main(gi, hd, st)
        hd = nxt


def _hgrn(proj, lb_logits, out_norm, batch, seq):
    hblk = lambda col: pl.BlockSpec((seq, HG_D), lambda b, h: (b, col // HG_D + h))
    masks = jnp.asarray(_hgrn_masks())
    tri = np.kron(np.eye(HG_SLAB // HG_CHUNK), np.tril(np.ones((HG_CHUNK, HG_CHUNK))))
    tri = jnp.asarray(tri.astype(np.float32), dtype=BF16)
    nlev = masks.shape[0]
    return pl.pallas_call(
        _hgrn_kernel,
        out_shape=jax.ShapeDtypeStruct((batch * seq, HG_WIDTH), BF16),
        grid=(batch, HG_HEADS),
        in_specs=[
            hblk(COL_HQ), hblk(COL_HF), hblk(COL_HI), hblk(COL_HG),
            pl.BlockSpec((lb_logits.shape[0], HG_D), lambda b, h: (0, h)),
            pl.BlockSpec((1, HG_D), lambda b, h: (0, 0)),
            pl.BlockSpec((nlev, HG_CHUNK, HG_CHUNK), lambda b, h: (0, 0, 0)),
            pl.BlockSpec((HG_SLAB, HG_SLAB), lambda b, h: (0, 0)),
        ],
        out_specs=pl.BlockSpec((seq, HG_D), lambda b, h: (b, h)),
        scratch_shapes=[pltpu.VMEM((seq // HG_GROUP, HG_GROUP, HG_D), F32)],
        compiler_params=_params("parallel", "parallel"),
        name="hgrn2",
    )(proj, proj, proj, proj, lb_logits, out_norm, masks, tri)


def _mem_kv_kernel(m_ref, g_ref, w_ref, kn_ref, k_ref, v_ref):
    m = m_ref[...]
    mn = (m * _rms_scale(m) * g_ref[...]).astype(BF16)
    kv = _dot(mn, w_ref[...].astype(BF16))
    for h in range(MEM_HEADS):
        k = kv[:, 2 * h * MEM_HD:(2 * h + 1) * MEM_HD]
        k_ref[:, h * MEM_HD:(h + 1) * MEM_HD] = (k * _rms_scale(k) * kn_ref[...]).astype(BF16)
        v_ref[:, h * MEM_HD:(h + 1) * MEM_HD] = kv[:, (2 * h + 1) * MEM_HD:(2 * h + 2) * MEM_HD].astype(BF16)


def _mem_kv(mem, gain, w, k_norm, batch, mem_len):
    return pl.pallas_call(
        _mem_kv_kernel,
        out_shape=(jax.ShapeDtypeStruct((batch * mem_len, MEM_WIDTH), BF16),
                   jax.ShapeDtypeStruct((batch * mem_len, MEM_WIDTH), BF16)),
        grid=(batch,),
        in_specs=[
            pl.BlockSpec((mem_len, D_MODEL), lambda b: (b, 0)),
            pl.BlockSpec((1, D_MODEL), lambda b: (0, 0)),
            pl.BlockSpec(w.shape, lambda b: (0, 0)),
            pl.BlockSpec((1, MEM_HD), lambda b: (0, 0)),
        ],
        out_specs=(pl.BlockSpec((mem_len, MEM_WIDTH), lambda b: (b, 0)),
                   pl.BlockSpec((mem_len, MEM_WIDTH), lambda b: (b, 0))),
        compiler_params=_params("parallel"),
        name="mem_kv",
    )(mem, gain, w, k_norm)


XA_TM = 512


def _out_xattn_kernel(a_ref, r_ref, x_ref, og_ref, wout_ref, g_ref, wq_ref, qn_ref, k_ref, v_ref, wo_ref,
                      o_ref, att_scr):
    a = jnp.concatenate([a_ref[g] for g in range(ATT_GROUPS)], axis=-1).astype(F32)
    an = (a * _rms_scale(a) * og_ref[...]).astype(BF16)
    x = (x_ref[...] + _dot(an, wout_ref[:MLA_WIDTH, :].astype(BF16))
         + _dot(r_ref[...], wout_ref[MLA_WIDTH:, :].astype(BF16)))

    h = (x * _rms_scale(x) * g_ref[...]).astype(BF16)
    q = _dot(h, wq_ref[...].astype(BF16))
    heads = [slice(hd * MEM_HD, (hd + 1) * MEM_HD) for hd in range(MEM_HEADS)]
    q_gain = qn_ref[...] * (float(np.sqrt(MEM_HD)) * XATT_Q_SCALE)
    scores = []
    for cols in heads:
        qh = q[:, cols]
        r = lax.rsqrt(jnp.sum(qh * qh, axis=-1, keepdims=True) + MEM_HD * EPS)
        scores.append(_dot_nt((qh * r * q_gain).astype(BF16), k_ref[:, cols]))
    for cols, s in zip(heads, scores):
        p = jnp.exp2(s - jnp.max(s, axis=-1, keepdims=True))
        l = jnp.sum(p, axis=-1, keepdims=True)
        att_scr[:, cols] = (_dot(p.astype(BF16), v_ref[:, cols]) / l).astype(BF16)
    o_ref[...] = x + _dot(att_scr[...], wo_ref[...].astype(BF16))


def _out_xattn(a, r, x, out_gain, w_out, gain, wq, q_norm, kx, vx, wo, seq, mem_len):
    t = x.shape[0]
    per_batch = seq // XA_TM
    tile = lambda n: pl.BlockSpec((XA_TM, n), lambda i: (i, 0))
    whole = lambda w: pl.BlockSpec(w.shape, lambda i: (0, 0), pipeline_mode=pl.Buffered(1))
    vec = lambda n: pl.BlockSpec((1, n), lambda i: (0, 0))
    mem = pl.BlockSpec((mem_len, MEM_WIDTH), lambda i: (i // per_batch, 0))
    groups = pl.BlockSpec((ATT_GROUPS, XA_TM, MLA_WIDTH // ATT_GROUPS), lambda i: (0, i, 0))
    return pl.pallas_call(
        _out_xattn_kernel,
        out_shape=jax.ShapeDtypeStruct((t, D_MODEL), F32),
        grid=(t // XA_TM,),
        in_specs=[groups, tile(HG_WIDTH), tile(D_MODEL), vec(MLA_WIDTH), whole(w_out),
                  vec(D_MODEL), whole(wq), vec(MEM_HD), mem, mem, whole(wo)],
        out_specs=tile(D_MODEL),
        scratch_shapes=[pltpu.VMEM((XA_TM, MEM_WIDTH), BF16)],
        compiler_params=_params("parallel", vmem_limit=WIDE_VMEM_LIMIT),
        name="out_xattn",
    )(a, r, x, out_gain, w_out, gain, wq, q_norm, kx, vx, wo)


def _pe_pair(x1, x2):
    z = jnp.zeros(x1.shape[:-1] + (LANES - MLA_ROPE,), x1.dtype)
    return jnp.concatenate([x1, x2, z], axis=-1), jnp.concatenate([x2, x1, z], axis=-1)


W_IN_TC = 512


def _w_in_prep_kernel(w_ref, o_ref):
    half = MLA_ROPE // 2
    kpe0 = MLA_Q_RANK + MLA_KV_RANK
    hg0 = kpe0 + MLA_ROPE
    zeros = jnp.zeros((LANES - MLA_ROPE, o_ref.shape[1]), BF16)
    o_ref[:kpe0, :] = w_ref[:kpe0, :].astype(BF16)
    o_ref[kpe0:COL_KPE, :] = w_ref[hg0:, :].astype(BF16)
    x1 = w_ref[kpe0:kpe0 + half, :].astype(BF16)
    x2 = w_ref[kpe0 + half:hg0, :].astype(BF16)
    o_ref[COL_KPE:COL_KPE + half, :] = x1
    o_ref[COL_KPE + half:COL_KPE + MLA_ROPE, :] = x2
    o_ref[COL_KPE + MLA_ROPE:COL_KPE_SW, :] = zeros
    o_ref[COL_KPE_SW:COL_KPE_SW + half, :] = x2
    o_ref[COL_KPE_SW + half:COL_KPE_SW + MLA_ROPE, :] = x1
    o_ref[COL_KPE_SW + MLA_ROPE:, :] = zeros


def _prep_w_in(w_in, layer):
    w_t = jnp.swapaxes(w_in, 1, 2)
    _, n, k = w_t.shape
    return pl.pallas_call(
        _w_in_prep_kernel,
        out_shape=jax.ShapeDtypeStruct((IN_COLS_PAD, k), BF16),
        grid=(k // W_IN_TC,),
        in_specs=[pl.BlockSpec((None, n, W_IN_TC), lambda i: (layer, 0, i))],
        out_specs=pl.BlockSpec((IN_COLS_PAD, W_IN_TC), lambda i: (0, i)),
        compiler_params=_params("parallel"),
        name="w_in_prep",
    )(w_t)


def _prep_w_q_up(w):
    w = w.astype(BF16).reshape(MLA_Q_RANK, MLA_HEADS, MLA_QK)
    half = MLA_ROPE // 2
    pe, pe_sw = _pe_pair(w[..., MLA_NOPE:MLA_NOPE + half], w[..., MLA_NOPE + half:])
    return jnp.concatenate([w[..., :MLA_NOPE], pe, pe_sw], axis=-1).reshape(MLA_Q_RANK, MLA_HEADS * Q_UP_PER_HEAD)


def _pe_gains(norm):
    half = MLA_ROPE // 2
    g1 = norm[MLA_NOPE:MLA_NOPE + half]
    g2 = norm[MLA_NOPE + half:]
    ga, gb = _pe_pair(g1, g2)
    sign = jnp.concatenate([-jnp.ones((half,), F32), jnp.ones((LANES - half,), F32)])
    return ga[None, :], (gb * sign)[None, :]


def kernel(x, mem, positions, ffn1_norm, ffn1_w_gate, ffn1_w_up, ffn1_w_down, mix_norm, w_in, mla_q_a_norm, mla_w_q_up, mla_kv_a_norm, mla_w_kv_up, mla_q_norm, mla_k_norm, mla_out_norm, hg_lb_logits, hg_out_norm, w_out, xattn_norm, mem_norm, xattn_w_q, xattn_w_kv, xattn_q_norm, xattn_k_norm, xattn_w_o, ffn2_norm, ffn2_w_gate, ffn2_w_up, ffn2_w_down):
    batch, seq, _ = x.shape
    mem_len = mem.shape[1]
    depth = ffn1_norm.shape[0]
    assert depth == 1 and seq % ATT_T == 0 and seq % XA_TM == 0
    t = batch * seq
    xt = x.reshape(t, D_MODEL)
    half = MLA_ROPE // 2
    quarters = LANES // half
    pos = positions.reshape(t // PREP_TM, quarters, PREP_TM // quarters).transpose(0, 2, 1)
    pos = jnp.repeat(pos, half, axis=-1).reshape(t // quarters, LANES)
    inv_freq = ROPE_BASE ** (-np.arange(half, dtype=np.float32) / half)
    invf = jnp.asarray(np.tile(inv_freq, quarters)[None, :])
    l = 0

    xt = _ffn(xt, ffn1_norm[l][None, :], ffn1_w_gate[l], ffn1_w_up[l], ffn1_w_down[l])

    proj = _in_proj(xt, mix_norm[l][None, :], _prep_w_in(w_in, l))
    qpa, qpb = _pe_gains(mla_q_norm[l])
    kpa, kpb = _pe_gains(mla_k_norm[l])
    q, k, v = _mla_prep(proj, pos, invf, mla_q_a_norm[l][None, :], mla_kv_a_norm[l][None, :],
                        _prep_w_q_up(mla_w_q_up[l]), mla_w_kv_up[l],
                        mla_q_norm[l][None, :MLA_NOPE], qpa, qpb,
                        mla_k_norm[l][None, :MLA_NOPE], kpa, kpb)
    a = _mla_attn(q, k, v, batch, seq)
    r = _hgrn(proj, hg_lb_logits, hg_out_norm[l][None, :], batch, seq)
    kx, vx = _mem_kv(mem.reshape(batch * mem_len, D_MODEL), mem_norm[l][None, :],
                     xattn_w_kv[l], xattn_k_norm[l][None, :], batch, mem_len)
    xt = _out_xattn(a, r, xt, mla_out_norm[l][None, :], w_out[l],
                    xattn_norm[l][None, :], xattn_w_q[l], xattn_q_norm[l][None, :],
                    kx, vx, xattn_w_o[l], seq, mem_len)

    xt = _ffn(xt, ffn2_norm[l][None, :], ffn2_w_gate[l], ffn2_w_up[l], ffn2_w_down[l])
    return xt.reshape(batch, seq, D_MODEL)
```

```python
import numpy as np
import jax
import jax.numpy as jnp
from jax import lax
from jax.experimental import pallas as pl
from jax.experimental.pallas import tpu as pltpu

F32 = jnp.float32
BF16 = jnp.bfloat16

EPS = 1e-6
ROPE_BASE = 10000.0
LANES = 128
SUBLANES = 8

D_MODEL = 2048
D_FF = 5504
FFN_TM = 1024
FFN_TF = 512

MLA_HEADS = 8
MLA_NOPE = 128
MLA_ROPE = 64
MLA_QK = MLA_NOPE + MLA_ROPE
MLA_V = 128
MLA_Q_RANK = 512
MLA_KV_RANK = 256
MLA_WIDTH = MLA_HEADS * MLA_V
QK_PAD = 256
Q_UP_PER_HEAD = 384
ATT_Q_SCALE = float(MLA_QK ** -0.5 * np.log2(np.e))
ATT_HEADS_PER_STEP = 2
ATT_GROUPS = MLA_HEADS // ATT_HEADS_PER_STEP

HG_HEADS = 8
HG_D = 128
HG_CHUNK = 64
HG_LEVELS = (32, 16, 8, 4, 2, 1)
HG_UNROLL = 16
HG_GROUP = HG_CHUNK * HG_UNROLL
HG_SLAB = 256
HG_WIDTH = HG_HEADS * HG_D

MEM_HEADS = 4
MEM_HD = 128
MEM_WIDTH = MEM_HEADS * MEM_HD
XATT_Q_SCALE = float(MEM_HD ** -0.5 * np.log2(np.e))

COL_CQ = 0
COL_CKV = MLA_Q_RANK
COL_HQ = MLA_Q_RANK + MLA_KV_RANK
COL_HF = COL_HQ + HG_WIDTH
COL_HI = COL_HF + HG_WIDTH
COL_HG = COL_HI + HG_WIDTH
COL_KPE = COL_HG + HG_WIDTH
COL_KPE_SW = COL_KPE + LANES
IN_COLS_PAD = COL_KPE_SW + LANES

VMEM_LIMIT = 56 * 1024 * 1024
WIDE_VMEM_LIMIT = 61 * 1024 * 1024
STAGING_DMA_PRIORITY = 1


def _params(*sem, vmem_limit=VMEM_LIMIT):
    return pltpu.CompilerParams(dimension_semantics=sem, vmem_limit_bytes=vmem_limit)


def _rms_scale(x):
    return lax.rsqrt(jnp.mean(x * x, axis=-1, keepdims=True) + EPS)


def _sigmoid(x):
    return 1.0 / (1.0 + jnp.exp(-x))


def _dot(a, b):
    return jnp.dot(a, b, preferred_element_type=F32)


def _dot_nt(a, b):
    return lax.dot_general(a, b, (((1,), (1,)), ((), ())), preferred_element_type=F32)


def _dot_tn(a, b):
    return lax.dot_general(a, b, (((0,), (0,)), ((), ())), preferred_element_type=F32)


def _ffn_kernel(x_hbm, g_ref, wg_ref, wu_ref, wd_ref, o_ref, h_scr, x_scr, x_sem):
    i = pl.program_id(0)

    def x_copy(tile):
        rows = pl.ds(pl.multiple_of(tile * FFN_TM, FFN_TM), FFN_TM)
        return pltpu.make_async_copy(x_hbm.at[rows, :], x_scr, x_sem)

    def branch(h):
        gate = _dot(h, wg_ref[...].astype(BF16))
        up = _dot(h, wu_ref[...].astype(BF16))
        a = 0.5 * gate * _sigmoid(gate) * up
        f0 = pl.program_id(1) * FFN_TF
        a_ok = f0 + lax.broadcasted_iota(jnp.int32, (1, FFN_TF), 1) < D_FF
        w_ok = f0 + lax.broadcasted_iota(jnp.int32, (FFN_TF, 1), 0) < D_FF
        a = jnp.where(a_ok, a, 0.0).astype(BF16)
        wd = jnp.where(w_ok, wd_ref[...], 0.0).astype(BF16)
        return _dot(a, wd)

    @pl.when(pl.program_id(1) == 0)
    def _():
        @pl.when(i == 0)
        def _():
            x_copy(0).start()

        x_copy(i).wait()
        x = x_scr[...]
        h = (x * _rms_scale(x) * g_ref[...]).astype(BF16)
        h_scr[...] = h
        o_ref[...] = x + branch(h)

        @pl.when(i + 1 < pl.num_programs(0))
        def _():
            x_copy(i + 1).start(priority=STAGING_DMA_PRIORITY)

    @pl.when(pl.program_id(1) > 0)
    def _():
        o_ref[...] += branch(h_scr[...])


def _ffn(x, gain, wg, wu, wd):
    t = x.shape[0]
    return pl.pallas_call(
        _ffn_kernel,
        out_shape=jax.ShapeDtypeStruct((t, D_MODEL), F32),
        grid=(t // FFN_TM, pl.cdiv(D_FF, FFN_TF)),
        in_specs=[
            pl.BlockSpec(memory_space=pl.ANY),
            pl.BlockSpec((1, D_MODEL), lambda i, f: (0, 0)),
            pl.BlockSpec((D_MODEL, FFN_TF), lambda i, f: (0, f)),
            pl.BlockSpec((D_MODEL, FFN_TF), lambda i, f: (0, f)),
            pl.BlockSpec((FFN_TF, D_MODEL), lambda i, f: (f, 0)),
        ],
        out_specs=pl.BlockSpec((FFN_TM, D_MODEL), lambda i, f: (i, 0)),
        scratch_shapes=[pltpu.VMEM((FFN_TM, D_MODEL), BF16), pltpu.VMEM((FFN_TM, D_MODEL), F32),
                        pltpu.SemaphoreType.DMA],
        compiler_params=_params("arbitrary", "arbitrary", vmem_limit=WIDE_VMEM_LIMIT),
        name="ffn",
    )(x, gain, wg, wu, wd)


PROJ_TM = 1024
PROJ_TN = 1280


def _in_proj_kernel(x_hbm, g_ref, w_ref, o_ref, h_scr, x_scr, x_sem):
    i = pl.program_id(0)

    def x_copy(tile):
        rows = pl.ds(pl.multiple_of(tile * PROJ_TM, PROJ_TM), PROJ_TM)
        return pltpu.make_async_copy(x_hbm.at[rows, :], x_scr, x_sem)

    @pl.when(pl.program_id(1) == 0)
    def _():
        @pl.when(i == 0)
        def _():
            x_copy(0).start()

        x_copy(i).wait()
        x = x_scr[...]
        h = (x * _rms_scale(x) * g_ref[...]).astype(BF16)
        h_scr[...] = h
        o_ref[...] = _dot_nt(h, w_ref[...])

        @pl.when(i + 1 < pl.num_programs(0))
        def _():
            x_copy(i + 1).start(priority=STAGING_DMA_PRIORITY)

    @pl.when(pl.program_id(1) > 0)
    def _():
        o_ref[...] = _dot_nt(h_scr[...], w_ref[...])


def _in_proj(x, gain, w_t):
    t = x.shape[0]
    n = w_t.shape[0]
    return pl.pallas_call(
        _in_proj_kernel,
        out_shape=jax.ShapeDtypeStruct((t, n), F32),
        grid=(t // PROJ_TM, n // PROJ_TN),
        in_specs=[
            pl.BlockSpec(memory_space=pl.ANY),
            pl.BlockSpec((1, D_MODEL), lambda i, j: (0, 0)),
            pl.BlockSpec((PROJ_TN, D_MODEL), lambda i, j: (j, 0)),
        ],
        out_specs=pl.BlockSpec((PROJ_TM, PROJ_TN), lambda i, j: (i, j)),
        scratch_shapes=[pltpu.VMEM((PROJ_TM, D_MODEL), BF16), pltpu.VMEM((PROJ_TM, D_MODEL), F32),
                        pltpu.SemaphoreType.DMA],
        compiler_params=_params("arbitrary", "arbitrary", vmem_limit=WIDE_VMEM_LIMIT),
        name="in_proj",
    )(x, gain, w_t)


PREP_TM = 1024


def _mla_prep_kernel(cq_ref, ckv_ref, kpe_ref, kpesw_ref, pos_ref, invf_ref,
                     qa_ref, kva_ref, wq_ref, wkv_ref,
                     qn_ref, qpa_ref, qpb_ref, kn_ref, kpa_ref, kpb_ref,
                     q_ref, k_ref, vt_ref):
    ang = pos_ref[...].astype(F32) * invf_ref[...]
    cos4 = jnp.cos(ang)
    sin4 = jnp.sin(ang)
    lane = lax.broadcasted_iota(jnp.int32, (1, LANES), 1)
    half = MLA_ROPE // 2

    def expand(t4):
        parts = []
        for part in range(LANES // half):
            t = pltpu.roll(t4, (LANES - half * part) % LANES, 1) if part else t4
            parts.append(jnp.where(lane < half, t, pltpu.roll(t, half, 1)))
        return jnp.concatenate(parts, axis=0)

    cos = expand(cos4)
    sin = expand(sin4)

    q_const = float(np.sqrt(MLA_QK)) * ATT_Q_SCALE
    k_const = float(np.sqrt(MLA_QK))

    cq = cq_ref[...]
    cqn = (cq * _rms_scale(cq) * qa_ref[...]).astype(BF16)
    qall = _dot(cqn, wq_ref[...])
    qn_gain = qn_ref[...] * q_const
    q_cos = (qpa_ref[...] * q_const) * cos
    q_sin = (qpb_ref[...] * q_const) * sin
    for h in range(MLA_HEADS):
        base = h * Q_UP_PER_HEAD
        qn = qall[:, base:base + LANES]
        y = qall[:, base + LANES:base + 2 * LANES]
        ysw = qall[:, base + 2 * LANES:base + 3 * LANES]
        r = lax.rsqrt(jnp.sum(qn * qn + y * y, axis=-1, keepdims=True) + MLA_QK * EPS)
        g, c = divmod(h, ATT_HEADS_PER_STEP)
        c *= QK_PAD
        q_ref[g, :, c:c + LANES] = (qn * r * qn_gain).astype(BF16)
        q_ref[g, :, c + LANES:c + QK_PAD] = ((y * q_cos + ysw * q_sin) * r).astype(BF16)

    ckv = ckv_ref[...]
    ckvn = (ckv * _rms_scale(ckv) * kva_ref[...]).astype(BF16)
    wkv = wkv_ref[...]
    per_head = MLA_NOPE + MLA_V
    wk = jnp.concatenate([wkv[:, h * per_head:h * per_head + MLA_NOPE] for h in range(MLA_HEADS)], axis=1)
    wv = jnp.concatenate([wkv[:, h * per_head + MLA_NOPE:(h + 1) * per_head] for h in range(MLA_HEADS)], axis=1)
    kn_all = _dot(ckvn, wk.astype(BF16))
    vt_ref[...] = _dot_nt(wv.T.astype(BF16), ckvn).astype(BF16)
    yk = kpe_ref[...]
    kn_gain = kn_ref[...] * k_const
    kr = yk * ((kpa_ref[...] * k_const) * cos) + kpesw_ref[...] * ((kpb_ref[...] * k_const) * sin)
    yk_sq = yk * yk
    for h in range(MLA_HEADS):
        kn = kn_all[:, h * MLA_NOPE:(h + 1) * MLA_NOPE]
        r = lax.rsqrt(jnp.sum(kn * kn + yk_sq, axis=-1, keepdims=True) + MLA_QK * EPS)
        g, c = divmod(h, ATT_HEADS_PER_STEP)
        c *= QK_PAD
        k_ref[g, :, c:c + LANES] = (kn * r * kn_gain).astype(BF16)
        k_ref[g, :, c + LANES:c + QK_PAD] = (kr * r).astype(BF16)


def _mla_prep(proj, pos, invf, qa, kva, wq, wkv, qn, qpa, qpb, kn, kpa, kpb):
    t = proj.shape[0]
    tm = PREP_TM
    row = lambda i: (0, 0)
    vec = lambda n: pl.BlockSpec((1, n), row)
    return pl.pallas_call(
        _mla_prep_kernel,
        out_shape=(jax.ShapeDtypeStruct((ATT_GROUPS, t, ATT_HEADS_PER_STEP * QK_PAD), BF16),
                   jax.ShapeDtypeStruct((ATT_GROUPS, t, ATT_HEADS_PER_STEP * QK_PAD), BF16),
                   jax.ShapeDtypeStruct((MLA_WIDTH, t), BF16)),
        grid=(t // tm,),
        in_specs=[
            pl.BlockSpec((tm, MLA_Q_RANK), lambda i: (i, COL_CQ // MLA_Q_RANK)),
            pl.BlockSpec((tm, MLA_KV_RANK), lambda i: (i, COL_CKV // MLA_KV_RANK)),
            pl.BlockSpec((tm, LANES), lambda i: (i, COL_KPE // LANES)),
            pl.BlockSpec((tm, LANES), lambda i: (i, COL_KPE_SW // LANES)),
            pl.BlockSpec((tm // (LANES // (MLA_ROPE // 2)), LANES), lambda i: (i, 0)),
            vec(LANES),
            vec(MLA_Q_RANK), vec(MLA_KV_RANK),
            pl.BlockSpec(wq.shape, row), pl.BlockSpec(wkv.shape, row),
            vec(LANES), vec(LANES), vec(LANES), vec(LANES), vec(LANES), vec(LANES),
        ],
        out_specs=(pl.BlockSpec((ATT_GROUPS, tm, ATT_HEADS_PER_STEP * QK_PAD), lambda i: (0, i, 0)),
                   pl.BlockSpec((ATT_GROUPS, tm, ATT_HEADS_PER_STEP * QK_PAD), lambda i: (0, i, 0)),
                   pl.BlockSpec((MLA_WIDTH, tm), lambda i: (0, i))),
        compiler_params=_params("parallel"),
        name="mla_prep",
    )(proj, proj, proj, proj, pos, invf, qa, kva, wq, wkv, qn, qpa, qpb, kn, kpa, kpb)


ATT_T = 256
ATT_AHEAD = 5


def _mla_attn_kernel(q_ref, k_ref, vt_ref, o_ref):
    kpos = lax.broadcasted_iota(jnp.int32, (ATT_T, ATT_T), 0)
    qpos = lax.broadcasted_iota(jnp.int32, (ATT_T, ATT_T), 1)
    nq = q_ref.shape[0] // ATT_T

    def scores(unit):
        h, i = unit
        lo, hi = i * ATT_T, (i + 1) * ATT_T
        cols = slice(h * QK_PAD, (h + 1) * QK_PAD)
        q = q_ref[lo:hi, cols]
        s_diag = jnp.where(kpos <= qpos, _dot_nt(k_ref[lo:hi, cols], q), -jnp.inf)
        s_off = _dot_nt(k_ref[0:lo, cols], q) if i > 0 else None
        return s_diag, s_off

    def finish(unit, s_diag, s_off):
        h, i = unit
        lo, hi = i * ATT_T, (i + 1) * ATT_T
        vt = vt_ref.at[h * MLA_V:(h + 1) * MLA_V, :]
        m = jnp.max(s_diag, axis=0, keepdims=True)
        if i > 0:
            m = jnp.maximum(m, jnp.max(s_off, axis=0, keepdims=True))
        p = jnp.exp2(s_diag - m)
        l = jnp.sum(p, axis=0, keepdims=True)
        acc = _dot(vt[:, lo:hi], p.astype(BF16))
        if i > 0:
            p = jnp.exp2(s_off - m)
            l = l + jnp.sum(p, axis=0, keepdims=True)
            acc = acc + _dot(vt[:, 0:lo], p.astype(BF16))
        o_ref[lo:hi, h * MLA_V:(h + 1) * MLA_V] = (acc / l).T.astype(BF16)

    order = [(h, i) for i in reversed(range(nq)) for h in range(ATT_HEADS_PER_STEP)]
    pending = [scores(u) for u in order[:ATT_AHEAD]]
    for n, unit in enumerate(order):
        if n + ATT_AHEAD < len(order):
            pending.append(scores(order[n + ATT_AHEAD]))
        finish(unit, *pending.pop(0))


def _mla_attn(q, k, vt, batch, seq):
    hs = ATT_HEADS_PER_STEP
    return pl.pallas_call(
        _mla_attn_kernel,
        out_shape=jax.ShapeDtypeStruct((ATT_GROUPS, batch * seq, hs * MLA_V), BF16),
        grid=(batch, ATT_GROUPS),
        in_specs=[
            pl.BlockSpec((None, seq, hs * QK_PAD), lambda b, h: (h, b, 0)),
            pl.BlockSpec((None, seq, hs * QK_PAD), lambda b, h: (h, b, 0)),
            pl.BlockSpec((hs * MLA_V, seq), lambda b, h: (h, b)),
        ],
        out_specs=pl.BlockSpec((None, seq, hs * MLA_V), lambda b, h: (h, b, 0)),
        compiler_params=_params("parallel", "parallel"),
        name="mla_attn",
    )(q, k, vt)


def _hgrn_masks():
    t = np.arange(HG_CHUNK)[:, None]
    s = np.arange(HG_CHUNK)[None, :]
    masks = [((t // c) % 2 == 1) & ((s // c) == (t // c) - 1) for c in HG_LEVELS]
    masks.append(t == s)
    return np.stack(masks).astype(np.float32)


def _hgrn_kernel(q_ref, f_ref, i_ref, g_ref, lbl_ref, gn_ref, msk_ref, tri_ref, o_ref, b_all):
    lbl = lbl_ref[...]
    e = jnp.exp(lbl - jnp.max(lbl, axis=0, keepdims=True))
    lb = e[0:1, :] / jnp.sum(e, axis=0, keepdims=True)
    gn = gn_ref[...]
    tri = tri_ref[...]
    sub = lax.broadcasted_iota(jnp.int32, (SUBLANES, LANES), 0)
    zeros8 = jnp.zeros((SUBLANES, LANES), F32)
    per_chunk = HG_CHUNK // SUBLANES
    chunk_rows = [slice(u * HG_CHUNK, (u + 1) * HG_CHUNK) for u in range(HG_UNROLL)]

    def sel(bounds, vals):
        out = vals[-1]
        for bound, val in zip(reversed(bounds), reversed(vals[:-1])):
            out = jnp.where(sub < bound, val, out)
        return out

    def head(gi):
        rows = slice(gi * HG_GROUP, (gi + 1) * HG_GROUP)
        q = q_ref[rows, :]
        v = i_ref[rows, :].astype(BF16)
        f = lb + (1.0 - lb) * _sigmoid(f_ref[rows, :])
        lg = jnp.log2(f)
        k = 1.0 - f

        l1 = lg.astype(BF16)
        l2 = (lg - l1.astype(F32)).astype(BF16)
        slabs = [slice(i, i + HG_SLAB) for i in range(0, HG_GROUP, HG_SLAB)]
        b = jnp.concatenate([_dot(tri, l1[r]) + _dot(tri, l2[r]) for r in slabs], axis=0)
        b_all[gi] = b
        return q, k, f, v, b

    def main(gi, hd, st):
        rows = slice(gi * HG_GROUP, (gi + 1) * HG_GROUP)
        q, k, f, v, b = hd
        b_scr = b_all.at[gi]
        cache = {}

        def rowb(u, r):
            if r < 0:
                return zeros8
            if (u, r) not in cache:
                cache[u, r] = jnp.broadcast_to(b_scr[u * HG_CHUNK + r:u * HG_CHUNK + r + 1, :], (SUBLANES, LANES))
            return cache[u, r]

        def build(fn):
            return jnp.concatenate([fn(u, j) for u in range(HG_UNROLL) for j in range(per_chunk)], axis=0)

        def group_rows(x, u, j):
            r0 = u * HG_CHUNK + j * SUBLANES
            return x[r0:r0 + SUBLANES]

        def level_rows(ci, j):
            if ("m", ci, j) not in cache:
                cache["m", ci, j] = msk_ref[ci, j * SUBLANES:(j + 1) * SUBLANES, :] != 0.0
            return cache["m", ci, j]

        diag = jnp.sum(q * k, axis=-1, keepdims=True)
        a = [[jnp.where(level_rows(len(HG_LEVELS), j), group_rows(diag, u, j), 0.0) for j in range(per_chunk)]
             for u in range(HG_UNROLL)]
        for ci, c in enumerate(HG_LEVELS):
            if c >= SUBLANES:
                m = c // SUBLANES
                right = [j for j in range(per_chunk) if (j // m) % 2 == 1]
                pick = lambda x, js: jnp.concatenate([group_rows(x, u, j) for u in range(HG_UNROLL) for j in js], axis=0)
                bs = jnp.concatenate([rowb(u, SUBLANES * (j // m) * m - 1) for u in range(HG_UNROLL) for j in right], axis=0)
                qd = (pick(q, right) * jnp.exp2(pick(b, right) - bs)).astype(BF16)
                kd = jnp.concatenate(
                    [group_rows(k, u, j) * jnp.exp2(rowb(u, SUBLANES * ((j // m) * m + m) - 1) - group_rows(b, u, j))
                     if (j // m) % 2 == 0 else zeros8
                     for u in range(HG_UNROLL) for j in range(per_chunk)], axis=0).astype(BF16)
                rows_per_chunk = len(right) * SUBLANES
                for u, r in enumerate(chunk_rows):
                    prod = _dot_nt(qd[u * rows_per_chunk:(u + 1) * rows_per_chunk], kd[r])
                    for n, j in enumerate(right):
                        a[u][j] = jnp.where(level_rows(ci, j), prod[n * SUBLANES:(n + 1) * SUBLANES], a[u][j])
                continue
            if c == 4:
                bs = build(lambda u, j: sel((4,), (rowb(u, 8 * j - 1), rowb(u, 8 * j + 3))))
                be = build(lambda u, j: sel((4,), (rowb(u, 8 * j + 3), rowb(u, 8 * j + 7))))
            elif c == 2:
                bs = build(lambda u, j: sel((2, 4, 6), (rowb(u, 8 * j - 1), rowb(u, 8 * j + 1),
                                                         rowb(u, 8 * j + 3), rowb(u, 8 * j + 5))))
                be = build(lambda u, j: sel((2, 4, 6), (rowb(u, 8 * j + 1), rowb(u, 8 * j + 3),
                                                         rowb(u, 8 * j + 5), rowb(u, 8 * j + 7))))
            if c == 1:
                qd = (q * f).astype(BF16)
                kd = k.astype(BF16)
            else:
                qd = (q * jnp.exp2(b - bs)).astype(BF16)
                kd = (k * jnp.exp2(be - b)).astype(BF16)
            for u, r in enumerate(chunk_rows):
                prod = _dot_nt(qd[r], kd[r])
                for j in range(per_chunk):
                    a[u][j] = jnp.where(level_rows(ci, j), prod[j * SUBLANES:(j + 1) * SUBLANES], a[u][j])
        a = [jnp.concatenate(a[u], axis=0) for u in range(HG_UNROLL)]

        q_exp = (q * jnp.exp2(b)).astype(BF16)
        b_last = build(lambda u, j: rowb(u, HG_CHUNK - 1))
        k_dec = (k * jnp.exp2(b_last - b)).astype(BF16)
        o_intra = [_dot(a[u].astype(BF16), v[r]) for u, r in enumerate(chunk_rows)]
        st_add = [_dot_tn(v[r], k_dec[r]) for r in chunk_rows]

        o = []
        for u, r in enumerate(chunk_rows):
            o.append(o_intra[u] + _dot_nt(q_exp[r], st.astype(BF16)))
            st = jnp.exp2(rowb(u, HG_CHUNK - 1)[0:1, :]) * st + st_add[u]
        o = jnp.concatenate(o, axis=0)

        g = g_ref[rows, :]
        o_ref[rows, :] = (o * _rms_scale(o) * gn * (g * _sigmoid(g))).astype(BF16)
        return st

    n_groups = q_ref.shape[0] // HG_GROUP
    st = jnp.zeros((HG_D, HG_D), F32)
    hd = head(0)
    for gi in range(n_groups):
        nxt = head(gi + 1) if gi + 1 < n_groups else None
        st = main(gi, hd, st)
        hd = nxt


def _hgrn(proj, lb_logits, out_norm, batch, seq):
    hblk = lambda col: pl.BlockSpec((seq, HG_D), lambda b, h: (b, col // HG_D + h))
    masks = jnp.asarray(_hgrn_masks())
    tri = np.kron(np.eye(HG_SLAB // HG_CHUNK), np.tril(np.ones((HG_CHUNK, HG_CHUNK))))
    tri = jnp.asarray(tri.astype(np.float32), dtype=BF16)
    nlev = masks.shape[0]
    return pl.pallas_call(
        _hgrn_kernel,
        out_shape=jax.ShapeDtypeStruct((batch * seq, HG_WIDTH), BF16),
        grid=(batch, HG_HEADS),
        in_specs=[
            hblk(COL_HQ), hblk(COL_HF), hblk(COL_HI), hblk(COL_HG),
            pl.BlockSpec((lb_logits.shape[0], HG_D), lambda b, h: (0, h)),
            pl.BlockSpec((1, HG_D), lambda b, h: (0, 0)),
            pl.BlockSpec((nlev, HG_CHUNK, HG_CHUNK), lambda b, h: (0, 0, 0)),
            pl.BlockSpec((HG_SLAB, HG_SLAB), lambda b, h: (0, 0)),
        ],
        out_specs=pl.BlockSpec((seq, HG_D), lambda b, h: (b, h)),
        scratch_shapes=[pltpu.VMEM((seq // HG_GROUP, HG_GROUP, HG_D), F32)],
        compiler_params=_params("parallel", "parallel"),
        name="hgrn2",
    )(proj, proj, proj, proj, lb_logits, out_norm, masks, tri)


def _mem_kv_kernel(m_ref, g_ref, w_ref, kn_ref, k_ref, v_ref):
    m = m_ref[...]
    mn = (m * _rms_scale(m) * g_ref[...]).astype(BF16)
    kv = _dot(mn, w_ref[...].astype(BF16))
    for h in range(MEM_HEADS):
        k = kv[:, 2 * h * MEM_HD:(2 * h + 1) * MEM_HD]
        k_ref[:, h * MEM_HD:(h + 1) * MEM_HD] = (k * _rms_scale(k) * kn_ref[...]).astype(BF16)
        v_ref[:, h * MEM_HD:(h + 1) * MEM_HD] = kv[:, (2 * h + 1) * MEM_HD:(2 * h + 2) * MEM_HD].astype(BF16)


def _mem_kv(mem, gain, w, k_norm, batch, mem_len):
    return pl.pallas_call(
        _mem_kv_kernel,
        out_shape=(jax.ShapeDtypeStruct((batch * mem_len, MEM_WIDTH), BF16),
                   jax.ShapeDtypeStruct((batch * mem_len, MEM_WIDTH), BF16)),
        grid=(batch,),
        in_specs=[
            pl.BlockSpec((mem_len, D_MODEL), lambda b: (b, 0)),
            pl.BlockSpec((1, D_MODEL), lambda b: (0, 0)),
            pl.BlockSpec(w.shape, lambda b: (0, 0)),
            pl.BlockSpec((1, MEM_HD), lambda b: (0, 0)),
        ],
        out_specs=(pl.BlockSpec((mem_len, MEM_WIDTH), lambda b: (b, 0)),
                   pl.BlockSpec((mem_len, MEM_WIDTH), lambda b: (b, 0))),
        compiler_params=_params("parallel"),
        name="mem_kv",
    )(mem, gain, w, k_norm)


XA_TM = 512


def _out_xattn_kernel(a_ref, r_ref, x_ref, og_ref, wout_ref, g_ref, wq_ref, qn_ref, k_ref, v_ref, wo_ref,
                      o_ref, att_scr):
    a = jnp.concatenate([a_ref[g] for g in range(ATT_GROUPS)], axis=-1).astype(F32)
    an = (a * _rms_scale(a) * og_ref[...]).astype(BF16)
    x = (x_ref[...] + _dot(an, wout_ref[:MLA_WIDTH, :].astype(BF16))
         + _dot(r_ref[...], wout_ref[MLA_WIDTH:, :].astype(BF16)))

    h = (x * _rms_scale(x) * g_ref[...]).astype(BF16)
    q = _dot(h, wq_ref[...].astype(BF16))
    heads = [slice(hd * MEM_HD, (hd + 1) * MEM_HD) for hd in range(MEM_HEADS)]
    q_gain = qn_ref[...] * (float(np.sqrt(MEM_HD)) * XATT_Q_SCALE)
    scores = []
    for cols in heads:
        qh = q[:, cols]
        r = lax.rsqrt(jnp.sum(qh * qh, axis=-1, keepdims=True) + MEM_HD * EPS)
        scores.append(_dot_nt((qh * r * q_gain).astype(BF16), k_ref[:, cols]))
    for cols, s in zip(heads, scores):
        p = jnp.exp2(s - jnp.max(s, axis=-1, keepdims=True))
        l = jnp.sum(p, axis=-1, keepdims=True)
        att_scr[:, cols] = (_dot(p.astype(BF16), v_ref[:, cols]) / l).astype(BF16)
    o_ref[...] = x + _dot(att_scr[...], wo_ref[...].astype(BF16))


def _out_xattn(a, r, x, out_gain, w_out, gain, wq, q_norm, kx, vx, wo, seq, mem_len):
    t = x.shape[0]
    per_batch = seq // XA_TM
    tile = lambda n: pl.BlockSpec((XA_TM, n), lambda i: (i, 0))
    whole = lambda w: pl.BlockSpec(w.shape, lambda i: (0, 0), pipeline_mode=pl.Buffered(1))
    vec = lambda n: pl.BlockSpec((1, n), lambda i: (0, 0))
    mem = pl.BlockSpec((mem_len, MEM_WIDTH), lambda i: (i // per_batch, 0))
    groups = pl.BlockSpec((ATT_GROUPS, XA_TM, MLA_WIDTH // ATT_GROUPS), lambda i: (0, i, 0))
    return pl.pallas_call(
        _out_xattn_kernel,
        out_shape=jax.ShapeDtypeStruct((t, D_MODEL), F32),
        grid=(t // XA_TM,),
        in_specs=[groups, tile(HG_WIDTH), tile(D_MODEL), vec(MLA_WIDTH), whole(w_out),
                  vec(D_MODEL), whole(wq), vec(MEM_HD), mem, mem, whole(wo)],
        out_specs=tile(D_MODEL),
        scratch_shapes=[pltpu.VMEM((XA_TM, MEM_WIDTH), BF16)],
        compiler_params=_params("parallel", vmem_limit=WIDE_VMEM_LIMIT),
        name="out_xattn",
    )(a, r, x, out_gain, w_out, gain, wq, q_norm, kx, vx, wo)


def _pe_pair(x1, x2):
    z = jnp.zeros(x1.shape[:-1] + (LANES - MLA_ROPE,), x1.dtype)
    return jnp.concatenate([x1, x2, z], axis=-1), jnp.concatenate([x2, x1, z], axis=-1)


W_IN_TC = 512


def _w_in_prep_kernel(w_ref, o_ref):
    half = MLA_ROPE // 2
    kpe0 = MLA_Q_RANK + MLA_KV_RANK
    hg0 = kpe0 + MLA_ROPE
    zeros = jnp.zeros((LANES - MLA_ROPE, o_ref.shape[1]), BF16)
    o_ref[:kpe0, :] = w_ref[:kpe0, :].astype(BF16)
    o_ref[kpe0:COL_KPE, :] = w_ref[hg0:, :].astype(BF16)
    x1 = w_ref[kpe0:kpe0 + half, :].astype(BF16)
    x2 = w_ref[kpe0 + half:hg0, :].astype(BF16)
    o_ref[COL_KPE:COL_KPE + half, :] = x1
    o_ref[COL_KPE + half:COL_KPE + MLA_ROPE, :] = x2
    o_ref[COL_KPE + MLA_ROPE:COL_KPE_SW, :] = zeros
    o_ref[COL_KPE_SW:COL_KPE_SW + half, :] = x2
    o_ref[COL_KPE_SW + half:COL_KPE_SW + MLA_ROPE, :] = x1
    o_ref[COL_KPE_SW + MLA_ROPE:, :] = zeros


def _prep_w_in(w_in, layer):
    w_t = jnp.swapaxes(w_in, 1, 2)
    _, n, k = w_t.shape
    return pl.pallas_call(
        _w_in_prep_kernel,
        out_shape=jax.ShapeDtypeStruct((IN_COLS_PAD, k), BF16),
        grid=(k // W_IN_TC,),
        in_specs=[pl.BlockSpec((None, n, W_IN_TC), lambda i: (layer, 0, i))],
        out_specs=pl.BlockSpec((IN_COLS_PAD, W_IN_TC), lambda i: (0, i)),
        compiler_params=_params("parallel"),
        name="w_in_prep",
    )(w_t)


def _prep_w_q_up(w):
    w = w.astype(BF16).reshape(MLA_Q_RANK, MLA_HEADS, MLA_QK)
    half = MLA_ROPE // 2
    pe, pe_sw = _pe_pair(w[..., MLA_NOPE:MLA_NOPE + half], w[..., MLA_NOPE + half:])
    return jnp.concatenate([w[..., :MLA_NOPE], pe, pe_sw], axis=-1).reshape(MLA_Q_RANK, MLA_HEADS * Q_UP_PER_HEAD)


def _pe_gains(norm):
    half = MLA_ROPE // 2
    g1 = norm[MLA_NOPE:MLA_NOPE + half]
    g2 = norm[MLA_NOPE + half:]
    ga, gb = _pe_pair(g1, g2)
    sign = jnp.concatenate([-jnp.ones((half,), F32), jnp.ones((LANES - half,), F32)])
    return ga[None, :], (gb * sign)[None, :]


def kernel(x, mem, positions, ffn1_norm, ffn1_w_gate, ffn1_w_up, ffn1_w_down, mix_norm, w_in, mla_q_a_norm, mla_w_q_up, mla_kv_a_norm, mla_w_kv_up, mla_q_norm, mla_k_norm, mla_out_norm, hg_lb_logits, hg_out_norm, w_out, xattn_norm, mem_norm, xattn_w_q, xattn_w_kv, xattn_q_norm, xattn_k_norm, xattn_w_o, ffn2_norm, ffn2_w_gate, ffn2_w_up, ffn2_w_down):
    batch, seq, _ = x.shape
    mem_len = mem.shape[1]
    depth = ffn1_norm.shape[0]
    assert depth == 1 and seq % ATT_T == 0 and seq % XA_TM == 0
    t = batch * seq
    xt = x.reshape(t, D_MODEL)
    half = MLA_ROPE // 2
    quarters = LANES // half
    pos = positions.reshape(t // PREP_TM, quarters, PREP_TM // quarters).transpose(0, 2, 1)
    pos = jnp.repeat(pos, half, axis=-1).reshape(t // quarters, LANES)
    inv_freq = ROPE_BASE ** (-np.arange(half, dtype=np.float32) / half)
    invf = jnp.asarray(np.tile(inv_freq, quarters)[None, :])
    l = 0

    xt = _ffn(xt, ffn1_norm[l][None, :], ffn1_w_gate[l], ffn1_w_up[l], ffn1_w_down[l])

    proj = _in_proj(xt, mix_norm[l][None, :], _prep_w_in(w_in, l))
    qpa, qpb = _pe_gains(mla_q_norm[l])
    kpa, kpb = _pe_gains(mla_k_norm[l])
    q, k, v = _mla_prep(proj, pos, invf, mla_q_a_norm[l][None, :], mla_kv_a_norm[l][None, :],
                        _prep_w_q_up(mla_w_q_up[l]), mla_w_kv_up[l],
                        mla_q_norm[l][None, :MLA_NOPE], qpa, qpb,
                        mla_k_norm[l][None, :MLA_NOPE], kpa, kpb)
    a = _mla_attn(q, k, v, batch, seq)
    r = _hgrn(proj, hg_lb_logits, hg_out_norm[l][None, :], batch, seq)
    kx, vx = _mem_kv(mem.reshape(batch * mem_len, D_MODEL), mem_norm[l][None, :],
                     xattn_w_kv[l], xattn_k_norm[l][None, :], batch, mem_len)
    xt = _out_xattn(a, r, xt, mla_out_norm[l][None, :], w_out[l],
                    xattn_norm[l][None, :], xattn_w_q[l], xattn_q_norm[l][None, :],
                    kx, vx, xattn_w_o[l], seq, mem_len)

    xt = _ffn(xt, ffn2_norm[l][None, :], ffn2_w_gate[l], ffn2_w_up[l], ffn2_w_down[l])
    return xt.reshape(batch, seq, D_MODEL)
```

```python
import numpy as np
import jax
import jax.numpy as jnp
from jax import lax
from jax.experimental import pallas as pl
from jax.experimental.pallas import tpu as pltpu

F32 = jnp.float32
BF16 = jnp.bfloat16

EPS = 1e-6
ROPE_BASE = 10000.0
LANES = 128
SUBLANES = 8

D_MODEL = 2048
D_FF = 5504
FFN_TM = 1024
FFN_TF = 512

MLA_HEADS = 8
MLA_NOPE = 128
MLA_ROPE = 64
MLA_QK = MLA_NOPE + MLA_ROPE
MLA_V = 128
MLA_Q_RANK = 512
MLA_KV_RANK = 256
MLA_WIDTH = MLA_HEADS * MLA_V
QK_PAD = 256
Q_UP_PER_HEAD = 384
ATT_Q_SCALE = float(MLA_QK ** -0.5 * np.log2(np.e))
ATT_HEADS_PER_STEP = 2
ATT_GROUPS = MLA_HEADS // ATT_HEADS_PER_STEP

HG_HEADS = 8
HG_D = 128
HG_CHUNK = 64
HG_LEVELS = (32, 16, 8, 4, 2, 1)
HG_UNROLL = 32
HG_GROUP = HG_CHUNK * HG_UNROLL
HG_SLAB = 256
HG_WIDTH = HG_HEADS * HG_D

MEM_HEADS = 4
MEM_HD = 128
MEM_WIDTH = MEM_HEADS * MEM_HD
XATT_Q_SCALE = float(MEM_HD ** -0.5 * np.log2(np.e))

COL_CQ = 0
COL_CKV = MLA_Q_RANK
COL_HQ = MLA_Q_RANK + MLA_KV_RANK
COL_HF = COL_HQ + HG_WIDTH
COL_HI = COL_HF + HG_WIDTH
COL_HG = COL_HI + HG_WIDTH
COL_KPE = COL_HG + HG_WIDTH
COL_KPE_SW = COL_KPE + LANES
IN_COLS_PAD = COL_KPE_SW + LANES

VMEM_LIMIT = 56 * 1024 * 1024
WIDE_VMEM_LIMIT = 61 * 1024 * 1024


def _params(*sem, vmem_limit=VMEM_LIMIT):
    return pltpu.CompilerParams(dimension_semantics=sem, vmem_limit_bytes=vmem_limit)


def _rms_scale(x):
    return lax.rsqrt(jnp.mean(x * x, axis=-1, keepdims=True) + EPS)


def _sigmoid(x):
    return 1.0 / (1.0 + jnp.exp(-x))


def _dot(a, b):
    return jnp.dot(a, b, preferred_element_type=F32)


def _dot_nt(a, b):
    return lax.dot_general(a, b, (((1,), (1,)), ((), ())), preferred_element_type=F32)


def _dot_tn(a, b):
    return lax.dot_general(a, b, (((0,), (0,)), ((), ())), preferred_element_type=F32)


def _ffn_kernel(x_hbm, g_ref, wg_ref, wu_ref, wd_ref, o_ref, h_scr, x_scr, x_sem):
    i = pl.program_id(0)

    def x_copy(tile):
        rows = pl.ds(pl.multiple_of(tile * FFN_TM, FFN_TM), FFN_TM)
        return pltpu.make_async_copy(x_hbm.at[rows, :], x_scr, x_sem)

    def branch(h):
        gate = _dot(h, wg_ref[...].astype(BF16))
        up = _dot(h, wu_ref[...].astype(BF16))
        a = 0.5 * gate * _sigmoid(gate) * up
        f0 = pl.program_id(1) * FFN_TF
        a_ok = f0 + lax.broadcasted_iota(jnp.int32, (1, FFN_TF), 1) < D_FF
        w_ok = f0 + lax.broadcasted_iota(jnp.int32, (FFN_TF, 1), 0) < D_FF
        a = jnp.where(a_ok, a, 0.0).astype(BF16)
        wd = jnp.where(w_ok, wd_ref[...], 0.0).astype(BF16)
        return _dot(a, wd)

    @pl.when(pl.program_id(1) == 0)
    def _():
        @pl.when(i == 0)
        def _():
            x_copy(0).start()

        x_copy(i).wait()
        x = x_scr[...]
        h = (x * _rms_scale(x) * g_ref[...]).astype(BF16)
        h_scr[...] = h
        o_ref[...] = x + branch(h)

        @pl.when(i + 1 < pl.num_programs(0))
        def _():
            x_copy(i + 1).start()

    @pl.when(pl.program_id(1) > 0)
    def _():
        o_ref[...] += branch(h_scr[...])


def _ffn(x, gain, wg, wu, wd):
    t = x.shape[0]
    return pl.pallas_call(
        _ffn_kernel,
        out_shape=jax.ShapeDtypeStruct((t, D_MODEL), F32),
        grid=(t // FFN_TM, pl.cdiv(D_FF, FFN_TF)),
        in_specs=[
            pl.BlockSpec(memory_space=pl.ANY),
            pl.BlockSpec((1, D_MODEL), lambda i, f: (0, 0)),
            pl.BlockSpec((D_MODEL, FFN_TF), lambda i, f: (0, f)),
            pl.BlockSpec((D_MODEL, FFN_TF), lambda i, f: (0, f)),
            pl.BlockSpec((FFN_TF, D_MODEL), lambda i, f: (f, 0)),
        ],
        out_specs=pl.BlockSpec((FFN_TM, D_MODEL), lambda i, f: (i, 0)),
        scratch_shapes=[pltpu.VMEM((FFN_TM, D_MODEL), BF16), pltpu.VMEM((FFN_TM, D_MODEL), F32),
                        pltpu.SemaphoreType.DMA],
        compiler_params=_params("arbitrary", "arbitrary", vmem_limit=WIDE_VMEM_LIMIT),
        name="ffn",
    )(x, gain, wg, wu, wd)


PROJ_TM = 1024
PROJ_TN = 1280


def _in_proj_kernel(x_hbm, g_ref, w_ref, o_ref, h_scr, x_scr, x_sem):
    i = pl.program_id(0)

    def x_copy(tile):
        rows = pl.ds(pl.multiple_of(tile * PROJ_TM, PROJ_TM), PROJ_TM)
        return pltpu.make_async_copy(x_hbm.at[rows, :], x_scr, x_sem)

    @pl.when(pl.program_id(1) == 0)
    def _():
        @pl.when(i == 0)
        def _():
            x_copy(0).start()

        x_copy(i).wait()
        x = x_scr[...]
        h = (x * _rms_scale(x) * g_ref[...]).astype(BF16)
        h_scr[...] = h
        o_ref[...] = _dot_nt(h, w_ref[...])

        @pl.when(i + 1 < pl.num_programs(0))
        def _():
            x_copy(i + 1).start()

    @pl.when(pl.program_id(1) > 0)
    def _():
        o_ref[...] = _dot_nt(h_scr[...], w_ref[...])


def _in_proj(x, gain, w_t):
    t = x.shape[0]
    n = w_t.shape[0]
    return pl.pallas_call(
        _in_proj_kernel,
        out_shape=jax.ShapeDtypeStruct((t, n), F32),
        grid=(t // PROJ_TM, n // PROJ_TN),
        in_specs=[
            pl.BlockSpec(memory_space=pl.ANY),
            pl.BlockSpec((1, D_MODEL), lambda i, j: (0, 0)),
            pl.BlockSpec((PROJ_TN, D_MODEL), lambda i, j: (j, 0)),
        ],
        out_specs=pl.BlockSpec((PROJ_TM, PROJ_TN), lambda i, j: (i, j)),
        scratch_shapes=[pltpu.VMEM((PROJ_TM, D_MODEL), BF16), pltpu.VMEM((PROJ_TM, D_MODEL), F32),
                        pltpu.SemaphoreType.DMA],
        compiler_params=_params("arbitrary", "arbitrary", vmem_limit=WIDE_VMEM_LIMIT),
        name="in_proj",
    )(x, gain, w_t)


PREP_TM = 1024


def _mla_prep_kernel(cq_ref, ckv_ref, kpe_ref, kpesw_ref, pos_ref, invf_ref,
                     qa_ref, kva_ref, wq_ref, wkv_ref,
                     qn_ref, qpa_ref, qpb_ref, kn_ref, kpa_ref, kpb_ref,
                     q_ref, k_ref, vt_ref):
    ang = pos_ref[...].astype(F32) * invf_ref[...]
    cos4 = jnp.cos(ang)
    sin4 = jnp.sin(ang)
    lane = lax.broadcasted_iota(jnp.int32, (1, LANES), 1)
    half = MLA_ROPE // 2

    def expand(t4):
        parts = []
        for part in range(LANES // half):
            t = pltpu.roll(t4, (LANES - half * part) % LANES, 1) if part else t4
            parts.append(jnp.where(lane < half, t, pltpu.roll(t, half, 1)))
        return jnp.concatenate(parts, axis=0)

    cos = expand(cos4)
    sin = expand(sin4)

    q_const = float(np.sqrt(MLA_QK)) * ATT_Q_SCALE
    k_const = float(np.sqrt(MLA_QK))

    cq = cq_ref[...]
    cqn = (cq * _rms_scale(cq) * qa_ref[...]).astype(BF16)
    qall = _dot(cqn, wq_ref[...])
    qn_gain = qn_ref[...] * q_const
    q_cos = (qpa_ref[...] * q_const) * cos
    q_sin = (qpb_ref[...] * q_const) * sin
    for h in range(MLA_HEADS):
        base = h * Q_UP_PER_HEAD
        qn = qall[:, base:base + LANES]
        y = qall[:, base + LANES:base + 2 * LANES]
        ysw = qall[:, base + 2 * LANES:base + 3 * LANES]
        r = lax.rsqrt(jnp.sum(qn * qn + y * y, axis=-1, keepdims=True) + MLA_QK * EPS)
        g, c = divmod(h, ATT_HEADS_PER_STEP)
        c *= QK_PAD
        q_ref[g, :, c:c + LANES] = (qn * r * qn_gain).astype(BF16)
        q_ref[g, :, c + LANES:c + QK_PAD] = ((y * q_cos + ysw * q_sin) * r).astype(BF16)

    ckv = ckv_ref[...]
    ckvn = (ckv * _rms_scale(ckv) * kva_ref[...]).astype(BF16)
    wkv = wkv_ref[...]
    per_head = MLA_NOPE + MLA_V
    wk = jnp.concatenate([wkv[:, h * per_head:h * per_head + MLA_NOPE] for h in range(MLA_HEADS)], axis=1)
    wv = jnp.concatenate([wkv[:, h * per_head + MLA_NOPE:(h + 1) * per_head] for h in range(MLA_HEADS)], axis=1)
    kn_all = _dot(ckvn, wk.astype(BF16))
    vt_ref[...] = _dot_nt(wv.T.astype(BF16), ckvn).astype(BF16)
    yk = kpe_ref[...]
    kn_gain = kn_ref[...] * k_const
    kr = yk * ((kpa_ref[...] * k_const) * cos) + kpesw_ref[...] * ((kpb_ref[...] * k_const) * sin)
    yk_sq = yk * yk
    for h in range(MLA_HEADS):
        kn = kn_all[:, h * MLA_NOPE:(h + 1) * MLA_NOPE]
        r = lax.rsqrt(jnp.sum(kn * kn + yk_sq, axis=-1, keepdims=True) + MLA_QK * EPS)
        g, c = divmod(h, ATT_HEADS_PER_STEP)
        c *= QK_PAD
        k_ref[g, :, c:c + LANES] = (kn * r * kn_gain).astype(BF16)
        k_ref[g, :, c + LANES:c + QK_PAD] = (kr * r).astype(BF16)


def _mla_prep(proj, pos, invf, qa, kva, wq, wkv, qn, qpa, qpb, kn, kpa, kpb):
    t = proj.shape[0]
    tm = PREP_TM
    row = lambda i: (0, 0)
    vec = lambda n: pl.BlockSpec((1, n), row)
    return pl.pallas_call(
        _mla_prep_kernel,
        out_shape=(jax.ShapeDtypeStruct((ATT_GROUPS, t, ATT_HEADS_PER_STEP * QK_PAD), BF16),
                   jax.ShapeDtypeStruct((ATT_GROUPS, t, ATT_HEADS_PER_STEP * QK_PAD), BF16),
                   jax.ShapeDtypeStruct((MLA_WIDTH, t), BF16)),
        grid=(t // tm,),
        in_specs=[
            pl.BlockSpec((tm, MLA_Q_RANK), lambda i: (i, COL_CQ // MLA_Q_RANK)),
            pl.BlockSpec((tm, MLA_KV_RANK), lambda i: (i, COL_CKV // MLA_KV_RANK)),
            pl.BlockSpec((tm, LANES), lambda i: (i, COL_KPE // LANES)),
            pl.BlockSpec((tm, LANES), lambda i: (i, COL_KPE_SW // LANES)),
            pl.BlockSpec((tm // (LANES // (MLA_ROPE // 2)), LANES), lambda i: (i, 0)),
            vec(LANES),
            vec(MLA_Q_RANK), vec(MLA_KV_RANK),
            pl.BlockSpec(wq.shape, row), pl.BlockSpec(wkv.shape, row),
            vec(LANES), vec(LANES), vec(LANES), vec(LANES), vec(LANES), vec(LANES),
        ],
        out_specs=(pl.BlockSpec((ATT_GROUPS, tm, ATT_HEADS_PER_STEP * QK_PAD), lambda i: (0, i, 0)),
                   pl.BlockSpec((ATT_GROUPS, tm, ATT_HEADS_PER_STEP * QK_PAD), lambda i: (0, i, 0)),
                   pl.BlockSpec((MLA_WIDTH, tm), lambda i: (0, i))),
        compiler_params=_params("parallel"),
        name="mla_prep",
    )(proj, proj, proj, proj, pos, invf, qa, kva, wq, wkv, qn, qpa, qpb, kn, kpa, kpb)


ATT_T = 256
ATT_AHEAD = 5


def _mla_attn_kernel(q_ref, k_ref, vt_ref, o_ref):
    kpos = lax.broadcasted_iota(jnp.int32, (ATT_T, ATT_T), 0)
    qpos = lax.broadcasted_iota(jnp.int32, (ATT_T, ATT_T), 1)
    nq = q_ref.shape[0] // ATT_T

    def scores(unit):
        h, i = unit
        lo, hi = i * ATT_T, (i + 1) * ATT_T
        cols = slice(h * QK_PAD, (h + 1) * QK_PAD)
        q = q_ref[lo:hi, cols]
        s_diag = jnp.where(kpos <= qpos, _dot_nt(k_ref[lo:hi, cols], q), -jnp.inf)
        s_off = _dot_nt(k_ref[0:lo, cols], q) if i > 0 else None
        return s_diag, s_off

    def finish(unit, s_diag, s_off):
        h, i = unit
        lo, hi = i * ATT_T, (i + 1) * ATT_T
        vt = vt_ref.at[h * MLA_V:(h + 1) * MLA_V, :]
        m = jnp.max(s_diag, axis=0, keepdims=True)
        if i > 0:
            m = jnp.maximum(m, jnp.max(s_off, axis=0, keepdims=True))
        p = jnp.exp2(s_diag - m)
        l = jnp.sum(p, axis=0, keepdims=True)
        acc = _dot(vt[:, lo:hi], p.astype(BF16))
        if i > 0:
            p = jnp.exp2(s_off - m)
            l = l + jnp.sum(p, axis=0, keepdims=True)
            acc = acc + _dot(vt[:, 0:lo], p.astype(BF16))
        o_ref[lo:hi, h * MLA_V:(h + 1) * MLA_V] = (acc / l).T.astype(BF16)

    order = [(h, i) for i in reversed(range(nq)) for h in range(ATT_HEADS_PER_STEP)]
    pending = [scores(u) for u in order[:ATT_AHEAD]]
    for n, unit in enumerate(order):
        if n + ATT_AHEAD < len(order):
            pending.append(scores(order[n + ATT_AHEAD]))
        finish(unit, *pending.pop(0))


def _mla_attn(q, k, vt, batch, seq):
    hs = ATT_HEADS_PER_STEP
    return pl.pallas_call(
        _mla_attn_kernel,
        out_shape=jax.ShapeDtypeStruct((ATT_GROUPS, batch * seq, hs * MLA_V), BF16),
        grid=(batch, ATT_GROUPS),
        in_specs=[
            pl.BlockSpec((None, seq, hs * QK_PAD), lambda b, h: (h, b, 0)),
            pl.BlockSpec((None, seq, hs * QK_PAD), lambda b, h: (h, b, 0)),
            pl.BlockSpec((hs * MLA_V, seq), lambda b, h: (h, b)),
        ],
        out_specs=pl.BlockSpec((None, seq, hs * MLA_V), lambda b, h: (h, b, 0)),
        compiler_params=_params("parallel", "parallel"),
        name="mla_attn",
    )(q, k, vt)


def _hgrn_masks():
    t = np.arange(HG_CHUNK)[:, None]
    s = np.arange(HG_CHUNK)[None, :]
    masks = [((t // c) % 2 == 1) & ((s // c) == (t // c) - 1) for c in HG_LEVELS]
    masks.append(t == s)
    return np.stack(masks).astype(np.float32)


def _hgrn_kernel(q_ref, f_ref, i_ref, g_ref, lbl_ref, gn_ref, msk_ref, tri_ref, o_ref, b_all):
    lbl = lbl_ref[...]
    e = jnp.exp(lbl - jnp.max(lbl, axis=0, keepdims=True))
    lb = e[0:1, :] / jnp.sum(e, axis=0, keepdims=True)
    gn = gn_ref[...]
    tri = tri_ref[...]
    sub = lax.broadcasted_iota(jnp.int32, (SUBLANES, LANES), 0)
    zeros8 = jnp.zeros((SUBLANES, LANES), F32)
    per_chunk = HG_CHUNK // SUBLANES
    chunk_rows = [slice(u * HG_CHUNK, (u + 1) * HG_CHUNK) for u in range(HG_UNROLL)]

    def sel(bounds, vals):
        out = vals[-1]
        for bound, val in zip(reversed(bounds), reversed(vals[:-1])):
            out = jnp.where(sub < bound, val, out)
        return out

    def head(gi):
        rows = slice(gi * HG_GROUP, (gi + 1) * HG_GROUP)
        q = q_ref[rows, :]
        v = i_ref[rows, :].astype(BF16)
        f = lb + (1.0 - lb) * _sigmoid(f_ref[rows, :])
        lg = jnp.log2(f)
        k = 1.0 - f

        l1 = lg.astype(BF16)
        l2 = (lg - l1.astype(F32)).astype(BF16)
        slabs = [slice(i, i + HG_SLAB) for i in range(0, HG_GROUP, HG_SLAB)]
        b = jnp.concatenate([_dot(tri, l1[r]) + _dot(tri, l2[r]) for r in slabs], axis=0)
        b_all[gi] = b
        return q, k, f, v, b

    def main(gi, hd, st):
        rows = slice(gi * HG_GROUP, (gi + 1) * HG_GROUP)
        q, k, f, v, b = hd
        b_scr = b_all.at[gi]
        cache = {}

        def rowb(u, r):
            if r < 0:
                return zeros8
            if (u, r) not in cache:
                cache[u, r] = jnp.broadcast_to(b_scr[u * HG_CHUNK + r:u * HG_CHUNK + r + 1, :], (SUBLANES, LANES))
            return cache[u, r]

        def build(fn):
            return jnp.concatenate([fn(u, j) for u in range(HG_UNROLL) for j in range(per_chunk)], axis=0)

        def group_rows(x, u, j):
            r0 = u * HG_CHUNK + j * SUBLANES
            return x[r0:r0 + SUBLANES]

        def level_rows(ci, j):
            if ("m", ci, j) not in cache:
                cache["m", ci, j] = msk_ref[ci, j * SUBLANES:(j + 1) * SUBLANES, :] != 0.0
            return cache["m", ci, j]

        diag = jnp.sum(q * k, axis=-1, keepdims=True)
        a = [[jnp.where(level_rows(len(HG_LEVELS), j), group_rows(diag, u, j), 0.0) for j in range(per_chunk)]
             for u in range(HG_UNROLL)]
        for ci, c in enumerate(HG_LEVELS):
            if c >= SUBLANES:
                m = c // SUBLANES
                right = [j for j in range(per_chunk) if (j // m) % 2 == 1]
                pick = lambda x, js: jnp.concatenate([group_rows(x, u, j) for u in range(HG_UNROLL) for j in js], axis=0)
                bs = jnp.concatenate([rowb(u, SUBLANES * (j // m) * m - 1) for u in range(HG_UNROLL) for j in right], axis=0)
                qd = (pick(q, right) * jnp.exp2(pick(b, right) - bs)).astype(BF16)
                kd = jnp.concatenate(
                    [group_rows(k, u, j) * jnp.exp2(rowb(u, SUBLANES * ((j // m) * m + m) - 1) - group_rows(b, u, j))
                     if (j // m) % 2 == 0 else zeros8
                     for u in range(HG_UNROLL) for j in range(per_chunk)], axis=0).astype(BF16)
                rows_per_chunk = len(right) * SUBLANES
                for u, r in enumerate(chunk_rows):
                    prod = _dot_nt(qd[u * rows_per_chunk:(u + 1) * rows_per_chunk], kd[r])
                    for n, j in enumerate(right):
                        a[u][j] = jnp.where(level_rows(ci, j), prod[n * SUBLANES:(n + 1) * SUBLANES], a[u][j])
                continue
            if c == 4:
                bs = build(lambda u, j: sel((4,), (rowb(u, 8 * j - 1), rowb(u, 8 * j + 3))))
                be = build(lambda u, j: sel((4,), (rowb(u, 8 * j + 3), rowb(u, 8 * j + 7))))
            elif c == 2:
                bs = build(lambda u, j: sel((2, 4, 6), (rowb(u, 8 * j - 1), rowb(u, 8 * j + 1),
                                                         rowb(u, 8 * j + 3), rowb(u, 8 * j + 5))))
                be = build(lambda u, j: sel((2, 4, 6), (rowb(u, 8 * j + 1), rowb(u, 8 * j + 3),
                                                         rowb(u, 8 * j + 5), rowb(u, 8 * j + 7))))
            if c == 1:
                qd = (q * f).astype(BF16)
                kd = k.astype(BF16)
            else:
                qd = (q * jnp.exp2(b - bs)).astype(BF16)
                kd = (k * jnp.exp2(be - b)).astype(BF16)
            for u, r in enumerate(chunk_rows):
                prod = _dot_nt(qd[r], kd[r])
                for j in range(per_chunk):
                    a[u][j] = jnp.where(level_rows(ci, j), prod[j * SUBLANES:(j + 1) * SUBLANES], a[u][j])
        a = [jnp.concatenate(a[u], axis=0) for u in range(HG_UNROLL)]

        q_exp = (q * jnp.exp2(b)).astype(BF16)
        b_last = build(lambda u, j: rowb(u, HG_CHUNK - 1))
        k_dec = (k * jnp.exp2(b_last - b)).astype(BF16)
        o_intra = [_dot(a[u].astype(BF16), v[r]) for u, r in enumerate(chunk_rows)]
        st_add = [_dot_tn(v[r], k_dec[r]) for r in chunk_rows]

        o = []
        for u, r in enumerate(chunk_rows):
            o.append(o_intra[u] + _dot_nt(q_exp[r], st.astype(BF16)))
            st = jnp.exp2(rowb(u, HG_CHUNK - 1)[0:1, :]) * st + st_add[u]
        o = jnp.concatenate(o, axis=0)

        g = g_ref[rows, :]
        o_ref[rows, :] = (o * _rms_scale(o) * gn * (g * _sigmoid(g))).astype(BF16)
        return st

    n_groups = q_ref.shape[0] // HG_GROUP
    st = jnp.zeros((HG_D, HG_D), F32)
    hd = head(0)
    for gi in range(n_groups):
        nxt = head(gi + 1) if gi + 1 < n_groups else None
        st = main(gi, hd, st)
        hd = nxt


def _hgrn(proj, lb_logits, out_norm, batch, seq):
    hblk = lambda col: pl.BlockSpec((seq, HG_D), lambda b, h: (b, col // HG_D + h))
    masks = jnp.asarray(_hgrn_masks())
    tri = np.kron(np.eye(HG_SLAB // HG_CHUNK), np.tril(np.ones((HG_CHUNK, HG_CHUNK))))
    tri = jnp.asarray(tri.astype(np.float32), dtype=BF16)
    nlev = masks.shape[0]
    return pl.pallas_call(
        _hgrn_kernel,
        out_shape=jax.ShapeDtypeStruct((batch * seq, HG_WIDTH), BF16),
        grid=(batch, HG_HEADS),
        in_specs=[
            hblk(COL_HQ), hblk(COL_HF), hblk(COL_HI), hblk(COL_HG),
            pl.BlockSpec((lb_logits.shape[0], HG_D), lambda b, h: (0, h)),
            pl.BlockSpec((1, HG_D), lambda b, h: (0, 0)),
            pl.BlockSpec((nlev, HG_CHUNK, HG_CHUNK), lambda b, h: (0, 0, 0)),
            pl.BlockSpec((HG_SLAB, HG_SLAB), lambda b, h: (0, 0)),
        ],
        out_specs=pl.BlockSpec((seq, HG_D), lambda b, h: (b, h)),
        scratch_shapes=[pltpu.VMEM((seq // HG_GROUP, HG_GROUP, HG_D), F32)],
        compiler_params=_params("parallel", "parallel"),
        name="hgrn2",
    )(proj, proj, proj, proj, lb_logits, out_norm, masks, tri)


def _mem_kv_kernel(m_ref, g_ref, w_ref, kn_ref, k_ref, v_ref):
    m = m_ref[...]
    mn = (m * _rms_scale(m) * g_ref[...]).astype(BF16)
    kv = _dot(mn, w_ref[...].astype(BF16))
    for h in range(MEM_HEADS):
        k = kv[:, 2 * h * MEM_HD:(2 * h + 1) * MEM_HD]
        k_ref[:, h * MEM_HD:(h + 1) * MEM_HD] = (k * _rms_scale(k) * kn_ref[...]).astype(BF16)
        v_ref[:, h * MEM_HD:(h + 1) * MEM_HD] = kv[:, (2 * h + 1) * MEM_HD:(2 * h + 2) * MEM_HD].astype(BF16)


def _mem_kv(mem, gain, w, k_norm, batch, mem_len):
    return pl.pallas_call(
        _mem_kv_kernel,
        out_shape=(jax.ShapeDtypeStruct((batch * mem_len, MEM_WIDTH), BF16),
                   jax.ShapeDtypeStruct((batch * mem_len, MEM_WIDTH), BF16)),
        grid=(batch,),
        in_specs=[
            pl.BlockSpec((mem_len, D_MODEL), lambda b: (b, 0)),
            pl.BlockSpec((1, D_MODEL), lambda b: (0, 0)),
            pl.BlockSpec(w.shape, lambda b: (0, 0)),
            pl.BlockSpec((1, MEM_HD), lambda b: (0, 0)),
        ],
        out_specs=(pl.BlockSpec((mem_len, MEM_WIDTH), lambda b: (b, 0)),
                   pl.BlockSpec((mem_len, MEM_WIDTH), lambda b: (b, 0))),
        compiler_params=_params("parallel"),
        name="mem_kv",
    )(mem, gain, w, k_norm)


XA_TM = 512


def _out_xattn_kernel(a_ref, r_ref, x_ref, og_ref, wout_ref, g_ref, wq_ref, qn_ref, k_ref, v_ref, wo_ref,
                      o_ref, att_scr):
    a = jnp.concatenate([a_ref[g] for g in range(ATT_GROUPS)], axis=-1).astype(F32)
    an = (a * _rms_scale(a) * og_ref[...]).astype(BF16)
    x = (x_ref[...] + _dot(an, wout_ref[:MLA_WIDTH, :].astype(BF16))
         + _dot(r_ref[...], wout_ref[MLA_WIDTH:, :].astype(BF16)))

    h = (x * _rms_scale(x) * g_ref[...]).astype(BF16)
    q = _dot(h, wq_ref[...].astype(BF16))
    heads = [slice(hd * MEM_HD, (hd + 1) * MEM_HD) for hd in range(MEM_HEADS)]
    q_gain = qn_ref[...] * (float(np.sqrt(MEM_HD)) * XATT_Q_SCALE)
    scores = []
    for cols in heads:
        qh = q[:, cols]
        r = lax.rsqrt(jnp.sum(qh * qh, axis=-1, keepdims=True) + MEM_HD * EPS)
        scores.append(_dot_nt((qh * r * q_gain).astype(BF16), k_ref[:, cols]))
    for cols, s in zip(heads, scores):
        p = jnp.exp2(s - jnp.max(s, axis=-1, keepdims=True))
        l = jnp.sum(p, axis=-1, keepdims=True)
        att_scr[:, cols] = (_dot(p.astype(BF16), v_ref[:, cols]) / l).astype(BF16)
    o_ref[...] = x + _dot(att_scr[...], wo_ref[...].astype(BF16))


def _out_xattn(a, r, x, out_gain, w_out, gain, wq, q_norm, kx, vx, wo, seq, mem_len):
    t = x.shape[0]
    per_batch = seq // XA_TM
    tile = lambda n: pl.BlockSpec((XA_TM, n), lambda i: (i, 0))
    whole = lambda w: pl.BlockSpec(w.shape, lambda i: (0, 0), pipeline_mode=pl.Buffered(1))
    vec = lambda n: pl.BlockSpec((1, n), lambda i: (0, 0))
    mem = pl.BlockSpec((mem_len, MEM_WIDTH), lambda i: (i // per_batch, 0))
    groups = pl.BlockSpec((ATT_GROUPS, XA_TM, MLA_WIDTH // ATT_GROUPS), lambda i: (0, i, 0))
    return pl.pallas_call(
        _out_xattn_kernel,
        out_shape=jax.ShapeDtypeStruct((t, D_MODEL), F32),
        grid=(t // XA_TM,),
        in_specs=[groups, tile(HG_WIDTH), tile(D_MODEL), vec(MLA_WIDTH), whole(w_out),
                  vec(D_MODEL), whole(wq), vec(MEM_HD), mem, mem, whole(wo)],
        out_specs=tile(D_MODEL),
        scratch_shapes=[pltpu.VMEM((XA_TM, MEM_WIDTH), BF16)],
        compiler_params=_params("parallel", vmem_limit=WIDE_VMEM_LIMIT),
        name="out_xattn",
    )(a, r, x, out_gain, w_out, gain, wq, q_norm, kx, vx, wo)


def _pe_pair(x1, x2):
    z = jnp.zeros(x1.shape[:-1] + (LANES - MLA_ROPE,), x1.dtype)
    return jnp.concatenate([x1, x2, z], axis=-1), jnp.concatenate([x2, x1, z], axis=-1)


W_IN_TC = 512


def _w_in_prep_kernel(w_ref, o_ref):
    half = MLA_ROPE // 2
    kpe0 = MLA_Q_RANK + MLA_KV_RANK
    hg0 = kpe0 + MLA_ROPE
    zeros = jnp.zeros((LANES - MLA_ROPE, o_ref.shape[1]), BF16)
    o_ref[:kpe0, :] = w_ref[:kpe0, :].astype(BF16)
    o_ref[kpe0:COL_KPE, :] = w_ref[hg0:, :].astype(BF16)
    x1 = w_ref[kpe0:kpe0 + half, :].astype(BF16)
    x2 = w_ref[kpe0 + half:hg0, :].astype(BF16)
    o_ref[COL_KPE:COL_KPE + half, :] = x1
    o_ref[COL_KPE + half:COL_KPE + MLA_ROPE, :] = x2
    o_ref[COL_KPE + MLA_ROPE:COL_KPE_SW, :] = zeros
    o_ref[COL_KPE_SW:COL_KPE_SW + half, :] = x2
    o_ref[COL_KPE_SW + half:COL_KPE_SW + MLA_ROPE, :] = x1
    o_ref[COL_KPE_SW + MLA_ROPE:, :] = zeros


def _prep_w_in(w_in, layer):
    w_t = jnp.swapaxes(w_in, 1, 2)
    _, n, k = w_t.shape
    return pl.pallas_call(
        _w_in_prep_kernel,
        out_shape=jax.ShapeDtypeStruct((IN_COLS_PAD, k), BF16),
        grid=(k // W_IN_TC,),
        in_specs=[pl.BlockSpec((None, n, W_IN_TC), lambda i: (layer, 0, i))],
        out_specs=pl.BlockSpec((IN_COLS_PAD, W_IN_TC), lambda i: (0, i)),
        compiler_params=_params("parallel"),
        name="w_in_prep",
    )(w_t)


def _prep_w_q_up(w):
    w = w.astype(BF16).reshape(MLA_Q_RANK, MLA_HEADS, MLA_QK)
    half = MLA_ROPE // 2
    pe, pe_sw = _pe_pair(w[..., MLA_NOPE:MLA_NOPE + half], w[..., MLA_NOPE + half:])
    return jnp.concatenate([w[..., :MLA_NOPE], pe, pe_sw], axis=-1).reshape(MLA_Q_RANK, MLA_HEADS * Q_UP_PER_HEAD)


def _pe_gains(norm):
    half = MLA_ROPE // 2
    g1 = norm[MLA_NOPE:MLA_NOPE + half]
    g2 = norm[MLA_NOPE + half:]
    ga, gb = _pe_pair(g1, g2)
    sign = jnp.concatenate([-jnp.ones((half,), F32), jnp.ones((LANES - half,), F32)])
    return ga[None, :], (gb * sign)[None, :]


def kernel(x, mem, positions, ffn1_norm, ffn1_w_gate, ffn1_w_up, ffn1_w_down, mix_norm, w_in, mla_q_a_norm, mla_w_q_up, mla_kv_a_norm, mla_w_kv_up, mla_q_norm, mla_k_norm, mla_out_norm, hg_lb_logits, hg_out_norm, w_out, xattn_norm, mem_norm, xattn_w_q, xattn_w_kv, xattn_q_norm, xattn_k_norm, xattn_w_o, ffn2_norm, ffn2_w_gate, ffn2_w_up, ffn2_w_down):
    batch, seq, _ = x.shape
    mem_len = mem.shape[1]
    depth = ffn1_norm.shape[0]
    assert depth == 1 and seq % ATT_T == 0 and seq % XA_TM == 0
    t = batch * seq
    xt = x.reshape(t, D_MODEL)
    half = MLA_ROPE // 2
    quarters = LANES // half
    pos = positions.reshape(t // PREP_TM, quarters, PREP_TM // quarters).transpose(0, 2, 1)
    pos = jnp.repeat(pos, half, axis=-1).reshape(t // quarters, LANES)
    inv_freq = ROPE_BASE ** (-np.arange(half, dtype=np.float32) / half)
    invf = jnp.asarray(np.tile(inv_freq, quarters)[None, :])
    l = 0

    xt = _ffn(xt, ffn1_norm[l][None, :], ffn1_w_gate[l], ffn1_w_up[l], ffn1_w_down[l])

    proj = _in_proj(xt, mix_norm[l][None, :], _prep_w_in(w_in, l))
    qpa, qpb = _pe_gains(mla_q_norm[l])
    kpa, kpb = _pe_gains(mla_k_norm[l])
    q, k, v = _mla_prep(proj, pos, invf, mla_q_a_norm[l][None, :], mla_kv_a_norm[l][None, :],
                        _prep_w_q_up(mla_w_q_up[l]), mla_w_kv_up[l],
                        mla_q_norm[l][None, :MLA_NOPE], qpa, qpb,
                        mla_k_norm[l][None, :MLA_NOPE], kpa, kpb)
    a = _mla_attn(q, k, v, batch, seq)
    r = _hgrn(proj, hg_lb_logits, hg_out_norm[l][None, :], batch, seq)
    kx, vx = _mem_kv(mem.reshape(batch * mem_len, D_MODEL), mem_norm[l][None, :],
                     xattn_w_kv[l], xattn_k_norm[l][None, :], batch, mem_len)
    xt = _out_xattn(a, r, xt, mla_out_norm[l][None, :], w_out[l],
                    xattn_norm[l][None, :], xattn_w_q[l], xattn_q_norm[l][None, :],
                    kx, vx, xattn_w_o[l], seq, mem_len)

    xt = _ffn(xt, ffn2_norm[l][None, :], ffn2_w_gate[l], ffn2_w_up[l], ffn2_w_down[l])
    return xt.reshape(batch, seq, D_MODEL)
```
